```python
import jax, jax.numpy as jnp
from jax import lax
import numpy as np

D_MODEL = 1024
BATCH = 8
SEQ = 4096
DEPTH = 2
DEC_BATCH = 4
DEC_SEQ = 4096
PAST_LEN = 128

N_META = 16
N_Q_HEADS = 16
N_KV_HEADS = 4
HEAD_DIM = 64
WINDOW = 128
ATT_BLOCK = 128
ROT_DIM = HEAD_DIM // 4
ROPE_THETA = 500000.0
HG_HEADS = 8
HG_KEY_DIM = 128
HG_VAL_DIM = 128
HG_CHUNK = 64
PEER_HEADS = 8
PEER_NKEYS = 128
PEER_NEXPERTS = PEER_NKEYS * PEER_NKEYS
PEER_QDIM = 256
PEER_TOPK = 16
PEER_BLOCK = 256

EPS = 1e-6
NEG = -1e30

ATT_Q_W = N_Q_HEADS * HEAD_DIM
ATT_KV_W = N_KV_HEADS * HEAD_DIM
HG_K_W = HG_HEADS * HG_KEY_DIM
HG_V_W = HG_HEADS * HG_VAL_DIM
IN_SIZES = (ATT_Q_W, ATT_KV_W, ATT_KV_W, HG_K_W, HG_K_W, HG_K_W, HG_V_W, HG_V_W, D_MODEL, D_MODEL)
IN_W = ATT_Q_W + 2 * ATT_KV_W + 3 * HG_K_W + 2 * HG_V_W + 2 * D_MODEL

kernel_name = "hybrid_bidir_swa_hgrn2_peer"


def rmsnorm(x, g):
    xf = x.astype(jnp.float32)
    y = xf * lax.rsqrt(jnp.mean(xf * xf, axis=-1, keepdims=True) + EPS)
    return (y * g.astype(jnp.float32)).astype(x.dtype)


def rope_partial(x, pos):
    half = ROT_DIM // 2
    inv = jnp.power(jnp.float32(ROPE_THETA), -jnp.arange(half, dtype=jnp.float32) * 2.0 / ROT_DIM)
    ang = pos.astype(jnp.float32)[:, None] * inv[None, :]
    cos = jnp.cos(ang)[None, :, None, :]
    sin = jnp.sin(ang)[None, :, None, :]
    xr = x[..., :ROT_DIM].astype(jnp.float32)
    x1, x2 = xr[..., :half], xr[..., half:]
    rot = jnp.concatenate([x1 * cos - x2 * sin, x2 * cos + x1 * sin], axis=-1).astype(x.dtype)
    return jnp.concatenate([rot, x[..., ROT_DIM:]], axis=-1)


def sink_softmax(s, sink):
    sk = sink.astype(jnp.float32)[:, :, None, None]
    m = jnp.maximum(jnp.max(s, axis=-1, keepdims=True), sk)
    p = jnp.exp(s - m)
    return p / (jnp.sum(p, axis=-1, keepdims=True) + jnp.exp(sk - m))


def window_attention(q, k, v, sinks):
    B, L, _, hd = q.shape
    S = L - N_META
    nb = S // ATT_BLOCK
    G = N_Q_HEADS // N_KV_HEADS
    scale = hd ** -0.5
    sink = sinks.reshape(N_KV_HEADS, G)
    qg = q.reshape(B, L, N_KV_HEADS, G, hd)
    nk = N_META + WINDOW
    s_m = jnp.einsum('bqhgd,bkhd->bhgqk', qg[:, :N_META], k[:, :nk]).astype(jnp.float32) * scale
    ok_m = (jnp.arange(nk)[None, :] - jnp.arange(N_META)[:, None]) <= WINDOW
    p_m = sink_softmax(jnp.where(ok_m, s_m, NEG), sink).astype(v.dtype)
    o_m = jnp.einsum('bhgqk,bkhd->bqhgd', p_m, v[:, :nk])
    km, vm = k[:, :N_META], v[:, :N_META]

    def band(t):
        tp = jnp.pad(t[:, N_META:], ((0, 0), (ATT_BLOCK, ATT_BLOCK), (0, 0), (0, 0)))
        tp = tp.reshape(B, nb + 2, ATT_BLOCK, N_KV_HEADS, hd)
        return jnp.concatenate([tp[:, :-2], tp[:, 1:-1], tp[:, 2:]], axis=2)

    qb = qg[:, N_META:].reshape(B, nb, ATT_BLOCK, N_KV_HEADS, G, hd)
    kb, vb = band(k), band(v)
    qi = jnp.arange(ATT_BLOCK)
    kj = jnp.arange(3 * ATT_BLOCK)
    in_win = jnp.abs(kj[None, :] - ATT_BLOCK - qi[:, None]) <= WINDOW

    def block(args):
        n, qn, kn, vn = args
        kpos = (n - 1) * ATT_BLOCK + kj
        ok = in_win & ((kpos >= 0) & (kpos < S))[None, :]
        s_loc = jnp.einsum('bqhgd,bkhd->bhgqk', qn, kn).astype(jnp.float32) * scale
        s_glb = jnp.einsum('bqhgd,bmhd->bhgqm', qn, km).astype(jnp.float32) * scale
        p = sink_softmax(jnp.concatenate([s_glb, jnp.where(ok, s_loc, NEG)], axis=-1), sink).astype(v.dtype)
        return (jnp.einsum('bhgqm,bmhd->bqhgd', p[..., :N_META], vm)
                + jnp.einsum('bhgqk,bkhd->bqhgd', p[..., N_META:], vn))

    o_r = lax.map(block, (jnp.arange(nb), jnp.moveaxis(qb, 1, 0), jnp.moveaxis(kb, 1, 0), jnp.moveaxis(vb, 1, 0)))
    o_r = jnp.moveaxis(o_r, 0, 1).reshape(B, S, N_KV_HEADS, G, hd)
    return jnp.concatenate([o_m, o_r], axis=1).reshape(B, L, N_Q_HEADS * hd)


def gla_chunk_scan(q, k, v, logf):
    B, H, T, Dk = q.shape
    Dv = v.shape[-1]
    C = HG_CHUNK
    n = T // C

    def to_chunks(t):
        return jnp.moveaxis(t.astype(jnp.float32).reshape(B, H, n, C, t.shape[-1]), 2, 0)

    qc, kc, vc, gc = (to_chunks(t) for t in (q, k, v, logf))
    causal = jnp.tril(jnp.ones((C, C), dtype=bool))[:, :, None]

    def step(S, inp):
        qi, ki, vi, gi = inp
        b = jnp.cumsum(gi, axis=2)
        o_inter = jnp.einsum('bhtk,bhkv->bhtv', qi * jnp.exp(b), S)
        diff = b[:, :, :, None, :] - b[:, :, None, :, :]
        dec = jnp.exp(jnp.where(causal, diff, -jnp.inf))
        A = jnp.einsum('bhtk,bhsk,bhtsk->bhts', qi, ki, dec)
        o = o_inter + jnp.einsum('bhts,bhsv->bhtv', A, vi)
        b_last = b[:, :, -1:, :]
        S_new = (jnp.exp(b_last[:, :, 0, :])[..., None] * S
                 + jnp.einsum('bhsk,bhsv->bhkv', ki * jnp.exp(b_last - b), vi))
        return S_new, o

    S0 = jnp.zeros((B, H, Dk, Dv), jnp.float32)
    _, o = lax.scan(step, S0, (qc, kc, vc, gc))
    return jnp.moveaxis(o, 0, 2).reshape(B, H, T, Dv)


def hgrn2_bidir(hq, zf_fwd, zf_bwd, hi, hg_gate, lb, norm_g):
    B, L, _ = hq.shape

    def heads(t):
        return t.reshape(B, L, HG_HEADS, -1).transpose(0, 2, 1, 3)

    def gates(z, lbd):
        zf = z.astype(jnp.float32)
        lbd = lbd.astype(jnp.float32)
        logf = jnp.logaddexp(jnp.log(lbd), jnp.log1p(-lbd) + jax.nn.log_sigmoid(zf))
        kk = (1.0 - lbd) * jax.nn.sigmoid(-zf)
        return heads(logf), heads(kk)

    q = heads(jax.nn.silu(hq.astype(jnp.float32)))
    v = heads(hi.astype(jnp.float32))
    g_f, k_f = gates(zf_fwd, lb[0])
    g_b, k_b = gates(zf_bwd, lb[1])
    P = HG_CHUNK - N_META

    def run(qh, kh, vh, gh, front):
        pad = ((0, 0), (0, 0), (P, 0) if front else (0, P), (0, 0))
        o = gla_chunk_scan(*(jnp.pad(t, pad) for t in (qh, kh, vh, gh)))
        return o[:, :, P:] if front else o[:, :, :-P]

    flip = lambda t: t[:, :, ::-1]
    o_f = run(q, k_f, v, g_f, True)
    o_b = flip(run(flip(q), flip(k_b), flip(v), flip(g_b), False))
    o = (o_f + o_b).transpose(0, 2, 1, 3)
    o = rmsnorm(o, norm_g.reshape(HG_HEADS, HG_VAL_DIM))
    o = o.reshape(B, L, HG_V_W) * jax.nn.silu(hg_gate.astype(jnp.float32))
    return o.astype(hq.dtype)


def peer(x, w_q, sub_keys, u, v):
    B, L, D = x.shape
    T = B * L
    pad = (-T) % PEER_BLOCK
    xb = jnp.pad(x.reshape(T, D), ((0, pad), (0, 0))).reshape(-1, PEER_BLOCK, D)

    def block(xt):
        qry = (xt @ w_q).reshape(PEER_BLOCK, PEER_HEADS, 2, PEER_QDIM // 2)
        s = jnp.einsum('thcd,hckd->thck', qry, sub_keys).astype(jnp.float32)
        sc, ix = lax.top_k(s, PEER_TOPK)
        cand = (sc[:, :, 0, :, None] + sc[:, :, 1, None, :]).reshape(PEER_BLOCK, PEER_HEADS, PEER_TOPK * PEER_TOPK)
        cidx = (ix[:, :, 0, :, None] * PEER_NKEYS + ix[:, :, 1, None, :]).reshape(PEER_BLOCK, PEER_HEADS, PEER_TOPK * PEER_TOPK)
        top_s, top_p = lax.top_k(cand, PEER_TOPK)
        eidx = jnp.take_along_axis(cidx, top_p, axis=-1).reshape(PEER_BLOCK, PEER_HEADS * PEER_TOPK)
        gate = jax.nn.softmax(top_s, axis=-1).reshape(PEER_BLOCK, PEER_HEADS * PEER_TOPK)
        act = jax.nn.gelu(jnp.einsum('tnd,td->tn', u[eidx], xt).astype(jnp.float32), approximate=False)
        return jnp.einsum('tn,tnd->td', (gate * act).astype(x.dtype), v[eidx])

    y = lax.map(block, xb)
    return y.reshape(-1, D)[:T].reshape(B, L, D)


def trunk(x, meta_tokens, g_mix, w_in, attn_sinks, lb_logits, hg_norm_g, w_att_branch, w_hg_branch,
          w_out, g_ffn, w_peer_q, peer_sub_keys, peer_u, peer_v, g_final):
    B = x.shape[0]
    h = jnp.concatenate([jnp.broadcast_to(meta_tokens.astype(x.dtype)[None], (B, N_META, D_MODEL)), x], axis=1)
    L = h.shape[1]
    pos = jnp.arange(L)
    sm = jax.nn.softmax(lb_logits.astype(jnp.float32), axis=0)
    lb_all = jnp.maximum(jnp.cumsum(sm, axis=0) - sm[0:1], 0.0)
    offs = [int(o) for o in np.cumsum(IN_SIZES)[:-1]]
    for l in range(DEPTH):
        n = rmsnorm(h, g_mix[l])
        parts = jnp.split(n @ w_in[l], offs, axis=-1)
        aq, ak, av, hq, zf_f, zf_b, hi, hgg, ga, gh = parts
        aq = rope_partial(aq.reshape(B, L, N_Q_HEADS, HEAD_DIM), pos)
        ak = rope_partial(ak.reshape(B, L, N_KV_HEADS, HEAD_DIM), pos)
        av = av.reshape(B, L, N_KV_HEADS, HEAD_DIM)
        a_out = window_attention(aq, ak, av, attn_sinks[l])
        h_out = hgrn2_bidir(hq, zf_f, zf_b, hi, hgg, lb_all[l], hg_norm_g[l])
        merged = (jax.nn.sigmoid(ga) * (a_out @ w_att_branch[l])
                  + jax.nn.sigmoid(gh) * (h_out @ w_hg_branch[l]))
        h = h + merged @ w_out[l]
        h = h + peer(rmsnorm(h, g_ffn[l]), w_peer_q[l], peer_sub_keys[l], peer_u[l], peer_v[l])
    return rmsnorm(h, g_final)[:, N_META:]


def setup_inputs(seed: int = 0) -> dict:
    key = jax.random.key(seed)
    ks = jax.random.split(key, 20)
    nrm = lambda k, shape, s: jax.random.normal(k, shape, jnp.float32) * s
    return {
        "x_prompt": nrm(ks[0], (BATCH, SEQ, D_MODEL), 1.0),
        "x_sample": nrm(ks[1], (DEC_BATCH, DEC_SEQ, D_MODEL), 1.0),
        "meta_tokens": nrm(ks[2], (N_META, D_MODEL), 1.0),
        "g_mix": 1.0 + nrm(ks[3], (DEPTH, D_MODEL), 0.02),
        "w_in": nrm(ks[4], (DEPTH, D_MODEL, IN_W), D_MODEL ** -0.5),
        "attn_sinks": nrm(ks[5], (DEPTH, N_Q_HEADS), 0.5),
        "lb_logits": nrm(ks[6], (DEPTH, 2, HG_K_W), 0.5),
        "hg_norm_g": 1.0 + nrm(ks[7], (DEPTH, HG_V_W), 0.02),
        "w_att_branch": nrm(ks[8], (DEPTH, ATT_Q_W, D_MODEL), ATT_Q_W ** -0.5),
        "w_hg_branch": nrm(ks[9], (DEPTH, HG_V_W, D_MODEL), HG_V_W ** -0.5),
        "w_out": nrm(ks[10], (DEPTH, D_MODEL, D_MODEL), D_MODEL ** -0.5),
        "g_ffn": 1.0 + nrm(ks[11], (DEPTH, D_MODEL), 0.02),
        "w_peer_q": nrm(ks[12], (DEPTH, D_MODEL, PEER_HEADS * PEER_QDIM), D_MODEL ** -0.5),
        "peer_sub_keys": nrm(ks[13], (DEPTH, PEER_HEADS, 2, PEER_NKEYS, PEER_QDIM // 2), (PEER_QDIM // 2) ** -0.5),
        "peer_u": nrm(ks[14], (DEPTH, PEER_NEXPERTS, D_MODEL), D_MODEL ** -0.5),
        "peer_v": nrm(ks[15], (DEPTH, PEER_NEXPERTS, D_MODEL), PEER_HEADS ** -0.5),
        "g_final": 1.0 + nrm(ks[16], (D_MODEL,), 0.02),
    }


def reference(x_prompt, x_sample, meta_tokens, g_mix, w_in, attn_sinks, lb_logits, hg_norm_g,
              w_att_branch, w_hg_branch, w_out, g_ffn, w_peer_q, peer_sub_keys, peer_u, peer_v, g_final):
    y_prompt = trunk(x_prompt, meta_tokens, g_mix, w_in, attn_sinks, lb_logits, hg_norm_g, w_att_branch,
                     w_hg_branch, w_out, g_ffn, w_peer_q, peer_sub_keys, peer_u, peer_v, g_final)
    y_sample = trunk(x_sample, meta_tokens, g_mix, w_in, attn_sinks, lb_logits, hg_norm_g, w_att_branch,
                     w_hg_branch, w_out, g_ffn, w_peer_q, peer_sub_keys, peer_u, peer_v, g_final)
    return (y_prompt, y_sample)
```

```python
import functools

import numpy as np
import jax
import jax.numpy as jnp
from jax import lax
from jax.experimental import pallas as pl
from jax.experimental.pallas import tpu as pltpu

D_MODEL = 1024
N_META = 16
N_Q_HEADS = 16
N_KV_HEADS = 4
HEAD_DIM = 64
WINDOW = 128
ROT_DIM = HEAD_DIM // 4
ROPE_THETA = 500000.0
HG_HEADS = 8
HG_DIM = 128
HG_CHUNK = 64
PEER_HEADS = 8
PEER_NKEYS = 128
PEER_QDIM = 256
PEER_TOPK = 16
EPS = 1e-6
NEG = -1e30

BLK = 128
PAD_ROWS = BLK - N_META
IN_W = 8704
COL_AQ, COL_GA, COL_GH, COL_HQ, COL_ZF, COL_ZB, COL_HI, COL_HGG, COL_AK, COL_AV = (
    0, 1024, 2048, 3072, 4096, 5120, 6144, 7168, 8192, 8448)

VMEM_LIMIT = 56 * 1024 * 1024
PEER_TB = 16
PEER_SLOTS = PEER_HEADS * PEER_TOPK


def _cparams(sem):
    return pltpu.CompilerParams(dimension_semantics=sem, vmem_limit_bytes=VMEM_LIMIT)


def _rms(x, g):
    return x * lax.rsqrt(jnp.mean(x * x, axis=-1, keepdims=True) + EPS) * g


def _nt(a, b):
    return lax.dot_general(a, b, (((1,), (1,)), ((), ())), preferred_element_type=jnp.float32)


def _tn(a, b):
    return lax.dot_general(a, b, (((0,), (0,)), ((), ())), preferred_element_type=jnp.float32)


def _inproj_kernel(x_ref, g_ref, w_ref, o_ref, n_ref):
    @pl.when(pl.program_id(1) == 0)
    def _():
        n_ref[...] = _rms(x_ref[...], g_ref[...]).astype(jnp.bfloat16)

    o_ref[...] = jnp.dot(n_ref[...], w_ref[...], preferred_element_type=jnp.float32)


def _inproj(h2d, g, w, tm, tn):
    T = h2d.shape[0]
    return pl.pallas_call(
        _inproj_kernel,
        grid=(T // tm, IN_W // tn),
        in_specs=[pl.BlockSpec((tm, D_MODEL), lambda i, j: (i, 0)),
                  pl.BlockSpec((1, D_MODEL), lambda i, j: (0, 0)),
                  pl.BlockSpec((D_MODEL, tn), lambda i, j: (0, j))],
        out_specs=pl.BlockSpec((tm, tn), lambda i, j: (i, j)),
        out_shape=jax.ShapeDtypeStruct((T, IN_W), jnp.float32),
        scratch_shapes=[pltpu.VMEM((tm, D_MODEL), jnp.bfloat16)],
        compiler_params=_cparams(("parallel", "arbitrary")),
        name="inproj",
    )(h2d, g, w)


def _rope_tables(Lp):
    half = ROT_DIM // 2
    pos = jnp.maximum(jnp.arange(Lp) - PAD_ROWS, 0)
    inv = jnp.power(jnp.float32(ROPE_THETA), -jnp.arange(half, dtype=jnp.float32) * 2.0 / ROT_DIM)
    ang = pos.astype(jnp.float32)[:, None] * inv[None, :]
    cos, sin = jnp.cos(ang), jnp.sin(ang)
    ones = jnp.ones((Lp, HEAD_DIM - ROT_DIM), jnp.float32)
    zeros = jnp.zeros((Lp, HEAD_DIM - ROT_DIM), jnp.float32)
    zh = jnp.zeros((Lp, half), jnp.float32)
    c = jnp.concatenate([cos, cos, ones], axis=1)
    s1 = jnp.concatenate([zh, sin, zeros], axis=1)
    s2 = jnp.concatenate([-sin, zh, zeros], axis=1)
    rep = BLK // HEAD_DIM
    return tuple(jnp.tile(t, (1, rep)) for t in (c, s1, s2))


def _attn_kernel(sink_ref, q_ref, kg_ref, k0_ref, k1_ref, k2_ref, vg_ref, v0_ref, v1_ref, v2_ref,
                 c_ref, s1_ref, s2_ref, o_ref):
    r = pl.program_id(1)
    nblk = pl.num_programs(1)
    half = ROT_DIM // 2

    def rope(x, blk):
        row0 = blk * BLK if isinstance(blk, int) else pl.multiple_of(blk * BLK, BLK)
        c = c_ref[pl.ds(row0, BLK), :]
        s1 = s1_ref[pl.ds(row0, BLK), :]
        s2 = s2_ref[pl.ds(row0, BLK), :]
        outs = []
        for j in range(x.shape[1] // BLK):
            xj = x[:, j * BLK:(j + 1) * BLK]
            outs.append(xj * c + pltpu.roll(xj, half, 1) * s1 + pltpu.roll(xj, BLK - half, 1) * s2)
        return jnp.concatenate(outs, axis=1)

    kb = [jnp.clip(r - 1 + j, 0, nblk - 1) for j in range(3)]
    q = (rope(q_ref[...], r) * (HEAD_DIM ** -0.5)).astype(jnp.bfloat16)
    kcat = jnp.concatenate([rope(kg_ref[...], 0), rope(k0_ref[...], kb[0]), rope(k1_ref[...], kb[1]),
                            rope(k2_ref[...], kb[2])], axis=0).astype(jnp.bfloat16)
    vcat = jnp.concatenate([vg_ref[...], v0_ref[...], v1_ref[...], v2_ref[...]], axis=0).astype(jnp.bfloat16)

    qrow = lax.broadcasted_iota(jnp.int32, (BLK, BLK), 0)
    kcol = lax.broadcasted_iota(jnp.int32, (BLK, BLK), 1)
    masks = [kcol >= PAD_ROWS]
    for j in range(3):
        kblk = r - 1 + j
        ok_blk = (kblk >= 1) & (kblk <= nblk - 1)
        masks.append((jnp.abs((j - 1) * BLK + kcol - qrow) <= WINDOW) & ok_blk)
    mask = jnp.concatenate(masks, axis=1)

    G = N_Q_HEADS // N_KV_HEADS
    outs = []
    for h in range(N_KV_HEADS):
        kh = kcat[:, h * HEAD_DIM:(h + 1) * HEAD_DIM]
        vh = vcat[:, h * HEAD_DIM:(h + 1) * HEAD_DIM]
        for g in range(G):
            qi = h * G + g
            s = _nt(q[:, qi * HEAD_DIM:(qi + 1) * HEAD_DIM], kh)
            s = jnp.where(mask, s, NEG)
            sk = sink_ref[qi]
            m = jnp.maximum(jnp.max(s, axis=-1, keepdims=True), sk)
            p = jnp.exp(s - m)
            p = p / (jnp.sum(p, axis=-1, keepdims=True) + jnp.exp(sk - m))
            outs.append(jnp.dot(p.astype(jnp.bfloat16), vh, preferred_element_type=jnp.float32))
    o_ref[...] = jnp.concatenate(outs, axis=1).astype(o_ref.dtype)


def _attention(proj3, sinks, tables):
    NB, Lp, _ = proj3.shape
    nblk = Lp // BLK
    kcol, vcol = COL_AK // 256, COL_AV // 256

    def loc(j, col):
        return pl.BlockSpec((None, BLK, 256),
                            lambda b, r: (b, jnp.clip(r - 1 + j, 0, nblk - 1), col))

    glob = lambda col: pl.BlockSpec((None, BLK, 256), lambda b, r: (b, 0, col))
    tab = pl.BlockSpec((Lp, BLK), lambda b, r: (0, 0))
    return pl.pallas_call(
        _attn_kernel,
        grid=(NB, nblk),
        in_specs=[pl.BlockSpec(memory_space=pltpu.SMEM),
                  pl.BlockSpec((None, BLK, 1024), lambda b, r: (b, r, 0)),
                  glob(kcol), loc(0, kcol), loc(1, kcol), loc(2, kcol),
                  glob(vcol), loc(0, vcol), loc(1, vcol), loc(2, vcol),
                  tab, tab, tab],
        out_specs=pl.BlockSpec((None, BLK, 1024), lambda b, r: (b, r, 0)),
        out_shape=jax.ShapeDtypeStruct((NB, Lp, 1024), jnp.bfloat16),
        compiler_params=_cparams(("parallel", "arbitrary")),
        name="window_attn",
    )(sinks, proj3, proj3, proj3, proj3, proj3, proj3, proj3, proj3, proj3, *tables)


_HG_LEVELS = (32, 16, 8, 4, 2, 1)


def _hg_consts(reverse):
    C = HG_CHUNK
    t = np.arange(C)[:, None]
    u = np.arange(C)[None, :]
    mats = [u <= t, u > t]
    ups, msks = [], []
    for B in _HG_LEVELS:
        same = (t // (2 * B)) == (u // (2 * B))
        tin, uin = t % (2 * B), u % (2 * B)
        upper = tin >= B
        wq = upper & same & (uin >= B) & (uin <= tin)
        wk = (~upper) & same & (uin > tin) & (uin <= B - 1)
        mats.append(wq | wk)
        ups.append(np.broadcast_to(upper, (C, HG_DIM)))
        msks.append(same & upper & (uin < B))
    msks.append(np.eye(C, dtype=bool))
    if reverse:
        mats = [m[::-1, ::-1] for m in mats]
        ups = [m[::-1] for m in ups]
        msks = [m[::-1, ::-1] for m in msks]
    W = np.concatenate(mats, axis=0).astype(np.float32)
    W3 = np.concatenate([W, W, W], axis=1)
    return (jnp.asarray(W3, jnp.bfloat16), jnp.asarray(np.stack(ups).astype(np.float32)),
            jnp.asarray(np.stack(msks).astype(np.float32)))


def _hg_chunk(q_raw, z, v, valid, lb, w3, up_ref, msk_ref, st_ref, reverse):
    C = HG_CHUNK
    f32, bf16 = jnp.float32, jnp.bfloat16
    log_lb = jnp.log(lb)
    log1m = jnp.log1p(-lb)
    logsig = jnp.minimum(z, 0.0) - jnp.log1p(jnp.exp(-jnp.abs(z)))
    a, b = log_lb, log1m + logsig
    g = jnp.maximum(a, b) + jnp.log1p(jnp.exp(-jnp.abs(a - b)))
    k = (1.0 - lb) * jax.nn.sigmoid(-z)
    q = q_raw * jax.nn.sigmoid(q_raw)
    if valid is not None:
        q = jnp.where(valid, q, 0.0)
        k = jnp.where(valid, k, 0.0)
        v = jnp.where(valid, v, 0.0)
        g = jnp.where(valid, g, 0.0)
    g1 = g.astype(bf16)
    r1 = g - g1.astype(f32)
    g2 = r1.astype(bf16)
    g3 = (r1 - g2.astype(f32)).astype(bf16)
    E = jnp.dot(w3, jnp.concatenate([g1, g2, g3], axis=0), preferred_element_type=f32)
    b_inc = E[0:C]
    b_sfx = E[C:2 * C]
    b_tot = b_inc[0:1] if reverse else b_inc[C - 1:C]

    A = msk_ref[len(_HG_LEVELS)] * jnp.sum(q * k, axis=-1, keepdims=True)
    for l in range(len(_HG_LEVELS)):
        ex = jnp.exp(E[(2 + l) * C:(3 + l) * C])
        up = up_ref[l]
        qs = (q * ex * up).astype(bf16)
        ks = (k * ex * (1.0 - up)).astype(bf16)
        A = A + _nt(qs, ks) * msk_ref[l]

    st = st_ref[...]
    vb = v.astype(bf16)
    o = _nt((q * jnp.exp(b_inc)).astype(bf16), st.astype(bf16))
    o = o + jnp.dot(A.astype(bf16), vb, preferred_element_type=f32)
    st_ref[...] = st * jnp.exp(b_tot) + _tn(vb, (k * jnp.exp(b_sfx)).astype(bf16))
    return o


def _hg_kernel(*refs, reverse, final):
    if final:
        (q_ref, z_ref, v_ref, lb_ref, w3_ref, up_ref, msk_ref, of_ref, gate_ref, ng_ref,
         o_ref, st_ref) = refs
    else:
        q_ref, z_ref, v_ref, lb_ref, w3_ref, up_ref, msk_ref, o_ref, st_ref = refs
    i = pl.program_id(2)
    nblk = pl.num_programs(2)
    blk = (nblk - 1 - i) if reverse else i

    @pl.when(i == 0)
    def _():
        st_ref[...] = jnp.zeros_like(st_ref)

    lb = lb_ref[...]
    w3 = w3_ref[...]
    C = HG_CHUNK
    order = (1, 0) if reverse else (0, 1)
    outs = [None, None]
    for c in order:
        rows = lax.broadcasted_iota(jnp.int32, (C, HG_DIM), 0) + c * C
        valid = (rows >= PAD_ROWS) | (blk > 0)
        sl = slice(c * C, (c + 1) * C)
        outs[c] = _hg_chunk(q_ref[sl, :], z_ref[sl, :], v_ref[sl, :], valid, lb, w3,
                            up_ref, msk_ref, st_ref, reverse)
    o = jnp.concatenate(outs, axis=0)
    if final:
        o = o + of_ref[...]
        gt = gate_ref[...]
        o_ref[...] = (_rms(o, ng_ref[...]) * (gt * jax.nn.sigmoid(gt))).astype(o_ref.dtype)
    else:
        o_ref[...] = o


def _hgrn(proj3, lb3, consts, reverse, extra=None):
    NB, Lp, _ = proj3.shape
    nblk = Lp // BLK
    final = extra is not None
    d = 1 if reverse else 0

    def rowblk(i):
        return (nblk - 1 - i) if reverse else i

    def col(base):
        return pl.BlockSpec((None, BLK, HG_DIM), lambda b, h, i: (b, rowblk(i), base // HG_DIM + h))

    w3, up, msk = consts
    in_specs = [col(COL_HQ), col(COL_ZB if reverse else COL_ZF), col(COL_HI),
                pl.BlockSpec((None, 1, HG_DIM), lambda b, h, i: (d, 0, h)),
                pl.BlockSpec(w3.shape, lambda b, h, i: (0, 0)),
                pl.BlockSpec(up.shape, lambda b, h, i: (0, 0, 0)),
                pl.BlockSpec(msk.shape, lambda b, h, i: (0, 0, 0))]
    args = [proj3, proj3, proj3, lb3, w3, up, msk]
    if final:
        o_fwd, norm_g = extra
        in_specs += [pl.BlockSpec((None, BLK, HG_DIM), lambda b, h, i: (b, rowblk(i), h)),
                     col(COL_HGG),
                     pl.BlockSpec((1, HG_DIM), lambda b, h, i: (0, h))]
        args += [o_fwd, proj3, norm_g]
    return pl.pallas_call(
        functools.partial(_hg_kernel, reverse=reverse, final=final),
        grid=(NB, HG_HEADS, nblk),
        in_specs=in_specs,
        out_specs=pl.BlockSpec((None, BLK, HG_DIM), lambda b, h, i: (b, rowblk(i), h)),
        out_shape=jax.ShapeDtypeStruct((NB, Lp, 1024), jnp.bfloat16 if final else jnp.float32),
        scratch_shapes=[pltpu.VMEM((HG_DIM, HG_DIM), jnp.float32)],
        compiler_params=_cparams(("parallel", "parallel", "arbitrary")),
        name="hgrn_bwd" if reverse else "hgrn_fwd",
    )(*args)


def _merge_kernel(a_ref, hh_ref, ga_ref, gh_ref, h_ref, wa_ref, wh_ref, wo_ref, o_ref):
    f32 = jnp.float32
    pa = jnp.dot(a_ref[...], wa_ref[...], preferred_element_type=f32)
    ph = jnp.dot(hh_ref[...], wh_ref[...], preferred_element_type=f32)
    merged = jax.nn.sigmoid(ga_ref[...]) * pa + jax.nn.sigmoid(gh_ref[...]) * ph
    o_ref[...] = h_ref[...] + jnp.dot(merged.astype(jnp.bfloat16), wo_ref[...], preferred_element_type=f32)


def _merge(a2d, hh2d, proj2d, h2d, wa, wh, wo, tm):
    T = h2d.shape[0]
    row = lambda c: pl.BlockSpec((tm, D_MODEL), lambda i: (i, c))
    wspec = pl.BlockSpec((D_MODEL, D_MODEL), lambda i: (0, 0))
    return pl.pallas_call(
        _merge_kernel,
        grid=(T // tm,),
        in_specs=[row(0), row(0), row(COL_GA // D_MODEL), row(COL_GH // D_MODEL), row(0), wspec, wspec, wspec],
        out_specs=row(0),
        out_shape=jax.ShapeDtypeStruct((T, D_MODEL), jnp.float32),
        input_output_aliases={4: 0},
        compiler_params=_cparams(("parallel",)),
        name="merge_out",
    )(a2d, hh2d, proj2d, proj2d, h2d, wa, wh, wo)


def _topk_axis0(s, kk, payload=None):
    n = s.shape[0]
    iota = lax.broadcasted_iota(jnp.int32, s.shape, 0)
    vals, ids = [], []
    for _ in range(kk):
        m = jnp.max(s, axis=0, keepdims=True)
        am = jnp.min(jnp.where(s == m, iota, n), axis=0, keepdims=True)
        hit = iota == am
        s = jnp.where(hit, -jnp.inf, s)
        vals.append(m)
        ids.append(am if payload is None else jnp.max(jnp.where(hit, payload, -1), axis=0, keepdims=True))
    return vals, ids


def _stack_rows(rows, dtype):
    n = len(rows)
    iota = lax.broadcasted_iota(jnp.int32, (n, rows[0].shape[1]), 0)
    out = jnp.zeros((n, rows[0].shape[1]), dtype)
    for j, rj in enumerate(rows):
        out = jnp.where(iota == j, rj, out)
    return out


def _route_kernel(h_ref, g_ref, wq_ref, key_ref, eidx_ref, gate_ref, xn_ref):
    @pl.when(pl.program_id(1) == 0)
    def _():
        xn_ref[...] = _rms(h_ref[...], g_ref[...]).astype(jnp.bfloat16)

    K = PEER_TOPK
    half = PEER_QDIM // 2
    qry = jnp.dot(xn_ref[...], wq_ref[...], preferred_element_type=jnp.float32).astype(jnp.bfloat16)
    s0 = _nt(key_ref[0], qry[:, :half])
    s1 = _nt(key_ref[1], qry[:, half:])
    v0, i0 = _topk_axis0(s0, K)
    v1, i1 = _topk_axis0(s1, K)
    m1 = _stack_rows(v1, jnp.float32)
    a1 = _stack_rows(i1, jnp.int32)
    cand = jnp.concatenate([v0[i] + m1 for i in range(K)], axis=0)
    cidx = jnp.concatenate([i0[i] * PEER_NKEYS + a1 for i in range(K)], axis=0)
    ts, te = _topk_axis0(cand, K, payload=cidx)
    ex = [jnp.exp(t - ts[0]) for t in ts]
    den = ex[0]
    for e in ex[1:]:
        den = den + e
    eidx_ref[...] = _stack_rows(te, jnp.int32)
    gate_ref[...] = _stack_rows([e / den for e in ex], jnp.float32)


def _route(h2d, g, wq, keys, tm):
    T = h2d.shape[0]
    out = jax.ShapeDtypeStruct((PEER_SLOTS, T), jnp.int32), jax.ShapeDtypeStruct((PEER_SLOTS, T), jnp.float32)
    ospec = pl.BlockSpec((PEER_TOPK, tm), lambda i, h: (h, i))
    return pl.pallas_call(
        _route_kernel,
        grid=(T // tm, PEER_HEADS),
        in_specs=[pl.BlockSpec((tm, D_MODEL), lambda i, h: (i, 0)),
                  pl.BlockSpec((1, D_MODEL), lambda i, h: (0, 0)),
                  pl.BlockSpec((D_MODEL, PEER_QDIM), lambda i, h: (0, h)),
                  pl.BlockSpec((2, PEER_NKEYS, PEER_QDIM // 2), lambda i, h: (h, 0, 0))],
        out_specs=(ospec, ospec),
        out_shape=out,
        scratch_shapes=[pltpu.VMEM((tm, D_MODEL), jnp.bfloat16)],
        compiler_params=_cparams(("parallel", "arbitrary")),
        name="peer_route",
    )(h2d, g, wq, keys)


def _expert_kernel(idx_ref, idxn_ref, h_ref, g_ref, gate_ref, u_hbm, v_hbm, o_ref, ubuf, vbuf, sem):
    TB, NS = PEER_TB, PEER_SLOTS
    i = pl.program_id(0)
    n = pl.num_programs(0)
    slot = lax.rem(i, 2)

    def row_copy(tab, buf, e, sl, row, s):
        return pltpu.make_async_copy(tab.at[pl.ds(e, 1)], buf.at[sl, pl.ds(row, 1)], sem.at[s, sl])

    def issue(iref, sl):
        for t in range(TB):
            def body(gi, carry):
                for j in range(8):
                    nn = gi * 8 + j
                    e = iref[t, nn]
                    row_copy(u_hbm, ubuf, e, sl, t * NS + nn, 0).start()
                    row_copy(v_hbm, vbuf, e, sl, t * NS + nn, 1).start()
                return carry
            lax.fori_loop(0, NS // 8, body, 0)

    @pl.when(i == 0)
    def _():
        issue(idx_ref, 0)

    @pl.when(i + 1 < n)
    def _():
        issue(idxn_ref, 1 - slot)

    pltpu.make_async_copy(u_hbm.at[pl.ds(0, TB * NS)], ubuf.at[slot], sem.at[0, slot]).wait()
    pltpu.make_async_copy(v_hbm.at[pl.ds(0, TB * NS)], vbuf.at[slot], sem.at[1, slot]).wait()

    f32, bf16 = jnp.float32, jnp.bfloat16
    x = h_ref[...]
    xn = _rms(x, g_ref[...]).astype(bf16)
    tok = lax.broadcasted_iota(jnp.int32, (TB, NS), 0)
    act = jnp.zeros((TB, NS), f32)
    for t in range(TB):
        ut = ubuf[slot, t * NS:(t + 1) * NS, :].astype(bf16)
        act = jnp.where(tok == t, _nt(xn, ut), act)
    gelu = 0.5 * act * (1.0 + lax.erf(act * (2.0 ** -0.5)))
    w = gate_ref[...] * gelu
    y = jnp.zeros((TB, D_MODEL), f32)
    for t in range(TB):
        vt = vbuf[slot, t * NS:(t + 1) * NS, :].astype(bf16)
        wt = jnp.where(tok == t, w, 0.0).astype(bf16)
        y = y + jnp.dot(wt, vt, preferred_element_type=f32)
    o_ref[...] = x + y


def _experts(h2d, g, eidx, gate, u, v):
    T = h2d.shape[0]
    TB, NS = PEER_TB, PEER_SLOTS
    nsteps = T // TB
    return pl.pallas_call(
        _expert_kernel,
        grid=(nsteps,),
        in_specs=[pl.BlockSpec((TB, NS), lambda i: (i, 0), memory_space=pltpu.SMEM),
                  pl.BlockSpec((TB, NS), lambda i: (jnp.minimum(i + 1, nsteps - 1), 0), memory_space=pltpu.SMEM),
                  pl.BlockSpec((TB, D_MODEL), lambda i: (i, 0)),
                  pl.BlockSpec((1, D_MODEL), lambda i: (0, 0)),
                  pl.BlockSpec((TB, NS), lambda i: (i, 0)),
                  pl.BlockSpec(memory_space=pl.ANY),
                  pl.BlockSpec(memory_space=pl.ANY)],
        out_specs=pl.BlockSpec((TB, D_MODEL), lambda i: (i, 0)),
        out_shape=jax.ShapeDtypeStruct((T, D_MODEL), jnp.float32),
        scratch_shapes=[pltpu.VMEM((2, TB * NS, D_MODEL), jnp.float32),
                        pltpu.VMEM((2, TB * NS, D_MODEL), jnp.float32),
                        pltpu.SemaphoreType.DMA((2, 2))],
        input_output_aliases={2: 0},
        compiler_params=_cparams(("arbitrary",)),
        name="peer_experts",
    )(eidx, eidx, h2d, g, gate, u, v)


def _final_kernel(h_ref, g_ref, o_ref):
    o_ref[...] = _rms(h_ref[...], g_ref[...])


def _final(h3, g, b0, nb):
    _, Lp, _ = h3.shape
    S = Lp - BLK
    return pl.pallas_call(
        _final_kernel,
        grid=(nb, S // BLK),
        in_specs=[pl.BlockSpec((None, BLK, D_MODEL), lambda b, r: (b + b0, r + 1, 0)),
                  pl.BlockSpec((1, D_MODEL), lambda b, r: (0, 0))],
        out_specs=pl.BlockSpec((None, BLK, D_MODEL), lambda b, r: (b, r, 0)),
        out_shape=jax.ShapeDtypeStruct((nb, S, D_MODEL), jnp.float32),
        compiler_params=_cparams(("parallel", "parallel")),
        name="final_norm",
    )(h3, g)


def _pick_tile(T, candidates):
    for c in candidates:
        if T % c == 0:
            return c
    raise ValueError(f"no tile for {T}")


def kernel(x_prompt, x_sample, meta_tokens, g_mix, w_in, attn_sinks, lb_logits, hg_norm_g, w_att_branch,
           w_hg_branch, w_out, g_ffn, w_peer_q, peer_sub_keys, peer_u, peer_v, g_final):
    f32, bf16 = jnp.float32, jnp.bfloat16
    depth = w_in.shape[0]
    nb_p, S, _ = x_prompt.shape
    nb_s = x_sample.shape[0]
    assert x_sample.shape[1] == S and S % BLK == 0
    NB = nb_p + nb_s
    Lp = S + BLK
    T = NB * Lp

    x = jnp.concatenate([x_prompt, x_sample], axis=0)
    head = jnp.concatenate([jnp.zeros((PAD_ROWS, D_MODEL), f32), meta_tokens.astype(f32)], axis=0)
    h = jnp.concatenate([jnp.broadcast_to(head[None], (NB, BLK, D_MODEL)), x], axis=1).reshape(T, D_MODEL)

    sizes = (1024, 256, 256, 1024, 1024, 1024, 1024, 1024, 1024, 1024)
    offs = np.concatenate([[0], np.cumsum(sizes)])
    order = (0, 8, 9, 3, 4, 5, 6, 7, 1, 2)
    perm = np.concatenate([np.arange(offs[k], offs[k + 1]) for k in order])

    sm = jax.nn.softmax(lb_logits.astype(f32), axis=0)
    lb_all = jnp.maximum(jnp.cumsum(sm, axis=0) - sm[0:1], 0.0)

    tables = _rope_tables(Lp)
    consts_f = _hg_consts(False)
    consts_b = _hg_consts(True)
    tm_big = _pick_tile(T, (512, 384, 256, 128))
    tm_route = _pick_tile(T, (256, 128))

    for l in range(depth):
        w_l = w_in[l][:, perm].astype(bf16)
        proj = _inproj(h, g_mix[l].reshape(1, -1), w_l, tm_big, IN_W // 4)
        proj3 = proj.reshape(NB, Lp, IN_W)
        a_out = _attention(proj3, attn_sinks[l].astype(f32), tables)
        lb3 = lb_all[l].reshape(2, 1, -1)
        o_fwd = _hgrn(proj3, lb3, consts_f, reverse=False)
        h_out = _hgrn(proj3, lb3, consts_b, reverse=True, extra=(o_fwd, hg_norm_g[l].reshape(1, -1)))
        h = _merge(a_out.reshape(T, -1), h_out.reshape(T, -1), proj, h,
                   w_att_branch[l].astype(bf16), w_hg_branch[l].astype(bf16), w_out[l].astype(bf16), tm_big)
        g_f = g_ffn[l].reshape(1, -1)
        keys = peer_sub_keys[l].reshape(2 * PEER_HEADS, PEER_NKEYS, PEER_QDIM // 2).astype(bf16)
        eidx_t, gate_t = _route(h, g_f, w_peer_q[l].astype(bf16), keys, tm_route)
        h = _experts(h, g_f, eidx_t.T, gate_t.T, peer_u[l], peer_v[l])

    h3 = h.reshape(NB, Lp, D_MODEL)
    g_fin = g_final.reshape(1, -1)
    return _final(h3, g_fin, 0, nb_p), _final(h3, g_fin, nb_p, nb_s)
```

```python
import functools

import numpy as np
import jax
import jax.numpy as jnp
from jax import lax
from jax.experimental import pallas as pl
from jax.experimental.pallas import tpu as pltpu

D_MODEL = 1024
N_META = 16
N_Q_HEADS = 16
N_KV_HEADS = 4
HEAD_DIM = 64
WINDOW = 128
ROT_DIM = HEAD_DIM // 4
ROPE_THETA = 500000.0
HG_HEADS = 8
HG_DIM = 128
HG_CHUNK = 64
PEER_HEADS = 8
PEER_NKEYS = 128
PEER_QDIM = 256
PEER_TOPK = 16
EPS = 1e-6
NEG = -1e30

BLK = 128
PAD_ROWS = BLK - N_META
IN_W = 8704
COL_AQ, COL_GA, COL_GH, COL_HQ, COL_ZF, COL_ZB, COL_HI, COL_HGG, COL_AK, COL_AV = (
    0, 1024, 2048, 3072, 4096, 5120, 6144, 7168, 8192, 8448)

VMEM_LIMIT = 56 * 1024 * 1024
PEER_TB = 16
PEER_SLOTS = PEER_HEADS * PEER_TOPK
PEER_UNROLL = 2
SUBLANES = 8
UV_CHUNKS = 2 * D_MODEL // 128


def _cparams(sem):
    return pltpu.CompilerParams(dimension_semantics=sem, vmem_limit_bytes=VMEM_LIMIT)


def _rms(x, g):
    return x * lax.rsqrt(jnp.mean(x * x, axis=-1, keepdims=True) + EPS) * g


def _nt(a, b):
    return lax.dot_general(a, b, (((1,), (1,)), ((), ())), preferred_element_type=jnp.float32)


def _tn(a, b):
    return lax.dot_general(a, b, (((0,), (0,)), ((), ())), preferred_element_type=jnp.float32)


def _inproj_kernel(x_ref, g_ref, w_ref, o_ref, n_ref):
    @pl.when(pl.program_id(1) == 0)
    def _():
        n_ref[...] = _rms(x_ref[...], g_ref[...]).astype(jnp.bfloat16)

    o_ref[...] = jnp.dot(n_ref[...], w_ref[...], preferred_element_type=jnp.float32)


def _inproj(h2d, g, w, tm, tn):
    T = h2d.shape[0]
    return pl.pallas_call(
        _inproj_kernel,
        grid=(T // tm, IN_W // tn),
        in_specs=[pl.BlockSpec((tm, D_MODEL), lambda i, j: (i, 0)),
                  pl.BlockSpec((1, D_MODEL), lambda i, j: (0, 0)),
                  pl.BlockSpec((D_MODEL, tn), lambda i, j: (0, j))],
        out_specs=pl.BlockSpec((tm, tn), lambda i, j: (i, j)),
        out_shape=jax.ShapeDtypeStruct((T, IN_W), jnp.float32),
        scratch_shapes=[pltpu.VMEM((tm, D_MODEL), jnp.bfloat16)],
        compiler_params=_cparams(("parallel", "arbitrary")),
        name="inproj",
    )(h2d, g, w)


def _rope_tables(Lp):
    half = ROT_DIM // 2
    pos = jnp.maximum(jnp.arange(Lp) - PAD_ROWS, 0)
    inv = jnp.power(jnp.float32(ROPE_THETA), -jnp.arange(half, dtype=jnp.float32) * 2.0 / ROT_DIM)
    ang = pos.astype(jnp.float32)[:, None] * inv[None, :]
    cos, sin = jnp.cos(ang), jnp.sin(ang)
    ones = jnp.ones((Lp, HEAD_DIM - ROT_DIM), jnp.float32)
    zeros = jnp.zeros((Lp, HEAD_DIM - ROT_DIM), jnp.float32)
    zh = jnp.zeros((Lp, half), jnp.float32)
    c = jnp.concatenate([cos, cos, ones], axis=1)
    s1 = jnp.concatenate([zh, sin, zeros], axis=1)
    s2 = jnp.concatenate([-sin, zh, zeros], axis=1)
    rep = BLK // HEAD_DIM
    return tuple(jnp.tile(t, (1, rep)) for t in (c, s1, s2))


def _attn_kernel(sink_ref, q_ref, kg_ref, k0_ref, k1_ref, k2_ref, vg_ref, v0_ref, v1_ref, v2_ref,
                 c_ref, s1_ref, s2_ref, o_ref):
    r = pl.program_id(1)
    nblk = pl.num_programs(1)
    half = ROT_DIM // 2

    def rope(x, blk):
        row0 = blk * BLK if isinstance(blk, int) else pl.multiple_of(blk * BLK, BLK)
        c = c_ref[pl.ds(row0, BLK), :]
        s1 = s1_ref[pl.ds(row0, BLK), :]
        s2 = s2_ref[pl.ds(row0, BLK), :]
        outs = []
        for j in range(x.shape[1] // BLK):
            xj = x[:, j * BLK:(j + 1) * BLK]
            outs.append(xj * c + pltpu.roll(xj, half, 1) * s1 + pltpu.roll(xj, BLK - half, 1) * s2)
        return jnp.concatenate(outs, axis=1)

    kb = [jnp.clip(r - 1 + j, 0, nblk - 1) for j in range(3)]
    q = (rope(q_ref[...], r) * (HEAD_DIM ** -0.5)).astype(jnp.bfloat16)
    kcat = jnp.concatenate([rope(kg_ref[...], 0), rope(k0_ref[...], kb[0]), rope(k1_ref[...], kb[1]),
                            rope(k2_ref[...], kb[2])], axis=0).astype(jnp.bfloat16)
    vcat = jnp.concatenate([vg_ref[...], v0_ref[...], v1_ref[...], v2_ref[...]], axis=0).astype(jnp.bfloat16)

    qrow = lax.broadcasted_iota(jnp.int32, (BLK, BLK), 0)
    kcol = lax.broadcasted_iota(jnp.int32, (BLK, BLK), 1)
    masks = [kcol >= PAD_ROWS]
    for j in range(3):
        kblk = r - 1 + j
        ok_blk = (kblk >= 1) & (kblk <= nblk - 1)
        masks.append((jnp.abs((j - 1) * BLK + kcol - qrow) <= WINDOW) & ok_blk)
    mask = jnp.concatenate(masks, axis=1)

    G = N_Q_HEADS // N_KV_HEADS
    outs = []
    for h in range(N_KV_HEADS):
        kh = kcat[:, h * HEAD_DIM:(h + 1) * HEAD_DIM]
        vh = vcat[:, h * HEAD_DIM:(h + 1) * HEAD_DIM]
        for g in range(G):
            qi = h * G + g
            s = _nt(q[:, qi * HEAD_DIM:(qi + 1) * HEAD_DIM], kh)
            s = jnp.where(mask, s, NEG)
            sk = sink_ref[qi]
            m = jnp.maximum(jnp.max(s, axis=-1, keepdims=True), sk)
            p = jnp.exp(s - m)
            p = p / (jnp.sum(p, axis=-1, keepdims=True) + jnp.exp(sk - m))
            outs.append(jnp.dot(p.astype(jnp.bfloat16), vh, preferred_element_type=jnp.float32))
    o_ref[...] = jnp.concatenate(outs, axis=1).astype(o_ref.dtype)


def _attention(proj3, sinks, tables):
    NB, Lp, _ = proj3.shape
    nblk = Lp // BLK
    kcol, vcol = COL_AK // 256, COL_AV // 256

    def loc(j, col):
        return pl.BlockSpec((None, BLK, 256),
                            lambda b, r: (b, jnp.clip(r - 1 + j, 0, nblk - 1), col))

    glob = lambda col: pl.BlockSpec((None, BLK, 256), lambda b, r: (b, 0, col))
    tab = pl.BlockSpec((Lp, BLK), lambda b, r: (0, 0))
    return pl.pallas_call(
        _attn_kernel,
        grid=(NB, nblk),
        in_specs=[pl.BlockSpec(memory_space=pltpu.SMEM),
                  pl.BlockSpec((None, BLK, 1024), lambda b, r: (b, r, 0)),
                  glob(kcol), loc(0, kcol), loc(1, kcol), loc(2, kcol),
                  glob(vcol), loc(0, vcol), loc(1, vcol), loc(2, vcol),
                  tab, tab, tab],
        out_specs=pl.BlockSpec((None, BLK, 1024), lambda b, r: (b, r, 0)),
        out_shape=jax.ShapeDtypeStruct((NB, Lp, 1024), jnp.bfloat16),
        compiler_params=_cparams(("parallel", "arbitrary")),
        name="window_attn",
    )(sinks, proj3, proj3, proj3, proj3, proj3, proj3, proj3, proj3, proj3, *tables)


_HG_LEVELS = (32, 16, 8, 4, 2, 1)


def _hg_consts(reverse):
    C = HG_CHUNK
    t = np.arange(C)[:, None]
    u = np.arange(C)[None, :]
    mats = [u <= t, u > t]
    ups, msks = [], []
    for B in _HG_LEVELS:
        same = (t // (2 * B)) == (u // (2 * B))
        tin, uin = t % (2 * B), u % (2 * B)
        upper = tin >= B
        wq = upper & same & (uin >= B) & (uin <= tin)
        wk = (~upper) & same & (uin > tin) & (uin <= B - 1)
        mats.append(wq | wk)
        ups.append(np.broadcast_to(upper, (C, HG_DIM)))
        msks.append(same & upper & (uin < B))
    msks.append(np.eye(C, dtype=bool))
    if reverse:
        mats = [m[::-1, ::-1] for m in mats]
        ups = [m[::-1] for m in ups]
        msks = [m[::-1, ::-1] for m in msks]
    W = np.concatenate(mats, axis=0).astype(np.float32)
    W3 = np.concatenate([W, W, W], axis=1)
    return (jnp.asarray(W3, jnp.bfloat16), jnp.asarray(np.stack(ups).astype(np.float32)),
            jnp.asarray(np.stack(msks).astype(np.float32)))


def _hg_chunk(q_raw, z, v, valid, lb, w3, up_ref, msk_ref, st_ref, reverse):
    C = HG_CHUNK
    f32, bf16 = jnp.float32, jnp.bfloat16
    log_lb = jnp.log(lb)
    log1m = jnp.log1p(-lb)
    logsig = jnp.minimum(z, 0.0) - jnp.log1p(jnp.exp(-jnp.abs(z)))
    a, b = log_lb, log1m + logsig
    g = jnp.maximum(a, b) + jnp.log1p(jnp.exp(-jnp.abs(a - b)))
    k = (1.0 - lb) * jax.nn.sigmoid(-z)
    q = q_raw * jax.nn.sigmoid(q_raw)
    if valid is not None:
        q = jnp.where(valid, q, 0.0)
        k = jnp.where(valid, k, 0.0)
        v = jnp.where(valid, v, 0.0)
        g = jnp.where(valid, g, 0.0)
    g1 = g.astype(bf16)
    r1 = g - g1.astype(f32)
    g2 = r1.astype(bf16)
    g3 = (r1 - g2.astype(f32)).astype(bf16)
    E = jnp.dot(w3, jnp.concatenate([g1, g2, g3], axis=0), preferred_element_type=f32)
    b_inc = E[0:C]
    b_sfx = E[C:2 * C]
    b_tot = b_inc[0:1] if reverse else b_inc[C - 1:C]

    A = msk_ref[len(_HG_LEVELS)] * jnp.sum(q * k, axis=-1, keepdims=True)
    for l in range(len(_HG_LEVELS)):
        ex = jnp.exp(E[(2 + l) * C:(3 + l) * C])
        up = up_ref[l]
        qs = (q * ex * up).astype(bf16)
        ks = (k * ex * (1.0 - up)).astype(bf16)
        A = A + _nt(qs, ks) * msk_ref[l]

    st = st_ref[...]
    vb = v.astype(bf16)
    o = _nt((q * jnp.exp(b_inc)).astype(bf16), st.astype(bf16))
    o = o + jnp.dot(A.astype(bf16), vb, preferred_element_type=f32)
    st_ref[...] = st * jnp.exp(b_tot) + _tn(vb, (k * jnp.exp(b_sfx)).astype(bf16))
    return o


def _hg_kernel(*refs, reverse, final):
    if final:
        (q_ref, z_ref, v_ref, lb_ref, w3_ref, up_ref, msk_ref, of_ref, gate_ref, ng_ref,
         o_ref, st_ref) = refs
    else:
        q_ref, z_ref, v_ref, lb_ref, w3_ref, up_ref, msk_ref, o_ref, st_ref = refs
    i = pl.program_id(2)
    nblk = pl.num_programs(2)
    blk = (nblk - 1 - i) if reverse else i

    @pl.when(i == 0)
    def _():
        st_ref[...] = jnp.zeros_like(st_ref)

    lb = lb_ref[...]
    w3 = w3_ref[...]
    C = HG_CHUNK
    order = (1, 0) if reverse else (0, 1)
    outs = [None, None]
    for c in order:
        rows = lax.broadcasted_iota(jnp.int32, (C, HG_DIM), 0) + c * C
        valid = (rows >= PAD_ROWS) | (blk > 0)
        sl = slice(c * C, (c + 1) * C)
        outs[c] = _hg_chunk(q_ref[sl, :], z_ref[sl, :], v_ref[sl, :], valid, lb, w3,
                            up_ref, msk_ref, st_ref, reverse)
    o = jnp.concatenate(outs, axis=0)
    if final:
        o = o + of_ref[...]
        gt = gate_ref[...]
        o_ref[...] = (_rms(o, ng_ref[...]) * (gt * jax.nn.sigmoid(gt))).astype(o_ref.dtype)
    else:
        o_ref[...] = o


def _hgrn(proj3, lb3, consts, reverse, extra=None):
    NB, Lp, _ = proj3.shape
    nblk = Lp // BLK
    final = extra is not None
    d = 1 if reverse else 0

    def rowblk(i):
        return (nblk - 1 - i) if reverse else i

    def col(base):
        return pl.BlockSpec((None, BLK, HG_DIM), lambda b, h, i: (b, rowblk(i), base // HG_DIM + h))

    w3, up, msk = consts
    in_specs = [col(COL_HQ), col(COL_ZB if reverse else COL_ZF), col(COL_HI),
                pl.BlockSpec((None, 1, HG_DIM), lambda b, h, i: (d, 0, h)),
                pl.BlockSpec(w3.shape, lambda b, h, i: (0, 0)),
                pl.BlockSpec(up.shape, lambda b, h, i: (0, 0, 0)),
                pl.BlockSpec(msk.shape, lambda b, h, i: (0, 0, 0))]
    args = [proj3, proj3, proj3, lb3, w3, up, msk]
    if final:
        o_fwd, norm_g = extra
        in_specs += [pl.BlockSpec((None, BLK, HG_DIM), lambda b, h, i: (b, rowblk(i), h)),
                     col(COL_HGG),
                     pl.BlockSpec((1, HG_DIM), lambda b, h, i: (0, h))]
        args += [o_fwd, proj3, norm_g]
    return pl.pallas_call(
        functools.partial(_hg_kernel, reverse=reverse, final=final),
        grid=(NB, HG_HEADS, nblk),
        in_specs=in_specs,
        out_specs=pl.BlockSpec((None, BLK, HG_DIM), lambda b, h, i: (b, rowblk(i), h)),
        out_shape=jax.ShapeDtypeStruct((NB, Lp, 1024), jnp.bfloat16 if final else jnp.float32),
        scratch_shapes=[pltpu.VMEM((HG_DIM, HG_DIM), jnp.float32)],
        compiler_params=_cparams(("parallel", "parallel", "arbitrary")),
        name="hgrn_bwd" if reverse else "hgrn_fwd",
    )(*args)


def _merge_kernel(a_ref, hh_ref, ga_ref, gh_ref, h_ref, wa_ref, wh_ref, wo_ref, o_ref):
    f32 = jnp.float32
    pa = jnp.dot(a_ref[...], wa_ref[...], preferred_element_type=f32)
    ph = jnp.dot(hh_ref[...], wh_ref[...], preferred_element_type=f32)
    merged = jax.nn.sigmoid(ga_ref[...]) * pa + jax.nn.sigmoid(gh_ref[...]) * ph
    o_ref[...] = h_ref[...] + jnp.dot(merged.astype(jnp.bfloat16), wo_ref[...], preferred_element_type=f32)


def _merge(a2d, hh2d, proj2d, h2d, wa, wh, wo, tm):
    T = h2d.shape[0]
    row = lambda c: pl.BlockSpec((tm, D_MODEL), lambda i: (i, c))
    wspec = pl.BlockSpec((D_MODEL, D_MODEL), lambda i: (0, 0))
    return pl.pallas_call(
        _merge_kernel,
        grid=(T // tm,),
        in_specs=[row(0), row(0), row(COL_GA // D_MODEL), row(COL_GH // D_MODEL), row(0), wspec, wspec, wspec],
        out_specs=row(0),
        out_shape=jax.ShapeDtypeStruct((T, D_MODEL), jnp.float32),
        input_output_aliases={4: 0},
        compiler_params=_cparams(("parallel",)),
        name="merge_out",
    )(a2d, hh2d, proj2d, proj2d, h2d, wa, wh, wo)


def _topk_axis0(s, kk, payload=None):
    n = s.shape[0]
    iota = lax.broadcasted_iota(jnp.int32, s.shape, 0)
    vals, ids = [], []
    for _ in range(kk):
        m = jnp.max(s, axis=0, keepdims=True)
        am = jnp.min(jnp.where(s == m, iota, n), axis=0, keepdims=True)
        hit = iota == am
        s = jnp.where(hit, -jnp.inf, s)
        vals.append(m)
        ids.append(am if payload is None else jnp.max(jnp.where(hit, payload, -1), axis=0, keepdims=True))
    return vals, ids


def _stack_rows(rows, dtype):
    n = len(rows)
    iota = lax.broadcasted_iota(jnp.int32, (n, rows[0].shape[1]), 0)
    out = jnp.zeros((n, rows[0].shape[1]), dtype)
    for j, rj in enumerate(rows):
        out = jnp.where(iota == j, rj, out)
    return out


def _route_kernel(h_ref, g_ref, wq_ref, key_ref, eidx_ref, gate_ref, xn_ref):
    @pl.when(pl.program_id(1) == 0)
    def _():
        xn_ref[...] = _rms(h_ref[...], g_ref[...]).astype(jnp.bfloat16)

    K = PEER_TOPK
    half = PEER_QDIM // 2
    qry = jnp.dot(xn_ref[...], wq_ref[...], preferred_element_type=jnp.float32).astype(jnp.bfloat16)
    s0 = _nt(key_ref[0], qry[:, :half])
    s1 = _nt(key_ref[1], qry[:, half:])
    v0, i0 = _topk_axis0(s0, K)
    v1, i1 = _topk_axis0(s1, K)
    m1 = _stack_rows(v1, jnp.float32)
    a1 = _stack_rows(i1, jnp.int32)
    cand = jnp.concatenate([v0[i] + m1 for i in range(K)], axis=0)
    cidx = jnp.concatenate([i0[i] * PEER_NKEYS + a1 for i in range(K)], axis=0)
    ts, te = _topk_axis0(cand, K, payload=cidx)
    ex = [jnp.exp(t - ts[0]) for t in ts]
    den = ex[0]
    for e in ex[1:]:
        den = den + e
    eidx_ref[...] = _stack_rows(te, jnp.int32)
    gate_ref[...] = _stack_rows([e / den for e in ex], jnp.float32)


def _route(h2d, g, wq, keys, tm):
    T = h2d.shape[0]
    out = jax.ShapeDtypeStruct((PEER_SLOTS, T), jnp.int32), jax.ShapeDtypeStruct((PEER_SLOTS, T), jnp.float32)
    ospec = pl.BlockSpec((PEER_TOPK, tm), lambda i, h: (h, i))
    return pl.pallas_call(
        _route_kernel,
        grid=(T // tm, PEER_HEADS),
        in_specs=[pl.BlockSpec((tm, D_MODEL), lambda i, h: (i, 0)),
                  pl.BlockSpec((1, D_MODEL), lambda i, h: (0, 0)),
                  pl.BlockSpec((D_MODEL, PEER_QDIM), lambda i, h: (0, h)),
                  pl.BlockSpec((2, PEER_NKEYS, PEER_QDIM // 2), lambda i, h: (h, 0, 0))],
        out_specs=(ospec, ospec),
        out_shape=out,
        scratch_shapes=[pltpu.VMEM((tm, D_MODEL), jnp.bfloat16)],
        compiler_params=_cparams(("parallel", "arbitrary")),
        name="peer_route",
    )(h2d, g, wq, keys)


def _expert_kernel(idx_ref, idxn_ref, h_ref, g_ref, gate_ref, uv_hbm, o_ref, buf, sem):
    TB, NS = PEER_TB, PEER_SLOTS
    GRP = NS // SUBLANES
    f32, bf16 = jnp.float32, jnp.bfloat16
    i = pl.program_id(0)
    n = pl.num_programs(0)
    slot = lax.rem(i, 2)

    def issue_token(iref, sl, t):
        for k in range(NS):
            pltpu.make_async_copy(uv_hbm.at[iref[t, k]], buf.at[sl, t * GRP + k // SUBLANES, :, k % SUBLANES],
                                  sem.at[sl]).start(priority=k % 2)

    @pl.when(i == 0)
    def _():
        def body(t, c):
            issue_token(idx_ref, 0, t)
            return c
        lax.fori_loop(0, TB, body, 0)

    pltpu.make_async_copy(buf.at[slot], buf.at[slot], sem.at[slot]).wait()

    x = h_ref[...]
    xn = _rms(x, g_ref[...]).astype(bf16)
    tok = lax.broadcasted_iota(jnp.int32, (TB, NS), 0)

    def pair(t, c):
        r = pl.ds(pl.multiple_of(t * GRP, GRP), GRP)
        return jnp.concatenate([buf[slot, r, c].reshape(NS, 128), buf[slot, r, c + 1].reshape(NS, 128)],
                               axis=1).astype(bf16)

    def act_token(t, act):
        acc = jnp.zeros((TB, NS), f32)
        for c in range(0, UV_CHUNKS // 2, 2):
            acc = acc + _nt(xn[:, c * 128:(c + 2) * 128], pair(t, c))
        return jnp.where(tok == t, acc, act)

    def mix_token(t, w, y):
        wt = jnp.where(tok == t, w, 0.0).astype(bf16)
        return y + jnp.concatenate([jnp.dot(wt, pair(t, UV_CHUNKS // 2 + c), preferred_element_type=f32)
                                    for c in range(0, UV_CHUNKS // 2, 2)], axis=1)

    def run(prefetch):
        def body_a(tt, act):
            for j in range(PEER_UNROLL):
                act = act_token(tt * PEER_UNROLL + j, act)
            if prefetch:
                for j in range(PEER_UNROLL):
                    issue_token(idxn_ref, 1 - slot, tt * PEER_UNROLL + j)
            return act
        act = lax.fori_loop(0, TB // PEER_UNROLL, body_a, jnp.zeros((TB, NS), f32))
        w = gate_ref[...] * (0.5 * act * (1.0 + lax.erf(act * (2.0 ** -0.5))))

        def body_c(tt, y):
            for j in range(PEER_UNROLL):
                y = mix_token(tt * PEER_UNROLL + j, w, y)
            return y
        o_ref[...] = x + lax.fori_loop(0, TB // PEER_UNROLL, body_c, jnp.zeros((TB, D_MODEL), f32))

    @pl.when(i + 1 < n)
    def _():
        run(True)

    @pl.when(i + 1 == n)
    def _():
        run(False)


def _experts(h2d, g, eidx, gate, uv):
    T = h2d.shape[0]
    TB, NS = PEER_TB, PEER_SLOTS
    nsteps = T // TB
    return pl.pallas_call(
        _expert_kernel,
        grid=(nsteps,),
        in_specs=[pl.BlockSpec((TB, NS), lambda i: (i, 0), memory_space=pltpu.SMEM),
                  pl.BlockSpec((TB, NS), lambda i: (jnp.minimum(i + 1, nsteps - 1), 0), memory_space=pltpu.SMEM),
                  pl.BlockSpec((TB, D_MODEL), lambda i: (i, 0)),
                  pl.BlockSpec((1, D_MODEL), lambda i: (0, 0)),
                  pl.BlockSpec((TB, NS), lambda i: (i, 0)),
                  pl.BlockSpec(memory_space=pl.ANY)],
        out_specs=pl.BlockSpec((TB, D_MODEL), lambda i: (i, 0)),
        out_shape=jax.ShapeDtypeStruct((T, D_MODEL), jnp.float32),
        scratch_shapes=[pltpu.VMEM((2, TB * NS // SUBLANES, UV_CHUNKS, SUBLANES, 128), jnp.float32),
                        pltpu.SemaphoreType.DMA((2,))],
        input_output_aliases={2: 0},
        compiler_params=_cparams(("arbitrary",)),
        name="peer_experts",
    )(eidx, eidx, h2d, g, gate, uv)


def _final_kernel(h_ref, g_ref, o_ref):
    o_ref[...] = _rms(h_ref[...], g_ref[...])


def _final(h3, g, b0, nb):
    _, Lp, _ = h3.shape
    S = Lp - BLK
    return pl.pallas_call(
        _final_kernel,
        grid=(nb, S // BLK),
        in_specs=[pl.BlockSpec((None, BLK, D_MODEL), lambda b, r: (b + b0, r + 1, 0)),
                  pl.BlockSpec((1, D_MODEL), lambda b, r: (0, 0))],
        out_specs=pl.BlockSpec((None, BLK, D_MODEL), lambda b, r: (b, r, 0)),
        out_shape=jax.ShapeDtypeStruct((nb, S, D_MODEL), jnp.float32),
        compiler_params=_cparams(("parallel", "parallel")),
        name="final_norm",
    )(h3, g)


def _pick_tile(T, candidates):
    for c in candidates:
        if T % c == 0:
            return c
    raise ValueError(f"no tile for {T}")


def kernel(x_prompt, x_sample, meta_tokens, g_mix, w_in, attn_sinks, lb_logits, hg_norm_g, w_att_branch,
           w_hg_branch, w_out, g_ffn, w_peer_q, peer_sub_keys, peer_u, peer_v, g_final):
    f32, bf16 = jnp.float32, jnp.bfloat16
    depth = w_in.shape[0]
    nb_p, S, _ = x_prompt.shape
    nb_s = x_sample.shape[0]
    assert x_sample.shape[1] == S and S % BLK == 0
    NB = nb_p + nb_s
    Lp = S + BLK
    T = NB * Lp

    x = jnp.concatenate([x_prompt, x_sample], axis=0)
    head = jnp.concatenate([jnp.zeros((PAD_ROWS, D_MODEL), f32), meta_tokens.astype(f32)], axis=0)
    h = jnp.concatenate([jnp.broadcast_to(head[None], (NB, BLK, D_MODEL)), x], axis=1).reshape(T, D_MODEL)

    sizes = (1024, 256, 256, 1024, 1024, 1024, 1024, 1024, 1024, 1024)
    offs = np.concatenate([[0], np.cumsum(sizes)])
    order = (0, 8, 9, 3, 4, 5, 6, 7, 1, 2)
    perm = np.concatenate([np.arange(offs[k], offs[k + 1]) for k in order])

    sm = jax.nn.softmax(lb_logits.astype(f32), axis=0)
    lb_all = jnp.maximum(jnp.cumsum(sm, axis=0) - sm[0:1], 0.0)

    tables = _rope_tables(Lp)
    consts_f = _hg_consts(False)
    consts_b = _hg_consts(True)
    tm_big = _pick_tile(T, (512, 384, 256, 128))
    tm_route = _pick_tile(T, (256, 128))

    for l in range(depth):
        w_l = w_in[l][:, perm].astype(bf16)
        proj = _inproj(h, g_mix[l].reshape(1, -1), w_l, tm_big, IN_W // 4)
        proj3 = proj.reshape(NB, Lp, IN_W)
        a_out = _attention(proj3, attn_sinks[l].astype(f32), tables)
        lb3 = lb_all[l].reshape(2, 1, -1)
        o_fwd = _hgrn(proj3, lb3, consts_f, reverse=False)
        h_out = _hgrn(proj3, lb3, consts_b, reverse=True, extra=(o_fwd, hg_norm_g[l].reshape(1, -1)))
        h = _merge(a_out.reshape(T, -1), h_out.reshape(T, -1), proj, h,
                   w_att_branch[l].astype(bf16), w_hg_branch[l].astype(bf16), w_out[l].astype(bf16), tm_big)
        g_f = g_ffn[l].reshape(1, -1)
        keys = peer_sub_keys[l].reshape(2 * PEER_HEADS, PEER_NKEYS, PEER_QDIM // 2).astype(bf16)
        eidx_t, gate_t = _route(h, g_f, w_peer_q[l].astype(bf16), keys, tm_route)
        n_exp = peer_u.shape[1]
        uv = jnp.concatenate([peer_u[l].reshape(n_exp, -1, 128), peer_v[l].reshape(n_exp, -1, 128)], axis=1)
        h = _experts(h, g_f, eidx_t.T, gate_t.T, uv)

    h3 = h.reshape(NB, Lp, D_MODEL)
    g_fin = g_final.reshape(1, -1)
    return _final(h3, g_fin, 0, nb_p), _final(h3, g_fin, nb_p, nb_s)
```

```python
import numpy as np
import jax
import jax.numpy as jnp
from jax import lax
from jax.experimental import pallas as pl
from jax.experimental.pallas import tpu as pltpu

D_MODEL = 1024
N_META = 16
N_Q_HEADS = 16
N_KV_HEADS = 4
HEAD_DIM = 64
WINDOW = 128
ROT_DIM = HEAD_DIM // 4
ROPE_THETA = 500000.0
HG_HEADS = 8
HG_DIM = 128
PEER_HEADS = 8
PEER_NKEYS = 128
PEER_QDIM = 256
PEER_TOPK = 16
EPS = 1e-6
NEG = -1e30

BLK = 128
PAD_ROWS = BLK - N_META
IN_W = 8704
COL_AQ, COL_GA, COL_GH, COL_HQ, COL_ZF, COL_ZB, COL_HI, COL_HGG, COL_AK, COL_AV = (
    0, 1024, 2048, 3072, 4096, 5120, 6144, 7168, 8192, 8448)

VMEM_LIMIT = 56 * 1024 * 1024
PEER_TB = 16
PEER_SLOTS = PEER_HEADS * PEER_TOPK
HG_HP = 4
PEER_UNROLL = 2
SUBLANES = 8
UV_WORD_CHUNKS = 2 * D_MODEL // 256


def _cparams(sem):
    return pltpu.CompilerParams(dimension_semantics=sem, vmem_limit_bytes=VMEM_LIMIT)


def _rms(x, g):
    return x * lax.rsqrt(jnp.mean(x * x, axis=-1, keepdims=True) + EPS) * g


def _nt(a, b):
    return lax.dot_general(a, b, (((1,), (1,)), ((), ())), preferred_element_type=jnp.float32)


def _tn(a, b):
    return lax.dot_general(a, b, (((0,), (0,)), ((), ())), preferred_element_type=jnp.float32)


def _inproj_kernel(x_ref, g_ref, w_ref, o_ref, n_ref):
    @pl.when(pl.program_id(1) == 0)
    def _():
        n_ref[...] = _rms(x_ref[...], g_ref[...]).astype(jnp.bfloat16)

    o_ref[...] = jnp.dot(n_ref[...], w_ref[...], preferred_element_type=jnp.float32)


def _inproj(h2d, g, w, tm, tn):
    T = h2d.shape[0]
    return pl.pallas_call(
        _inproj_kernel,
        grid=(T // tm, IN_W // tn),
        in_specs=[pl.BlockSpec((tm, D_MODEL), lambda i, j: (i, 0)),
                  pl.BlockSpec((1, D_MODEL), lambda i, j: (0, 0)),
                  pl.BlockSpec((D_MODEL, tn), lambda i, j: (0, j))],
        out_specs=pl.BlockSpec((tm, tn), lambda i, j: (i, j)),
        out_shape=jax.ShapeDtypeStruct((T, IN_W), jnp.float32),
        scratch_shapes=[pltpu.VMEM((tm, D_MODEL), jnp.bfloat16)],
        compiler_params=_cparams(("parallel", "arbitrary")),
        name="inproj",
    )(h2d, g, w)


def _rope_tables(Lp):
    half = ROT_DIM // 2
    pos = jnp.maximum(jnp.arange(Lp) - PAD_ROWS, 0)
    inv = jnp.power(jnp.float32(ROPE_THETA), -jnp.arange(half, dtype=jnp.float32) * 2.0 / ROT_DIM)
    ang = pos.astype(jnp.float32)[:, None] * inv[None, :]
    cos, sin = jnp.cos(ang), jnp.sin(ang)
    ones = jnp.ones((Lp, HEAD_DIM - ROT_DIM), jnp.float32)
    zeros = jnp.zeros((Lp, HEAD_DIM - ROT_DIM), jnp.float32)
    zh = jnp.zeros((Lp, half), jnp.float32)
    c = jnp.concatenate([cos, cos, ones], axis=1)
    s1 = jnp.concatenate([zh, sin, zeros], axis=1)
    s2 = jnp.concatenate([-sin, zh, zeros], axis=1)
    rep = BLK // HEAD_DIM
    return tuple(jnp.tile(t, (1, rep)) for t in (c, s1, s2))


def _attn_kernel(sink_ref, q_ref, kg_ref, k0_ref, k1_ref, k2_ref, vg_ref, v0_ref, v1_ref, v2_ref,
                 c_ref, s1_ref, s2_ref, o_ref):
    r = pl.program_id(1)
    nblk = pl.num_programs(1)
    half = ROT_DIM // 2

    def rope(x, blk):
        row0 = blk * BLK if isinstance(blk, int) else pl.multiple_of(blk * BLK, BLK)
        c = c_ref[pl.ds(row0, BLK), :]
        s1 = s1_ref[pl.ds(row0, BLK), :]
        s2 = s2_ref[pl.ds(row0, BLK), :]
        outs = []
        for j in range(x.shape[1] // BLK):
            xj = x[:, j * BLK:(j + 1) * BLK]
            outs.append(xj * c + pltpu.roll(xj, half, 1) * s1 + pltpu.roll(xj, BLK - half, 1) * s2)
        return jnp.concatenate(outs, axis=1)

    kb = [jnp.clip(r - 1 + j, 0, nblk - 1) for j in range(3)]
    q = rope(q_ref[...], r) * (HEAD_DIM ** -0.5)
    kcat = jnp.concatenate([rope(kg_ref[...], 0), rope(k0_ref[...], kb[0]), rope(k1_ref[...], kb[1]),
                            rope(k2_ref[...], kb[2])], axis=0).astype(jnp.bfloat16)
    vcat = jnp.concatenate([vg_ref[...], v0_ref[...], v1_ref[...], v2_ref[...]], axis=0).astype(jnp.bfloat16)

    qrow = lax.broadcasted_iota(jnp.int32, (BLK, BLK), 0)
    kcol = lax.broadcasted_iota(jnp.int32, (BLK, BLK), 1)
    masks = [kcol >= PAD_ROWS]
    for j in range(3):
        kblk = r - 1 + j
        ok_blk = (kblk >= 1) & (kblk <= nblk - 1)
        masks.append((jnp.abs((j - 1) * BLK + kcol - qrow) <= WINDOW) & ok_blk)
    mask = jnp.concatenate(masks, axis=1)

    G = N_Q_HEADS // N_KV_HEADS
    lane_half = kcol // HEAD_DIM
    pairs_per_slice = (BLK // HEAD_DIM) * G // 2
    outs = [None] * (N_Q_HEADS // 2)
    for m2 in range(N_KV_HEADS * HEAD_DIM // BLK):
        k2 = kcat[:, m2 * BLK:(m2 + 1) * BLK]
        v2 = vcat[:, m2 * BLK:(m2 + 1) * BLK]
        qs, sks, halves = [], [], []
        for j in range(m2 * pairs_per_slice, (m2 + 1) * pairs_per_slice):
            hk = ((2 * j) // G) % 2
            in_half = lane_half == hk
            q2 = q[:, j * BLK:(j + 1) * BLK]
            qs += [jnp.where(in_half, q2, 0.0), jnp.where(in_half, pltpu.roll(q2, HEAD_DIM, 1), 0.0)]
            sks += [jnp.full((BLK, 1), sink_ref[2 * j + hk]), jnp.full((BLK, 1), sink_ref[2 * j + 1 - hk])]
            halves.append(in_half)
        nq = len(qs)
        sk = jnp.concatenate(sks, axis=0)
        s = _nt(jnp.concatenate(qs, axis=0).astype(jnp.bfloat16), k2)
        s = jnp.where(jnp.tile(mask, (nq, 1)), s, NEG)
        m = jnp.maximum(jnp.max(s, axis=-1, keepdims=True), sk)
        p = jnp.exp(s - m)
        p = p / (jnp.sum(p, axis=-1, keepdims=True) + jnp.exp(sk - m))
        o = jnp.dot(p.astype(jnp.bfloat16), v2, preferred_element_type=jnp.float32)
        for jj, in_half in enumerate(halves):
            o_same = o[(2 * jj) * BLK:(2 * jj + 1) * BLK]
            o_other = o[(2 * jj + 1) * BLK:(2 * jj + 2) * BLK]
            outs[m2 * pairs_per_slice + jj] = jnp.where(in_half, o_same, pltpu.roll(o_other, HEAD_DIM, 1))
    o_ref[...] = jnp.concatenate(outs, axis=1).astype(o_ref.dtype)


def _attention(proj3, sinks, tables):
    NB, Lp, _ = proj3.shape
    nblk = Lp // BLK
    kcol, vcol = COL_AK // 256, COL_AV // 256

    def loc(j, col):
        return pl.BlockSpec((None, BLK, 256),
                            lambda b, r: (b, jnp.clip(r - 1 + j, 0, nblk - 1), col))

    glob = lambda col: pl.BlockSpec((None, BLK, 256), lambda b, r: (b, 0, col))
    tab = pl.BlockSpec((Lp, BLK), lambda b, r: (0, 0))
    return pl.pallas_call(
        _attn_kernel,
        grid=(NB, nblk),
        in_specs=[pl.BlockSpec(memory_space=pltpu.SMEM),
                  pl.BlockSpec((None, BLK, 1024), lambda b, r: (b, r, 0)),
                  glob(kcol), loc(0, kcol), loc(1, kcol), loc(2, kcol),
                  glob(vcol), loc(0, vcol), loc(1, vcol), loc(2, vcol),
                  tab, tab, tab],
        out_specs=pl.BlockSpec((None, BLK, 1024), lambda b, r: (b, r, 0)),
        out_shape=jax.ShapeDtypeStruct((NB, Lp, 1024), jnp.bfloat16),
        compiler_params=_cparams(("parallel", "arbitrary")),
        name="window_attn",
    )(sinks, proj3, proj3, proj3, proj3, proj3, proj3, proj3, proj3, proj3, *tables)


HG_ROWS = BLK
_HG_LEVELS = (64, 32, 16, 8, 4, 2, 1)


def _hg_consts(reverse):
    C = HG_ROWS
    t = np.arange(C)[:, None]
    u = np.arange(C)[None, :]
    mats = [u <= t, u > t]
    ups, msks = [], []
    for B in _HG_LEVELS:
        same = (t // (2 * B)) == (u // (2 * B))
        tin, uin = t % (2 * B), u % (2 * B)
        upper = tin >= B
        wq = upper & same & (uin >= B) & (uin <= tin)
        wk = (~upper) & same & (uin > tin) & (uin <= B - 1)
        mats.append(wq | wk)
        ups.append(np.broadcast_to(upper, (C, HG_DIM)))
        msks.append(same & upper & (uin < B))
    msks.append(np.eye(C, dtype=bool))
    if reverse:
        mats = [m[::-1, ::-1] for m in mats]
        ups = [m[::-1] for m in ups]
        msks = [m[::-1, ::-1] for m in msks]
    W = np.concatenate(mats, axis=0).astype(np.float32)
    W2 = np.concatenate([W, W], axis=1)
    return (jnp.asarray(W2, jnp.bfloat16), jnp.asarray(np.stack(ups).astype(np.float32)),
            jnp.asarray(np.stack(msks).astype(np.float32)))


def _hg_chunk(q_raw, z, v, valid, lb, w2, up_ref, msk_ref, st, reverse):
    C = HG_ROWS
    f32, bf16 = jnp.float32, jnp.bfloat16
    log_lb = jnp.log(lb)
    log1m = jnp.log1p(-lb)
    logsig = jnp.minimum(z, 0.0) - jnp.log1p(jnp.exp(-jnp.abs(z)))
    a, b = log_lb, log1m + logsig
    g = jnp.maximum(a, b) + jnp.log1p(jnp.exp(-jnp.abs(a - b)))
    k = (1.0 - lb) * jax.nn.sigmoid(-z)
    q = q_raw * jax.nn.sigmoid(q_raw)
    q = jnp.where(valid, q, 0.0)
    k = jnp.where(valid, k, 0.0)
    v = jnp.where(valid, v, 0.0)
    g = jnp.where(valid, g, 0.0)
    g1 = g.astype(bf16)
    g2 = (g - g1.astype(f32)).astype(bf16)
    E = jnp.dot(w2, jnp.concatenate([g1, g2], axis=0), preferred_element_type=f32)
    b_inc = E[0:C]
    b_sfx = E[C:2 * C]
    b_tot = b_inc[0:1] if reverse else b_inc[C - 1:C]

    A = msk_ref[len(_HG_LEVELS)] * jnp.sum(q * k, axis=-1, keepdims=True)
    for l in range(len(_HG_LEVELS)):
        ex = jnp.exp(E[(2 + l) * C:(3 + l) * C])
        up = up_ref[l] > 0.5
        qs = jnp.where(up, q * ex, 0.0).astype(bf16)
        ks = jnp.where(up, 0.0, k * ex).astype(bf16)
        A = A + _nt(qs, ks) * msk_ref[l]

    vb = v.astype(bf16)
    o = _nt((q * jnp.exp(b_inc)).astype(bf16), st.astype(bf16))
    o = o + jnp.dot(A.astype(bf16), vb, preferred_element_type=f32)
    st = st * jnp.exp(b_tot) + _tn(vb, (k * jnp.exp(b_sfx)).astype(bf16))
    return o, st


def _hg_kernel(qf_ref, zf_ref, vf_ref, qb_ref, zb_ref, vb_ref, lb_ref, w2f_ref, upf_ref, mskf_ref,
               w2b_ref, upb_ref, mskb_ref, of_ref, ob_ref, st_ref):
    i = pl.program_id(2)
    nblk = pl.num_programs(2)

    @pl.when(i == 0)
    def _():
        st_ref[...] = jnp.zeros_like(st_ref)

    rows = lax.broadcasted_iota(jnp.int32, (HG_ROWS, HG_DIM), 0)
    dirs = ((False, qf_ref, zf_ref, vf_ref, w2f_ref, upf_ref, mskf_ref, of_ref, i),
            (True, qb_ref, zb_ref, vb_ref, w2b_ref, upb_ref, mskb_ref, ob_ref, nblk - 1 - i))
    for d, (reverse, q_ref, z_ref, v_ref, w2_ref, up_ref, msk_ref, o_ref, blk) in enumerate(dirs):
        w2 = w2_ref[...]
        for hh in range(HG_HP):
            cols = slice(hh * HG_DIM, (hh + 1) * HG_DIM)
            lb = lb_ref[d, :, cols]
            valid = (rows >= PAD_ROWS) | (blk > 0)
            o, st = _hg_chunk(q_ref[:, cols], z_ref[:, cols], v_ref[:, cols], valid, lb, w2,
                              up_ref, msk_ref, st_ref[d, hh], reverse)
            st_ref[d, hh] = st
            o_ref[:, cols] = o


def _hgrn(proj3, lb3, consts_f, consts_b):
    NB, Lp, _ = proj3.shape
    nblk = Lp // BLK
    W = HG_DIM * HG_HP

    def col(base, reverse):
        return pl.BlockSpec((None, BLK, W),
                            lambda b, h, i: (b, (nblk - 1 - i) if reverse else i, base // W + h))

    def const(c):
        return pl.BlockSpec(c.shape, lambda b, h, i: (0,) * c.ndim)

    in_specs = [col(COL_HQ, False), col(COL_ZF, False), col(COL_HI, False),
                col(COL_HQ, True), col(COL_ZB, True), col(COL_HI, True),
                pl.BlockSpec((2, 1, W), lambda b, h, i: (0, 0, h))]
    in_specs += [const(c) for c in consts_f] + [const(c) for c in consts_b]
    oshape = jax.ShapeDtypeStruct((NB, Lp, HG_HEADS * HG_DIM), jnp.float32)
    return pl.pallas_call(
        _hg_kernel,
        grid=(NB, HG_HEADS // HG_HP, nblk),
        in_specs=in_specs,
        out_specs=(pl.BlockSpec((None, BLK, W), lambda b, h, i: (b, i, h)),
                   pl.BlockSpec((None, BLK, W), lambda b, h, i: (b, nblk - 1 - i, h))),
        out_shape=(oshape, oshape),
        scratch_shapes=[pltpu.VMEM((2, HG_HP, HG_DIM, HG_DIM), jnp.float32)],
        compiler_params=_cparams(("parallel", "parallel", "arbitrary")),
        name="hgrn_bidir",
    )(proj3, proj3, proj3, proj3, proj3, proj3, lb3, *consts_f, *consts_b)


def _merge_kernel(a_ref, of_ref, ob_ref, hgg_ref, ng_ref, ga_ref, gh_ref, h_ref, wa_ref, wh_ref, wo_ref, o_ref):
    f32, bf16 = jnp.float32, jnp.bfloat16
    o = of_ref[...] + ob_ref[...]
    ng = ng_ref[...]
    gt = hgg_ref[...]
    hh = jnp.concatenate([_rms(o[:, c:c + HG_DIM], ng[:, c:c + HG_DIM]) for c in range(0, HG_HEADS * HG_DIM, HG_DIM)],
                         axis=1) * (gt * jax.nn.sigmoid(gt))
    pa = jnp.dot(a_ref[...], wa_ref[...], preferred_element_type=f32)
    ph = jnp.dot(hh.astype(bf16), wh_ref[...], preferred_element_type=f32)
    merged = jax.nn.sigmoid(ga_ref[...]) * pa + jax.nn.sigmoid(gh_ref[...]) * ph
    o_ref[...] = h_ref[...] + jnp.dot(merged.astype(bf16), wo_ref[...], preferred_element_type=f32)


def _merge(a2d, of2d, ob2d, ng, proj2d, h2d, wa, wh, wo, tm):
    T = h2d.shape[0]
    row = lambda c: pl.BlockSpec((tm, D_MODEL), lambda i: (i, c))
    wspec = pl.BlockSpec((D_MODEL, D_MODEL), lambda i: (0, 0))
    return pl.pallas_call(
        _merge_kernel,
        grid=(T // tm,),
        in_specs=[row(0), row(0), row(0), row(COL_HGG // D_MODEL), pl.BlockSpec((1, D_MODEL), lambda i: (0, 0)),
                  row(COL_GA // D_MODEL), row(COL_GH // D_MODEL), row(0), wspec, wspec, wspec],
        out_specs=row(0),
        out_shape=jax.ShapeDtypeStruct((T, D_MODEL), jnp.float32),
        input_output_aliases={7: 0},
        compiler_params=_cparams(("parallel",)),
        name="merge_out",
    )(a2d, of2d, ob2d, proj2d, ng, proj2d, proj2d, h2d, wa, wh, wo)


def _topk_axis0(s, kk):
    n = s.shape[0]
    iota = lax.broadcasted_iota(jnp.int32, s.shape, 0)
    vals, ids = [], []
    for _ in range(kk):
        m = jnp.max(s, axis=0, keepdims=True)
        am = jnp.min(jnp.where(s == m, iota, n), axis=0, keepdims=True)
        s = jnp.where(iota == am, -jnp.inf, s)
        vals.append(m)
        ids.append(am)
    return vals, ids


def _stack_rows(rows, dtype):
    n = len(rows)
    iota = lax.broadcasted_iota(jnp.int32, (n, rows[0].shape[1]), 0)
    out = jnp.zeros((n, rows[0].shape[1]), dtype)
    for j, rj in enumerate(rows):
        out = jnp.where(iota == j, rj, out)
    return out


def _route_kernel(h_ref, g_ref, wq_ref, key_ref, eidx_ref, gate_ref, xn_ref):
    @pl.when(pl.program_id(1) == 0)
    def _():
        xn_ref[...] = _rms(h_ref[...], g_ref[...]).astype(jnp.bfloat16)

    K = PEER_TOPK
    half = PEER_QDIM // 2
    i32 = jnp.int32
    qry = jnp.dot(xn_ref[...], wq_ref[...], preferred_element_type=jnp.float32).astype(jnp.bfloat16)
    s0 = _nt(key_ref[0], qry[:, :half])
    s1 = _nt(key_ref[1], qry[:, half:])
    tm = s0.shape[1]
    v0, i0 = _topk_axis0(s0, K)
    v1, i1 = _topk_axis0(s1, K)
    m1 = _stack_rows(v1, jnp.float32)
    m0_hi = _stack_rows(v0[K // 2:], jnp.float32)
    r8 = lax.broadcasted_iota(i32, (K // 2, tm), 0)
    cand = [v0[0] + m1] + [v0[a] + m1[:K // 2] for a in range(1, K // 2)] + [m0_hi + v1[0]]
    flat = [lax.broadcasted_iota(i32, (K, tm), 0)] + [a * K + r8 for a in range(1, K // 2)] + [(K // 2 + r8) * K]
    cand = jnp.concatenate(cand, axis=0)
    flat = jnp.concatenate(flat, axis=0)
    big = K * K
    ts, tp = [], []
    for _ in range(K):
        m = jnp.max(cand, axis=0, keepdims=True)
        p = jnp.min(jnp.where(cand == m, flat, big), axis=0, keepdims=True)
        cand = jnp.where(flat == p, -jnp.inf, cand)
        ts.append(m)
        tp.append(p)
    pk = _stack_rows(tp, i32)
    pa, pb = pk // K, pk % K
    e0 = jnp.zeros((K, tm), i32)
    e1 = jnp.zeros((K, tm), i32)
    for a in range(K):
        e0 = jnp.where(pa == a, i0[a], e0)
        e1 = jnp.where(pb == a, i1[a], e1)
    ex = [jnp.exp(t - ts[0]) for t in ts]
    den = ex[0]
    for e in ex[1:]:
        den = den + e
    eidx_ref[...] = e0 * PEER_NKEYS + e1
    gate_ref[...] = _stack_rows([e / den for e in ex], jnp.float32)


def _route(h2d, g, wq, keys, tm):
    T = h2d.shape[0]
    out = jax.ShapeDtypeStruct((PEER_SLOTS, T), jnp.int32), jax.ShapeDtypeStruct((PEER_SLOTS, T), jnp.float32)
    ospec = pl.BlockSpec((PEER_TOPK, tm), lambda i, h: (h, i))
    return pl.pallas_call(
        _route_kernel,
        grid=(T // tm, PEER_HEADS),
        in_specs=[pl.BlockSpec((tm, D_MODEL), lambda i, h: (i, 0)),
                  pl.BlockSpec((1, D_MODEL), lambda i, h: (0, 0)),
                  pl.BlockSpec((D_MODEL, PEER_QDIM), lambda i, h: (0, h)),
                  pl.BlockSpec((2, PEER_NKEYS, PEER_QDIM // 2), lambda i, h: (h, 0, 0))],
        out_specs=(ospec, ospec),
        out_shape=out,
        scratch_shapes=[pltpu.VMEM((tm, D_MODEL), jnp.bfloat16)],
        compiler_params=_cparams(("parallel", "arbitrary")),
        name="peer_route",
    )(h2d, g, wq, keys)


def _pack_bf16_pairs(x):
    e, d = x.shape
    bits = lax.bitcast_convert_type(x.astype(jnp.bfloat16), jnp.uint16).astype(jnp.uint32)
    bits = bits.reshape(e, d // 256, 2, 128)
    return bits[:, :, 0, :] | (bits[:, :, 1, :] << 16)


def _expert_kernel(idx_ref, idxn_ref, h_ref, g_ref, gate_ref, uv_hbm, o_ref, buf, sem):
    TB, NS = PEER_TB, PEER_SLOTS
    GRP = NS // SUBLANES
    NCH = UV_WORD_CHUNKS // 2
    f32, bf16 = jnp.float32, jnp.bfloat16
    i = pl.program_id(0)
    n = pl.num_programs(0)
    slot = lax.rem(i, 2)

    def issue_token(iref, sl, t):
        for k in range(NS):
            pltpu.make_async_copy(uv_hbm.at[iref[t, k]], buf.at[sl, t * GRP + k // SUBLANES, :, k % SUBLANES],
                                  sem.at[sl]).start(priority=k % 2)

    @pl.when(i == 0)
    def _():
        def body(t, c):
            issue_token(idx_ref, 0, t)
            return c
        lax.fori_loop(0, TB, body, 0)

    pltpu.make_async_copy(buf.at[slot], buf.at[slot], sem.at[slot]).wait()

    x = h_ref[...]
    xn = _rms(x, g_ref[...]).astype(bf16)
    xs = [jnp.concatenate([xn[:, c * 256:c * 256 + 128], xn[:, c * 256 + 128:(c + 1) * 256]], axis=0)
          for c in range(NCH)]
    tok = lax.broadcasted_iota(jnp.int32, (TB, 2 * NS), 0)
    even = lax.broadcasted_iota(jnp.int32, (TB, 2 * NS), 1) % 2 == 0

    def halves(t, c):
        r = pl.ds(pl.multiple_of(t * GRP, GRP), GRP)
        return pltpu.bitcast(buf[slot, r, c].reshape(NS, 128), bf16)

    def act_token(t, act):
        acc = jnp.zeros((TB, 2 * NS), f32)
        for c in range(NCH):
            r = _nt(xs[c], halves(t, c))
            acc = acc + jnp.where(even, r[:TB], r[TB:])
        pair = acc + jnp.where(even, pltpu.roll(acc, 2 * NS - 1, 1), pltpu.roll(acc, 1, 1))
        return jnp.where(tok == t, pair, act)

    def mix_token(t, w, y):
        wt = jnp.where(tok == t, w, 0.0)
        w2 = jnp.concatenate([jnp.where(even, wt, 0.0), jnp.where(even, 0.0, wt)], axis=0).astype(bf16)
        parts = []
        for c in range(NCH):
            r = jnp.dot(w2, halves(t, NCH + c), preferred_element_type=f32)
            parts += [r[:TB], r[TB:]]
        return y + jnp.concatenate(parts, axis=1)

    def run(prefetch):
        def body_a(tt, act):
            for j in range(PEER_UNROLL):
                act = act_token(tt * PEER_UNROLL + j, act)
            if prefetch:
                for j in range(PEER_UNROLL):
                    issue_token(idxn_ref, 1 - slot, tt * PEER_UNROLL + j)
            return act
        act = lax.fori_loop(0, TB // PEER_UNROLL, body_a, jnp.zeros((TB, 2 * NS), f32))
        w = gate_ref[...] * (0.5 * act * (1.0 + lax.erf(act * (2.0 ** -0.5))))

        def body_c(tt, y):
            for j in range(PEER_UNROLL):
                y = mix_token(tt * PEER_UNROLL + j, w, y)
            return y
        o_ref[...] = x + lax.fori_loop(0, TB // PEER_UNROLL, body_c, jnp.zeros((TB, D_MODEL), f32))

    @pl.when(i + 1 < n)
    def _():
        run(True)

    @pl.when(i + 1 == n)
    def _():
        run(False)


def _experts(h2d, g, eidx, gate2, uv):
    T = h2d.shape[0]
    TB, NS = PEER_TB, PEER_SLOTS
    nsteps = T // TB
    return pl.pallas_call(
        _expert_kernel,
        grid=(nsteps,),
        in_specs=[pl.BlockSpec((TB, NS), lambda i: (i, 0), memory_space=pltpu.SMEM),
                  pl.BlockSpec((TB, NS), lambda i: (jnp.minimum(i + 1, nsteps - 1), 0), memory_space=pltpu.SMEM),
                  pl.BlockSpec((TB, D_MODEL), lambda i: (i, 0)),
                  pl.BlockSpec((1, D_MODEL), lambda i: (0, 0)),
                  pl.BlockSpec((TB, 2 * NS), lambda i: (i, 0)),
                  pl.BlockSpec(memory_space=pl.ANY)],
        out_specs=pl.BlockSpec((TB, D_MODEL), lambda i: (i, 0)),
        out_shape=jax.ShapeDtypeStruct((T, D_MODEL), jnp.float32),
        scratch_shapes=[pltpu.VMEM((2, TB * NS // SUBLANES, UV_WORD_CHUNKS, SUBLANES, 128), jnp.uint32),
                        pltpu.SemaphoreType.DMA((2,))],
        input_output_aliases={2: 0},
        compiler_params=_cparams(("arbitrary",)),
        name="peer_experts",
    )(eidx, eidx, h2d, g, gate2, uv)


def _final_kernel(h_ref, g_ref, o_ref):
    o_ref[...] = _rms(h_ref[...], g_ref[...])


def _final(h3, g, b0, nb):
    _, Lp, _ = h3.shape
    S = Lp - BLK
    return pl.pallas_call(
        _final_kernel,
        grid=(nb, S // BLK),
        in_specs=[pl.BlockSpec((None, BLK, D_MODEL), lambda b, r: (b + b0, r + 1, 0)),
                  pl.BlockSpec((1, D_MODEL), lambda b, r: (0, 0))],
        out_specs=pl.BlockSpec((None, BLK, D_MODEL), lambda b, r: (b, r, 0)),
        out_shape=jax.ShapeDtypeStruct((nb, S, D_MODEL), jnp.float32),
        compiler_params=_cparams(("parallel", "parallel")),
        name="final_norm",
    )(h3, g)


def _pick_tile(T, candidates):
    for c in candidates:
        if T % c == 0:
            return c
    raise ValueError(f"no tile for {T}")


def kernel(x_prompt, x_sample, meta_tokens, g_mix, w_in, attn_sinks, lb_logits, hg_norm_g, w_att_branch,
           w_hg_branch, w_out, g_ffn, w_peer_q, peer_sub_keys, peer_u, peer_v, g_final):
    f32, bf16 = jnp.float32, jnp.bfloat16
    depth = w_in.shape[0]
    nb_p, S, _ = x_prompt.shape
    nb_s = x_sample.shape[0]
    assert x_sample.shape[1] == S and S % BLK == 0
    NB = nb_p + nb_s
    Lp = S + BLK
    T = NB * Lp

    x = jnp.concatenate([x_prompt, x_sample], axis=0)
    head = jnp.concatenate([jnp.zeros((PAD_ROWS, D_MODEL), f32), meta_tokens.astype(f32)], axis=0)
    h = jnp.concatenate([jnp.broadcast_to(head[None], (NB, BLK, D_MODEL)), x], axis=1).reshape(T, D_MODEL)

    sizes = (1024, 256, 256, 1024, 1024, 1024, 1024, 1024, 1024, 1024)
    offs = np.concatenate([[0], np.cumsum(sizes)])
    order = (0, 8, 9, 3, 4, 5, 6, 7, 1, 2)
    perm = np.concatenate([np.arange(offs[k], offs[k + 1]) for k in order])

    sm = jax.nn.softmax(lb_logits.astype(f32), axis=0)
    lb_all = jnp.maximum(jnp.cumsum(sm, axis=0) - sm[0:1], 0.0)

    tables = _rope_tables(Lp)
    consts_f = _hg_consts(False)
    consts_b = _hg_consts(True)
    tm_big = _pick_tile(T, (512, 384, 256, 128))
    tm_route = _pick_tile(T, (256, 128))

    for l in range(depth):
        w_l = w_in[l][:, perm].astype(bf16)
        proj = _inproj(h, g_mix[l].reshape(1, -1), w_l, tm_big, IN_W // 4)
        proj3 = proj.reshape(NB, Lp, IN_W)
        a_out = _attention(proj3, attn_sinks[l].astype(f32), tables)
        lb3 = lb_all[l].reshape(2, 1, -1)
        o_fwd, o_bwd = _hgrn(proj3, lb3, consts_f, consts_b)
        h = _merge(a_out.reshape(T, -1), o_fwd.reshape(T, -1), o_bwd.reshape(T, -1), hg_norm_g[l].reshape(1, -1),
                   proj, h, w_att_branch[l].astype(bf16), w_hg_branch[l].astype(bf16), w_out[l].astype(bf16),
                   tm_route)
        g_f = g_ffn[l].reshape(1, -1)
        keys = peer_sub_keys[l].reshape(2 * PEER_HEADS, PEER_NKEYS, PEER_QDIM // 2).astype(bf16)
        eidx_t, gate_t = _route(h, g_f, w_peer_q[l].astype(bf16), keys, tm_route)
        uv = jnp.concatenate([_pack_bf16_pairs(peer_u[l]), _pack_bf16_pairs(peer_v[l])], axis=1)
        h = _experts(h, g_f, eidx_t.T, jnp.repeat(gate_t.T, 2, axis=1), uv)

    h3 = h.reshape(NB, Lp, D_MODEL)
    g_fin = g_final.reshape(1, -1)
    return _final(h3, g_fin, 0, nb_p), _final(h3, g_fin, nb_p, nb_s)
```

```python
import functools

import numpy as np
import jax
import jax.numpy as jnp
from jax import lax
from jax.experimental import pallas as pl
from jax.experimental.pallas import tpu as pltpu
from jax.experimental.pallas import tpu_sc as plsc

D_MODEL = 1024
N_META = 16
N_Q_HEADS = 16
N_KV_HEADS = 4
HEAD_DIM = 64
WINDOW = 128
ROT_DIM = HEAD_DIM // 4
ROPE_THETA = 500000.0
HG_HEADS = 8
HG_DIM = 128
PEER_HEADS = 8
PEER_NKEYS = 128
PEER_QDIM = 256
PEER_TOPK = 16
EPS = 1e-6
NEG = -1e30

BLK = 128
PAD_ROWS = BLK - N_META
IN_W = 8704
COL_AQ, COL_GA, COL_GH, COL_HQ, COL_ZF, COL_ZB, COL_HI, COL_HGG, COL_AK, COL_AV = (
    0, 1024, 2048, 3072, 4096, 5120, 6144, 7168, 8192, 8448)

VMEM_LIMIT = 56 * 1024 * 1024
PEER_TD = 16
PEER_TS = 16
SC_WINDOW = 64
PEER_SLOTS = PEER_HEADS * PEER_TOPK
HG_HP = 4
PEER_UNROLL = 2
PEER_BATCH = 8
SUBLANES = 8


def _cparams(sem):
    return pltpu.CompilerParams(dimension_semantics=sem, vmem_limit_bytes=VMEM_LIMIT)


def _rms(x, g):
    return x * lax.rsqrt(jnp.mean(x * x, axis=-1, keepdims=True) + EPS) * g


def _nt(a, b):
    return lax.dot_general(a, b, (((1,), (1,)), ((), ())), preferred_element_type=jnp.float32)


def _tn(a, b):
    return lax.dot_general(a, b, (((0,), (0,)), ((), ())), preferred_element_type=jnp.float32)


def _inproj_kernel(x_ref, g_ref, w_ref, o_ref):
    n = _rms(x_ref[...], g_ref[...]).astype(jnp.bfloat16)
    o_ref[...] = jnp.dot(n, w_ref[...], preferred_element_type=jnp.float32)


def _inproj(h2d, g, w, tm, tn):
    T = h2d.shape[0]
    return pl.pallas_call(
        _inproj_kernel,
        grid=(IN_W // tn, T // tm),
        in_specs=[pl.BlockSpec((tm, D_MODEL), lambda j, i: (i, 0)),
                  pl.BlockSpec((1, D_MODEL), lambda j, i: (0, 0)),
                  pl.BlockSpec((D_MODEL, tn), lambda j, i: (0, j))],
        out_specs=pl.BlockSpec((tm, tn), lambda j, i: (i, j)),
        out_shape=jax.ShapeDtypeStruct((T, IN_W), jnp.float32),
        compiler_params=_cparams(("parallel", "parallel")),
        name="inproj",
    )(h2d, g, w)


def _rope_tables(Lp):
    half = ROT_DIM // 2
    pos = jnp.maximum(jnp.arange(Lp) - PAD_ROWS, 0)
    inv = jnp.power(jnp.float32(ROPE_THETA), -jnp.arange(half, dtype=jnp.float32) * 2.0 / ROT_DIM)
    ang = pos.astype(jnp.float32)[:, None] * inv[None, :]
    cos, sin = jnp.cos(ang), jnp.sin(ang)
    ones = jnp.ones((Lp, HEAD_DIM - ROT_DIM), jnp.float32)
    zeros = jnp.zeros((Lp, HEAD_DIM - ROT_DIM), jnp.float32)
    zh = jnp.zeros((Lp, half), jnp.float32)
    c = jnp.concatenate([cos, cos, ones], axis=1)
    s1 = jnp.concatenate([zh, sin, zeros], axis=1)
    s2 = jnp.concatenate([-sin, zh, zeros], axis=1)
    rep = BLK // HEAD_DIM
    return tuple(jnp.tile(t, (1, rep)) for t in (c, s1, s2))


def _attn_kernel(sink_ref, q_ref, kg_ref, k0_ref, k1_ref, k2_ref, vg_ref, v0_ref, v1_ref, v2_ref,
                 c_ref, s1_ref, s2_ref, o_ref):
    r = pl.program_id(1)
    nblk = pl.num_programs(1)
    half = ROT_DIM // 2

    def rope(x, blk):
        row0 = blk * BLK if isinstance(blk, int) else pl.multiple_of(blk * BLK, BLK)
        c = c_ref[pl.ds(row0, BLK), :]
        s1 = s1_ref[pl.ds(row0, BLK), :]
        s2 = s2_ref[pl.ds(row0, BLK), :]
        outs = []
        for j in range(x.shape[1] // BLK):
            xj = x[:, j * BLK:(j + 1) * BLK]
            outs.append(xj * c + pltpu.roll(xj, half, 1) * s1 + pltpu.roll(xj, BLK - half, 1) * s2)
        return jnp.concatenate(outs, axis=1)

    kb = [jnp.clip(r - 1 + j, 0, nblk - 1) for j in range(3)]
    q = rope(q_ref[...], r) * (HEAD_DIM ** -0.5)
    kcat = jnp.concatenate([rope(kg_ref[...], 0), rope(k0_ref[...], kb[0]), rope(k1_ref[...], kb[1]),
                            rope(k2_ref[...], kb[2])], axis=0).astype(jnp.bfloat16)
    vcat = jnp.concatenate([vg_ref[...], v0_ref[...], v1_ref[...], v2_ref[...]], axis=0).astype(jnp.bfloat16)

    qrow = lax.broadcasted_iota(jnp.int32, (BLK, BLK), 0)
    kcol = lax.broadcasted_iota(jnp.int32, (BLK, BLK), 1)
    masks = [kcol >= PAD_ROWS]
    for j in range(3):
        kblk = r - 1 + j
        ok_blk = (kblk >= 1) & (kblk <= nblk - 1)
        masks.append((jnp.abs((j - 1) * BLK + kcol - qrow) <= WINDOW) & ok_blk)
    mask = jnp.concatenate(masks, axis=1)

    G = N_Q_HEADS // N_KV_HEADS
    lane_half = kcol // HEAD_DIM
    pairs_per_slice = (BLK // HEAD_DIM) * G // 2
    outs = [None] * (N_Q_HEADS // 2)
    for m2 in range(N_KV_HEADS * HEAD_DIM // BLK):
        k2 = kcat[:, m2 * BLK:(m2 + 1) * BLK]
        v2 = vcat[:, m2 * BLK:(m2 + 1) * BLK]
        qs, sks, halves = [], [], []
        for j in range(m2 * pairs_per_slice, (m2 + 1) * pairs_per_slice):
            hk = ((2 * j) // G) % 2
            in_half = lane_half == hk
            q2 = q[:, j * BLK:(j + 1) * BLK]
            qs += [jnp.where(in_half, q2, 0.0), jnp.where(in_half, pltpu.roll(q2, HEAD_DIM, 1), 0.0)]
            sks += [jnp.full((BLK, 1), sink_ref[2 * j + hk]), jnp.full((BLK, 1), sink_ref[2 * j + 1 - hk])]
            halves.append(in_half)
        nq = len(qs)
        sk = jnp.concatenate(sks, axis=0)
        s = _nt(jnp.concatenate(qs, axis=0).astype(jnp.bfloat16), k2)
        s = jnp.where(jnp.tile(mask, (nq, 1)), s, NEG)
        m = jnp.maximum(jnp.max(s, axis=-1, keepdims=True), sk)
        p = jnp.exp(s - m)
        p = p / (jnp.sum(p, axis=-1, keepdims=True) + jnp.exp(sk - m))
        o = jnp.dot(p.astype(jnp.bfloat16), v2, preferred_element_type=jnp.float32)
        for jj, in_half in enumerate(halves):
            o_same = o[(2 * jj) * BLK:(2 * jj + 1) * BLK]
            o_other = o[(2 * jj + 1) * BLK:(2 * jj + 2) * BLK]
            outs[m2 * pairs_per_slice + jj] = jnp.where(in_half, o_same, pltpu.roll(o_other, HEAD_DIM, 1))
    o_ref[...] = jnp.concatenate(outs, axis=1).astype(o_ref.dtype)


def _attention(proj3, sinks, tables):
    NB, Lp, _ = proj3.shape
    nblk = Lp // BLK
    kcol, vcol = COL_AK // 256, COL_AV // 256

    def loc(j, col):
        return pl.BlockSpec((None, BLK, 256),
                            lambda b, r: (b, jnp.clip(r - 1 + j, 0, nblk - 1), col))

    glob = lambda col: pl.BlockSpec((None, BLK, 256), lambda b, r: (b, 0, col))
    tab = pl.BlockSpec((Lp, BLK), lambda b, r: (0, 0))
    return pl.pallas_call(
        _attn_kernel,
        grid=(NB, nblk),
        in_specs=[pl.BlockSpec(memory_space=pltpu.SMEM),
                  pl.BlockSpec((None, BLK, 1024), lambda b, r: (b, r, 0)),
                  glob(kcol), loc(0, kcol), loc(1, kcol), loc(2, kcol),
                  glob(vcol), loc(0, vcol), loc(1, vcol), loc(2, vcol),
                  tab, tab, tab],
        out_specs=pl.BlockSpec((None, BLK, 1024), lambda b, r: (b, r, 0)),
        out_shape=jax.ShapeDtypeStruct((NB, Lp, 1024), jnp.bfloat16),
        compiler_params=_cparams(("parallel", "arbitrary")),
        name="window_attn",
    )(sinks, proj3, proj3, proj3, proj3, proj3, proj3, proj3, proj3, proj3, *tables)


HG_ROWS = BLK
_HG_LEVELS = (64, 32, 16, 8, 4, 2, 1)


def _hg_consts(reverse):
    C = HG_ROWS
    t = np.arange(C)[:, None]
    u = np.arange(C)[None, :]
    mats = [u <= t, u > t]
    ups, msks = [], []
    for B in _HG_LEVELS:
        same = (t // (2 * B)) == (u // (2 * B))
        tin, uin = t % (2 * B), u % (2 * B)
        upper = tin >= B
        wq = upper & same & (uin >= B) & (uin <= tin)
        wk = (~upper) & same & (uin > tin) & (uin <= B - 1)
        mats.append(wq | wk)
        ups.append(np.broadcast_to(upper, (C, HG_DIM)))
        msks.append(same & upper & (uin < B))
    msks.append(np.eye(C, dtype=bool))
    if reverse:
        mats = [m[::-1, ::-1] for m in mats]
        ups = [m[::-1] for m in ups]
        msks = [m[::-1, ::-1] for m in msks]
    W = np.concatenate(mats, axis=0).astype(np.float32)
    W2 = np.concatenate([W, W], axis=1)
    return (jnp.asarray(W2, jnp.bfloat16), jnp.asarray(np.stack(ups).astype(np.float32)),
            jnp.asarray(np.stack(msks).astype(np.float32)))


def _hg_chunk(q_raw, z, v, valid, lb, w2, up_ref, msk_ref, st, reverse):
    C = HG_ROWS
    f32, bf16 = jnp.float32, jnp.bfloat16
    log_lb = jnp.log(lb)
    log1m = jnp.log1p(-lb)
    logsig = jnp.minimum(z, 0.0) - jnp.log1p(jnp.exp(-jnp.abs(z)))
    a, b = log_lb, log1m + logsig
    g = jnp.maximum(a, b) + jnp.log1p(jnp.exp(-jnp.abs(a - b)))
    k = (1.0 - lb) * jax.nn.sigmoid(-z)
    q = q_raw * jax.nn.sigmoid(q_raw)
    q = jnp.where(valid, q, 0.0)
    k = jnp.where(valid, k, 0.0)
    v = jnp.where(valid, v, 0.0)
    g = jnp.where(valid, g, 0.0)
    g1 = g.astype(bf16)
    g2 = (g - g1.astype(f32)).astype(bf16)
    E = jnp.dot(w2, jnp.concatenate([g1, g2], axis=0), preferred_element_type=f32)
    b_inc = E[0:C]
    b_sfx = E[C:2 * C]
    b_tot = b_inc[0:1] if reverse else b_inc[C - 1:C]

    A = msk_ref[len(_HG_LEVELS)] * jnp.sum(q * k, axis=-1, keepdims=True)
    for l in range(len(_HG_LEVELS)):
        ex = jnp.exp(E[(2 + l) * C:(3 + l) * C])
        up = up_ref[l] > 0.5
        qs = jnp.where(up, q * ex, 0.0).astype(bf16)
        ks = jnp.where(up, 0.0, k * ex).astype(bf16)
        A = A + _nt(qs, ks) * msk_ref[l]

    vb = v.astype(bf16)
    o = _nt((q * jnp.exp(b_inc)).astype(bf16), st.astype(bf16))
    o = o + jnp.dot(A.astype(bf16), vb, preferred_element_type=f32)
    st = st * jnp.exp(b_tot) + _tn(vb, (k * jnp.exp(b_sfx)).astype(bf16))
    return o, st


def _hg_kernel(qf_ref, zf_ref, vf_ref, qb_ref, zb_ref, vb_ref, lb_ref, w2f_ref, upf_ref, mskf_ref,
               w2b_ref, upb_ref, mskb_ref, of_ref, ob_ref, st_ref):
    i = pl.program_id(2)
    nblk = pl.num_programs(2)

    @pl.when(i == 0)
    def _():
        st_ref[...] = jnp.zeros_like(st_ref)

    rows = lax.broadcasted_iota(jnp.int32, (HG_ROWS, HG_DIM), 0)
    dirs = ((False, qf_ref, zf_ref, vf_ref, w2f_ref, upf_ref, mskf_ref, of_ref, i),
            (True, qb_ref, zb_ref, vb_ref, w2b_ref, upb_ref, mskb_ref, ob_ref, nblk - 1 - i))
    for d, (reverse, q_ref, z_ref, v_ref, w2_ref, up_ref, msk_ref, o_ref, blk) in enumerate(dirs):
        w2 = w2_ref[...]
        for hh in range(HG_HP):
            cols = slice(hh * HG_DIM, (hh + 1) * HG_DIM)
            lb = lb_ref[d, :, cols]
            valid = (rows >= PAD_ROWS) | (blk > 0)
            o, st = _hg_chunk(q_ref[:, cols], z_ref[:, cols], v_ref[:, cols], valid, lb, w2,
                              up_ref, msk_ref, st_ref[d, hh], reverse)
            st_ref[d, hh] = st
            o_ref[:, cols] = o


def _hgrn(proj3, lb3, consts_f, consts_b):
    NB, Lp, _ = proj3.shape
    nblk = Lp // BLK
    W = HG_DIM * HG_HP

    def col(base, reverse):
        return pl.BlockSpec((None, BLK, W),
                            lambda b, h, i: (b, (nblk - 1 - i) if reverse else i, base // W + h))

    def const(c):
        return pl.BlockSpec(c.shape, lambda b, h, i: (0,) * c.ndim)

    in_specs = [col(COL_HQ, False), col(COL_ZF, False), col(COL_HI, False),
                col(COL_HQ, True), col(COL_ZB, True), col(COL_HI, True),
                pl.BlockSpec((2, 1, W), lambda b, h, i: (0, 0, h))]
    in_specs += [const(c) for c in consts_f] + [const(c) for c in consts_b]
    oshape = jax.ShapeDtypeStruct((NB, Lp, HG_HEADS * HG_DIM), jnp.float32)
    return pl.pallas_call(
        _hg_kernel,
        grid=(NB, HG_HEADS // HG_HP, nblk),
        in_specs=in_specs,
        out_specs=(pl.BlockSpec((None, BLK, W), lambda b, h, i: (b, i, h)),
                   pl.BlockSpec((None, BLK, W), lambda b, h, i: (b, nblk - 1 - i, h))),
        out_shape=(oshape, oshape),
        scratch_shapes=[pltpu.VMEM((2, HG_HP, HG_DIM, HG_DIM), jnp.float32)],
        compiler_params=_cparams(("parallel", "parallel", "arbitrary")),
        name="hgrn_bidir",
    )(proj3, proj3, proj3, proj3, proj3, proj3, lb3, *consts_f, *consts_b)


def _merge_kernel(a_ref, of_ref, ob_ref, hgg_ref, ng_ref, ga_ref, gh_ref, h_ref, wa_ref, wh_ref, wo_ref, o_ref):
    f32, bf16 = jnp.float32, jnp.bfloat16
    o = of_ref[...] + ob_ref[...]
    ng = ng_ref[...]
    gt = hgg_ref[...]
    hh = jnp.concatenate([_rms(o[:, c:c + HG_DIM], ng[:, c:c + HG_DIM]) for c in range(0, HG_HEADS * HG_DIM, HG_DIM)],
                         axis=1) * (gt * jax.nn.sigmoid(gt))
    pa = jnp.dot(a_ref[...], wa_ref[...], preferred_element_type=f32)
    ph = jnp.dot(hh.astype(bf16), wh_ref[...], preferred_element_type=f32)
    merged = jax.nn.sigmoid(ga_ref[...]) * pa + jax.nn.sigmoid(gh_ref[...]) * ph
    o_ref[...] = h_ref[...] + jnp.dot(merged.astype(bf16), wo_ref[...], preferred_element_type=f32)


def _merge(a2d, of2d, ob2d, ng, proj2d, h2d, wa, wh, wo, tm):
    T = h2d.shape[0]
    row = lambda c: pl.BlockSpec((tm, D_MODEL), lambda i: (i, c))
    wspec = pl.BlockSpec((D_MODEL, D_MODEL), lambda i: (0, 0))
    return pl.pallas_call(
        _merge_kernel,
        grid=(T // tm,),
        in_specs=[row(0), row(0), row(0), row(COL_HGG // D_MODEL), pl.BlockSpec((1, D_MODEL), lambda i: (0, 0)),
                  row(COL_GA // D_MODEL), row(COL_GH // D_MODEL), row(0), wspec, wspec, wspec],
        out_specs=row(0),
        out_shape=jax.ShapeDtypeStruct((T, D_MODEL), jnp.float32),
        input_output_aliases={7: 0},
        compiler_params=_cparams(("parallel",)),
        name="merge_out",
    )(a2d, of2d, ob2d, proj2d, ng, proj2d, proj2d, h2d, wa, wh, wo)


def _topk_axis0(s, kk):
    n = s.shape[0]
    iota = lax.broadcasted_iota(jnp.int32, s.shape, 0)
    vals, ids = [], []
    for _ in range(kk):
        m = jnp.max(s, axis=0, keepdims=True)
        am = jnp.min(jnp.where(s == m, iota, n), axis=0, keepdims=True)
        s = jnp.where(iota == am, -jnp.inf, s)
        vals.append(m)
        ids.append(am)
    return vals, ids


def _stack_rows(rows, dtype):
    n = len(rows)
    iota = lax.broadcasted_iota(jnp.int32, (n, rows[0].shape[1]), 0)
    out = jnp.zeros((n, rows[0].shape[1]), dtype)
    for j, rj in enumerate(rows):
        out = jnp.where(iota == j, rj, out)
    return out


def _route_kernel(h_ref, g_ref, wq_ref, key_ref, eidx_ref, gate_ref, xn_ref):
    @pl.when(pl.program_id(1) == 0)
    def _():
        xn_ref[...] = _rms(h_ref[...], g_ref[...]).astype(jnp.bfloat16)

    K = PEER_TOPK
    half = PEER_QDIM // 2
    i32 = jnp.int32
    qry = jnp.dot(xn_ref[...], wq_ref[...], preferred_element_type=jnp.float32).astype(jnp.bfloat16)
    s0 = _nt(key_ref[0], qry[:, :half])
    s1 = _nt(key_ref[1], qry[:, half:])
    tm = s0.shape[1]
    v0, i0 = _topk_axis0(s0, K)
    v1, i1 = _topk_axis0(s1, K)
    m1 = _stack_rows(v1, jnp.float32)
    m0_hi = _stack_rows(v0[K // 2:], jnp.float32)
    r8 = lax.broadcasted_iota(i32, (K // 2, tm), 0)
    cand = [v0[0] + m1] + [v0[a] + m1[:K // 2] for a in range(1, K // 2)] + [m0_hi + v1[0]]
    flat = [lax.broadcasted_iota(i32, (K, tm), 0)] + [a * K + r8 for a in range(1, K // 2)] + [(K // 2 + r8) * K]
    cand = jnp.concatenate(cand, axis=0)
    flat = jnp.concatenate(flat, axis=0)
    big = K * K
    ts, tp = [], []
    for _ in range(K):
        m = jnp.max(cand, axis=0, keepdims=True)
        p = jnp.min(jnp.where(cand == m, flat, big), axis=0, keepdims=True)
        cand = jnp.where(flat == p, -jnp.inf, cand)
        ts.append(m)
        tp.append(p)
    pk = _stack_rows(tp, i32)
    pa, pb = pk // K, pk % K
    e0 = jnp.zeros((K, tm), i32)
    e1 = jnp.zeros((K, tm), i32)
    for a in range(K):
        e0 = jnp.where(pa == a, i0[a], e0)
        e1 = jnp.where(pb == a, i1[a], e1)
    ex = [jnp.exp(t - ts[0]) for t in ts]
    den = ex[0]
    for e in ex[1:]:
        den = den + e
    eidx_ref[...] = e0 * PEER_NKEYS + e1
    gate_ref[...] = _stack_rows([e / den for e in ex], jnp.float32)


def _route(h2d, g, wq, keys, tm):
    T = h2d.shape[0]
    out = jax.ShapeDtypeStruct((PEER_SLOTS, T), jnp.int32), jax.ShapeDtypeStruct((PEER_SLOTS, T), jnp.float32)
    ospec = pl.BlockSpec((PEER_TOPK, tm), lambda i, h: (h, i))
    return pl.pallas_call(
        _route_kernel,
        grid=(T // tm, PEER_HEADS),
        in_specs=[pl.BlockSpec((tm, D_MODEL), lambda i, h: (i, 0)),
                  pl.BlockSpec((1, D_MODEL), lambda i, h: (0, 0)),
                  pl.BlockSpec((D_MODEL, PEER_QDIM), lambda i, h: (0, h)),
                  pl.BlockSpec((2, PEER_NKEYS, PEER_QDIM // 2), lambda i, h: (h, 0, 0))],
        out_specs=(ospec, ospec),
        out_shape=out,
        scratch_shapes=[pltpu.VMEM((tm, D_MODEL), jnp.bfloat16)],
        compiler_params=_cparams(("parallel", "arbitrary")),
        name="peer_route",
    )(h2d, g, wq, keys)


def _pack_bf16_pairs(x):
    e, d = x.shape
    bits = lax.bitcast_convert_type(x.astype(jnp.bfloat16), jnp.uint16).astype(jnp.uint32)
    bits = bits.reshape(e, d // 256, 2, 128)
    return bits[:, :, 0, :] | (bits[:, :, 1, :] << 16)


def _sc_gather(uv, idx):
    M = idx.shape[0]
    info = plsc.get_sparse_core_info()
    nw = info.num_cores * info.num_subcores
    per = M // nw
    assert per * nw == M and per % SC_WINDOW == 0
    mesh = plsc.VectorSubcoreMesh(core_axis_name="c", subcore_axis_name="s")

    @functools.partial(
        pl.kernel, mesh=mesh,
        out_type=jax.ShapeDtypeStruct((M,) + uv.shape[1:], uv.dtype),
        scratch_types=[pltpu.VMEM((SC_WINDOW,), jnp.int32), pltpu.VMEM((SC_WINDOW,) + uv.shape[1:], uv.dtype),
                       pltpu.SemaphoreType.DMA],
    )
    def gather(uv_hbm, idx_hbm, out_hbm, idx_v, rows_v, sem):
        wid = lax.axis_index("s") * info.num_cores + lax.axis_index("c")
        base = wid * per

        @pl.loop(0, per // SC_WINDOW)
        def _(g):
            off = pl.multiple_of(base + g * SC_WINDOW, SC_WINDOW)
            pltpu.sync_copy(idx_hbm.at[pl.ds(off, SC_WINDOW)], idx_v)
            pltpu.async_copy(uv_hbm.at[idx_v], rows_v, sem).wait()
            pltpu.sync_copy(rows_v, out_hbm.at[pl.ds(off, SC_WINDOW)])

    return gather(uv, idx)


def _expert_kernel(idx_ref, idxn_ref, h_ref, g_ref, gate_ref, stag_ref, uv_hbm, o_ref, buf, z_ref, xn_ref, sem):
    TD, TS, NS = PEER_TD, PEER_TS, PEER_SLOTS
    TT, LW = TD + TS, 16 * PEER_SLOTS
    f32, bf16 = jnp.float32, jnp.bfloat16
    i = pl.program_id(0)
    n = pl.num_programs(0)
    slot = lax.rem(i, 2)

    def issue_token(iref, sl, t):
        for k in range(NS):
            pltpu.make_async_copy(uv_hbm.at[iref[t, k]], buf.at[sl, t * NS + k], sem.at[sl]).start(priority=k % 2)

    @pl.when(i == 0)
    def _():
        def body(t, c):
            issue_token(idx_ref, 0, t)
            return c
        lax.fori_loop(0, TD, body, 0)

    pltpu.make_async_copy(buf.at[slot], buf.at[slot], sem.at[slot]).wait()

    x = h_ref[...]
    ms = jnp.mean(jnp.mean(x * x, axis=2, keepdims=True), axis=1, keepdims=True)
    xn_ref[...] = x * lax.rsqrt(ms + EPS) * g_ref[...]
    q_of_lane = lax.broadcasted_iota(jnp.int32, (SUBLANES, LW), 1) % 16
    row = lax.broadcasted_iota(jnp.int32, (SUBLANES, LW), 0)
    m_u = q_of_lane == row
    m_v = q_of_lane == row + SUBLANES

    def experts_of(t, staged):
        r = pl.ds(pl.multiple_of(t * NS, NS), NS)
        tile = stag_ref[r] if staged else buf[slot, r]
        return pltpu.bitcast(tile.reshape(NS * SUBLANES, 128), bf16)

    def score_token(t, staged):
        tg = t + TD if staged else t
        z = _nt(xn_ref[tg].astype(bf16), experts_of(t, staged))
        z_ref[pl.ds(tg, 1), :] = jnp.sum(jnp.where(m_u, z, 0.0), axis=0, keepdims=True)

    def mix_token(t, staged):
        tg = t + TD if staged else t
        w = jnp.where(m_v, jnp.broadcast_to(z_ref[pl.ds(tg, 1), :], (SUBLANES, LW)), 0.0).astype(bf16)
        o_ref[tg] = h_ref[tg] + jnp.dot(w, experts_of(t, staged), preferred_element_type=f32)

    def batched(count, fn):
        def body(tt, c):
            for j in range(PEER_BATCH):
                fn(tt * PEER_BATCH + j)
            return c
        lax.fori_loop(0, count // PEER_BATCH, body, 0)

    @pl.when(i + 1 < n)
    def _():
        def body(tt, c):
            for j in range(PEER_UNROLL):
                score_token(tt * PEER_UNROLL + j, False)
            for j in range(PEER_UNROLL):
                issue_token(idxn_ref, 1 - slot, tt * PEER_UNROLL + j)
            return c
        lax.fori_loop(0, TD // PEER_UNROLL, body, 0)

    @pl.when(i + 1 == n)
    def _():
        batched(TD, lambda t: score_token(t, False))

    batched(TS, lambda t: score_token(t, True))

    z = z_ref[...]
    z = z + pltpu.roll(z, 4, 1)
    z = z + pltpu.roll(z, 2, 1)
    z = z + pltpu.roll(z, 1, 1)
    lane_q = lax.broadcasted_iota(jnp.int32, (TT, LW), 1) % 16
    w = jnp.where(lane_q == SUBLANES - 1, gate_ref[...] * (0.5 * z * (1.0 + lax.erf(z * (2.0 ** -0.5)))), 0.0)
    s = w + pltpu.roll(w, 1, 1)
    s = s + pltpu.roll(s, 2, 1)
    s = s + pltpu.roll(s, 4, 1)
    z_ref[...] = s + pltpu.roll(w, 8, 1)

    batched(TD, lambda t: mix_token(t, False))
    batched(TS, lambda t: mix_token(t, True))


def _experts_chunk(step0, nsteps, idx_d, h3, g8, gate16, stag, uv):
    TD, TS, NS = PEER_TD, PEER_TS, PEER_SLOTS
    TT = TD + TS
    last = step0 + nsteps - 1
    return pl.pallas_call(
        _expert_kernel,
        grid=(nsteps,),
        in_specs=[pl.BlockSpec((TD, NS), lambda i: (step0 + i, 0), memory_space=pltpu.SMEM),
                  pl.BlockSpec((TD, NS), lambda i: (jnp.minimum(step0 + i + 1, last), 0), memory_space=pltpu.SMEM),
                  pl.BlockSpec((TT, SUBLANES, 128), lambda i: (step0 + i, 0, 0)),
                  pl.BlockSpec((SUBLANES, 128), lambda i: (0, 0)),
                  pl.BlockSpec((TT, 16 * NS), lambda i: (step0 + i, 0)),
                  pl.BlockSpec((TS * NS, SUBLANES, 128), lambda i: (i, 0, 0)),
                  pl.BlockSpec(memory_space=pl.ANY)],
        out_specs=pl.BlockSpec((TT, SUBLANES, 128), lambda i: (step0 + i, 0, 0)),
        out_shape=jax.ShapeDtypeStruct(h3.shape, jnp.float32),
        scratch_shapes=[pltpu.VMEM((2, TD * NS, SUBLANES, 128), jnp.uint32),
                        pltpu.VMEM((TT, 16 * NS), jnp.float32),
                        pltpu.VMEM((TT, SUBLANES, 128), jnp.float32),
                        pltpu.SemaphoreType.DMA((2,))],
        input_output_aliases={2: 0},
        compiler_params=_cparams(("arbitrary",)),
        name="peer_experts",
    )(idx_d, idx_d, h3, g8, gate16, stag, uv)


def _experts(h2d, g, eidx, gate, uv):
    T = h2d.shape[0]
    TD, TS, NS = PEER_TD, PEER_TS, PEER_SLOTS
    TT = TD + TS
    steps = T // TT
    chunks = _pick_tile(steps, (12, 8, 6, 4, 3, 2, 1))
    spc = steps // chunks
    e3 = eidx.reshape(steps, TT, NS)
    idx_d = e3[:, :TD].reshape(steps * TD, NS)
    idx_s = e3[:, TD:].reshape(chunks, spc * TS * NS)
    gate16 = jnp.repeat(gate, 16, axis=1)
    h3 = h2d.reshape(T, SUBLANES, 128)
    g8 = g.reshape(SUBLANES, 128)
    for c in range(chunks):
        stag = _sc_gather(uv, idx_s[c])
        h3 = _experts_chunk(c * spc, spc, idx_d, h3, g8, gate16, stag, uv)
    return h3.reshape(T, D_MODEL)


def _final_kernel(h_ref, g_ref, o_ref):
    o_ref[...] = _rms(h_ref[...], g_ref[...])


def _final(h3, g, b0, nb):
    _, Lp, _ = h3.shape
    S = Lp - BLK
    return pl.pallas_call(
        _final_kernel,
        grid=(nb, S // BLK),
        in_specs=[pl.BlockSpec((None, BLK, D_MODEL), lambda b, r: (b + b0, r + 1, 0)),
                  pl.BlockSpec((1, D_MODEL), lambda b, r: (0, 0))],
        out_specs=pl.BlockSpec((None, BLK, D_MODEL), lambda b, r: (b, r, 0)),
        out_shape=jax.ShapeDtypeStruct((nb, S, D_MODEL), jnp.float32),
        compiler_params=_cparams(("parallel", "parallel")),
        name="final_norm",
    )(h3, g)


def _pick_tile(T, candidates):
    for c in candidates:
        if T % c == 0:
            return c
    raise ValueError(f"no tile for {T}")


def kernel(x_prompt, x_sample, meta_tokens, g_mix, w_in, attn_sinks, lb_logits, hg_norm_g, w_att_branch,
           w_hg_branch, w_out, g_ffn, w_peer_q, peer_sub_keys, peer_u, peer_v, g_final):
    f32, bf16 = jnp.float32, jnp.bfloat16
    depth = w_in.shape[0]
    nb_p, S, _ = x_prompt.shape
    nb_s = x_sample.shape[0]
    assert x_sample.shape[1] == S and S % BLK == 0
    NB = nb_p + nb_s
    Lp = S + BLK
    T = NB * Lp

    x = jnp.concatenate([x_prompt, x_sample], axis=0)
    head = jnp.concatenate([jnp.zeros((PAD_ROWS, D_MODEL), f32), meta_tokens.astype(f32)], axis=0)
    h = jnp.concatenate([jnp.broadcast_to(head[None], (NB, BLK, D_MODEL)), x], axis=1).reshape(T, D_MODEL)

    sizes = (1024, 256, 256, 1024, 1024, 1024, 1024, 1024, 1024, 1024)
    offs = np.concatenate([[0], np.cumsum(sizes)])
    order = (0, 8, 9, 3, 4, 5, 6, 7, 1, 2)
    perm = np.concatenate([np.arange(offs[k], offs[k + 1]) for k in order])

    sm = jax.nn.softmax(lb_logits.astype(f32), axis=0)
    lb_all = jnp.maximum(jnp.cumsum(sm, axis=0) - sm[0:1], 0.0)

    tables = _rope_tables(Lp)
    consts_f = _hg_consts(False)
    consts_b = _hg_consts(True)
    tm_big = _pick_tile(T, (512, 384, 256, 128))
    tm_route = _pick_tile(T, (256, 128))

    for l in range(depth):
        w_l = w_in[l][:, perm].astype(bf16)
        proj = _inproj(h, g_mix[l].reshape(1, -1), w_l, tm_big, IN_W // 4)
        proj3 = proj.reshape(NB, Lp, IN_W)
        a_out = _attention(proj3, attn_sinks[l].astype(f32), tables)
        lb3 = lb_all[l].reshape(2, 1, -1)
        o_fwd, o_bwd = _hgrn(proj3, lb3, consts_f, consts_b)
        h = _merge(a_out.reshape(T, -1), o_fwd.reshape(T, -1), o_bwd.reshape(T, -1), hg_norm_g[l].reshape(1, -1),
                   proj, h, w_att_branch[l].astype(bf16), w_hg_branch[l].astype(bf16), w_out[l].astype(bf16),
                   tm_route)
        g_f = g_ffn[l].reshape(1, -1)
        keys = peer_sub_keys[l].reshape(2 * PEER_HEADS, PEER_NKEYS, PEER_QDIM // 2).astype(bf16)
        eidx_t, gate_t = _route(h, g_f, w_peer_q[l].astype(bf16), keys, tm_route)
        uv = jnp.concatenate([_pack_bf16_pairs(peer_u[l]), _pack_bf16_pairs(peer_v[l])], axis=1)
        h = _experts(h, g_f, eidx_t.T, gate_t.T, uv)

    h3 = h.reshape(NB, Lp, D_MODEL)
    g_fin = g_final.reshape(1, -1)
    return _final(h3, g_fin, 0, nb_p), _final(h3, g_fin, nb_p, nb_s)
```

```python
import functools

import numpy as np
import jax
import jax.numpy as jnp
from jax import lax
from jax.experimental import pallas as pl
from jax.experimental.pallas import tpu as pltpu
from jax.experimental.pallas import tpu_sc as plsc

D_MODEL = 1024
N_META = 16
N_Q_HEADS = 16
N_KV_HEADS = 4
HEAD_DIM = 64
WINDOW = 128
ROT_DIM = HEAD_DIM // 4
ROPE_THETA = 500000.0
HG_HEADS = 8
HG_DIM = 128
PEER_HEADS = 8
PEER_NKEYS = 128
PEER_QDIM = 256
PEER_TOPK = 16
EPS = 1e-6
NEG = -1e30

BLK = 128
PAD_ROWS = BLK - N_META
IN_W = 8704
COL_AQ, COL_GA, COL_GH, COL_HQ, COL_ZF, COL_ZB, COL_HI, COL_HGG, COL_AK, COL_AV = (
    0, 1024, 2048, 3072, 4096, 5120, 6144, 7168, 8192, 8448)

VMEM_LIMIT = 56 * 1024 * 1024
PEER_TD = 16
PEER_TS = 16
SC_WINDOW = 64
PEER_SLOTS = PEER_HEADS * PEER_TOPK
HG_HP = 4
PEER_UNROLL = 2
PEER_BATCH = 8
SUBLANES = 8


def _cparams(sem):
    return pltpu.CompilerParams(dimension_semantics=sem, vmem_limit_bytes=VMEM_LIMIT)


def _rms(x, g):
    return x * lax.rsqrt(jnp.mean(x * x, axis=-1, keepdims=True) + EPS) * g


def _nt(a, b):
    return lax.dot_general(a, b, (((1,), (1,)), ((), ())), preferred_element_type=jnp.float32)


def _tn(a, b):
    return lax.dot_general(a, b, (((0,), (0,)), ((), ())), preferred_element_type=jnp.float32)


def _inproj_kernel(x_ref, g_ref, w_ref, o_ref):
    n = _rms(x_ref[...], g_ref[...]).astype(jnp.bfloat16)
    o_ref[...] = jnp.dot(n, w_ref[...], preferred_element_type=jnp.float32)


def _inproj(h2d, g, w, tm, tn):
    T = h2d.shape[0]
    return pl.pallas_call(
        _inproj_kernel,
        grid=(IN_W // tn, T // tm),
        in_specs=[pl.BlockSpec((tm, D_MODEL), lambda j, i: (i, 0)),
                  pl.BlockSpec((1, D_MODEL), lambda j, i: (0, 0)),
                  pl.BlockSpec((D_MODEL, tn), lambda j, i: (0, j))],
        out_specs=pl.BlockSpec((tm, tn), lambda j, i: (i, j)),
        out_shape=jax.ShapeDtypeStruct((T, IN_W), jnp.float32),
        compiler_params=_cparams(("parallel", "parallel")),
        name="inproj",
    )(h2d, g, w)


def _rope_tables(Lp):
    half = ROT_DIM // 2
    pos = jnp.maximum(jnp.arange(Lp) - PAD_ROWS, 0)
    inv = jnp.power(jnp.float32(ROPE_THETA), -jnp.arange(half, dtype=jnp.float32) * 2.0 / ROT_DIM)
    ang = pos.astype(jnp.float32)[:, None] * inv[None, :]
    cos, sin = jnp.cos(ang), jnp.sin(ang)
    ones = jnp.ones((Lp, HEAD_DIM - ROT_DIM), jnp.float32)
    zeros = jnp.zeros((Lp, HEAD_DIM - ROT_DIM), jnp.float32)
    zh = jnp.zeros((Lp, half), jnp.float32)
    c = jnp.concatenate([cos, cos, ones], axis=1)
    s1 = jnp.concatenate([zh, sin, zeros], axis=1)
    s2 = jnp.concatenate([-sin, zh, zeros], axis=1)
    rep = BLK // HEAD_DIM
    return tuple(jnp.tile(t, (1, rep)) for t in (c, s1, s2))


def _attn_kernel(sink_ref, q_ref, kg_ref, k0_ref, k1_ref, k2_ref, vg_ref, v0_ref, v1_ref, v2_ref,
                 c_ref, s1_ref, s2_ref, o_ref):
    r = pl.program_id(1)
    nblk = pl.num_programs(1)
    half = ROT_DIM // 2

    def rope(x, blk):
        row0 = blk * BLK if isinstance(blk, int) else pl.multiple_of(blk * BLK, BLK)
        c = c_ref[pl.ds(row0, BLK), :]
        s1 = s1_ref[pl.ds(row0, BLK), :]
        s2 = s2_ref[pl.ds(row0, BLK), :]
        outs = []
        for j in range(x.shape[1] // BLK):
            xj = x[:, j * BLK:(j + 1) * BLK]
            outs.append(xj * c + pltpu.roll(xj, half, 1) * s1 + pltpu.roll(xj, BLK - half, 1) * s2)
        return jnp.concatenate(outs, axis=1)

    kb = [jnp.clip(r - 1 + j, 0, nblk - 1) for j in range(3)]
    q = rope(q_ref[...], r) * (HEAD_DIM ** -0.5)
    kcat = jnp.concatenate([rope(kg_ref[...], 0), rope(k0_ref[...], kb[0]), rope(k1_ref[...], kb[1]),
                            rope(k2_ref[...], kb[2])], axis=0).astype(jnp.bfloat16)
    vcat = jnp.concatenate([vg_ref[...], v0_ref[...], v1_ref[...], v2_ref[...]], axis=0).astype(jnp.bfloat16)

    qrow = lax.broadcasted_iota(jnp.int32, (BLK, BLK), 0)
    kcol = lax.broadcasted_iota(jnp.int32, (BLK, BLK), 1)
    masks = [kcol >= PAD_ROWS]
    for j in range(3):
        kblk = r - 1 + j
        ok_blk = (kblk >= 1) & (kblk <= nblk - 1)
        masks.append((jnp.abs((j - 1) * BLK + kcol - qrow) <= WINDOW) & ok_blk)
    mask = jnp.concatenate(masks, axis=1)

    G = N_Q_HEADS // N_KV_HEADS
    lane_half = kcol // HEAD_DIM
    pairs_per_slice = (BLK // HEAD_DIM) * G // 2
    outs = [None] * (N_Q_HEADS // 2)
    for m2 in range(N_KV_HEADS * HEAD_DIM // BLK):
        k2 = kcat[:, m2 * BLK:(m2 + 1) * BLK]
        v2 = vcat[:, m2 * BLK:(m2 + 1) * BLK]
        qs, sks, halves = [], [], []
        for j in range(m2 * pairs_per_slice, (m2 + 1) * pairs_per_slice):
            hk = ((2 * j) // G) % 2
            in_half = lane_half == hk
            q2 = q[:, j * BLK:(j + 1) * BLK]
            qs += [jnp.where(in_half, q2, 0.0), jnp.where(in_half, pltpu.roll(q2, HEAD_DIM, 1), 0.0)]
            sks += [jnp.full((BLK, 1), sink_ref[2 * j + hk]), jnp.full((BLK, 1), sink_ref[2 * j + 1 - hk])]
            halves.append(in_half)
        nq = len(qs)
        sk = jnp.concatenate(sks, axis=0)
        s = _nt(jnp.concatenate(qs, axis=0).astype(jnp.bfloat16), k2)
        s = jnp.where(jnp.tile(mask, (nq, 1)), s, NEG)
        m = jnp.maximum(jnp.max(s, axis=-1, keepdims=True), sk)
        p = jnp.exp(s - m)
        p = p / (jnp.sum(p, axis=-1, keepdims=True) + jnp.exp(sk - m))
        o = jnp.dot(p.astype(jnp.bfloat16), v2, preferred_element_type=jnp.float32)
        for jj, in_half in enumerate(halves):
            o_same = o[(2 * jj) * BLK:(2 * jj + 1) * BLK]
            o_other = o[(2 * jj + 1) * BLK:(2 * jj + 2) * BLK]
            outs[m2 * pairs_per_slice + jj] = jnp.where(in_half, o_same, pltpu.roll(o_other, HEAD_DIM, 1))
    o_ref[...] = jnp.concatenate(outs, axis=1).astype(o_ref.dtype)


def _attention(proj3, sinks, tables):
    NB, Lp, _ = proj3.shape
    nblk = Lp // BLK
    kcol, vcol = COL_AK // 256, COL_AV // 256

    def loc(j, col):
        return pl.BlockSpec((None, BLK, 256),
                            lambda b, r: (b, jnp.clip(r - 1 + j, 0, nblk - 1), col))

    glob = lambda col: pl.BlockSpec((None, BLK, 256), lambda b, r: (b, 0, col))
    tab = pl.BlockSpec((Lp, BLK), lambda b, r: (0, 0))
    return pl.pallas_call(
        _attn_kernel,
        grid=(NB, nblk),
        in_specs=[pl.BlockSpec(memory_space=pltpu.SMEM),
                  pl.BlockSpec((None, BLK, 1024), lambda b, r: (b, r, 0)),
                  glob(kcol), loc(0, kcol), loc(1, kcol), loc(2, kcol),
                  glob(vcol), loc(0, vcol), loc(1, vcol), loc(2, vcol),
                  tab, tab, tab],
        out_specs=pl.BlockSpec((None, BLK, 1024), lambda b, r: (b, r, 0)),
        out_shape=jax.ShapeDtypeStruct((NB, Lp, 1024), jnp.bfloat16),
        compiler_params=_cparams(("parallel", "arbitrary")),
        name="window_attn",
    )(sinks, proj3, proj3, proj3, proj3, proj3, proj3, proj3, proj3, proj3, *tables)


HG_ROWS = BLK
_HG_LEVELS = (64, 32, 16, 8, 4, 2, 1)


def _hg_consts(reverse):
    C = HG_ROWS
    t = np.arange(C)[:, None]
    u = np.arange(C)[None, :]
    mats = [u <= t, u > t]
    ups, msks = [], []
    for B in _HG_LEVELS:
        same = (t // (2 * B)) == (u // (2 * B))
        tin, uin = t % (2 * B), u % (2 * B)
        upper = tin >= B
        wq = upper & same & (uin >= B) & (uin <= tin)
        wk = (~upper) & same & (uin > tin) & (uin <= B - 1)
        mats.append(wq | wk)
        ups.append(np.broadcast_to(upper, (C, HG_DIM)))
        msks.append(same & upper & (uin < B))
    msks.append(np.eye(C, dtype=bool))
    if reverse:
        mats = [m[::-1, ::-1] for m in mats]
        ups = [m[::-1] for m in ups]
        msks = [m[::-1, ::-1] for m in msks]
    W = np.concatenate(mats, axis=0).astype(np.float32)
    W2 = np.concatenate([W, W], axis=1)
    return (jnp.asarray(W2, jnp.bfloat16), jnp.asarray(np.stack(ups).astype(np.float32)),
            jnp.asarray(np.stack(msks).astype(np.float32)))


def _hg_chunk(q_raw, z, v, valid, lb, w2, up_ref, msk_ref, st, reverse):
    C = HG_ROWS
    f32, bf16 = jnp.float32, jnp.bfloat16
    log_lb = jnp.log(lb)
    log1m = jnp.log1p(-lb)
    logsig = jnp.minimum(z, 0.0) - jnp.log1p(jnp.exp(-jnp.abs(z)))
    a, b = log_lb, log1m + logsig
    g = jnp.maximum(a, b) + jnp.log1p(jnp.exp(-jnp.abs(a - b)))
    k = (1.0 - lb) * jax.nn.sigmoid(-z)
    q = q_raw * jax.nn.sigmoid(q_raw)
    q = jnp.where(valid, q, 0.0)
    k = jnp.where(valid, k, 0.0)
    v = jnp.where(valid, v, 0.0)
    g = jnp.where(valid, g, 0.0)
    g1 = g.astype(bf16)
    g2 = (g - g1.astype(f32)).astype(bf16)
    E = jnp.dot(w2, jnp.concatenate([g1, g2], axis=0), preferred_element_type=f32)
    b_inc = E[0:C]
    b_sfx = E[C:2 * C]
    b_tot = b_inc[0:1] if reverse else b_inc[C - 1:C]

    A = msk_ref[len(_HG_LEVELS)] * jnp.sum(q * k, axis=-1, keepdims=True)
    for l in range(len(_HG_LEVELS)):
        ex = jnp.exp(E[(2 + l) * C:(3 + l) * C])
        up = up_ref[l] > 0.5
        qs = jnp.where(up, q * ex, 0.0).astype(bf16)
        ks = jnp.where(up, 0.0, k * ex).astype(bf16)
        A = A + _nt(qs, ks) * msk_ref[l]

    vb = v.astype(bf16)
    o = _nt((q * jnp.exp(b_inc)).astype(bf16), st.astype(bf16))
    o = o + jnp.dot(A.astype(bf16), vb, preferred_element_type=f32)
    st = st * jnp.exp(b_tot) + _tn(vb, (k * jnp.exp(b_sfx)).astype(bf16))
    return o, st


def _hg_kernel(qf_ref, zf_ref, vf_ref, qb_ref, zb_ref, vb_ref, lb_ref, w2f_ref, upf_ref, mskf_ref,
               w2b_ref, upb_ref, mskb_ref, of_ref, ob_ref, st_ref):
    i = pl.program_id(2)
    nblk = pl.num_programs(2)

    @pl.when(i == 0)
    def _():
        st_ref[...] = jnp.zeros_like(st_ref)

    rows = lax.broadcasted_iota(jnp.int32, (HG_ROWS, HG_DIM), 0)
    dirs = ((False, qf_ref, zf_ref, vf_ref, w2f_ref, upf_ref, mskf_ref, of_ref, i),
            (True, qb_ref, zb_ref, vb_ref, w2b_ref, upb_ref, mskb_ref, ob_ref, nblk - 1 - i))
    for d, (reverse, q_ref, z_ref, v_ref, w2_ref, up_ref, msk_ref, o_ref, blk) in enumerate(dirs):
        w2 = w2_ref[...]
        for hh in range(HG_HP):
            cols = slice(hh * HG_DIM, (hh + 1) * HG_DIM)
            lb = lb_ref[d, :, cols]
            valid = (rows >= PAD_ROWS) | (blk > 0)
            o, st = _hg_chunk(q_ref[:, cols], z_ref[:, cols], v_ref[:, cols], valid, lb, w2,
                              up_ref, msk_ref, st_ref[d, hh], reverse)
            st_ref[d, hh] = st
            o_ref[:, cols] = o


def _hgrn(proj3, lb3, consts_f, consts_b):
    NB, Lp, _ = proj3.shape
    nblk = Lp // BLK
    W = HG_DIM * HG_HP

    def col(base, reverse):
        return pl.BlockSpec((None, BLK, W),
                            lambda b, h, i: (b, (nblk - 1 - i) if reverse else i, base // W + h))

    def const(c):
        return pl.BlockSpec(c.shape, lambda b, h, i: (0,) * c.ndim)

    in_specs = [col(COL_HQ, False), col(COL_ZF, False), col(COL_HI, False),
                col(COL_HQ, True), col(COL_ZB, True), col(COL_HI, True),
                pl.BlockSpec((2, 1, W), lambda b, h, i: (0, 0, h))]
    in_specs += [const(c) for c in consts_f] + [const(c) for c in consts_b]
    oshape = jax.ShapeDtypeStruct((NB, Lp, HG_HEADS * HG_DIM), jnp.float32)
    return pl.pallas_call(
        _hg_kernel,
        grid=(NB, HG_HEADS // HG_HP, nblk),
        in_specs=in_specs,
        out_specs=(pl.BlockSpec((None, BLK, W), lambda b, h, i: (b, i, h)),
                   pl.BlockSpec((None, BLK, W), lambda b, h, i: (b, nblk - 1 - i, h))),
        out_shape=(oshape, oshape),
        scratch_shapes=[pltpu.VMEM((2, HG_HP, HG_DIM, HG_DIM), jnp.float32)],
        compiler_params=_cparams(("parallel", "parallel", "arbitrary")),
        name="hgrn_bidir",
    )(proj3, proj3, proj3, proj3, proj3, proj3, lb3, *consts_f, *consts_b)


def _merge_kernel(a_ref, of_ref, ob_ref, hgg_ref, ng_ref, ga_ref, gh_ref, h_ref, wa_ref, wh_ref, wo_ref, o_ref):
    f32, bf16 = jnp.float32, jnp.bfloat16
    o = of_ref[...] + ob_ref[...]
    ng = ng_ref[...]
    gt = hgg_ref[...]
    hh = jnp.concatenate([_rms(o[:, c:c + HG_DIM], ng[:, c:c + HG_DIM]) for c in range(0, HG_HEADS * HG_DIM, HG_DIM)],
                         axis=1) * (gt * jax.nn.sigmoid(gt))
    pa = jnp.dot(a_ref[...], wa_ref[...], preferred_element_type=f32)
    ph = jnp.dot(hh.astype(bf16), wh_ref[...], preferred_element_type=f32)
    merged = jax.nn.sigmoid(ga_ref[...]) * pa + jax.nn.sigmoid(gh_ref[...]) * ph
    o_ref[...] = h_ref[...] + jnp.dot(merged.astype(bf16), wo_ref[...], preferred_element_type=f32)


def _merge(a2d, of2d, ob2d, ng, proj2d, h2d, wa, wh, wo, tm):
    T = h2d.shape[0]
    row = lambda c: pl.BlockSpec((tm, D_MODEL), lambda i: (i, c))
    wspec = pl.BlockSpec((D_MODEL, D_MODEL), lambda i: (0, 0))
    return pl.pallas_call(
        _merge_kernel,
        grid=(T // tm,),
        in_specs=[row(0), row(0), row(0), row(COL_HGG // D_MODEL), pl.BlockSpec((1, D_MODEL), lambda i: (0, 0)),
                  row(COL_GA // D_MODEL), row(COL_GH // D_MODEL), row(0), wspec, wspec, wspec],
        out_specs=row(0),
        out_shape=jax.ShapeDtypeStruct((T, D_MODEL), jnp.float32),
        input_output_aliases={7: 0},
        compiler_params=_cparams(("parallel",)),
        name="merge_out",
    )(a2d, of2d, ob2d, proj2d, ng, proj2d, proj2d, h2d, wa, wh, wo)


def _topk_axis0(s, kk):
    n = s.shape[0]
    iota = lax.broadcasted_iota(jnp.int32, s.shape, 0)
    vals, ids = [], []
    for _ in range(kk):
        m = jnp.max(s, axis=0, keepdims=True)
        am = jnp.min(jnp.where(s == m, iota, n), axis=0, keepdims=True)
        s = jnp.where(iota == am, -jnp.inf, s)
        vals.append(m)
        ids.append(am)
    return vals, ids


def _stack_rows(rows, dtype):
    n = len(rows)
    iota = lax.broadcasted_iota(jnp.int32, (n, rows[0].shape[1]), 0)
    out = jnp.zeros((n, rows[0].shape[1]), dtype)
    for j, rj in enumerate(rows):
        out = jnp.where(iota == j, rj, out)
    return out


def _route_kernel(h_ref, g_ref, wq_ref, key_ref, eidx_ref, gate_ref, xn_ref):
    @pl.when(pl.program_id(1) == 0)
    def _():
        xn_ref[...] = _rms(h_ref[...], g_ref[...]).astype(jnp.bfloat16)

    K = PEER_TOPK
    half = PEER_QDIM // 2
    i32 = jnp.int32
    qry = jnp.dot(xn_ref[...], wq_ref[...], preferred_element_type=jnp.float32).astype(jnp.bfloat16)
    s0 = _nt(key_ref[0], qry[:, :half])
    s1 = _nt(key_ref[1], qry[:, half:])
    tm = s0.shape[1]
    v0, i0 = _topk_axis0(s0, K)
    v1, i1 = _topk_axis0(s1, K)
    m1 = _stack_rows(v1, jnp.float32)
    m0_hi = _stack_rows(v0[K // 2:], jnp.float32)
    r8 = lax.broadcasted_iota(i32, (K // 2, tm), 0)
    cand = [v0[0] + m1] + [v0[a] + m1[:K // 2] for a in range(1, K // 2)] + [m0_hi + v1[0]]
    flat = [lax.broadcasted_iota(i32, (K, tm), 0)] + [a * K + r8 for a in range(1, K // 2)] + [(K // 2 + r8) * K]
    cand = jnp.concatenate(cand, axis=0)
    flat = jnp.concatenate(flat, axis=0)
    big = K * K
    ts, tp = [], []
    for _ in range(K):
        m = jnp.max(cand, axis=0, keepdims=True)
        p = jnp.min(jnp.where(cand == m, flat, big), axis=0, keepdims=True)
        cand = jnp.where(flat == p, -jnp.inf, cand)
        ts.append(m)
        tp.append(p)
    pk = _stack_rows(tp, i32)
    pa, pb = pk // K, pk % K
    e0 = jnp.zeros((K, tm), i32)
    e1 = jnp.zeros((K, tm), i32)
    for a in range(K):
        e0 = jnp.where(pa == a, i0[a], e0)
        e1 = jnp.where(pb == a, i1[a], e1)
    ex = [jnp.exp(t - ts[0]) for t in ts]
    den = ex[0]
    for e in ex[1:]:
        den = den + e
    eidx_ref[...] = e0 * PEER_NKEYS + e1
    gate_ref[...] = _stack_rows([e / den for e in ex], jnp.float32)


def _route(h2d, g, wq, keys, tm):
    T = h2d.shape[0]
    out = jax.ShapeDtypeStruct((PEER_SLOTS, T), jnp.int32), jax.ShapeDtypeStruct((PEER_SLOTS, T), jnp.float32)
    ospec = pl.BlockSpec((PEER_TOPK, tm), lambda i, h: (h, i))
    return pl.pallas_call(
        _route_kernel,
        grid=(T // tm, PEER_HEADS),
        in_specs=[pl.BlockSpec((tm, D_MODEL), lambda i, h: (i, 0)),
                  pl.BlockSpec((1, D_MODEL), lambda i, h: (0, 0)),
                  pl.BlockSpec((D_MODEL, PEER_QDIM), lambda i, h: (0, h)),
                  pl.BlockSpec((2, PEER_NKEYS, PEER_QDIM // 2), lambda i, h: (h, 0, 0))],
        out_specs=(ospec, ospec),
        out_shape=out,
        scratch_shapes=[pltpu.VMEM((tm, D_MODEL), jnp.bfloat16)],
        compiler_params=_cparams(("parallel", "arbitrary")),
        name="peer_route",
    )(h2d, g, wq, keys)


def _pack_bf16_pairs(x):
    e, d = x.shape
    bits = lax.bitcast_convert_type(x.astype(jnp.bfloat16), jnp.uint16).astype(jnp.uint32)
    bits = bits.reshape(e, d // 256, 2, 128)
    return bits[:, :, 0, :] | (bits[:, :, 1, :] << 16)


def _sc_gather(uv, idx):
    M = idx.shape[0]
    info = plsc.get_sparse_core_info()
    nw = info.num_cores * info.num_subcores
    per = M // nw
    assert per * nw == M and per % SC_WINDOW == 0
    mesh = plsc.VectorSubcoreMesh(core_axis_name="c", subcore_axis_name="s")

    @functools.partial(
        pl.kernel, mesh=mesh,
        out_type=jax.ShapeDtypeStruct((M,) + uv.shape[1:], uv.dtype),
        scratch_types=[pltpu.VMEM((SC_WINDOW,), jnp.int32), pltpu.VMEM((SC_WINDOW,) + uv.shape[1:], uv.dtype),
                       pltpu.SemaphoreType.DMA],
    )
    def gather(uv_hbm, idx_hbm, out_hbm, idx_v, rows_v, sem):
        wid = lax.axis_index("s") * info.num_cores + lax.axis_index("c")
        base = wid * per

        @pl.loop(0, per // SC_WINDOW)
        def _(g):
            off = pl.multiple_of(base + g * SC_WINDOW, SC_WINDOW)
            pltpu.sync_copy(idx_hbm.at[pl.ds(off, SC_WINDOW)], idx_v)
            pltpu.async_copy(uv_hbm.at[idx_v], rows_v, sem).wait()
            pltpu.sync_copy(rows_v, out_hbm.at[pl.ds(off, SC_WINDOW)])

    return gather(uv, idx)


def _expert_kernel(idx_ref, idxn_ref, h_ref, g_ref, gate_ref, spread_ref, stag_ref, uv_hbm, o_ref, buf, z_ref, xn_ref,
                   y_ref, sem):
    TD, TS, NS = PEER_TD, PEER_TS, PEER_SLOTS
    TT, LW = TD + TS, 16 * PEER_SLOTS
    f32, bf16 = jnp.float32, jnp.bfloat16
    i = pl.program_id(0)
    n = pl.num_programs(0)
    slot = lax.rem(i, 2)

    def issue_token(iref, sl, t):
        for k in range(NS):
            pltpu.make_async_copy(uv_hbm.at[iref[t, k]], buf.at[sl, t * NS + k], sem.at[sl]).start(priority=k % 2)

    @pl.when(i == 0)
    def _():
        def body(t, c):
            issue_token(idx_ref, 0, t)
            return c
        lax.fori_loop(0, TD, body, 0)

    pltpu.make_async_copy(buf.at[slot], buf.at[slot], sem.at[slot]).wait()

    x = h_ref[...]
    xn_ref[...] = _rms(x, g_ref[...]).reshape(TT, SUBLANES, 128)
    q_of_lane = lax.broadcasted_iota(jnp.int32, (SUBLANES, LW), 1) % 16
    row = lax.broadcasted_iota(jnp.int32, (SUBLANES, LW), 0)
    m_u = q_of_lane == row
    m_v = q_of_lane == row + SUBLANES

    def experts_of(t, staged):
        r = pl.ds(pl.multiple_of(t * NS, NS), NS)
        tile = stag_ref[r] if staged else buf[slot, r]
        return pltpu.bitcast(tile.reshape(NS * SUBLANES, 128), bf16)

    def score_token(t, staged):
        tg = t + TD if staged else t
        z = _nt(xn_ref[tg].astype(bf16), experts_of(t, staged))
        z_ref[pl.ds(tg, 1), :] = jnp.sum(jnp.where(m_u, z, 0.0), axis=0, keepdims=True)

    def mix_token(t, staged):
        tg = t + TD if staged else t
        w = jnp.where(m_v, jnp.broadcast_to(z_ref[pl.ds(tg, 1), :], (SUBLANES, LW)), 0.0).astype(bf16)
        y_ref[tg] = jnp.dot(w, experts_of(t, staged), preferred_element_type=f32)

    def batched(count, fn):
        def body(tt, c):
            for j in range(PEER_BATCH):
                fn(tt * PEER_BATCH + j)
            return c
        lax.fori_loop(0, count // PEER_BATCH, body, 0)

    @pl.when(i + 1 < n)
    def _():
        def body(tt, c):
            for j in range(PEER_UNROLL):
                score_token(tt * PEER_UNROLL + j, False)
            for j in range(PEER_UNROLL):
                issue_token(idxn_ref, 1 - slot, tt * PEER_UNROLL + j)
            return c
        lax.fori_loop(0, TD // PEER_UNROLL, body, 0)

    @pl.when(i + 1 == n)
    def _():
        batched(TD, lambda t: score_token(t, False))

    batched(TS, lambda t: score_token(t, True))

    z = z_ref[...]
    z = z + pltpu.roll(z, 4, 1)
    z = z + pltpu.roll(z, 2, 1)
    z = z + pltpu.roll(z, 1, 1)
    gt = gate_ref[...]
    g1 = gt.astype(bf16)
    r1 = gt - g1.astype(f32)
    g2 = r1.astype(bf16)
    g3 = (r1 - g2.astype(f32)).astype(bf16)
    spread = spread_ref[...]
    gate16 = (jnp.dot(g1, spread, preferred_element_type=f32) + jnp.dot(g2, spread, preferred_element_type=f32)
              + jnp.dot(g3, spread, preferred_element_type=f32))
    w = gate16 * (0.5 * z * (1.0 + lax.erf(z * (2.0 ** -0.5))))
    s = w + pltpu.roll(w, 1, 1)
    s = s + pltpu.roll(s, 2, 1)
    s = s + pltpu.roll(s, 4, 1)
    z_ref[...] = s + pltpu.roll(w, 8, 1)

    batched(TD, lambda t: mix_token(t, False))
    batched(TS, lambda t: mix_token(t, True))
    o_ref[...] = x + y_ref[...].reshape(TT, D_MODEL)


def _experts_chunk(step0, nsteps, idx_d, h2d, g, gate, spread, stag, uv):
    TD, TS, NS = PEER_TD, PEER_TS, PEER_SLOTS
    TT = TD + TS
    last = step0 + nsteps - 1
    return pl.pallas_call(
        _expert_kernel,
        grid=(nsteps,),
        in_specs=[pl.BlockSpec((TD, NS), lambda i: (step0 + i, 0), memory_space=pltpu.SMEM),
                  pl.BlockSpec((TD, NS), lambda i: (jnp.minimum(step0 + i + 1, last), 0), memory_space=pltpu.SMEM),
                  pl.BlockSpec((TT, D_MODEL), lambda i: (step0 + i, 0)),
                  pl.BlockSpec((1, D_MODEL), lambda i: (0, 0)),
                  pl.BlockSpec((TT, NS), lambda i: (step0 + i, 0)),
                  pl.BlockSpec((NS, 16 * NS), lambda i: (0, 0)),
                  pl.BlockSpec((TS * NS, SUBLANES, 128), lambda i: (i, 0, 0)),
                  pl.BlockSpec(memory_space=pl.ANY)],
        out_specs=pl.BlockSpec((TT, D_MODEL), lambda i: (step0 + i, 0)),
        out_shape=jax.ShapeDtypeStruct(h2d.shape, jnp.float32),
        scratch_shapes=[pltpu.VMEM((2, TD * NS, SUBLANES, 128), jnp.uint32),
                        pltpu.VMEM((TT, 16 * NS), jnp.float32),
                        pltpu.VMEM((TT, SUBLANES, 128), jnp.float32),
                        pltpu.VMEM((TT, SUBLANES, 128), jnp.float32),
                        pltpu.SemaphoreType.DMA((2,))],
        input_output_aliases={2: 0},
        compiler_params=_cparams(("arbitrary",)),
        name="peer_experts",
    )(idx_d, idx_d, h2d, g, gate, spread, stag, uv)


def _experts(h2d, g, eidx, gate, uv):
    T = h2d.shape[0]
    TD, TS, NS = PEER_TD, PEER_TS, PEER_SLOTS
    TT = TD + TS
    steps = T // TT
    chunks = _pick_tile(steps, (12, 8, 6, 4, 3, 2, 1))
    spc = steps // chunks
    e3 = eidx.reshape(steps, TT, NS)
    idx_d = e3[:, :TD].reshape(steps * TD, NS)
    idx_s = e3[:, TD:].reshape(chunks, spc * TS * NS)
    spread = np.zeros((NS, 16 * NS), np.float32)
    spread[np.arange(NS), 16 * np.arange(NS) + SUBLANES - 1] = 1.0
    spread = jnp.asarray(spread, jnp.bfloat16)
    done = []
    for c in range(chunks):
        idx_c = idx_s[c]
        if c >= 2:
            keep = (done[c - 2][0, 0] < jnp.inf).astype(jnp.int32)
            idx_c = idx_c * keep
        stag = _sc_gather(uv, idx_c)
        h2d = _experts_chunk(c * spc, spc, idx_d, h2d, g, gate, spread, stag, uv)
        done.append(h2d)
    return h2d


def _final_kernel(h_ref, g_ref, o_ref):
    o_ref[...] = _rms(h_ref[...], g_ref[...])


def _final(h3, g, b0, nb):
    _, Lp, _ = h3.shape
    S = Lp - BLK
    return pl.pallas_call(
        _final_kernel,
        grid=(nb, S // BLK),
        in_specs=[pl.BlockSpec((None, BLK, D_MODEL), lambda b, r: (b + b0, r + 1, 0)),
                  pl.BlockSpec((1, D_MODEL), lambda b, r: (0, 0))],
        out_specs=pl.BlockSpec((None, BLK, D_MODEL), lambda b, r: (b, r, 0)),
        out_shape=jax.ShapeDtypeStruct((nb, S, D_MODEL), jnp.float32),
        compiler_params=_cparams(("parallel", "parallel")),
        name="final_norm",
    )(h3, g)


def _pick_tile(T, candidates):
    for c in candidates:
        if T % c == 0:
            return c
    raise ValueError(f"no tile for {T}")


def kernel(x_prompt, x_sample, meta_tokens, g_mix, w_in, attn_sinks, lb_logits, hg_norm_g, w_att_branch,
           w_hg_branch, w_out, g_ffn, w_peer_q, peer_sub_keys, peer_u, peer_v, g_final):
    f32, bf16 = jnp.float32, jnp.bfloat16
    depth = w_in.shape[0]
    nb_p, S, _ = x_prompt.shape
    nb_s = x_sample.shape[0]
    assert x_sample.shape[1] == S and S % BLK == 0
    NB = nb_p + nb_s
    Lp = S + BLK
    T = NB * Lp

    x = jnp.concatenate([x_prompt, x_sample], axis=0)
    head = jnp.concatenate([jnp.zeros((PAD_ROWS, D_MODEL), f32), meta_tokens.astype(f32)], axis=0)
    h = jnp.concatenate([jnp.broadcast_to(head[None], (NB, BLK, D_MODEL)), x], axis=1).reshape(T, D_MODEL)

    sizes = (1024, 256, 256, 1024, 1024, 1024, 1024, 1024, 1024, 1024)
    offs = np.concatenate([[0], np.cumsum(sizes)])
    order = (0, 8, 9, 3, 4, 5, 6, 7, 1, 2)
    perm = np.concatenate([np.arange(offs[k], offs[k + 1]) for k in order])

    sm = jax.nn.softmax(lb_logits.astype(f32), axis=0)
    lb_all = jnp.maximum(jnp.cumsum(sm, axis=0) - sm[0:1], 0.0)

    tables = _rope_tables(Lp)
    consts_f = _hg_consts(False)
    consts_b = _hg_consts(True)
    tm_big = _pick_tile(T, (512, 384, 256, 128))
    tm_route = _pick_tile(T, (256, 128))

    for l in range(depth):
        w_l = w_in[l][:, perm].astype(bf16)
        proj = _inproj(h, g_mix[l].reshape(1, -1), w_l, tm_big, IN_W // 4)
        proj3 = proj.reshape(NB, Lp, IN_W)
        a_out = _attention(proj3, attn_sinks[l].astype(f32), tables)
        lb3 = lb_all[l].reshape(2, 1, -1)
        o_fwd, o_bwd = _hgrn(proj3, lb3, consts_f, consts_b)
        h = _merge(a_out.reshape(T, -1), o_fwd.reshape(T, -1), o_bwd.reshape(T, -1), hg_norm_g[l].reshape(1, -1),
                   proj, h, w_att_branch[l].astype(bf16), w_hg_branch[l].astype(bf16), w_out[l].astype(bf16),
                   tm_route)
        g_f = g_ffn[l].reshape(1, -1)
        keys = peer_sub_keys[l].reshape(2 * PEER_HEADS, PEER_NKEYS, PEER_QDIM // 2).astype(bf16)
        eidx_t, gate_t = _route(h, g_f, w_peer_q[l].astype(bf16), keys, tm_route)
        uv = jnp.concatenate([_pack_bf16_pairs(peer_u[l]), _pack_bf16_pairs(peer_v[l])], axis=1)
        h = _experts(h, g_f, eidx_t.T, gate_t.T, uv)

    h3 = h.reshape(NB, Lp, D_MODEL)
    g_fin = g_final.reshape(1, -1)
    return _final(h3, g_fin, 0, nb_p), _final(h3, g_fin, nb_p, nb_s)
```

```python
import functools

import numpy as np
import jax
import jax.numpy as jnp
from jax import lax
from jax.experimental import pallas as pl
from jax.experimental.pallas import tpu as pltpu
from jax.experimental.pallas import tpu_sc as plsc

D_MODEL = 1024
N_META = 16
N_Q_HEADS = 16
N_KV_HEADS = 4
HEAD_DIM = 64
WINDOW = 128
ROT_DIM = HEAD_DIM // 4
ROPE_THETA = 500000.0
HG_HEADS = 8
HG_DIM = 128
PEER_HEADS = 8
PEER_NKEYS = 128
PEER_QDIM = 256
PEER_TOPK = 16
EPS = 1e-6
NEG = -1e30

BLK = 128
PAD_ROWS = BLK - N_META
IN_W = 8704
COL_AQ, COL_GA, COL_GH, COL_HQ, COL_ZF, COL_ZB, COL_HI, COL_HGG, COL_AK, COL_AV = (
    0, 1024, 2048, 3072, 4096, 5120, 6144, 7168, 8192, 8448)

VMEM_LIMIT = 56 * 1024 * 1024
PEER_TD = 16
PEER_TS = 16
SC_WINDOW = 64
PEER_SLOTS = PEER_HEADS * PEER_TOPK
HG_HP = 4
PEER_UNROLL = 2
PEER_BATCH = 8
SUBLANES = 8


def _cparams(sem):
    return pltpu.CompilerParams(dimension_semantics=sem, vmem_limit_bytes=VMEM_LIMIT)


def _rms(x, g):
    return x * lax.rsqrt(jnp.mean(x * x, axis=-1, keepdims=True) + EPS) * g


def _nt(a, b):
    return lax.dot_general(a, b, (((1,), (1,)), ((), ())), preferred_element_type=jnp.float32)


def _tn(a, b):
    return lax.dot_general(a, b, (((0,), (0,)), ((), ())), preferred_element_type=jnp.float32)


def _inproj_kernel(x_ref, g_ref, w_ref, o_ref):
    n = _rms(x_ref[...], g_ref[...]).astype(jnp.bfloat16)
    o_ref[...] = jnp.dot(n, w_ref[...], preferred_element_type=jnp.float32)


def _inproj(h2d, g, w, tm, tn):
    T = h2d.shape[0]
    return pl.pallas_call(
        _inproj_kernel,
        grid=(IN_W // tn, T // tm),
        in_specs=[pl.BlockSpec((tm, D_MODEL), lambda j, i: (i, 0)),
                  pl.BlockSpec((1, D_MODEL), lambda j, i: (0, 0)),
                  pl.BlockSpec((D_MODEL, tn), lambda j, i: (0, j))],
        out_specs=pl.BlockSpec((tm, tn), lambda j, i: (i, j)),
        out_shape=jax.ShapeDtypeStruct((T, IN_W), jnp.float32),
        compiler_params=_cparams(("parallel", "parallel")),
        name="inproj",
    )(h2d, g, w)


def _rope_tables(Lp):
    half = ROT_DIM // 2
    pos = jnp.maximum(jnp.arange(Lp) - PAD_ROWS, 0)
    inv = jnp.power(jnp.float32(ROPE_THETA), -jnp.arange(half, dtype=jnp.float32) * 2.0 / ROT_DIM)
    ang = pos.astype(jnp.float32)[:, None] * inv[None, :]
    cos, sin = jnp.cos(ang), jnp.sin(ang)
    ones = jnp.ones((Lp, HEAD_DIM - ROT_DIM), jnp.float32)
    zeros = jnp.zeros((Lp, HEAD_DIM - ROT_DIM), jnp.float32)
    zh = jnp.zeros((Lp, half), jnp.float32)
    c = jnp.concatenate([cos, cos, ones], axis=1)
    s1 = jnp.concatenate([zh, sin, zeros], axis=1)
    s2 = jnp.concatenate([-sin, zh, zeros], axis=1)
    rep = BLK // HEAD_DIM
    return tuple(jnp.tile(t, (1, rep)) for t in (c, s1, s2))


def _attn_kernel(sink_ref, q_ref, kg_ref, k0_ref, k1_ref, k2_ref, vg_ref, v0_ref, v1_ref, v2_ref,
                 c_ref, s1_ref, s2_ref, o_ref):
    r = pl.program_id(1)
    nblk = pl.num_programs(1)
    half = ROT_DIM // 2

    def rope(x, blk):
        row0 = blk * BLK if isinstance(blk, int) else pl.multiple_of(blk * BLK, BLK)
        c = c_ref[pl.ds(row0, BLK), :]
        s1 = s1_ref[pl.ds(row0, BLK), :]
        s2 = s2_ref[pl.ds(row0, BLK), :]
        outs = []
        for j in range(x.shape[1] // BLK):
            xj = x[:, j * BLK:(j + 1) * BLK]
            outs.append(xj * c + pltpu.roll(xj, half, 1) * s1 + pltpu.roll(xj, BLK - half, 1) * s2)
        return jnp.concatenate(outs, axis=1)

    kb = [jnp.clip(r - 1 + j, 0, nblk - 1) for j in range(3)]
    q = rope(q_ref[...], r) * (HEAD_DIM ** -0.5)
    kcat = jnp.concatenate([rope(kg_ref[...], 0), rope(k0_ref[...], kb[0]), rope(k1_ref[...], kb[1]),
                            rope(k2_ref[...], kb[2])], axis=0).astype(jnp.bfloat16)
    vcat = jnp.concatenate([vg_ref[...], v0_ref[...], v1_ref[...], v2_ref[...]], axis=0).astype(jnp.bfloat16)

    qrow = lax.broadcasted_iota(jnp.int32, (BLK, BLK), 0)
    kcol = lax.broadcasted_iota(jnp.int32, (BLK, BLK), 1)
    masks = [kcol >= PAD_ROWS]
    for j in range(3):
        kblk = r - 1 + j
        ok_blk = (kblk >= 1) & (kblk <= nblk - 1)
        masks.append((jnp.abs((j - 1) * BLK + kcol - qrow) <= WINDOW) & ok_blk)
    mask = jnp.concatenate(masks, axis=1)

    G = N_Q_HEADS // N_KV_HEADS
    lane_half = kcol // HEAD_DIM
    pairs_per_slice = (BLK // HEAD_DIM) * G // 2
    outs = [None] * (N_Q_HEADS // 2)
    for m2 in range(N_KV_HEADS * HEAD_DIM // BLK):
        k2 = kcat[:, m2 * BLK:(m2 + 1) * BLK]
        v2 = vcat[:, m2 * BLK:(m2 + 1) * BLK]
        qs, sks, halves = [], [], []
        for j in range(m2 * pairs_per_slice, (m2 + 1) * pairs_per_slice):
            hk = ((2 * j) // G) % 2
            in_half = lane_half == hk
            q2 = q[:, j * BLK:(j + 1) * BLK]
            qs += [jnp.where(in_half, q2, 0.0), jnp.where(in_half, pltpu.roll(q2, HEAD_DIM, 1), 0.0)]
            sks += [jnp.full((BLK, 1), sink_ref[2 * j + hk]), jnp.full((BLK, 1), sink_ref[2 * j + 1 - hk])]
            halves.append(in_half)
        nq = len(qs)
        sk = jnp.concatenate(sks, axis=0)
        s = _nt(jnp.concatenate(qs, axis=0).astype(jnp.bfloat16), k2)
        s = jnp.where(jnp.tile(mask, (nq, 1)), s, NEG)
        m = jnp.maximum(jnp.max(s, axis=-1, keepdims=True), sk)
        p = jnp.exp(s - m)
        p = p / (jnp.sum(p, axis=-1, keepdims=True) + jnp.exp(sk - m))
        o = jnp.dot(p.astype(jnp.bfloat16), v2, preferred_element_type=jnp.float32)
        for jj, in_half in enumerate(halves):
            o_same = o[(2 * jj) * BLK:(2 * jj + 1) * BLK]
            o_other = o[(2 * jj + 1) * BLK:(2 * jj + 2) * BLK]
            outs[m2 * pairs_per_slice + jj] = jnp.where(in_half, o_same, pltpu.roll(o_other, HEAD_DIM, 1))
    o_ref[...] = jnp.concatenate(outs, axis=1).astype(o_ref.dtype)


def _attention(proj3, sinks, tables):
    NB, Lp, _ = proj3.shape
    nblk = Lp // BLK
    kcol, vcol = COL_AK // 256, COL_AV // 256

    def loc(j, col):
        return pl.BlockSpec((None, BLK, 256),
                            lambda b, r: (b, jnp.clip(r - 1 + j, 0, nblk - 1), col))

    glob = lambda col: pl.BlockSpec((None, BLK, 256), lambda b, r: (b, 0, col))
    tab = pl.BlockSpec((Lp, BLK), lambda b, r: (0, 0))
    return pl.pallas_call(
        _attn_kernel,
        grid=(NB, nblk),
        in_specs=[pl.BlockSpec(memory_space=pltpu.SMEM),
                  pl.BlockSpec((None, BLK, 1024), lambda b, r: (b, r, 0)),
                  glob(kcol), loc(0, kcol), loc(1, kcol), loc(2, kcol),
                  glob(vcol), loc(0, vcol), loc(1, vcol), loc(2, vcol),
                  tab, tab, tab],
        out_specs=pl.BlockSpec((None, BLK, 1024), lambda b, r: (b, r, 0)),
        out_shape=jax.ShapeDtypeStruct((NB, Lp, 1024), jnp.bfloat16),
        compiler_params=_cparams(("parallel", "arbitrary")),
        name="window_attn",
    )(sinks, proj3, proj3, proj3, proj3, proj3, proj3, proj3, proj3, proj3, *tables)


HG_ROWS = BLK
_HG_LEVELS = (64, 32, 16, 8, 4, 2, 1)


def _hg_consts(reverse):
    C = HG_ROWS
    t = np.arange(C)[:, None]
    u = np.arange(C)[None, :]
    mats = [u <= t, u > t]
    ups, msks = [], []
    for B in _HG_LEVELS:
        same = (t // (2 * B)) == (u // (2 * B))
        tin, uin = t % (2 * B), u % (2 * B)
        upper = tin >= B
        wq = upper & same & (uin >= B) & (uin <= tin)
        wk = (~upper) & same & (uin > tin) & (uin <= B - 1)
        mats.append(wq | wk)
        ups.append(np.broadcast_to(upper, (C, HG_DIM)))
        msks.append(same & upper & (uin < B))
    msks.append(np.eye(C, dtype=bool))
    if reverse:
        mats = [m[::-1, ::-1] for m in mats]
        ups = [m[::-1] for m in ups]
        msks = [m[::-1, ::-1] for m in msks]
    W = np.concatenate(mats, axis=0).astype(np.float32)
    W2 = np.concatenate([W, W], axis=1)
    return (jnp.asarray(W2, jnp.bfloat16), jnp.asarray(np.stack(ups).astype(np.float32)),
            jnp.asarray(np.stack(msks).astype(np.float32)))


def _hg_chunk(q_raw, z, v, valid, lb, w2, up_ref, msk_ref, st, reverse):
    C = HG_ROWS
    f32, bf16 = jnp.float32, jnp.bfloat16
    log_lb = jnp.log(lb)
    log1m = jnp.log1p(-lb)
    logsig = jnp.minimum(z, 0.0) - jnp.log1p(jnp.exp(-jnp.abs(z)))
    a, b = log_lb, log1m + logsig
    g = jnp.maximum(a, b) + jnp.log1p(jnp.exp(-jnp.abs(a - b)))
    k = (1.0 - lb) * jax.nn.sigmoid(-z)
    q = q_raw * jax.nn.sigmoid(q_raw)
    q = jnp.where(valid, q, 0.0)
    k = jnp.where(valid, k, 0.0)
    v = jnp.where(valid, v, 0.0)
    g = jnp.where(valid, g, 0.0)
    g1 = g.astype(bf16)
    g2 = (g - g1.astype(f32)).astype(bf16)
    E = jnp.dot(w2, jnp.concatenate([g1, g2], axis=0), preferred_element_type=f32)
    b_inc = E[0:C]
    b_sfx = E[C:2 * C]
    b_tot = b_inc[0:1] if reverse else b_inc[C - 1:C]

    A = msk_ref[len(_HG_LEVELS)] * jnp.sum(q * k, axis=-1, keepdims=True)
    for l in range(len(_HG_LEVELS)):
        ex = jnp.exp(E[(2 + l) * C:(3 + l) * C])
        up = up_ref[l] > 0.5
        qs = jnp.where(up, q * ex, 0.0).astype(bf16)
        ks = jnp.where(up, 0.0, k * ex).astype(bf16)
        A = A + _nt(qs, ks) * msk_ref[l]

    vb = v.astype(bf16)
    o = _nt((q * jnp.exp(b_inc)).astype(bf16), st.astype(bf16))
    o = o + jnp.dot(A.astype(bf16), vb, preferred_element_type=f32)
    st = st * jnp.exp(b_tot) + _tn(vb, (k * jnp.exp(b_sfx)).astype(bf16))
    return o, st


def _hg_kernel(qf_ref, zf_ref, vf_ref, qb_ref, zb_ref, vb_ref, lb_ref, w2f_ref, upf_ref, mskf_ref,
               w2b_ref, upb_ref, mskb_ref, of_ref, ob_ref, st_ref):
    i = pl.program_id(2)
    nblk = pl.num_programs(2)

    @pl.when(i == 0)
    def _():
        st_ref[...] = jnp.zeros_like(st_ref)

    rows = lax.broadcasted_iota(jnp.int32, (HG_ROWS, HG_DIM), 0)
    dirs = ((False, qf_ref, zf_ref, vf_ref, w2f_ref, upf_ref, mskf_ref, of_ref, i),
            (True, qb_ref, zb_ref, vb_ref, w2b_ref, upb_ref, mskb_ref, ob_ref, nblk - 1 - i))
    for d, (reverse, q_ref, z_ref, v_ref, w2_ref, up_ref, msk_ref, o_ref, blk) in enumerate(dirs):
        w2 = w2_ref[...]
        for hh in range(HG_HP):
            cols = slice(hh * HG_DIM, (hh + 1) * HG_DIM)
            lb = lb_ref[d, :, cols]
            valid = (rows >= PAD_ROWS) | (blk > 0)
            o, st = _hg_chunk(q_ref[:, cols], z_ref[:, cols], v_ref[:, cols], valid, lb, w2,
                              up_ref, msk_ref, st_ref[d, hh], reverse)
            st_ref[d, hh] = st
            o_ref[:, cols] = o


def _hgrn(proj3, lb3, consts_f, consts_b):
    NB, Lp, _ = proj3.shape
    nblk = Lp // BLK
    W = HG_DIM * HG_HP

    def col(base, reverse):
        return pl.BlockSpec((None, BLK, W),
                            lambda b, h, i: (b, (nblk - 1 - i) if reverse else i, base // W + h))

    def const(c):
        return pl.BlockSpec(c.shape, lambda b, h, i: (0,) * c.ndim)

    in_specs = [col(COL_HQ, False), col(COL_ZF, False), col(COL_HI, False),
                col(COL_HQ, True), col(COL_ZB, True), col(COL_HI, True),
                pl.BlockSpec((2, 1, W), lambda b, h, i: (0, 0, h))]
    in_specs += [const(c) for c in consts_f] + [const(c) for c in consts_b]
    oshape = jax.ShapeDtypeStruct((NB, Lp, HG_HEADS * HG_DIM), jnp.float32)
    return pl.pallas_call(
        _hg_kernel,
        grid=(NB, HG_HEADS // HG_HP, nblk),
        in_specs=in_specs,
        out_specs=(pl.BlockSpec((None, BLK, W), lambda b, h, i: (b, i, h)),
                   pl.BlockSpec((None, BLK, W), lambda b, h, i: (b, nblk - 1 - i, h))),
        out_shape=(oshape, oshape),
        scratch_shapes=[pltpu.VMEM((2, HG_HP, HG_DIM, HG_DIM), jnp.float32)],
        compiler_params=_cparams(("parallel", "parallel", "arbitrary")),
        name="hgrn_bidir",
    )(proj3, proj3, proj3, proj3, proj3, proj3, lb3, *consts_f, *consts_b)


def _merge_kernel(a_ref, of_ref, ob_ref, hgg_ref, ng_ref, ga_ref, gh_ref, h_ref, wa_ref, wh_ref, wo_ref, o_ref):
    f32, bf16 = jnp.float32, jnp.bfloat16
    o = of_ref[...] + ob_ref[...]
    ng = ng_ref[...]
    gt = hgg_ref[...]
    hh = jnp.concatenate([_rms(o[:, c:c + HG_DIM], ng[:, c:c + HG_DIM]) for c in range(0, HG_HEADS * HG_DIM, HG_DIM)],
                         axis=1) * (gt * jax.nn.sigmoid(gt))
    pa = jnp.dot(a_ref[...], wa_ref[...], preferred_element_type=f32)
    ph = jnp.dot(hh.astype(bf16), wh_ref[...], preferred_element_type=f32)
    merged = jax.nn.sigmoid(ga_ref[...]) * pa + jax.nn.sigmoid(gh_ref[...]) * ph
    o_ref[...] = h_ref[...] + jnp.dot(merged.astype(bf16), wo_ref[...], preferred_element_type=f32)


def _merge(a2d, of2d, ob2d, ng, proj2d, h2d, wa, wh, wo, tm):
    T = h2d.shape[0]
    row = lambda c: pl.BlockSpec((tm, D_MODEL), lambda i: (i, c))
    wspec = pl.BlockSpec((D_MODEL, D_MODEL), lambda i: (0, 0))
    return pl.pallas_call(
        _merge_kernel,
        grid=(T // tm,),
        in_specs=[row(0), row(0), row(0), row(COL_HGG // D_MODEL), pl.BlockSpec((1, D_MODEL), lambda i: (0, 0)),
                  row(COL_GA // D_MODEL), row(COL_GH // D_MODEL), row(0), wspec, wspec, wspec],
        out_specs=row(0),
        out_shape=jax.ShapeDtypeStruct((T, D_MODEL), jnp.float32),
        input_output_aliases={7: 0},
        compiler_params=_cparams(("parallel",)),
        name="merge_out",
    )(a2d, of2d, ob2d, proj2d, ng, proj2d, proj2d, h2d, wa, wh, wo)


def _topk_axis0(s, kk):
    n = s.shape[0]
    iota = lax.broadcasted_iota(jnp.int32, s.shape, 0)
    vals, ids = [], []
    for _ in range(kk):
        m = jnp.max(s, axis=0, keepdims=True)
        am = jnp.min(jnp.where(s == m, iota, n), axis=0, keepdims=True)
        s = jnp.where(iota == am, -jnp.inf, s)
        vals.append(m)
        ids.append(am)
    return vals, ids


def _stack_rows(rows, dtype):
    n = len(rows)
    iota = lax.broadcasted_iota(jnp.int32, (n, rows[0].shape[1]), 0)
    out = jnp.zeros((n, rows[0].shape[1]), dtype)
    for j, rj in enumerate(rows):
        out = jnp.where(iota == j, rj, out)
    return out


def _route_kernel(h_ref, g_ref, wq_ref, key_ref, eidx_ref, gate_ref, xn_ref):
    @pl.when(pl.program_id(1) == 0)
    def _():
        xn_ref[...] = _rms(h_ref[...], g_ref[...]).astype(jnp.bfloat16)

    K = PEER_TOPK
    half = PEER_QDIM // 2
    i32 = jnp.int32
    qry = jnp.dot(xn_ref[...], wq_ref[...], preferred_element_type=jnp.float32).astype(jnp.bfloat16)
    s0 = _nt(key_ref[0], qry[:, :half])
    s1 = _nt(key_ref[1], qry[:, half:])
    tm = s0.shape[1]
    v0, i0 = _topk_axis0(s0, K)
    v1, i1 = _topk_axis0(s1, K)
    m1 = _stack_rows(v1, jnp.float32)
    m0_hi = _stack_rows(v0[K // 2:], jnp.float32)
    r8 = lax.broadcasted_iota(i32, (K // 2, tm), 0)
    cand = [v0[0] + m1] + [v0[a] + m1[:K // 2] for a in range(1, K // 2)] + [m0_hi + v1[0]]
    flat = [lax.broadcasted_iota(i32, (K, tm), 0)] + [a * K + r8 for a in range(1, K // 2)] + [(K // 2 + r8) * K]
    cand = jnp.concatenate(cand, axis=0)
    flat = jnp.concatenate(flat, axis=0)
    big = K * K
    ts, tp = [], []
    for _ in range(K):
        m = jnp.max(cand, axis=0, keepdims=True)
        p = jnp.min(jnp.where(cand == m, flat, big), axis=0, keepdims=True)
        cand = jnp.where(flat == p, -jnp.inf, cand)
        ts.append(m)
        tp.append(p)
    pk = _stack_rows(tp, i32)
    pa, pb = pk // K, pk % K
    e0 = jnp.zeros((K, tm), i32)
    e1 = jnp.zeros((K, tm), i32)
    for a in range(K):
        e0 = jnp.where(pa == a, i0[a], e0)
        e1 = jnp.where(pb == a, i1[a], e1)
    ex = [jnp.exp(t - ts[0]) for t in ts]
    den = ex[0]
    for e in ex[1:]:
        den = den + e
    eidx_ref[...] = e0 * PEER_NKEYS + e1
    gate_ref[...] = _stack_rows([e / den for e in ex], jnp.float32)


def _route(h2d, g, wq, keys, tm):
    T = h2d.shape[0]
    out = jax.ShapeDtypeStruct((PEER_SLOTS, T), jnp.int32), jax.ShapeDtypeStruct((PEER_SLOTS, T), jnp.float32)
    ospec = pl.BlockSpec((PEER_TOPK, tm), lambda i, h: (h, i))
    return pl.pallas_call(
        _route_kernel,
        grid=(T // tm, PEER_HEADS),
        in_specs=[pl.BlockSpec((tm, D_MODEL), lambda i, h: (i, 0)),
                  pl.BlockSpec((1, D_MODEL), lambda i, h: (0, 0)),
                  pl.BlockSpec((D_MODEL, PEER_QDIM), lambda i, h: (0, h)),
                  pl.BlockSpec((2, PEER_NKEYS, PEER_QDIM // 2), lambda i, h: (h, 0, 0))],
        out_specs=(ospec, ospec),
        out_shape=out,
        scratch_shapes=[pltpu.VMEM((tm, D_MODEL), jnp.bfloat16)],
        compiler_params=_cparams(("parallel", "arbitrary")),
        name="peer_route",
    )(h2d, g, wq, keys)


def _pack_bf16_pairs(x):
    e, d = x.shape
    bits = lax.bitcast_convert_type(x.astype(jnp.bfloat16), jnp.uint16).astype(jnp.uint32)
    bits = bits.reshape(e, d // 256, 2, 128)
    return bits[:, :, 0, :] | (bits[:, :, 1, :] << 16)


def _sc_gather(uv, idx):
    M = idx.shape[0]
    info = plsc.get_sparse_core_info()
    nw = info.num_cores * info.num_subcores
    per = M // nw
    assert per * nw == M and per % SC_WINDOW == 0
    mesh = plsc.VectorSubcoreMesh(core_axis_name="c", subcore_axis_name="s")

    @functools.partial(
        pl.kernel, mesh=mesh,
        out_type=jax.ShapeDtypeStruct((M,) + uv.shape[1:], uv.dtype),
        scratch_types=[pltpu.VMEM((SC_WINDOW,), jnp.int32), pltpu.VMEM((SC_WINDOW,) + uv.shape[1:], uv.dtype),
                       pltpu.SemaphoreType.DMA],
    )
    def gather(uv_hbm, idx_hbm, out_hbm, idx_v, rows_v, sem):
        wid = lax.axis_index("s") * info.num_cores + lax.axis_index("c")
        base = wid * per

        @pl.loop(0, per // SC_WINDOW)
        def _(g):
            off = pl.multiple_of(base + g * SC_WINDOW, SC_WINDOW)
            pltpu.sync_copy(idx_hbm.at[pl.ds(off, SC_WINDOW)], idx_v)
            pltpu.async_copy(uv_hbm.at[idx_v], rows_v, sem).wait()
            pltpu.sync_copy(rows_v, out_hbm.at[pl.ds(off, SC_WINDOW)])

    return gather(uv, idx)


def _expert_kernel(idx_ref, idxn_ref, h_ref, g_ref, gate_ref, spread_ref, stag_ref, uv_hbm, o_ref, buf, z_ref, xn_ref,
                   y_ref, sem):
    TD, TS, NS = PEER_TD, PEER_TS, PEER_SLOTS
    TT, LW = TD + TS, 16 * PEER_SLOTS
    f32, bf16 = jnp.float32, jnp.bfloat16
    i = pl.program_id(0)
    n = pl.num_programs(0)
    slot = lax.rem(i, 2)

    def issue_token(iref, sl, t):
        for k in range(NS):
            pltpu.make_async_copy(uv_hbm.at[iref[t, k]], buf.at[sl, t * NS + k], sem.at[sl]).start(priority=k % 2)

    @pl.when(i == 0)
    def _():
        def body(t, c):
            issue_token(idx_ref, 0, t)
            return c
        lax.fori_loop(0, TD, body, 0)

    pltpu.make_async_copy(buf.at[slot], buf.at[slot], sem.at[slot]).wait()

    x = h_ref[...]
    xn_ref[...] = _rms(x, g_ref[...]).reshape(TT, SUBLANES, 128)
    q_of_lane = lax.broadcasted_iota(jnp.int32, (SUBLANES, LW), 1) % 16
    row = lax.broadcasted_iota(jnp.int32, (SUBLANES, LW), 0)
    m_u = q_of_lane == row
    m_v = q_of_lane == row + SUBLANES

    def experts_of(t, staged):
        r = pl.ds(pl.multiple_of(t * NS, NS), NS)
        tile = stag_ref[r] if staged else buf[slot, r]
        return pltpu.bitcast(tile.reshape(NS * SUBLANES, 128), bf16)

    def score_token(t, staged):
        tg = t + TD if staged else t
        z = _nt(xn_ref[tg].astype(bf16), experts_of(t, staged))
        z_ref[pl.ds(tg, 1), :] = jnp.sum(jnp.where(m_u, z, 0.0), axis=0, keepdims=True)

    def mix_token(t, staged):
        tg = t + TD if staged else t
        w = jnp.where(m_v, jnp.broadcast_to(z_ref[pl.ds(tg, 1), :], (SUBLANES, LW)), 0.0).astype(bf16)
        y_ref[tg] = jnp.dot(w, experts_of(t, staged), preferred_element_type=f32)

    def batched(count, fn):
        def body(tt, c):
            for j in range(PEER_BATCH):
                fn(tt * PEER_BATCH + j)
            return c
        lax.fori_loop(0, count // PEER_BATCH, body, 0)

    @pl.when(i + 1 < n)
    def _():
        def body(tt, c):
            for j in range(PEER_UNROLL):
                score_token(tt * PEER_UNROLL + j, False)
            for j in range(PEER_UNROLL):
                issue_token(idxn_ref, 1 - slot, tt * PEER_UNROLL + j)
            return c
        lax.fori_loop(0, TD // PEER_UNROLL, body, 0)

    @pl.when(i + 1 == n)
    def _():
        batched(TD, lambda t: score_token(t, False))

    batched(TS, lambda t: score_token(t, True))

    z = z_ref[...]
    z = z + pltpu.roll(z, 4, 1)
    z = z + pltpu.roll(z, 2, 1)
    z = z + pltpu.roll(z, 1, 1)
    gt = gate_ref[...]
    g1 = gt.astype(bf16)
    r1 = gt - g1.astype(f32)
    g2 = r1.astype(bf16)
    g3 = (r1 - g2.astype(f32)).astype(bf16)
    spread = spread_ref[...]
    gate16 = (jnp.dot(g1, spread, preferred_element_type=f32) + jnp.dot(g2, spread, preferred_element_type=f32)
              + jnp.dot(g3, spread, preferred_element_type=f32))
    w = gate16 * (0.5 * z * (1.0 + lax.erf(z * (2.0 ** -0.5))))
    s = w + pltpu.roll(w, 1, 1)
    s = s + pltpu.roll(s, 2, 1)
    s = s + pltpu.roll(s, 4, 1)
    z_ref[...] = s + pltpu.roll(w, 8, 1)

    batched(TD, lambda t: mix_token(t, False))
    batched(TS, lambda t: mix_token(t, True))
    o_ref[...] = x + y_ref[...].reshape(TT, D_MODEL)


def _experts_chunk(step0, nsteps, idx_d, h2d, g, gate, spread, stag, uv):
    TD, TS, NS = PEER_TD, PEER_TS, PEER_SLOTS
    TT = TD + TS
    last = step0 + nsteps - 1
    return pl.pallas_call(
        _expert_kernel,
        grid=(nsteps,),
        in_specs=[pl.BlockSpec((TD, NS), lambda i: (step0 + i, 0), memory_space=pltpu.SMEM),
                  pl.BlockSpec((TD, NS), lambda i: (jnp.minimum(step0 + i + 1, last), 0), memory_space=pltpu.SMEM),
                  pl.BlockSpec((TT, D_MODEL), lambda i: (step0 + i, 0)),
                  pl.BlockSpec((1, D_MODEL), lambda i: (0, 0)),
                  pl.BlockSpec((TT, NS), lambda i: (step0 + i, 0)),
                  pl.BlockSpec((NS, 16 * NS), lambda i: (0, 0)),
                  pl.BlockSpec((TS * NS, SUBLANES, 128), lambda i: (i, 0, 0)),
                  pl.BlockSpec(memory_space=pl.ANY)],
        out_specs=pl.BlockSpec((TT, D_MODEL), lambda i: (step0 + i, 0)),
        out_shape=jax.ShapeDtypeStruct(h2d.shape, jnp.float32),
        scratch_shapes=[pltpu.VMEM((2, TD * NS, SUBLANES, 128), jnp.uint32),
                        pltpu.VMEM((TT, 16 * NS), jnp.float32),
                        pltpu.VMEM((TT, SUBLANES, 128), jnp.float32),
                        pltpu.VMEM((TT, SUBLANES, 128), jnp.float32),
                        pltpu.SemaphoreType.DMA((2,))],
        input_output_aliases={2: 0},
        compiler_params=_cparams(("arbitrary",)),
        name="peer_experts",
    )(idx_d, idx_d, h2d, g, gate, spread, stag, uv)


def _experts(h2d, g, eidx, gate, uv):
    T = h2d.shape[0]
    TD, TS, NS = PEER_TD, PEER_TS, PEER_SLOTS
    TT = TD + TS
    steps = T // TT
    chunks = _pick_tile(steps, (12, 8, 6, 4, 3, 2, 1))
    spc = steps // chunks
    e3 = eidx.reshape(steps, TT, NS)
    idx_d = e3[:, :TD].reshape(steps * TD, NS)
    idx_s = e3[:, TD:].reshape(chunks, spc * TS * NS)
    spread = np.zeros((NS, 16 * NS), np.float32)
    spread[np.arange(NS), 16 * np.arange(NS) + SUBLANES - 1] = 1.0
    spread = jnp.asarray(spread, jnp.bfloat16)
    for c in range(chunks):
        stag = _sc_gather(uv, idx_s[c])
        h2d = _experts_chunk(c * spc, spc, idx_d, h2d, g, gate, spread, stag, uv)
    return h2d


def _final_kernel(h_ref, g_ref, o_ref):
    o_ref[...] = _rms(h_ref[...], g_ref[...])


def _final(h3, g, b0, nb):
    _, Lp, _ = h3.shape
    S = Lp - BLK
    return pl.pallas_call(
        _final_kernel,
        grid=(nb, S // BLK),
        in_specs=[pl.BlockSpec((None, BLK, D_MODEL), lambda b, r: (b + b0, r + 1, 0)),
                  pl.BlockSpec((1, D_MODEL), lambda b, r: (0, 0))],
        out_specs=pl.BlockSpec((None, BLK, D_MODEL), lambda b, r: (b, r, 0)),
        out_shape=jax.ShapeDtypeStruct((nb, S, D_MODEL), jnp.float32),
        compiler_params=_cparams(("parallel", "parallel")),
        name="final_norm",
    )(h3, g)


def _pick_tile(T, candidates):
    for c in candidates:
        if T % c == 0:
            return c
    raise ValueError(f"no tile for {T}")


def kernel(x_prompt, x_sample, meta_tokens, g_mix, w_in, attn_sinks, lb_logits, hg_norm_g, w_att_branch,
           w_hg_branch, w_out, g_ffn, w_peer_q, peer_sub_keys, peer_u, peer_v, g_final):
    f32, bf16 = jnp.float32, jnp.bfloat16
    depth = w_in.shape[0]
    nb_p, S, _ = x_prompt.shape
    nb_s = x_sample.shape[0]
    assert x_sample.shape[1] == S and S % BLK == 0
    NB = nb_p + nb_s
    Lp = S + BLK
    T = NB * Lp

    x = jnp.concatenate([x_prompt, x_sample], axis=0)
    head = jnp.concatenate([jnp.zeros((PAD_ROWS, D_MODEL), f32), meta_tokens.astype(f32)], axis=0)
    h = jnp.concatenate([jnp.broadcast_to(head[None], (NB, BLK, D_MODEL)), x], axis=1).reshape(T, D_MODEL)

    sizes = (1024, 256, 256, 1024, 1024, 1024, 1024, 1024, 1024, 1024)
    offs = np.concatenate([[0], np.cumsum(sizes)])
    order = (0, 8, 9, 3, 4, 5, 6, 7, 1, 2)
    perm = np.concatenate([np.arange(offs[k], offs[k + 1]) for k in order])

    sm = jax.nn.softmax(lb_logits.astype(f32), axis=0)
    lb_all = jnp.maximum(jnp.cumsum(sm, axis=0) - sm[0:1], 0.0)

    tables = _rope_tables(Lp)
    consts_f = _hg_consts(False)
    consts_b = _hg_consts(True)
    tm_big = _pick_tile(T, (512, 384, 256, 128))
    tm_route = _pick_tile(T, (256, 128))

    for l in range(depth):
        w_l = w_in[l][:, perm].astype(bf16)
        proj = _inproj(h, g_mix[l].reshape(1, -1), w_l, tm_big, IN_W // 4)
        proj3 = proj.reshape(NB, Lp, IN_W)
        a_out = _attention(proj3, attn_sinks[l].astype(f32), tables)
        lb3 = lb_all[l].reshape(2, 1, -1)
        o_fwd, o_bwd = _hgrn(proj3, lb3, consts_f, consts_b)
        h = _merge(a_out.reshape(T, -1), o_fwd.reshape(T, -1), o_bwd.reshape(T, -1), hg_norm_g[l].reshape(1, -1),
                   proj, h, w_att_branch[l].astype(bf16), w_hg_branch[l].astype(bf16), w_out[l].astype(bf16),
                   tm_route)
        g_f = g_ffn[l].reshape(1, -1)
        keys = peer_sub_keys[l].reshape(2 * PEER_HEADS, PEER_NKEYS, PEER_QDIM // 2).astype(bf16)
        eidx_t, gate_t = _route(h, g_f, w_peer_q[l].astype(bf16), keys, tm_route)
        uv = jnp.concatenate([_pack_bf16_pairs(peer_u[l]), _pack_bf16_pairs(peer_v[l])], axis=1)
        h = _experts(h, g_f, eidx_t.T, gate_t.T, uv)

    h3 = h.reshape(NB, Lp, D_MODEL)
    g_fin = g_final.reshape(1, -1)
    return _final(h3, g_fin, 0, nb_p), _final(h3, g_fin, nb_p, nb_s)
```

```python
import functools

import numpy as np
import jax
import jax.numpy as jnp
from jax import lax
from jax.experimental import pallas as pl
from jax.experimental.pallas import tpu as pltpu
from jax.experimental.pallas import tpu_sc as plsc

D_MODEL = 1024
N_META = 16
N_Q_HEADS = 16
N_KV_HEADS = 4
HEAD_DIM = 64
WINDOW = 128
ROT_DIM = HEAD_DIM // 4
ROPE_THETA = 500000.0
HG_HEADS = 8
HG_DIM = 128
PEER_HEADS = 8
PEER_NKEYS = 128
PEER_QDIM = 256
PEER_TOPK = 16
EPS = 1e-6
NEG = -1e30

BLK = 128
PAD_ROWS = BLK - N_META
IN_W = 8704
COL_AQ, COL_GA, COL_GH, COL_HQ, COL_ZF, COL_ZB, COL_HI, COL_HGG, COL_AK, COL_AV = (
    0, 1024, 2048, 3072, 4096, 5120, 6144, 7168, 8192, 8448)

VMEM_LIMIT = 56 * 1024 * 1024
PEER_TD = 16
PEER_TS = 16
SC_WINDOW = 64
PEER_SLOTS = PEER_HEADS * PEER_TOPK
HG_HP = 4
PEER_UNROLL = 2
PEER_BATCH = 8
SUBLANES = 8


def _cparams(sem):
    return pltpu.CompilerParams(dimension_semantics=sem, vmem_limit_bytes=VMEM_LIMIT)


def _rms(x, g):
    return x * lax.rsqrt(jnp.mean(x * x, axis=-1, keepdims=True) + EPS) * g


def _nt(a, b):
    return lax.dot_general(a, b, (((1,), (1,)), ((), ())), preferred_element_type=jnp.float32)


def _tn(a, b):
    return lax.dot_general(a, b, (((0,), (0,)), ((), ())), preferred_element_type=jnp.float32)


def _inproj_kernel(x_ref, g_ref, w_ref, o_ref):
    n = _rms(x_ref[...], g_ref[...]).astype(jnp.bfloat16)
    o_ref[...] = jnp.dot(n, w_ref[...], preferred_element_type=jnp.float32)


def _inproj(h2d, g, w, tm, tn):
    T = h2d.shape[0]
    return pl.pallas_call(
        _inproj_kernel,
        grid=(IN_W // tn, T // tm),
        in_specs=[pl.BlockSpec((tm, D_MODEL), lambda j, i: (i, 0)),
                  pl.BlockSpec((1, D_MODEL), lambda j, i: (0, 0)),
                  pl.BlockSpec((D_MODEL, tn), lambda j, i: (0, j))],
        out_specs=pl.BlockSpec((tm, tn), lambda j, i: (i, j)),
        out_shape=jax.ShapeDtypeStruct((T, IN_W), jnp.float32),
        compiler_params=_cparams(("parallel", "parallel")),
        name="inproj",
    )(h2d, g, w)


def _rope_tables(Lp):
    half = ROT_DIM // 2
    pos = jnp.maximum(jnp.arange(Lp) - PAD_ROWS, 0)
    inv = jnp.power(jnp.float32(ROPE_THETA), -jnp.arange(half, dtype=jnp.float32) * 2.0 / ROT_DIM)
    ang = pos.astype(jnp.float32)[:, None] * inv[None, :]
    cos, sin = jnp.cos(ang), jnp.sin(ang)
    ones = jnp.ones((Lp, HEAD_DIM - ROT_DIM), jnp.float32)
    zeros = jnp.zeros((Lp, HEAD_DIM - ROT_DIM), jnp.float32)
    zh = jnp.zeros((Lp, half), jnp.float32)
    c = jnp.concatenate([cos, cos, ones], axis=1)
    s1 = jnp.concatenate([zh, sin, zeros], axis=1)
    s2 = jnp.concatenate([-sin, zh, zeros], axis=1)
    rep = BLK // HEAD_DIM
    return tuple(jnp.tile(t, (1, rep)) for t in (c, s1, s2))


def _attn_kernel(sink_ref, q_ref, kg_ref, k0_ref, k1_ref, k2_ref, vg_ref, v0_ref, v1_ref, v2_ref,
                 c_ref, s1_ref, s2_ref, o_ref):
    r = pl.program_id(1)
    nblk = pl.num_programs(1)
    half = ROT_DIM // 2

    def rope(x, blk):
        row0 = blk * BLK if isinstance(blk, int) else pl.multiple_of(blk * BLK, BLK)
        c = c_ref[pl.ds(row0, BLK), :]
        s1 = s1_ref[pl.ds(row0, BLK), :]
        s2 = s2_ref[pl.ds(row0, BLK), :]
        outs = []
        for j in range(x.shape[1] // BLK):
            xj = x[:, j * BLK:(j + 1) * BLK]
            outs.append(xj * c + pltpu.roll(xj, half, 1) * s1 + pltpu.roll(xj, BLK - half, 1) * s2)
        return jnp.concatenate(outs, axis=1)

    kb = [jnp.clip(r - 1 + j, 0, nblk - 1) for j in range(3)]
    q = rope(q_ref[...], r) * (HEAD_DIM ** -0.5)
    kcat = jnp.concatenate([rope(kg_ref[...], 0), rope(k0_ref[...], kb[0]), rope(k1_ref[...], kb[1]),
                            rope(k2_ref[...], kb[2])], axis=0).astype(jnp.bfloat16)
    vcat = jnp.concatenate([vg_ref[...], v0_ref[...], v1_ref[...], v2_ref[...]], axis=0).astype(jnp.bfloat16)

    qrow = lax.broadcasted_iota(jnp.int32, (BLK, BLK), 0)
    kcol = lax.broadcasted_iota(jnp.int32, (BLK, BLK), 1)
    masks = [kcol >= PAD_ROWS]
    for j in range(3):
        kblk = r - 1 + j
        ok_blk = (kblk >= 1) & (kblk <= nblk - 1)
        masks.append((jnp.abs((j - 1) * BLK + kcol - qrow) <= WINDOW) & ok_blk)
    mask = jnp.concatenate(masks, axis=1)

    G = N_Q_HEADS // N_KV_HEADS
    lane_half = kcol // HEAD_DIM
    pairs_per_slice = (BLK // HEAD_DIM) * G // 2
    outs = [None] * (N_Q_HEADS // 2)
    for m2 in range(N_KV_HEADS * HEAD_DIM // BLK):
        k2 = kcat[:, m2 * BLK:(m2 + 1) * BLK]
        v2 = vcat[:, m2 * BLK:(m2 + 1) * BLK]
        qs, sks, halves = [], [], []
        for j in range(m2 * pairs_per_slice, (m2 + 1) * pairs_per_slice):
            hk = ((2 * j) // G) % 2
            in_half = lane_half == hk
            q2 = q[:, j * BLK:(j + 1) * BLK]
            qs += [jnp.where(in_half, q2, 0.0), jnp.where(in_half, pltpu.roll(q2, HEAD_DIM, 1), 0.0)]
            sks += [jnp.full((BLK, 1), sink_ref[2 * j + hk]), jnp.full((BLK, 1), sink_ref[2 * j + 1 - hk])]
            halves.append(in_half)
        nq = len(qs)
        sk = jnp.concatenate(sks, axis=0)
        s = _nt(jnp.concatenate(qs, axis=0).astype(jnp.bfloat16), k2)
        s = jnp.where(jnp.tile(mask, (nq, 1)), s, NEG)
        m = jnp.maximum(jnp.max(s, axis=-1, keepdims=True), sk)
        p = jnp.exp(s - m)
        p = p / (jnp.sum(p, axis=-1, keepdims=True) + jnp.exp(sk - m))
        o = jnp.dot(p.astype(jnp.bfloat16), v2, preferred_element_type=jnp.float32)
        for jj, in_half in enumerate(halves):
            o_same = o[(2 * jj) * BLK:(2 * jj + 1) * BLK]
            o_other = o[(2 * jj + 1) * BLK:(2 * jj + 2) * BLK]
            outs[m2 * pairs_per_slice + jj] = jnp.where(in_half, o_same, pltpu.roll(o_other, HEAD_DIM, 1))
    o_ref[...] = jnp.concatenate(outs, axis=1).astype(o_ref.dtype)


def _attention(proj3, sinks, tables):
    NB, Lp, _ = proj3.shape
    nblk = Lp // BLK
    kcol, vcol = COL_AK // 256, COL_AV // 256

    def loc(j, col):
        return pl.BlockSpec((None, BLK, 256),
                            lambda b, r: (b, jnp.clip(r - 1 + j, 0, nblk - 1), col))

    glob = lambda col: pl.BlockSpec((None, BLK, 256), lambda b, r: (b, 0, col))
    tab = pl.BlockSpec((Lp, BLK), lambda b, r: (0, 0))
    return pl.pallas_call(
        _attn_kernel,
        grid=(NB, nblk),
        in_specs=[pl.BlockSpec(memory_space=pltpu.SMEM),
                  pl.BlockSpec((None, BLK, 1024), lambda b, r: (b, r, 0)),
                  glob(kcol), loc(0, kcol), loc(1, kcol), loc(2, kcol),
                  glob(vcol), loc(0, vcol), loc(1, vcol), loc(2, vcol),
                  tab, tab, tab],
        out_specs=pl.BlockSpec((None, BLK, 1024), lambda b, r: (b, r, 0)),
        out_shape=jax.ShapeDtypeStruct((NB, Lp, 1024), jnp.bfloat16),
        compiler_params=_cparams(("parallel", "arbitrary")),
        name="window_attn",
    )(sinks, proj3, proj3, proj3, proj3, proj3, proj3, proj3, proj3, proj3, *tables)


HG_ROWS = BLK
_HG_LEVELS = (64, 32, 16, 8, 4, 2, 1)


def _hg_consts(reverse):
    C = HG_ROWS
    t = np.arange(C)[:, None]
    u = np.arange(C)[None, :]
    mats = [u <= t, u > t]
    ups, msks = [], []
    for B in _HG_LEVELS:
        same = (t // (2 * B)) == (u // (2 * B))
        tin, uin = t % (2 * B), u % (2 * B)
        upper = tin >= B
        wq = upper & same & (uin >= B) & (uin <= tin)
        wk = (~upper) & same & (uin > tin) & (uin <= B - 1)
        mats.append(wq | wk)
        ups.append(np.broadcast_to(upper, (C, HG_DIM)))
        msks.append(same & upper & (uin < B))
    msks.append(np.eye(C, dtype=bool))
    if reverse:
        mats = [m[::-1, ::-1] for m in mats]
        ups = [m[::-1] for m in ups]
        msks = [m[::-1, ::-1] for m in msks]
    W = np.concatenate(mats, axis=0).astype(np.float32)
    W2 = np.concatenate([W, W], axis=1)
    return (jnp.asarray(W2, jnp.bfloat16), jnp.asarray(np.stack(ups).astype(np.float32)),
            jnp.asarray(np.stack(msks).astype(np.float32)))


def _hg_chunk(q_raw, z, v, valid, lb, w2, up_ref, msk_ref, st, reverse):
    C = HG_ROWS
    f32, bf16 = jnp.float32, jnp.bfloat16
    log_lb = jnp.log(lb)
    log1m = jnp.log1p(-lb)
    logsig = jnp.minimum(z, 0.0) - jnp.log1p(jnp.exp(-jnp.abs(z)))
    a, b = log_lb, log1m + logsig
    g = jnp.maximum(a, b) + jnp.log1p(jnp.exp(-jnp.abs(a - b)))
    k = (1.0 - lb) * jax.nn.sigmoid(-z)
    q = q_raw * jax.nn.sigmoid(q_raw)
    q = jnp.where(valid, q, 0.0)
    k = jnp.where(valid, k, 0.0)
    v = jnp.where(valid, v, 0.0)
    g = jnp.where(valid, g, 0.0)
    g1 = g.astype(bf16)
    g2 = (g - g1.astype(f32)).astype(bf16)
    E = jnp.dot(w2, jnp.concatenate([g1, g2], axis=0), preferred_element_type=f32)
    b_inc = E[0:C]
    b_sfx = E[C:2 * C]
    b_tot = b_inc[0:1] if reverse else b_inc[C - 1:C]

    A = msk_ref[len(_HG_LEVELS)] * jnp.sum(q * k, axis=-1, keepdims=True)
    for l in range(len(_HG_LEVELS)):
        ex = jnp.exp(E[(2 + l) * C:(3 + l) * C])
        up = up_ref[l] > 0.5
        qs = jnp.where(up, q * ex, 0.0).astype(bf16)
        ks = jnp.where(up, 0.0, k * ex).astype(bf16)
        A = A + _nt(qs, ks) * msk_ref[l]

    vb = v.astype(bf16)
    o = _nt((q * jnp.exp(b_inc)).astype(bf16), st.astype(bf16))
    o = o + jnp.dot(A.astype(bf16), vb, preferred_element_type=f32)
    st = st * jnp.exp(b_tot) + _tn(vb, (k * jnp.exp(b_sfx)).astype(bf16))
    return o, st


def _hg_kernel(qf_ref, zf_ref, vf_ref, qb_ref, zb_ref, vb_ref, lb_ref, w2f_ref, upf_ref, mskf_ref,
               w2b_ref, upb_ref, mskb_ref, of_ref, ob_ref, st_ref):
    i = pl.program_id(2)
    nblk = pl.num_programs(2)

    @pl.when(i == 0)
    def _():
        st_ref[...] = jnp.zeros_like(st_ref)

    rows = lax.broadcasted_iota(jnp.int32, (HG_ROWS, HG_DIM), 0)
    dirs = ((False, qf_ref, zf_ref, vf_ref, w2f_ref, upf_ref, mskf_ref, of_ref, i),
            (True, qb_ref, zb_ref, vb_ref, w2b_ref, upb_ref, mskb_ref, ob_ref, nblk - 1 - i))
    for d, (reverse, q_ref, z_ref, v_ref, w2_ref, up_ref, msk_ref, o_ref, blk) in enumerate(dirs):
        w2 = w2_ref[...]
        for hh in range(HG_HP):
            cols = slice(hh * HG_DIM, (hh + 1) * HG_DIM)
            lb = lb_ref[d, :, cols]
            valid = (rows >= PAD_ROWS) | (blk > 0)
            o, st = _hg_chunk(q_ref[:, cols], z_ref[:, cols], v_ref[:, cols], valid, lb, w2,
                              up_ref, msk_ref, st_ref[d, hh], reverse)
            st_ref[d, hh] = st
            o_ref[:, cols] = o


def _hgrn(proj3, lb3, consts_f, consts_b):
    NB, Lp, _ = proj3.shape
    nblk = Lp // BLK
    W = HG_DIM * HG_HP

    def col(base, reverse):
        return pl.BlockSpec((None, BLK, W),
                            lambda b, h, i: (b, (nblk - 1 - i) if reverse else i, base // W + h))

    def const(c):
        return pl.BlockSpec(c.shape, lambda b, h, i: (0,) * c.ndim)

    in_specs = [col(COL_HQ, False), col(COL_ZF, False), col(COL_HI, False),
                col(COL_HQ, True), col(COL_ZB, True), col(COL_HI, True),
                pl.BlockSpec((2, 1, W), lambda b, h, i: (0, 0, h))]
    in_specs += [const(c) for c in consts_f] + [const(c) for c in consts_b]
    oshape = jax.ShapeDtypeStruct((NB, Lp, HG_HEADS * HG_DIM), jnp.float32)
    return pl.pallas_call(
        _hg_kernel,
        grid=(NB, HG_HEADS // HG_HP, nblk),
        in_specs=in_specs,
        out_specs=(pl.BlockSpec((None, BLK, W), lambda b, h, i: (b, i, h)),
                   pl.BlockSpec((None, BLK, W), lambda b, h, i: (b, nblk - 1 - i, h))),
        out_shape=(oshape, oshape),
        scratch_shapes=[pltpu.VMEM((2, HG_HP, HG_DIM, HG_DIM), jnp.float32)],
        compiler_params=_cparams(("parallel", "parallel", "arbitrary")),
        name="hgrn_bidir",
    )(proj3, proj3, proj3, proj3, proj3, proj3, lb3, *consts_f, *consts_b)


def _merge_kernel(a_ref, of_ref, ob_ref, hgg_ref, ng_ref, ga_ref, gh_ref, h_ref, wa_ref, wh_ref, wo_ref, o_ref):
    f32, bf16 = jnp.float32, jnp.bfloat16
    o = of_ref[...] + ob_ref[...]
    ng = ng_ref[...]
    gt = hgg_ref[...]
    hh = jnp.concatenate([_rms(o[:, c:c + HG_DIM], ng[:, c:c + HG_DIM]) for c in range(0, HG_HEADS * HG_DIM, HG_DIM)],
                         axis=1) * (gt * jax.nn.sigmoid(gt))
    pa = jnp.dot(a_ref[...], wa_ref[...], preferred_element_type=f32)
    ph = jnp.dot(hh.astype(bf16), wh_ref[...], preferred_element_type=f32)
    merged = jax.nn.sigmoid(ga_ref[...]) * pa + jax.nn.sigmoid(gh_ref[...]) * ph
    o_ref[...] = h_ref[...] + jnp.dot(merged.astype(bf16), wo_ref[...], preferred_element_type=f32)


def _merge(a2d, of2d, ob2d, ng, proj2d, h2d, wa, wh, wo, tm):
    T = h2d.shape[0]
    row = lambda c: pl.BlockSpec((tm, D_MODEL), lambda i: (i, c))
    wspec = pl.BlockSpec((D_MODEL, D_MODEL), lambda i: (0, 0))
    return pl.pallas_call(
        _merge_kernel,
        grid=(T // tm,),
        in_specs=[row(0), row(0), row(0), row(COL_HGG // D_MODEL), pl.BlockSpec((1, D_MODEL), lambda i: (0, 0)),
                  row(COL_GA // D_MODEL), row(COL_GH // D_MODEL), row(0), wspec, wspec, wspec],
        out_specs=row(0),
        out_shape=jax.ShapeDtypeStruct((T, D_MODEL), jnp.float32),
        input_output_aliases={7: 0},
        compiler_params=_cparams(("parallel",)),
        name="merge_out",
    )(a2d, of2d, ob2d, proj2d, ng, proj2d, proj2d, h2d, wa, wh, wo)


def _topk_axis0(s, kk):
    n = s.shape[0]
    iota = lax.broadcasted_iota(jnp.int32, s.shape, 0)
    vals, ids = [], []
    for _ in range(kk):
        m = jnp.max(s, axis=0, keepdims=True)
        am = jnp.min(jnp.where(s == m, iota, n), axis=0, keepdims=True)
        s = jnp.where(iota == am, -jnp.inf, s)
        vals.append(m)
        ids.append(am)
    return vals, ids


def _stack_rows(rows, dtype):
    n = len(rows)
    iota = lax.broadcasted_iota(jnp.int32, (n, rows[0].shape[1]), 0)
    out = jnp.zeros((n, rows[0].shape[1]), dtype)
    for j, rj in enumerate(rows):
        out = jnp.where(iota == j, rj, out)
    return out


def _route_kernel(h_ref, g_ref, wq_ref, key_ref, eidx_ref, gate_ref, xn_ref):
    @pl.when(pl.program_id(1) == 0)
    def _():
        xn_ref[...] = _rms(h_ref[...], g_ref[...]).astype(jnp.bfloat16)

    K = PEER_TOPK
    half = PEER_QDIM // 2
    i32 = jnp.int32
    qry = jnp.dot(xn_ref[...], wq_ref[...], preferred_element_type=jnp.float32).astype(jnp.bfloat16)
    s0 = _nt(key_ref[0], qry[:, :half])
    s1 = _nt(key_ref[1], qry[:, half:])
    tm = s0.shape[1]
    v0, i0 = _topk_axis0(s0, K)
    v1, i1 = _topk_axis0(s1, K)
    m1 = _stack_rows(v1, jnp.float32)
    m0_hi = _stack_rows(v0[K // 2:], jnp.float32)
    r8 = lax.broadcasted_iota(i32, (K // 2, tm), 0)
    cand = [v0[0] + m1] + [v0[a] + m1[:K // 2] for a in range(1, K // 2)] + [m0_hi + v1[0]]
    flat = [lax.broadcasted_iota(i32, (K, tm), 0)] + [a * K + r8 for a in range(1, K // 2)] + [(K // 2 + r8) * K]
    cand = jnp.concatenate(cand, axis=0)
    flat = jnp.concatenate(flat, axis=0)
    big = K * K
    ts, tp = [], []
    for _ in range(K):
        m = jnp.max(cand, axis=0, keepdims=True)
        p = jnp.min(jnp.where(cand == m, flat, big), axis=0, keepdims=True)
        cand = jnp.where(flat == p, -jnp.inf, cand)
        ts.append(m)
        tp.append(p)
    pk = _stack_rows(tp, i32)
    pa, pb = pk // K, pk % K
    e0 = jnp.zeros((K, tm), i32)
    e1 = jnp.zeros((K, tm), i32)
    for a in range(K):
        e0 = jnp.where(pa == a, i0[a], e0)
        e1 = jnp.where(pb == a, i1[a], e1)
    ex = [jnp.exp(t - ts[0]) for t in ts]
    den = ex[0]
    for e in ex[1:]:
        den = den + e
    eidx_ref[...] = e0 * PEER_NKEYS + e1
    gate_ref[...] = _stack_rows([e / den for e in ex], jnp.float32)


def _route(h2d, g, wq, keys, tm):
    T = h2d.shape[0]
    out = jax.ShapeDtypeStruct((PEER_SLOTS, T), jnp.int32), jax.ShapeDtypeStruct((PEER_SLOTS, T), jnp.float32)
    ospec = pl.BlockSpec((PEER_TOPK, tm), lambda i, h: (h, i))
    return pl.pallas_call(
        _route_kernel,
        grid=(T // tm, PEER_HEADS),
        in_specs=[pl.BlockSpec((tm, D_MODEL), lambda i, h: (i, 0)),
                  pl.BlockSpec((1, D_MODEL), lambda i, h: (0, 0)),
                  pl.BlockSpec((D_MODEL, PEER_QDIM), lambda i, h: (0, h)),
                  pl.BlockSpec((2, PEER_NKEYS, PEER_QDIM // 2), lambda i, h: (h, 0, 0))],
        out_specs=(ospec, ospec),
        out_shape=out,
        scratch_shapes=[pltpu.VMEM((tm, D_MODEL), jnp.bfloat16)],
        compiler_params=_cparams(("parallel", "arbitrary")),
        name="peer_route",
    )(h2d, g, wq, keys)


def _pack_bf16_pairs(x):
    e, d = x.shape
    bits = lax.bitcast_convert_type(x.astype(jnp.bfloat16), jnp.uint16).astype(jnp.uint32)
    bits = bits.reshape(e, d // 256, 2, 128)
    return bits[:, :, 0, :] | (bits[:, :, 1, :] << 16)


def _sc_gather(u4, v4, idx):
    M = idx.shape[0]
    info = plsc.get_sparse_core_info()
    nw = info.num_cores * info.num_subcores
    per = M // nw
    assert per * nw == M and per % SC_WINDOW == 0
    mesh = plsc.VectorSubcoreMesh(core_axis_name="c", subcore_axis_name="s")
    oshape = jax.ShapeDtypeStruct((M,) + u4.shape[1:], u4.dtype)
    rows = pltpu.VMEM((SC_WINDOW,) + u4.shape[1:], u4.dtype)

    @functools.partial(
        pl.kernel, mesh=mesh, out_type=(oshape, oshape),
        scratch_types=[pltpu.VMEM((SC_WINDOW,), jnp.int32), rows, rows, pltpu.SemaphoreType.DMA, pltpu.SemaphoreType.DMA],
    )
    def gather(u_hbm, v_hbm, idx_hbm, ou_hbm, ov_hbm, idx_v, ru, rv, sem_u, sem_v):
        wid = lax.axis_index("s") * info.num_cores + lax.axis_index("c")
        base = wid * per

        @pl.loop(0, per // SC_WINDOW)
        def _(g):
            off = pl.multiple_of(base + g * SC_WINDOW, SC_WINDOW)
            pltpu.sync_copy(idx_hbm.at[pl.ds(off, SC_WINDOW)], idx_v)
            cu = pltpu.async_copy(u_hbm.at[idx_v], ru, sem_u)
            cv = pltpu.async_copy(v_hbm.at[idx_v], rv, sem_v)
            cu.wait()
            cv.wait()
            pltpu.sync_copy(ru, ou_hbm.at[pl.ds(off, SC_WINDOW)])
            pltpu.sync_copy(rv, ov_hbm.at[pl.ds(off, SC_WINDOW)])

    return gather(u4, v4, idx)


def _expert_kernel(idx_ref, idxn_ref, h_ref, g_ref, gate_ref, spread_ref, su_ref, sv_ref, uv_hbm, o_ref, buf, z_ref,
                   xn_ref, y_ref, sem):
    TD, TS, NS = PEER_TD, PEER_TS, PEER_SLOTS
    TT, LW = TD + TS, SUBLANES * PEER_SLOTS
    PR = SUBLANES // 2
    f32, bf16 = jnp.float32, jnp.bfloat16
    i = pl.program_id(0)
    n = pl.num_programs(0)
    slot = lax.rem(i, 2)

    def issue_token(iref, sl, t):
        for k in range(NS):
            pltpu.make_async_copy(uv_hbm.at[iref[t, k]], buf.at[sl, :, pl.ds((t * NS + k) * PR, PR)],
                                  sem.at[sl]).start(priority=k % 2)

    @pl.when(i == 0)
    def _():
        def body(t, c):
            issue_token(idx_ref, 0, t)
            return c
        lax.fori_loop(0, TD, body, 0)

    pltpu.make_async_copy(buf.at[slot], buf.at[slot], sem.at[slot]).wait()

    x = h_ref[...]
    xn_ref[...] = _rms(x, g_ref[...]).reshape(TT, SUBLANES, 128)
    on_diag = (lax.broadcasted_iota(jnp.int32, (SUBLANES, LW), 1) % SUBLANES
               == lax.broadcasted_iota(jnp.int32, (SUBLANES, LW), 0))

    def pieces_of(t, staged, part):
        r = pl.ds(pl.multiple_of(t * (NS * PR), NS * PR), NS * PR)
        words = (sv_ref if part else su_ref)[r, :] if staged else buf[slot, part, r, :]
        return pltpu.bitcast(words, bf16)

    def score_token(t, staged):
        tg = t + TD if staged else t
        z = _nt(xn_ref[tg].astype(bf16), pieces_of(t, staged, 0))
        z_ref[pl.ds(tg, 1), :] = jnp.sum(jnp.where(on_diag, z, 0.0), axis=0, keepdims=True)

    def mix_token(t, staged):
        tg = t + TD if staged else t
        w = jnp.where(on_diag, jnp.broadcast_to(z_ref[pl.ds(tg, 1), :], (SUBLANES, LW)), 0.0).astype(bf16)
        y_ref[tg] = jnp.dot(w, pieces_of(t, staged, 1), preferred_element_type=f32)

    def batched(count, fn):
        def body(tt, c):
            for j in range(PEER_BATCH):
                fn(tt * PEER_BATCH + j)
            return c
        lax.fori_loop(0, count // PEER_BATCH, body, 0)

    @pl.when(i + 1 < n)
    def _():
        def body(tt, c):
            for j in range(PEER_UNROLL):
                score_token(tt * PEER_UNROLL + j, False)
            for j in range(PEER_UNROLL):
                issue_token(idxn_ref, 1 - slot, tt * PEER_UNROLL + j)
            return c
        lax.fori_loop(0, TD // PEER_UNROLL, body, 0)

    @pl.when(i + 1 == n)
    def _():
        batched(TD, lambda t: score_token(t, False))

    batched(TS, lambda t: score_token(t, True))

    z = z_ref[...]
    z = z + pltpu.roll(z, 4, 1)
    z = z + pltpu.roll(z, 2, 1)
    z = z + pltpu.roll(z, 1, 1)
    gt = gate_ref[...]
    g1 = gt.astype(bf16)
    r1 = gt - g1.astype(f32)
    g2 = r1.astype(bf16)
    g3 = (r1 - g2.astype(f32)).astype(bf16)
    spread = spread_ref[...]
    gate8 = (jnp.dot(g1, spread, preferred_element_type=f32) + jnp.dot(g2, spread, preferred_element_type=f32)
             + jnp.dot(g3, spread, preferred_element_type=f32))
    w = gate8 * (0.5 * z * (1.0 + lax.erf(z * (2.0 ** -0.5))))
    w = w + pltpu.roll(w, LW - 1, 1)
    w = w + pltpu.roll(w, LW - 2, 1)
    z_ref[...] = w + pltpu.roll(w, LW - 4, 1)

    batched(TD, lambda t: mix_token(t, False))
    batched(TS, lambda t: mix_token(t, True))
    o_ref[...] = x + y_ref[...].reshape(TT, D_MODEL)


def _experts_chunk(step0, nsteps, idx_d, h2d, g, gate, spread, su, sv, uv):
    TD, TS, NS = PEER_TD, PEER_TS, PEER_SLOTS
    TT, PR = TD + TS, SUBLANES // 2
    last = step0 + nsteps - 1
    staged = pl.BlockSpec((TS * NS * PR, 128), lambda i: (i, 0))
    return pl.pallas_call(
        _expert_kernel,
        grid=(nsteps,),
        in_specs=[pl.BlockSpec((TD, NS), lambda i: (step0 + i, 0), memory_space=pltpu.SMEM),
                  pl.BlockSpec((TD, NS), lambda i: (jnp.minimum(step0 + i + 1, last), 0), memory_space=pltpu.SMEM),
                  pl.BlockSpec((TT, D_MODEL), lambda i: (step0 + i, 0)),
                  pl.BlockSpec((1, D_MODEL), lambda i: (0, 0)),
                  pl.BlockSpec((TT, NS), lambda i: (step0 + i, 0)),
                  pl.BlockSpec((NS, SUBLANES * NS), lambda i: (0, 0)),
                  staged, staged,
                  pl.BlockSpec(memory_space=pl.ANY)],
        out_specs=pl.BlockSpec((TT, D_MODEL), lambda i: (step0 + i, 0)),
        out_shape=jax.ShapeDtypeStruct(h2d.shape, jnp.float32),
        scratch_shapes=[pltpu.VMEM((2, 2, TD * NS * PR, 128), jnp.uint32),
                        pltpu.VMEM((TT, SUBLANES * NS), jnp.float32),
                        pltpu.VMEM((TT, SUBLANES, 128), jnp.float32),
                        pltpu.VMEM((TT, SUBLANES, 128), jnp.float32),
                        pltpu.SemaphoreType.DMA((2,))],
        input_output_aliases={2: 0},
        compiler_params=_cparams(("arbitrary",)),
        name="peer_experts",
    )(idx_d, idx_d, h2d, g, gate, spread, su, sv, uv)


def _experts(h2d, g, eidx, gate, u4, v4):
    T = h2d.shape[0]
    TD, TS, NS = PEER_TD, PEER_TS, PEER_SLOTS
    TT = TD + TS
    steps = T // TT
    chunks = _pick_tile(steps, (12, 8, 6, 4, 3, 2, 1))
    spc = steps // chunks
    e3 = eidx.reshape(steps, TT, NS)
    idx_d = e3[:, :TD].reshape(steps * TD, NS)
    idx_s = e3[:, TD:].reshape(chunks, spc * TS * NS)
    spread = np.zeros((NS, SUBLANES * NS), np.float32)
    spread[np.arange(NS), SUBLANES * np.arange(NS) + SUBLANES - 1] = 1.0
    spread = jnp.asarray(spread, jnp.bfloat16)
    uv = jnp.stack([u4, v4], axis=1)
    for c in range(chunks):
        su, sv = _sc_gather(u4, v4, idx_s[c])
        su, sv = (a.reshape(-1, 128) for a in (su, sv))
        h2d = _experts_chunk(c * spc, spc, idx_d, h2d, g, gate, spread, su, sv, uv)
    return h2d


def _final_kernel(h_ref, g_ref, o_ref):
    o_ref[...] = _rms(h_ref[...], g_ref[...])


def _final(h3, g, b0, nb):
    _, Lp, _ = h3.shape
    S = Lp - BLK
    return pl.pallas_call(
        _final_kernel,
        grid=(nb, S // BLK),
        in_specs=[pl.BlockSpec((None, BLK, D_MODEL), lambda b, r: (b + b0, r + 1, 0)),
                  pl.BlockSpec((1, D_MODEL), lambda b, r: (0, 0))],
        out_specs=pl.BlockSpec((None, BLK, D_MODEL), lambda b, r: (b, r, 0)),
        out_shape=jax.ShapeDtypeStruct((nb, S, D_MODEL), jnp.float32),
        compiler_params=_cparams(("parallel", "parallel")),
        name="final_norm",
    )(h3, g)


def _pick_tile(T, candidates):
    for c in candidates:
        if T % c == 0:
            return c
    raise ValueError(f"no tile for {T}")


def kernel(x_prompt, x_sample, meta_tokens, g_mix, w_in, attn_sinks, lb_logits, hg_norm_g, w_att_branch,
           w_hg_branch, w_out, g_ffn, w_peer_q, peer_sub_keys, peer_u, peer_v, g_final):
    f32, bf16 = jnp.float32, jnp.bfloat16
    depth = w_in.shape[0]
    nb_p, S, _ = x_prompt.shape
    nb_s = x_sample.shape[0]
    assert x_sample.shape[1] == S and S % BLK == 0
    NB = nb_p + nb_s
    Lp = S + BLK
    T = NB * Lp

    x = jnp.concatenate([x_prompt, x_sample], axis=0)
    head = jnp.concatenate([jnp.zeros((PAD_ROWS, D_MODEL), f32), meta_tokens.astype(f32)], axis=0)
    h = jnp.concatenate([jnp.broadcast_to(head[None], (NB, BLK, D_MODEL)), x], axis=1).reshape(T, D_MODEL)

    sizes = (1024, 256, 256, 1024, 1024, 1024, 1024, 1024, 1024, 1024)
    offs = np.concatenate([[0], np.cumsum(sizes)])
    order = (0, 8, 9, 3, 4, 5, 6, 7, 1, 2)
    perm = np.concatenate([np.arange(offs[k], offs[k + 1]) for k in order])

    sm = jax.nn.softmax(lb_logits.astype(f32), axis=0)
    lb_all = jnp.maximum(jnp.cumsum(sm, axis=0) - sm[0:1], 0.0)

    tables = _rope_tables(Lp)
    consts_f = _hg_consts(False)
    consts_b = _hg_consts(True)
    tm_big = _pick_tile(T, (512, 384, 256, 128))
    tm_route = _pick_tile(T, (256, 128))

    for l in range(depth):
        w_l = w_in[l][:, perm].astype(bf16)
        proj = _inproj(h, g_mix[l].reshape(1, -1), w_l, tm_big, IN_W // 4)
        proj3 = proj.reshape(NB, Lp, IN_W)
        a_out = _attention(proj3, attn_sinks[l].astype(f32), tables)
        lb3 = lb_all[l].reshape(2, 1, -1)
        o_fwd, o_bwd = _hgrn(proj3, lb3, consts_f, consts_b)
        h = _merge(a_out.reshape(T, -1), o_fwd.reshape(T, -1), o_bwd.reshape(T, -1), hg_norm_g[l].reshape(1, -1),
                   proj, h, w_att_branch[l].astype(bf16), w_hg_branch[l].astype(bf16), w_out[l].astype(bf16),
                   tm_route)
        g_f = g_ffn[l].reshape(1, -1)
        keys = peer_sub_keys[l].reshape(2 * PEER_HEADS, PEER_NKEYS, PEER_QDIM // 2).astype(bf16)
        eidx_t, gate_t = _route(h, g_f, w_peer_q[l].astype(bf16), keys, tm_route)
        h = _experts(h, g_f, eidx_t.T, gate_t.T, _pack_bf16_pairs(peer_u[l]), _pack_bf16_pairs(peer_v[l]))

    h3 = h.reshape(NB, Lp, D_MODEL)
    g_fin = g_final.reshape(1, -1)
    return _final(h3, g_fin, 0, nb_p), _final(h3, g_fin, nb_p, nb_s)
```

```python
import functools

import numpy as np
import jax
import jax.numpy as jnp
from jax import lax
from jax.experimental import pallas as pl
from jax.experimental.pallas import tpu as pltpu
from jax.experimental.pallas import tpu_sc as plsc

D_MODEL = 1024
N_META = 16
N_Q_HEADS = 16
N_KV_HEADS = 4
HEAD_DIM = 64
WINDOW = 128
ROT_DIM = HEAD_DIM // 4
ROPE_THETA = 500000.0
HG_HEADS = 8
HG_DIM = 128
PEER_HEADS = 8
PEER_NKEYS = 128
PEER_QDIM = 256
PEER_TOPK = 16
EPS = 1e-6
NEG = -1e30

BLK = 128
PAD_ROWS = BLK - N_META
IN_W = 8704
COL_AQ, COL_GA, COL_GH, COL_HQ, COL_ZF, COL_ZB, COL_HI, COL_HGG, COL_AK, COL_AV = (
    0, 1024, 2048, 3072, 4096, 5120, 6144, 7168, 8192, 8448)

VMEM_LIMIT = 56 * 1024 * 1024
PEER_TD = 16
PEER_TS = 16
SC_WINDOW = 64
PEER_SLOTS = PEER_HEADS * PEER_TOPK
HG_HP = 4
PEER_UNROLL = 2
PEER_BATCH = 8
SUBLANES = 8


def _cparams(sem):
    return pltpu.CompilerParams(dimension_semantics=sem, vmem_limit_bytes=VMEM_LIMIT)


def _rms(x, g):
    return x * lax.rsqrt(jnp.mean(x * x, axis=-1, keepdims=True) + EPS) * g


def _nt(a, b):
    return lax.dot_general(a, b, (((1,), (1,)), ((), ())), preferred_element_type=jnp.float32)


def _tn(a, b):
    return lax.dot_general(a, b, (((0,), (0,)), ((), ())), preferred_element_type=jnp.float32)


def _inproj_kernel(x_ref, g_ref, w_ref, o_ref):
    n = _rms(x_ref[...], g_ref[...]).astype(jnp.bfloat16)
    o_ref[...] = jnp.dot(n, w_ref[...], preferred_element_type=jnp.float32)


def _inproj(h2d, g, w, tm, tn):
    T = h2d.shape[0]
    return pl.pallas_call(
        _inproj_kernel,
        grid=(IN_W // tn, T // tm),
        in_specs=[pl.BlockSpec((tm, D_MODEL), lambda j, i: (i, 0)),
                  pl.BlockSpec((1, D_MODEL), lambda j, i: (0, 0)),
                  pl.BlockSpec((D_MODEL, tn), lambda j, i: (0, j))],
        out_specs=pl.BlockSpec((tm, tn), lambda j, i: (i, j)),
        out_shape=jax.ShapeDtypeStruct((T, IN_W), jnp.float32),
        compiler_params=_cparams(("parallel", "parallel")),
        name="inproj",
    )(h2d, g, w)


def _rope_tables(Lp):
    half = ROT_DIM // 2
    pos = jnp.maximum(jnp.arange(Lp) - PAD_ROWS, 0)
    inv = jnp.power(jnp.float32(ROPE_THETA), -jnp.arange(half, dtype=jnp.float32) * 2.0 / ROT_DIM)
    ang = pos.astype(jnp.float32)[:, None] * inv[None, :]
    cos, sin = jnp.cos(ang), jnp.sin(ang)
    ones = jnp.ones((Lp, HEAD_DIM - ROT_DIM), jnp.float32)
    zeros = jnp.zeros((Lp, HEAD_DIM - ROT_DIM), jnp.float32)
    zh = jnp.zeros((Lp, half), jnp.float32)
    c = jnp.concatenate([cos, cos, ones], axis=1)
    s1 = jnp.concatenate([zh, sin, zeros], axis=1)
    s2 = jnp.concatenate([-sin, zh, zeros], axis=1)
    rep = BLK // HEAD_DIM
    return tuple(jnp.tile(t, (1, rep)) for t in (c, s1, s2))


def _attn_kernel(sink_ref, q_ref, kg_ref, k0_ref, k1_ref, k2_ref, vg_ref, v0_ref, v1_ref, v2_ref,
                 c_ref, s1_ref, s2_ref, o_ref):
    r = pl.program_id(1)
    nblk = pl.num_programs(1)
    half = ROT_DIM // 2

    def rope(x, blk):
        row0 = blk * BLK if isinstance(blk, int) else pl.multiple_of(blk * BLK, BLK)
        c = c_ref[pl.ds(row0, BLK), :]
        s1 = s1_ref[pl.ds(row0, BLK), :]
        s2 = s2_ref[pl.ds(row0, BLK), :]
        outs = []
        for j in range(x.shape[1] // BLK):
            xj = x[:, j * BLK:(j + 1) * BLK]
            outs.append(xj * c + pltpu.roll(xj, half, 1) * s1 + pltpu.roll(xj, BLK - half, 1) * s2)
        return jnp.concatenate(outs, axis=1)

    kb = [jnp.clip(r - 1 + j, 0, nblk - 1) for j in range(3)]
    q = rope(q_ref[...], r) * (HEAD_DIM ** -0.5)
    kcat = jnp.concatenate([rope(kg_ref[...], 0), rope(k0_ref[...], kb[0]), rope(k1_ref[...], kb[1]),
                            rope(k2_ref[...], kb[2])], axis=0).astype(jnp.bfloat16)
    vcat = jnp.concatenate([vg_ref[...], v0_ref[...], v1_ref[...], v2_ref[...]], axis=0).astype(jnp.bfloat16)

    qrow = lax.broadcasted_iota(jnp.int32, (BLK, BLK), 0)
    kcol = lax.broadcasted_iota(jnp.int32, (BLK, BLK), 1)
    masks = [kcol >= PAD_ROWS]
    for j in range(3):
        kblk = r - 1 + j
        ok_blk = (kblk >= 1) & (kblk <= nblk - 1)
        masks.append((jnp.abs((j - 1) * BLK + kcol - qrow) <= WINDOW) & ok_blk)
    mask = jnp.concatenate(masks, axis=1)

    G = N_Q_HEADS // N_KV_HEADS
    lane_half = kcol // HEAD_DIM
    pairs_per_slice = (BLK // HEAD_DIM) * G // 2
    outs = [None] * (N_Q_HEADS // 2)
    for m2 in range(N_KV_HEADS * HEAD_DIM // BLK):
        k2 = kcat[:, m2 * BLK:(m2 + 1) * BLK]
        v2 = vcat[:, m2 * BLK:(m2 + 1) * BLK]
        qs, sks, halves = [], [], []
        for j in range(m2 * pairs_per_slice, (m2 + 1) * pairs_per_slice):
            hk = ((2 * j) // G) % 2
            in_half = lane_half == hk
            q2 = q[:, j * BLK:(j + 1) * BLK]
            qs += [jnp.where(in_half, q2, 0.0), jnp.where(in_half, pltpu.roll(q2, HEAD_DIM, 1), 0.0)]
            sks += [jnp.full((BLK, 1), sink_ref[2 * j + hk]), jnp.full((BLK, 1), sink_ref[2 * j + 1 - hk])]
            halves.append(in_half)
        nq = len(qs)
        sk = jnp.concatenate(sks, axis=0)
        s = _nt(jnp.concatenate(qs, axis=0).astype(jnp.bfloat16), k2)
        s = jnp.where(jnp.tile(mask, (nq, 1)), s, NEG)
        m = jnp.maximum(jnp.max(s, axis=-1, keepdims=True), sk)
        p = jnp.exp(s - m)
        p = p / (jnp.sum(p, axis=-1, keepdims=True) + jnp.exp(sk - m))
        o = jnp.dot(p.astype(jnp.bfloat16), v2, preferred_element_type=jnp.float32)
        for jj, in_half in enumerate(halves):
            o_same = o[(2 * jj) * BLK:(2 * jj + 1) * BLK]
            o_other = o[(2 * jj + 1) * BLK:(2 * jj + 2) * BLK]
            outs[m2 * pairs_per_slice + jj] = jnp.where(in_half, o_same, pltpu.roll(o_other, HEAD_DIM, 1))
    o_ref[...] = jnp.concatenate(outs, axis=1).astype(o_ref.dtype)


def _attention(proj3, sinks, tables):
    NB, Lp, _ = proj3.shape
    nblk = Lp // BLK
    kcol, vcol = COL_AK // 256, COL_AV // 256

    def loc(j, col):
        return pl.BlockSpec((None, BLK, 256),
                            lambda b, r: (b, jnp.clip(r - 1 + j, 0, nblk - 1), col))

    glob = lambda col: pl.BlockSpec((None, BLK, 256), lambda b, r: (b, 0, col))
    tab = pl.BlockSpec((Lp, BLK), lambda b, r: (0, 0))
    return pl.pallas_call(
        _attn_kernel,
        grid=(NB, nblk),
        in_specs=[pl.BlockSpec(memory_space=pltpu.SMEM),
                  pl.BlockSpec((None, BLK, 1024), lambda b, r: (b, r, 0)),
                  glob(kcol), loc(0, kcol), loc(1, kcol), loc(2, kcol),
                  glob(vcol), loc(0, vcol), loc(1, vcol), loc(2, vcol),
                  tab, tab, tab],
        out_specs=pl.BlockSpec((None, BLK, 1024), lambda b, r: (b, r, 0)),
        out_shape=jax.ShapeDtypeStruct((NB, Lp, 1024), jnp.bfloat16),
        compiler_params=_cparams(("parallel", "arbitrary")),
        name="window_attn",
    )(sinks, proj3, proj3, proj3, proj3, proj3, proj3, proj3, proj3, proj3, *tables)


HG_ROWS = BLK
_HG_LEVELS = (64, 32, 16, 8, 4, 2, 1)


def _hg_consts(reverse):
    C = HG_ROWS
    t = np.arange(C)[:, None]
    u = np.arange(C)[None, :]
    mats = [u <= t, u > t]
    ups, msks = [], []
    for B in _HG_LEVELS:
        same = (t // (2 * B)) == (u // (2 * B))
        tin, uin = t % (2 * B), u % (2 * B)
        upper = tin >= B
        wq = upper & same & (uin >= B) & (uin <= tin)
        wk = (~upper) & same & (uin > tin) & (uin <= B - 1)
        mats.append(wq | wk)
        ups.append(np.broadcast_to(upper, (C, HG_DIM)))
        msks.append(same & upper & (uin < B))
    msks.append(np.eye(C, dtype=bool))
    if reverse:
        mats = [m[::-1, ::-1] for m in mats]
        ups = [m[::-1] for m in ups]
        msks = [m[::-1, ::-1] for m in msks]
    W = np.concatenate(mats, axis=0).astype(np.float32)
    W2 = np.concatenate([W, W], axis=1)
    return (jnp.asarray(W2, jnp.bfloat16), jnp.asarray(np.stack(ups).astype(np.float32)),
            jnp.asarray(np.stack(msks).astype(np.float32)))


def _hg_chunk(q_raw, z, v, valid, lb, w2, up_ref, msk_ref, st, reverse):
    C = HG_ROWS
    f32, bf16 = jnp.float32, jnp.bfloat16
    log_lb = jnp.log(lb)
    log1m = jnp.log1p(-lb)
    logsig = jnp.minimum(z, 0.0) - jnp.log1p(jnp.exp(-jnp.abs(z)))
    a, b = log_lb, log1m + logsig
    g = jnp.maximum(a, b) + jnp.log1p(jnp.exp(-jnp.abs(a - b)))
    k = (1.0 - lb) * jax.nn.sigmoid(-z)
    q = q_raw * jax.nn.sigmoid(q_raw)
    q = jnp.where(valid, q, 0.0)
    k = jnp.where(valid, k, 0.0)
    v = jnp.where(valid, v, 0.0)
    g = jnp.where(valid, g, 0.0)
    g1 = g.astype(bf16)
    g2 = (g - g1.astype(f32)).astype(bf16)
    E = jnp.dot(w2, jnp.concatenate([g1, g2], axis=0), preferred_element_type=f32)
    b_inc = E[0:C]
    b_sfx = E[C:2 * C]
    b_tot = b_inc[0:1] if reverse else b_inc[C - 1:C]

    A = msk_ref[len(_HG_LEVELS)] * jnp.sum(q * k, axis=-1, keepdims=True)
    for l in range(len(_HG_LEVELS)):
        ex = jnp.exp(E[(2 + l) * C:(3 + l) * C])
        up = up_ref[l] > 0.5
        qs = jnp.where(up, q * ex, 0.0).astype(bf16)
        ks = jnp.where(up, 0.0, k * ex).astype(bf16)
        A = A + _nt(qs, ks) * msk_ref[l]

    vb = v.astype(bf16)
    o = _nt((q * jnp.exp(b_inc)).astype(bf16), st.astype(bf16))
    o = o + jnp.dot(A.astype(bf16), vb, preferred_element_type=f32)
    st = st * jnp.exp(b_tot) + _tn(vb, (k * jnp.exp(b_sfx)).astype(bf16))
    return o, st


def _hg_kernel(qf_ref, zf_ref, vf_ref, qb_ref, zb_ref, vb_ref, lb_ref, w2f_ref, upf_ref, mskf_ref,
               w2b_ref, upb_ref, mskb_ref, of_ref, ob_ref, st_ref):
    i = pl.program_id(2)
    nblk = pl.num_programs(2)

    @pl.when(i == 0)
    def _():
        st_ref[...] = jnp.zeros_like(st_ref)

    rows = lax.broadcasted_iota(jnp.int32, (HG_ROWS, HG_DIM), 0)
    dirs = ((False, qf_ref, zf_ref, vf_ref, w2f_ref, upf_ref, mskf_ref, of_ref, i),
            (True, qb_ref, zb_ref, vb_ref, w2b_ref, upb_ref, mskb_ref, ob_ref, nblk - 1 - i))
    for d, (reverse, q_ref, z_ref, v_ref, w2_ref, up_ref, msk_ref, o_ref, blk) in enumerate(dirs):
        w2 = w2_ref[...]
        for hh in range(HG_HP):
            cols = slice(hh * HG_DIM, (hh + 1) * HG_DIM)
            lb = lb_ref[d, :, cols]
            valid = (rows >= PAD_ROWS) | (blk > 0)
            o, st = _hg_chunk(q_ref[:, cols], z_ref[:, cols], v_ref[:, cols], valid, lb, w2,
                              up_ref, msk_ref, st_ref[d, hh], reverse)
            st_ref[d, hh] = st
            o_ref[:, cols] = o


def _hgrn(proj3, lb3, consts_f, consts_b):
    NB, Lp, _ = proj3.shape
    nblk = Lp // BLK
    W = HG_DIM * HG_HP

    def col(base, reverse):
        return pl.BlockSpec((None, BLK, W),
                            lambda b, h, i: (b, (nblk - 1 - i) if reverse else i, base // W + h))

    def const(c):
        return pl.BlockSpec(c.shape, lambda b, h, i: (0,) * c.ndim)

    in_specs = [col(COL_HQ, False), col(COL_ZF, False), col(COL_HI, False),
                col(COL_HQ, True), col(COL_ZB, True), col(COL_HI, True),
                pl.BlockSpec((2, 1, W), lambda b, h, i: (0, 0, h))]
    in_specs += [const(c) for c in consts_f] + [const(c) for c in consts_b]
    oshape = jax.ShapeDtypeStruct((NB, Lp, HG_HEADS * HG_DIM), jnp.float32)
    return pl.pallas_call(
        _hg_kernel,
        grid=(NB, HG_HEADS // HG_HP, nblk),
        in_specs=in_specs,
        out_specs=(pl.BlockSpec((None, BLK, W), lambda b, h, i: (b, i, h)),
                   pl.BlockSpec((None, BLK, W), lambda b, h, i: (b, nblk - 1 - i, h))),
        out_shape=(oshape, oshape),
        scratch_shapes=[pltpu.VMEM((2, HG_HP, HG_DIM, HG_DIM), jnp.float32)],
        compiler_params=_cparams(("parallel", "parallel", "arbitrary")),
        name="hgrn_bidir",
    )(proj3, proj3, proj3, proj3, proj3, proj3, lb3, *consts_f, *consts_b)


def _merge_kernel(a_ref, of_ref, ob_ref, hgg_ref, ng_ref, ga_ref, gh_ref, h_ref, wa_ref, wh_ref, wo_ref, o_ref):
    f32, bf16 = jnp.float32, jnp.bfloat16
    o = of_ref[...] + ob_ref[...]
    ng = ng_ref[...]
    gt = hgg_ref[...]
    hh = jnp.concatenate([_rms(o[:, c:c + HG_DIM], ng[:, c:c + HG_DIM]) for c in range(0, HG_HEADS * HG_DIM, HG_DIM)],
                         axis=1) * (gt * jax.nn.sigmoid(gt))
    pa = jnp.dot(a_ref[...], wa_ref[...], preferred_element_type=f32)
    ph = jnp.dot(hh.astype(bf16), wh_ref[...], preferred_element_type=f32)
    merged = jax.nn.sigmoid(ga_ref[...]) * pa + jax.nn.sigmoid(gh_ref[...]) * ph
    o_ref[...] = h_ref[...] + jnp.dot(merged.astype(bf16), wo_ref[...], preferred_element_type=f32)


def _merge(a2d, of2d, ob2d, ng, proj2d, h2d, wa, wh, wo, tm):
    T = h2d.shape[0]
    row = lambda c: pl.BlockSpec((tm, D_MODEL), lambda i: (i, c))
    wspec = pl.BlockSpec((D_MODEL, D_MODEL), lambda i: (0, 0))
    return pl.pallas_call(
        _merge_kernel,
        grid=(T // tm,),
        in_specs=[row(0), row(0), row(0), row(COL_HGG // D_MODEL), pl.BlockSpec((1, D_MODEL), lambda i: (0, 0)),
                  row(COL_GA // D_MODEL), row(COL_GH // D_MODEL), row(0), wspec, wspec, wspec],
        out_specs=row(0),
        out_shape=jax.ShapeDtypeStruct((T, D_MODEL), jnp.float32),
        input_output_aliases={7: 0},
        compiler_params=_cparams(("parallel",)),
        name="merge_out",
    )(a2d, of2d, ob2d, proj2d, ng, proj2d, proj2d, h2d, wa, wh, wo)


def _topk_axis0(s, kk):
    n = s.shape[0]
    iota = lax.broadcasted_iota(jnp.int32, s.shape, 0)
    vals, ids = [], []
    for _ in range(kk):
        m = jnp.max(s, axis=0, keepdims=True)
        am = jnp.min(jnp.where(s == m, iota, n), axis=0, keepdims=True)
        s = jnp.where(iota == am, -jnp.inf, s)
        vals.append(m)
        ids.append(am)
    return vals, ids


def _stack_rows(rows, dtype):
    n = len(rows)
    iota = lax.broadcasted_iota(jnp.int32, (n, rows[0].shape[1]), 0)
    out = jnp.zeros((n, rows[0].shape[1]), dtype)
    for j, rj in enumerate(rows):
        out = jnp.where(iota == j, rj, out)
    return out


def _route_kernel(h_ref, g_ref, wq_ref, key_ref, eidx_ref, gate_ref, xn_ref):
    @pl.when(pl.program_id(1) == 0)
    def _():
        xn_ref[...] = _rms(h_ref[...], g_ref[...]).astype(jnp.bfloat16)

    K = PEER_TOPK
    half = PEER_QDIM // 2
    i32 = jnp.int32
    qry = jnp.dot(xn_ref[...], wq_ref[...], preferred_element_type=jnp.float32).astype(jnp.bfloat16)
    s0 = _nt(key_ref[0], qry[:, :half])
    s1 = _nt(key_ref[1], qry[:, half:])
    tm = s0.shape[1]
    v0, i0 = _topk_axis0(s0, K)
    v1, i1 = _topk_axis0(s1, K)
    m1 = _stack_rows(v1, jnp.float32)
    m0_hi = _stack_rows(v0[K // 2:], jnp.float32)
    r8 = lax.broadcasted_iota(i32, (K // 2, tm), 0)
    cand = [v0[0] + m1] + [v0[a] + m1[:K // 2] for a in range(1, K // 2)] + [m0_hi + v1[0]]
    flat = [lax.broadcasted_iota(i32, (K, tm), 0)] + [a * K + r8 for a in range(1, K // 2)] + [(K // 2 + r8) * K]
    cand = jnp.concatenate(cand, axis=0)
    flat = jnp.concatenate(flat, axis=0)
    big = K * K
    ts, tp = [], []
    for _ in range(K):
        m = jnp.max(cand, axis=0, keepdims=True)
        p = jnp.min(jnp.where(cand == m, flat, big), axis=0, keepdims=True)
        cand = jnp.where(flat == p, -jnp.inf, cand)
        ts.append(m)
        tp.append(p)
    pk = _stack_rows(tp, i32)
    pa, pb = pk // K, pk % K
    e0 = jnp.zeros((K, tm), i32)
    e1 = jnp.zeros((K, tm), i32)
    for a in range(K):
        e0 = jnp.where(pa == a, i0[a], e0)
        e1 = jnp.where(pb == a, i1[a], e1)
    ex = [jnp.exp(t - ts[0]) for t in ts]
    den = ex[0]
    for e in ex[1:]:
        den = den + e
    eidx_ref[...] = e0 * PEER_NKEYS + e1
    gate_ref[...] = _stack_rows([e / den for e in ex], jnp.float32)


def _route(h2d, g, wq, keys, tm):
    T = h2d.shape[0]
    out = jax.ShapeDtypeStruct((PEER_SLOTS, T), jnp.int32), jax.ShapeDtypeStruct((PEER_SLOTS, T), jnp.float32)
    ospec = pl.BlockSpec((PEER_TOPK, tm), lambda i, h: (h, i))
    return pl.pallas_call(
        _route_kernel,
        grid=(T // tm, PEER_HEADS),
        in_specs=[pl.BlockSpec((tm, D_MODEL), lambda i, h: (i, 0)),
                  pl.BlockSpec((1, D_MODEL), lambda i, h: (0, 0)),
                  pl.BlockSpec((D_MODEL, PEER_QDIM), lambda i, h: (0, h)),
                  pl.BlockSpec((2, PEER_NKEYS, PEER_QDIM // 2), lambda i, h: (h, 0, 0))],
        out_specs=(ospec, ospec),
        out_shape=out,
        scratch_shapes=[pltpu.VMEM((tm, D_MODEL), jnp.bfloat16)],
        compiler_params=_cparams(("parallel", "arbitrary")),
        name="peer_route",
    )(h2d, g, wq, keys)


def _pack_bf16_pairs(x):
    e, d = x.shape
    bits = lax.bitcast_convert_type(x.astype(jnp.bfloat16), jnp.uint16).astype(jnp.uint32)
    bits = bits.reshape(e, d // 256, 2, 128)
    return bits[:, :, 0, :] | (bits[:, :, 1, :] << 16)


def _sc_gather(u4, v4, idx, after):
    M = idx.shape[0]
    info = plsc.get_sparse_core_info()
    nw = info.num_cores * info.num_subcores
    per = M // nw
    assert per * nw == M and per % SC_WINDOW == 0
    mesh = plsc.VectorSubcoreMesh(core_axis_name="c", subcore_axis_name="s")
    oshape = jax.ShapeDtypeStruct((M,) + u4.shape[1:], u4.dtype)
    rows = pltpu.VMEM((SC_WINDOW,) + u4.shape[1:], u4.dtype)

    @functools.partial(
        pl.kernel, mesh=mesh, out_type=(oshape, oshape),
        scratch_types=[pltpu.VMEM((SC_WINDOW,), jnp.int32), rows, rows, pltpu.SemaphoreType.DMA, pltpu.SemaphoreType.DMA],
    )
    def gather(u_hbm, v_hbm, idx_hbm, after_hbm, ou_hbm, ov_hbm, idx_v, ru, rv, sem_u, sem_v):
        wid = lax.axis_index("s") * info.num_cores + lax.axis_index("c")
        base = wid * per

        @pl.loop(0, per // SC_WINDOW)
        def _(g):
            off = pl.multiple_of(base + g * SC_WINDOW, SC_WINDOW)
            pltpu.sync_copy(idx_hbm.at[pl.ds(off, SC_WINDOW)], idx_v)
            cu = pltpu.async_copy(u_hbm.at[idx_v], ru, sem_u)
            cv = pltpu.async_copy(v_hbm.at[idx_v], rv, sem_v)
            cu.wait()
            cv.wait()
            pltpu.sync_copy(ru, ou_hbm.at[pl.ds(off, SC_WINDOW)])
            pltpu.sync_copy(rv, ov_hbm.at[pl.ds(off, SC_WINDOW)])

    return gather(u4, v4, idx, after)


def _expert_kernel(idx_ref, idxn_ref, h_ref, g_ref, gate_ref, spread_ref, su_ref, sv_ref, uv_hbm, o_ref, done_ref, buf,
                   z_ref, xn_ref, y_ref, sem):
    TD, TS, NS = PEER_TD, PEER_TS, PEER_SLOTS
    TT, LW = TD + TS, SUBLANES * PEER_SLOTS
    PR = SUBLANES // 2
    f32, bf16 = jnp.float32, jnp.bfloat16
    i = pl.program_id(0)
    n = pl.num_programs(0)
    slot = lax.rem(i, 2)

    def issue_token(iref, sl, t):
        for k in range(NS):
            pltpu.make_async_copy(uv_hbm.at[iref[t, k]], buf.at[sl, :, pl.ds((t * NS + k) * PR, PR)],
                                  sem.at[sl]).start(priority=k % 2)

    @pl.when(i == 0)
    def _():
        def body(t, c):
            issue_token(idx_ref, 0, t)
            return c
        lax.fori_loop(0, TD, body, 0)

    pltpu.make_async_copy(buf.at[slot], buf.at[slot], sem.at[slot]).wait()

    x = h_ref[...]
    xn_ref[...] = _rms(x, g_ref[...]).reshape(TT, SUBLANES, 128)
    on_diag = (lax.broadcasted_iota(jnp.int32, (SUBLANES, LW), 1) % SUBLANES
               == lax.broadcasted_iota(jnp.int32, (SUBLANES, LW), 0))

    def pieces_of(t, staged, part):
        r = pl.ds(pl.multiple_of(t * (NS * PR), NS * PR), NS * PR)
        words = (sv_ref if part else su_ref)[r, :] if staged else buf[slot, part, r, :]
        return pltpu.bitcast(words, bf16)

    def score_token(t, staged):
        tg = t + TD if staged else t
        z = _nt(xn_ref[tg].astype(bf16), pieces_of(t, staged, 0))
        z_ref[pl.ds(tg, 1), :] = jnp.sum(jnp.where(on_diag, z, 0.0), axis=0, keepdims=True)

    def mix_token(t, staged):
        tg = t + TD if staged else t
        w = jnp.where(on_diag, jnp.broadcast_to(z_ref[pl.ds(tg, 1), :], (SUBLANES, LW)), 0.0).astype(bf16)
        y_ref[tg] = jnp.dot(w, pieces_of(t, staged, 1), preferred_element_type=f32)

    def batched(count, fn):
        def body(tt, c):
            for j in range(PEER_BATCH):
                fn(tt * PEER_BATCH + j)
            return c
        lax.fori_loop(0, count // PEER_BATCH, body, 0)

    @pl.when(i + 1 < n)
    def _():
        def body(tt, c):
            for j in range(PEER_UNROLL):
                score_token(tt * PEER_UNROLL + j, False)
            for j in range(PEER_UNROLL):
                issue_token(idxn_ref, 1 - slot, tt * PEER_UNROLL + j)
            return c
        lax.fori_loop(0, TD // PEER_UNROLL, body, 0)

    @pl.when(i + 1 == n)
    def _():
        batched(TD, lambda t: score_token(t, False))

    batched(TS, lambda t: score_token(t, True))

    z = z_ref[...]
    z = z + pltpu.roll(z, 4, 1)
    z = z + pltpu.roll(z, 2, 1)
    z = z + pltpu.roll(z, 1, 1)
    gt = gate_ref[...]
    g1 = gt.astype(bf16)
    r1 = gt - g1.astype(f32)
    g2 = r1.astype(bf16)
    g3 = (r1 - g2.astype(f32)).astype(bf16)
    spread = spread_ref[...]
    gate8 = (jnp.dot(g1, spread, preferred_element_type=f32) + jnp.dot(g2, spread, preferred_element_type=f32)
             + jnp.dot(g3, spread, preferred_element_type=f32))
    w = gate8 * (0.5 * z * (1.0 + lax.erf(z * (2.0 ** -0.5))))
    w = w + pltpu.roll(w, LW - 1, 1)
    w = w + pltpu.roll(w, LW - 2, 1)
    z_ref[...] = w + pltpu.roll(w, LW - 4, 1)

    batched(TD, lambda t: mix_token(t, False))
    batched(TS, lambda t: mix_token(t, True))
    o_ref[...] = x + y_ref[...].reshape(TT, D_MODEL)
    done_ref[...] = jnp.zeros_like(done_ref)


def _experts_chunk(step0, nsteps, idx_d, h2d, g, gate, spread, su, sv, uv):
    TD, TS, NS = PEER_TD, PEER_TS, PEER_SLOTS
    TT, PR = TD + TS, SUBLANES // 2
    last = step0 + nsteps - 1
    staged = pl.BlockSpec((TS * NS * PR, 128), lambda i: (i, 0))
    return pl.pallas_call(
        _expert_kernel,
        grid=(nsteps,),
        in_specs=[pl.BlockSpec((TD, NS), lambda i: (step0 + i, 0), memory_space=pltpu.SMEM),
                  pl.BlockSpec((TD, NS), lambda i: (jnp.minimum(step0 + i + 1, last), 0), memory_space=pltpu.SMEM),
                  pl.BlockSpec((TT, D_MODEL), lambda i: (step0 + i, 0)),
                  pl.BlockSpec((1, D_MODEL), lambda i: (0, 0)),
                  pl.BlockSpec((TT, NS), lambda i: (step0 + i, 0)),
                  pl.BlockSpec((NS, SUBLANES * NS), lambda i: (0, 0)),
                  staged, staged,
                  pl.BlockSpec(memory_space=pl.ANY)],
        out_specs=(pl.BlockSpec((TT, D_MODEL), lambda i: (step0 + i, 0)),
                   pl.BlockSpec((SUBLANES, 128), lambda i: (0, 0))),
        out_shape=(jax.ShapeDtypeStruct(h2d.shape, jnp.float32), jax.ShapeDtypeStruct((SUBLANES, 128), jnp.int32)),
        scratch_shapes=[pltpu.VMEM((2, 2, TD * NS * PR, 128), jnp.uint32),
                        pltpu.VMEM((TT, SUBLANES * NS), jnp.float32),
                        pltpu.VMEM((TT, SUBLANES, 128), jnp.float32),
                        pltpu.VMEM((TT, SUBLANES, 128), jnp.float32),
                        pltpu.SemaphoreType.DMA((2,))],
        input_output_aliases={2: 0},
        compiler_params=_cparams(("arbitrary",)),
        name="peer_experts",
    )(idx_d, idx_d, h2d, g, gate, spread, su, sv, uv)


def _experts(h2d, g, eidx, gate, u4, v4):
    T = h2d.shape[0]
    TD, TS, NS = PEER_TD, PEER_TS, PEER_SLOTS
    TT = TD + TS
    steps = T // TT
    chunks = _pick_tile(steps, (12, 8, 6, 4, 3, 2, 1))
    spc = steps // chunks
    e3 = eidx.reshape(steps, TT, NS)
    idx_d = e3[:, :TD].reshape(steps * TD, NS)
    idx_s = e3[:, TD:].reshape(chunks, spc * TS * NS)
    spread = np.zeros((NS, SUBLANES * NS), np.float32)
    spread[np.arange(NS), SUBLANES * np.arange(NS) + SUBLANES - 1] = 1.0
    spread = jnp.asarray(spread, jnp.bfloat16)
    uv = jnp.stack([u4, v4], axis=1)
    marks = [jnp.zeros((SUBLANES, 128), jnp.int32)] * 2
    for c in range(chunks):
        su, sv = _sc_gather(u4, v4, idx_s[c], marks[c])
        su, sv = (a.reshape(-1, 128) for a in (su, sv))
        h2d, mark = _experts_chunk(c * spc, spc, idx_d, h2d, g, gate, spread, su, sv, uv)
        marks.append(mark)
    return h2d


def _final_kernel(h_ref, g_ref, o_ref):
    o_ref[...] = _rms(h_ref[...], g_ref[...])


def _final(h3, g, b0, nb):
    _, Lp, _ = h3.shape
    S = Lp - BLK
    return pl.pallas_call(
        _final_kernel,
        grid=(nb, S // BLK),
        in_specs=[pl.BlockSpec((None, BLK, D_MODEL), lambda b, r: (b + b0, r + 1, 0)),
                  pl.BlockSpec((1, D_MODEL), lambda b, r: (0, 0))],
        out_specs=pl.BlockSpec((None, BLK, D_MODEL), lambda b, r: (b, r, 0)),
        out_shape=jax.ShapeDtypeStruct((nb, S, D_MODEL), jnp.float32),
        compiler_params=_cparams(("parallel", "parallel")),
        name="final_norm",
    )(h3, g)


def _pick_tile(T, candidates):
    for c in candidates:
        if T % c == 0:
            return c
    raise ValueError(f"no tile for {T}")


def kernel(x_prompt, x_sample, meta_tokens, g_mix, w_in, attn_sinks, lb_logits, hg_norm_g, w_att_branch,
           w_hg_branch, w_out, g_ffn, w_peer_q, peer_sub_keys, peer_u, peer_v, g_final):
    f32, bf16 = jnp.float32, jnp.bfloat16
    depth = w_in.shape[0]
    nb_p, S, _ = x_prompt.shape
    nb_s = x_sample.shape[0]
    assert x_sample.shape[1] == S and S % BLK == 0
    NB = nb_p + nb_s
    Lp = S + BLK
    T = NB * Lp

    x = jnp.concatenate([x_prompt, x_sample], axis=0)
    head = jnp.concatenate([jnp.zeros((PAD_ROWS, D_MODEL), f32), meta_tokens.astype(f32)], axis=0)
    h = jnp.concatenate([jnp.broadcast_to(head[None], (NB, BLK, D_MODEL)), x], axis=1).reshape(T, D_MODEL)

    sizes = (1024, 256, 256, 1024, 1024, 1024, 1024, 1024, 1024, 1024)
    offs = np.concatenate([[0], np.cumsum(sizes)])
    order = (0, 8, 9, 3, 4, 5, 6, 7, 1, 2)
    perm = np.concatenate([np.arange(offs[k], offs[k + 1]) for k in order])

    sm = jax.nn.softmax(lb_logits.astype(f32), axis=0)
    lb_all = jnp.maximum(jnp.cumsum(sm, axis=0) - sm[0:1], 0.0)

    tables = _rope_tables(Lp)
    consts_f = _hg_consts(False)
    consts_b = _hg_consts(True)
    tm_big = _pick_tile(T, (512, 384, 256, 128))
    tm_route = _pick_tile(T, (256, 128))

    for l in range(depth):
        w_l = w_in[l][:, perm].astype(bf16)
        proj = _inproj(h, g_mix[l].reshape(1, -1), w_l, tm_big, IN_W // 4)
        proj3 = proj.reshape(NB, Lp, IN_W)
        a_out = _attention(proj3, attn_sinks[l].astype(f32), tables)
        lb3 = lb_all[l].reshape(2, 1, -1)
        o_fwd, o_bwd = _hgrn(proj3, lb3, consts_f, consts_b)
        h = _merge(a_out.reshape(T, -1), o_fwd.reshape(T, -1), o_bwd.reshape(T, -1), hg_norm_g[l].reshape(1, -1),
                   proj, h, w_att_branch[l].astype(bf16), w_hg_branch[l].astype(bf16), w_out[l].astype(bf16),
                   tm_route)
        g_f = g_ffn[l].reshape(1, -1)
        keys = peer_sub_keys[l].reshape(2 * PEER_HEADS, PEER_NKEYS, PEER_QDIM // 2).astype(bf16)
        eidx_t, gate_t = _route(h, g_f, w_peer_q[l].astype(bf16), keys, tm_route)
        h = _experts(h, g_f, eidx_t.T, gate_t.T, _pack_bf16_pairs(peer_u[l]), _pack_bf16_pairs(peer_v[l]))

    h3 = h.reshape(NB, Lp, D_MODEL)
    g_fin = g_final.reshape(1, -1)
    return _final(h3, g_fin, 0, nb_p), _final(h3, g_fin, nb_p, nb_s)
```

```python
import functools

import numpy as np
import jax
import jax.numpy as jnp
from jax import lax
from jax.experimental import pallas as pl
from jax.experimental.pallas import tpu as pltpu
from jax.experimental.pallas import tpu_sc as plsc

D_MODEL = 1024
N_META = 16
N_Q_HEADS = 16
N_KV_HEADS = 4
HEAD_DIM = 64
WINDOW = 128
ROT_DIM = HEAD_DIM // 4
ROPE_THETA = 500000.0
HG_HEADS = 8
HG_DIM = 128
PEER_HEADS = 8
PEER_NKEYS = 128
PEER_QDIM = 256
PEER_TOPK = 16
EPS = 1e-6
NEG = -1e30

BLK = 128
PAD_ROWS = BLK - N_META
IN_W = 8704
COL_AQ, COL_GA, COL_GH, COL_HQ, COL_ZF, COL_ZB, COL_HI, COL_HGG, COL_AK, COL_AV = (
    0, 1024, 2048, 3072, 4096, 5120, 6144, 7168, 8192, 8448)

VMEM_LIMIT = 56 * 1024 * 1024
PEER_TD = 8
PEER_TS = 24
SC_WINDOW = 64
PEER_SLOTS = PEER_HEADS * PEER_TOPK
HG_HP = 4
PEER_UNROLL = 2
PEER_BATCH = 8
SUBLANES = 8


def _cparams(sem):
    return pltpu.CompilerParams(dimension_semantics=sem, vmem_limit_bytes=VMEM_LIMIT)


def _rms(x, g):
    return x * lax.rsqrt(jnp.mean(x * x, axis=-1, keepdims=True) + EPS) * g


def _nt(a, b):
    return lax.dot_general(a, b, (((1,), (1,)), ((), ())), preferred_element_type=jnp.float32)


def _tn(a, b):
    return lax.dot_general(a, b, (((0,), (0,)), ((), ())), preferred_element_type=jnp.float32)


def _inproj_kernel(x_ref, g_ref, w_ref, o_ref):
    n = _rms(x_ref[...], g_ref[...]).astype(jnp.bfloat16)
    o_ref[...] = jnp.dot(n, w_ref[...], preferred_element_type=jnp.float32)


def _inproj(h2d, g, w, tm, tn):
    T = h2d.shape[0]
    return pl.pallas_call(
        _inproj_kernel,
        grid=(IN_W // tn, T // tm),
        in_specs=[pl.BlockSpec((tm, D_MODEL), lambda j, i: (i, 0)),
                  pl.BlockSpec((1, D_MODEL), lambda j, i: (0, 0)),
                  pl.BlockSpec((D_MODEL, tn), lambda j, i: (0, j))],
        out_specs=pl.BlockSpec((tm, tn), lambda j, i: (i, j)),
        out_shape=jax.ShapeDtypeStruct((T, IN_W), jnp.float32),
        compiler_params=_cparams(("parallel", "parallel")),
        name="inproj",
    )(h2d, g, w)


def _rope_tables(Lp):
    half = ROT_DIM // 2
    pos = jnp.maximum(jnp.arange(Lp) - PAD_ROWS, 0)
    inv = jnp.power(jnp.float32(ROPE_THETA), -jnp.arange(half, dtype=jnp.float32) * 2.0 / ROT_DIM)
    ang = pos.astype(jnp.float32)[:, None] * inv[None, :]
    cos, sin = jnp.cos(ang), jnp.sin(ang)
    ones = jnp.ones((Lp, HEAD_DIM - ROT_DIM), jnp.float32)
    zeros = jnp.zeros((Lp, HEAD_DIM - ROT_DIM), jnp.float32)
    zh = jnp.zeros((Lp, half), jnp.float32)
    c = jnp.concatenate([cos, cos, ones], axis=1)
    s1 = jnp.concatenate([zh, sin, zeros], axis=1)
    s2 = jnp.concatenate([-sin, zh, zeros], axis=1)
    rep = BLK // HEAD_DIM
    return tuple(jnp.tile(t, (1, rep)) for t in (c, s1, s2))


def _attn_kernel(sink_ref, q_ref, kg_ref, k0_ref, k1_ref, k2_ref, vg_ref, v0_ref, v1_ref, v2_ref,
                 c_ref, s1_ref, s2_ref, o_ref):
    r = pl.program_id(1)
    nblk = pl.num_programs(1)
    half = ROT_DIM // 2

    def rope(x, blk):
        row0 = blk * BLK if isinstance(blk, int) else pl.multiple_of(blk * BLK, BLK)
        c = c_ref[pl.ds(row0, BLK), :]
        s1 = s1_ref[pl.ds(row0, BLK), :]
        s2 = s2_ref[pl.ds(row0, BLK), :]
        outs = []
        for j in range(x.shape[1] // BLK):
            xj = x[:, j * BLK:(j + 1) * BLK]
            outs.append(xj * c + pltpu.roll(xj, half, 1) * s1 + pltpu.roll(xj, BLK - half, 1) * s2)
        return jnp.concatenate(outs, axis=1)

    kb = [jnp.clip(r - 1 + j, 0, nblk - 1) for j in range(3)]
    q = rope(q_ref[...], r) * (HEAD_DIM ** -0.5)
    kcat = jnp.concatenate([rope(kg_ref[...], 0), rope(k0_ref[...], kb[0]), rope(k1_ref[...], kb[1]),
                            rope(k2_ref[...], kb[2])], axis=0).astype(jnp.bfloat16)
    vcat = jnp.concatenate([vg_ref[...], v0_ref[...], v1_ref[...], v2_ref[...]], axis=0).astype(jnp.bfloat16)

    qrow = lax.broadcasted_iota(jnp.int32, (BLK, BLK), 0)
    kcol = lax.broadcasted_iota(jnp.int32, (BLK, BLK), 1)
    masks = [kcol >= PAD_ROWS]
    for j in range(3):
        kblk = r - 1 + j
        ok_blk = (kblk >= 1) & (kblk <= nblk - 1)
        masks.append((jnp.abs((j - 1) * BLK + kcol - qrow) <= WINDOW) & ok_blk)
    mask = jnp.concatenate(masks, axis=1)

    G = N_Q_HEADS // N_KV_HEADS
    lane_half = kcol // HEAD_DIM
    pairs_per_slice = (BLK // HEAD_DIM) * G // 2
    outs = [None] * (N_Q_HEADS // 2)
    for m2 in range(N_KV_HEADS * HEAD_DIM // BLK):
        k2 = kcat[:, m2 * BLK:(m2 + 1) * BLK]
        v2 = vcat[:, m2 * BLK:(m2 + 1) * BLK]
        qs, sks, halves = [], [], []
        for j in range(m2 * pairs_per_slice, (m2 + 1) * pairs_per_slice):
            hk = ((2 * j) // G) % 2
            in_half = lane_half == hk
            q2 = q[:, j * BLK:(j + 1) * BLK]
            qs += [jnp.where(in_half, q2, 0.0), jnp.where(in_half, pltpu.roll(q2, HEAD_DIM, 1), 0.0)]
            sks += [jnp.full((BLK, 1), sink_ref[2 * j + hk]), jnp.full((BLK, 1), sink_ref[2 * j + 1 - hk])]
            halves.append(in_half)
        nq = len(qs)
        sk = jnp.concatenate(sks, axis=0)
        s = _nt(jnp.concatenate(qs, axis=0).astype(jnp.bfloat16), k2)
        s = jnp.where(jnp.tile(mask, (nq, 1)), s, NEG)
        m = jnp.maximum(jnp.max(s, axis=-1, keepdims=True), sk)
        p = jnp.exp(s - m)
        p = p / (jnp.sum(p, axis=-1, keepdims=True) + jnp.exp(sk - m))
        o = jnp.dot(p.astype(jnp.bfloat16), v2, preferred_element_type=jnp.float32)
        for jj, in_half in enumerate(halves):
            o_same = o[(2 * jj) * BLK:(2 * jj + 1) * BLK]
            o_other = o[(2 * jj + 1) * BLK:(2 * jj + 2) * BLK]
            outs[m2 * pairs_per_slice + jj] = jnp.where(in_half, o_same, pltpu.roll(o_other, HEAD_DIM, 1))
    o_ref[...] = jnp.concatenate(outs, axis=1).astype(o_ref.dtype)


def _attention(proj3, sinks, tables):
    NB, Lp, _ = proj3.shape
    nblk = Lp // BLK
    kcol, vcol = COL_AK // 256, COL_AV // 256

    def loc(j, col):
        return pl.BlockSpec((None, BLK, 256),
                            lambda b, r: (b, jnp.clip(r - 1 + j, 0, nblk - 1), col))

    glob = lambda col: pl.BlockSpec((None, BLK, 256), lambda b, r: (b, 0, col))
    tab = pl.BlockSpec((Lp, BLK), lambda b, r: (0, 0))
    return pl.pallas_call(
        _attn_kernel,
        grid=(NB, nblk),
        in_specs=[pl.BlockSpec(memory_space=pltpu.SMEM),
                  pl.BlockSpec((None, BLK, 1024), lambda b, r: (b, r, 0)),
                  glob(kcol), loc(0, kcol), loc(1, kcol), loc(2, kcol),
                  glob(vcol), loc(0, vcol), loc(1, vcol), loc(2, vcol),
                  tab, tab, tab],
        out_specs=pl.BlockSpec((None, BLK, 1024), lambda b, r: (b, r, 0)),
        out_shape=jax.ShapeDtypeStruct((NB, Lp, 1024), jnp.bfloat16),
        compiler_params=_cparams(("parallel", "arbitrary")),
        name="window_attn",
    )(sinks, proj3, proj3, proj3, proj3, proj3, proj3, proj3, proj3, proj3, *tables)


HG_ROWS = BLK
_HG_LEVELS = (64, 32, 16, 8, 4, 2, 1)


def _hg_consts(reverse):
    C = HG_ROWS
    t = np.arange(C)[:, None]
    u = np.arange(C)[None, :]
    mats = [u <= t, u > t]
    ups, msks = [], []
    for B in _HG_LEVELS:
        same = (t // (2 * B)) == (u // (2 * B))
        tin, uin = t % (2 * B), u % (2 * B)
        upper = tin >= B
        wq = upper & same & (uin >= B) & (uin <= tin)
        wk = (~upper) & same & (uin > tin) & (uin <= B - 1)
        mats.append(wq | wk)
        ups.append(np.broadcast_to(upper, (C, HG_DIM)))
        msks.append(same & upper & (uin < B))
    msks.append(np.eye(C, dtype=bool))
    if reverse:
        mats = [m[::-1, ::-1] for m in mats]
        ups = [m[::-1] for m in ups]
        msks = [m[::-1, ::-1] for m in msks]
    W = np.concatenate(mats, axis=0).astype(np.float32)
    W2 = np.concatenate([W, W], axis=1)
    return (jnp.asarray(W2, jnp.bfloat16), jnp.asarray(np.stack(ups).astype(np.float32)),
            jnp.asarray(np.stack(msks).astype(np.float32)))


def _hg_chunk(q_raw, z, v, valid, lb, w2, up_ref, msk_ref, st, reverse):
    C = HG_ROWS
    f32, bf16 = jnp.float32, jnp.bfloat16
    log_lb = jnp.log(lb)
    log1m = jnp.log1p(-lb)
    logsig = jnp.minimum(z, 0.0) - jnp.log1p(jnp.exp(-jnp.abs(z)))
    a, b = log_lb, log1m + logsig
    g = jnp.maximum(a, b) + jnp.log1p(jnp.exp(-jnp.abs(a - b)))
    k = (1.0 - lb) * jax.nn.sigmoid(-z)
    q = q_raw * jax.nn.sigmoid(q_raw)
    q = jnp.where(valid, q, 0.0)
    k = jnp.where(valid, k, 0.0)
    v = jnp.where(valid, v, 0.0)
    g = jnp.where(valid, g, 0.0)
    g1 = g.astype(bf16)
    g2 = (g - g1.astype(f32)).astype(bf16)
    E = jnp.dot(w2, jnp.concatenate([g1, g2], axis=0), preferred_element_type=f32)
    b_inc = E[0:C]
    b_sfx = E[C:2 * C]
    b_tot = b_inc[0:1] if reverse else b_inc[C - 1:C]

    A = msk_ref[len(_HG_LEVELS)] * jnp.sum(q * k, axis=-1, keepdims=True)
    for l in range(len(_HG_LEVELS)):
        ex = jnp.exp(E[(2 + l) * C:(3 + l) * C])
        up = up_ref[l] > 0.5
        qs = jnp.where(up, q * ex, 0.0).astype(bf16)
        ks = jnp.where(up, 0.0, k * ex).astype(bf16)
        A = A + _nt(qs, ks) * msk_ref[l]

    vb = v.astype(bf16)
    o = _nt((q * jnp.exp(b_inc)).astype(bf16), st.astype(bf16))
    o = o + jnp.dot(A.astype(bf16), vb, preferred_element_type=f32)
    st = st * jnp.exp(b_tot) + _tn(vb, (k * jnp.exp(b_sfx)).astype(bf16))
    return o, st


def _hg_kernel(qf_ref, zf_ref, vf_ref, qb_ref, zb_ref, vb_ref, lb_ref, w2f_ref, upf_ref, mskf_ref,
               w2b_ref, upb_ref, mskb_ref, of_ref, ob_ref, st_ref):
    i = pl.program_id(2)
    nblk = pl.num_programs(2)

    @pl.when(i == 0)
    def _():
        st_ref[...] = jnp.zeros_like(st_ref)

    rows = lax.broadcasted_iota(jnp.int32, (HG_ROWS, HG_DIM), 0)
    dirs = ((False, qf_ref, zf_ref, vf_ref, w2f_ref, upf_ref, mskf_ref, of_ref, i),
            (True, qb_ref, zb_ref, vb_ref, w2b_ref, upb_ref, mskb_ref, ob_ref, nblk - 1 - i))
    for d, (reverse, q_ref, z_ref, v_ref, w2_ref, up_ref, msk_ref, o_ref, blk) in enumerate(dirs):
        w2 = w2_ref[...]
        for hh in range(HG_HP):
            cols = slice(hh * HG_DIM, (hh + 1) * HG_DIM)
            lb = lb_ref[d, :, cols]
            valid = (rows >= PAD_ROWS) | (blk > 0)
            o, st = _hg_chunk(q_ref[:, cols], z_ref[:, cols], v_ref[:, cols], valid, lb, w2,
                              up_ref, msk_ref, st_ref[d, hh], reverse)
            st_ref[d, hh] = st
            o_ref[:, cols] = o


def _hgrn(proj3, lb3, consts_f, consts_b):
    NB, Lp, _ = proj3.shape
    nblk = Lp // BLK
    W = HG_DIM * HG_HP

    def col(base, reverse):
        return pl.BlockSpec((None, BLK, W),
                            lambda b, h, i: (b, (nblk - 1 - i) if reverse else i, base // W + h))

    def const(c):
        return pl.BlockSpec(c.shape, lambda b, h, i: (0,) * c.ndim)

    in_specs = [col(COL_HQ, False), col(COL_ZF, False), col(COL_HI, False),
                col(COL_HQ, True), col(COL_ZB, True), col(COL_HI, True),
                pl.BlockSpec((2, 1, W), lambda b, h, i: (0, 0, h))]
    in_specs += [const(c) for c in consts_f] + [const(c) for c in consts_b]
    oshape = jax.ShapeDtypeStruct((NB, Lp, HG_HEADS * HG_DIM), jnp.float32)
    return pl.pallas_call(
        _hg_kernel,
        grid=(NB, HG_HEADS // HG_HP, nblk),
        in_specs=in_specs,
        out_specs=(pl.BlockSpec((None, BLK, W), lambda b, h, i: (b, i, h)),
                   pl.BlockSpec((None, BLK, W), lambda b, h, i: (b, nblk - 1 - i, h))),
        out_shape=(oshape, oshape),
        scratch_shapes=[pltpu.VMEM((2, HG_HP, HG_DIM, HG_DIM), jnp.float32)],
        compiler_params=_cparams(("parallel", "parallel", "arbitrary")),
        name="hgrn_bidir",
    )(proj3, proj3, proj3, proj3, proj3, proj3, lb3, *consts_f, *consts_b)


def _merge_kernel(a_ref, of_ref, ob_ref, hgg_ref, ng_ref, ga_ref, gh_ref, h_ref, wa_ref, wh_ref, wo_ref, o_ref):
    f32, bf16 = jnp.float32, jnp.bfloat16
    o = of_ref[...] + ob_ref[...]
    ng = ng_ref[...]
    gt = hgg_ref[...]
    hh = jnp.concatenate([_rms(o[:, c:c + HG_DIM], ng[:, c:c + HG_DIM]) for c in range(0, HG_HEADS * HG_DIM, HG_DIM)],
                         axis=1) * (gt * jax.nn.sigmoid(gt))
    pa = jnp.dot(a_ref[...], wa_ref[...], preferred_element_type=f32)
    ph = jnp.dot(hh.astype(bf16), wh_ref[...], preferred_element_type=f32)
    merged = jax.nn.sigmoid(ga_ref[...]) * pa + jax.nn.sigmoid(gh_ref[...]) * ph
    o_ref[...] = h_ref[...] + jnp.dot(merged.astype(bf16), wo_ref[...], preferred_element_type=f32)


def _merge(a2d, of2d, ob2d, ng, proj2d, h2d, wa, wh, wo, tm):
    T = h2d.shape[0]
    row = lambda c: pl.BlockSpec((tm, D_MODEL), lambda i: (i, c))
    wspec = pl.BlockSpec((D_MODEL, D_MODEL), lambda i: (0, 0))
    return pl.pallas_call(
        _merge_kernel,
        grid=(T // tm,),
        in_specs=[row(0), row(0), row(0), row(COL_HGG // D_MODEL), pl.BlockSpec((1, D_MODEL), lambda i: (0, 0)),
                  row(COL_GA // D_MODEL), row(COL_GH // D_MODEL), row(0), wspec, wspec, wspec],
        out_specs=row(0),
        out_shape=jax.ShapeDtypeStruct((T, D_MODEL), jnp.float32),
        input_output_aliases={7: 0},
        compiler_params=_cparams(("parallel",)),
        name="merge_out",
    )(a2d, of2d, ob2d, proj2d, ng, proj2d, proj2d, h2d, wa, wh, wo)


def _topk_axis0(s, kk):
    n = s.shape[0]
    iota = lax.broadcasted_iota(jnp.int32, s.shape, 0)
    vals, ids = [], []
    for _ in range(kk):
        m = jnp.max(s, axis=0, keepdims=True)
        am = jnp.min(jnp.where(s == m, iota, n), axis=0, keepdims=True)
        s = jnp.where(iota == am, -jnp.inf, s)
        vals.append(m)
        ids.append(am)
    return vals, ids


def _stack_rows(rows, dtype):
    n = len(rows)
    iota = lax.broadcasted_iota(jnp.int32, (n, rows[0].shape[1]), 0)
    out = jnp.zeros((n, rows[0].shape[1]), dtype)
    for j, rj in enumerate(rows):
        out = jnp.where(iota == j, rj, out)
    return out


def _route_kernel(h_ref, g_ref, wq_ref, key_ref, eidx_ref, gate_ref, xn_ref):
    @pl.when(pl.program_id(1) == 0)
    def _():
        xn_ref[...] = _rms(h_ref[...], g_ref[...]).astype(jnp.bfloat16)

    K = PEER_TOPK
    half = PEER_QDIM // 2
    i32 = jnp.int32
    qry = jnp.dot(xn_ref[...], wq_ref[...], preferred_element_type=jnp.float32).astype(jnp.bfloat16)
    s0 = _nt(key_ref[0], qry[:, :half])
    s1 = _nt(key_ref[1], qry[:, half:])
    tm = s0.shape[1]
    v0, i0 = _topk_axis0(s0, K)
    v1, i1 = _topk_axis0(s1, K)
    m1 = _stack_rows(v1, jnp.float32)
    m0_hi = _stack_rows(v0[K // 2:], jnp.float32)
    r8 = lax.broadcasted_iota(i32, (K // 2, tm), 0)
    cand = [v0[0] + m1] + [v0[a] + m1[:K // 2] for a in range(1, K // 2)] + [m0_hi + v1[0]]
    flat = [lax.broadcasted_iota(i32, (K, tm), 0)] + [a * K + r8 for a in range(1, K // 2)] + [(K // 2 + r8) * K]
    cand = jnp.concatenate(cand, axis=0)
    flat = jnp.concatenate(flat, axis=0)
    big = K * K
    ts, tp = [], []
    for _ in range(K):
        m = jnp.max(cand, axis=0, keepdims=True)
        p = jnp.min(jnp.where(cand == m, flat, big), axis=0, keepdims=True)
        cand = jnp.where(flat == p, -jnp.inf, cand)
        ts.append(m)
        tp.append(p)
    pk = _stack_rows(tp, i32)
    pa, pb = pk // K, pk % K
    e0 = jnp.zeros((K, tm), i32)
    e1 = jnp.zeros((K, tm), i32)
    for a in range(K):
        e0 = jnp.where(pa == a, i0[a], e0)
        e1 = jnp.where(pb == a, i1[a], e1)
    ex = [jnp.exp(t - ts[0]) for t in ts]
    den = ex[0]
    for e in ex[1:]:
        den = den + e
    eidx_ref[...] = e0 * PEER_NKEYS + e1
    gate_ref[...] = _stack_rows([e / den for e in ex], jnp.float32)


def _route(h2d, g, wq, keys, tm):
    T = h2d.shape[0]
    out = jax.ShapeDtypeStruct((PEER_SLOTS, T), jnp.int32), jax.ShapeDtypeStruct((PEER_SLOTS, T), jnp.float32)
    ospec = pl.BlockSpec((PEER_TOPK, tm), lambda i, h: (h, i))
    return pl.pallas_call(
        _route_kernel,
        grid=(T // tm, PEER_HEADS),
        in_specs=[pl.BlockSpec((tm, D_MODEL), lambda i, h: (i, 0)),
                  pl.BlockSpec((1, D_MODEL), lambda i, h: (0, 0)),
                  pl.BlockSpec((D_MODEL, PEER_QDIM), lambda i, h: (0, h)),
                  pl.BlockSpec((2, PEER_NKEYS, PEER_QDIM // 2), lambda i, h: (h, 0, 0))],
        out_specs=(ospec, ospec),
        out_shape=out,
        scratch_shapes=[pltpu.VMEM((tm, D_MODEL), jnp.bfloat16)],
        compiler_params=_cparams(("parallel", "arbitrary")),
        name="peer_route",
    )(h2d, g, wq, keys)


def _pack_bf16_pairs(x):
    e, d = x.shape
    bits = lax.bitcast_convert_type(x.astype(jnp.bfloat16), jnp.uint16).astype(jnp.uint32)
    bits = bits.reshape(e, d // 256, 2, 128)
    return bits[:, :, 0, :] | (bits[:, :, 1, :] << 16)


def _sc_gather(u4, v4, idx):
    M = idx.shape[0]
    info = plsc.get_sparse_core_info()
    nw = info.num_cores * info.num_subcores
    per = M // nw
    assert per * nw == M and per % SC_WINDOW == 0
    mesh = plsc.VectorSubcoreMesh(core_axis_name="c", subcore_axis_name="s")
    oshape = jax.ShapeDtypeStruct((M,) + u4.shape[1:], u4.dtype)
    rows = pltpu.VMEM((SC_WINDOW,) + u4.shape[1:], u4.dtype)

    @functools.partial(
        pl.kernel, mesh=mesh, out_type=(oshape, oshape),
        scratch_types=[pltpu.VMEM((SC_WINDOW,), jnp.int32), rows, rows, pltpu.SemaphoreType.DMA, pltpu.SemaphoreType.DMA],
    )
    def gather(u_hbm, v_hbm, idx_hbm, ou_hbm, ov_hbm, idx_v, ru, rv, sem_u, sem_v):
        wid = lax.axis_index("s") * info.num_cores + lax.axis_index("c")
        base = wid * per

        @pl.loop(0, per // SC_WINDOW)
        def _(g):
            off = pl.multiple_of(base + g * SC_WINDOW, SC_WINDOW)
            pltpu.sync_copy(idx_hbm.at[pl.ds(off, SC_WINDOW)], idx_v)
            cu = pltpu.async_copy(u_hbm.at[idx_v], ru, sem_u)
            cv = pltpu.async_copy(v_hbm.at[idx_v], rv, sem_v)
            cu.wait()
            cv.wait()
            pltpu.sync_copy(ru, ou_hbm.at[pl.ds(off, SC_WINDOW)])
            pltpu.sync_copy(rv, ov_hbm.at[pl.ds(off, SC_WINDOW)])

    return gather(u4, v4, idx)


def _expert_kernel(idx_ref, idxn_ref, h_ref, g_ref, gate_ref, spread_ref, su_ref, sv_ref, uv_hbm, o_ref, buf, z_ref,
                   xn_ref, y_ref, sem):
    TD, TS, NS = PEER_TD, PEER_TS, PEER_SLOTS
    TT, LW = TD + TS, SUBLANES * PEER_SLOTS
    PR = SUBLANES // 2
    f32, bf16 = jnp.float32, jnp.bfloat16
    i = pl.program_id(0)
    n = pl.num_programs(0)
    slot = lax.rem(i, 2)

    def issue_token(iref, sl, t):
        for k in range(NS):
            pltpu.make_async_copy(uv_hbm.at[iref[t, k]], buf.at[sl, :, pl.ds((t * NS + k) * PR, PR)],
                                  sem.at[sl]).start(priority=k % 2)

    @pl.when(i == 0)
    def _():
        def body(t, c):
            issue_token(idx_ref, 0, t)
            return c
        lax.fori_loop(0, TD, body, 0)

    pltpu.make_async_copy(buf.at[slot], buf.at[slot], sem.at[slot]).wait()

    x = h_ref[...]
    xn_ref[...] = _rms(x, g_ref[...]).reshape(TT, SUBLANES, 128)
    on_diag = (lax.broadcasted_iota(jnp.int32, (SUBLANES, LW), 1) % SUBLANES
               == lax.broadcasted_iota(jnp.int32, (SUBLANES, LW), 0))

    def pieces_of(t, staged, part):
        r = pl.ds(pl.multiple_of(t * (NS * PR), NS * PR), NS * PR)
        words = (sv_ref if part else su_ref)[r, :] if staged else buf[slot, part, r, :]
        return pltpu.bitcast(words, bf16)

    def score_token(t, staged):
        tg = t + TD if staged else t
        z = _nt(xn_ref[tg].astype(bf16), pieces_of(t, staged, 0))
        z_ref[pl.ds(tg, 1), :] = jnp.sum(jnp.where(on_diag, z, 0.0), axis=0, keepdims=True)

    def mix_token(t, staged):
        tg = t + TD if staged else t
        w = jnp.where(on_diag, jnp.broadcast_to(z_ref[pl.ds(tg, 1), :], (SUBLANES, LW)), 0.0).astype(bf16)
        y_ref[tg] = jnp.dot(w, pieces_of(t, staged, 1), preferred_element_type=f32)

    def batched(count, fn):
        def body(tt, c):
            for j in range(PEER_BATCH):
                fn(tt * PEER_BATCH + j)
            return c
        lax.fori_loop(0, count // PEER_BATCH, body, 0)

    @pl.when(i + 1 < n)
    def _():
        def body(tt, c):
            for j in range(PEER_UNROLL):
                score_token(tt * PEER_UNROLL + j, False)
            for j in range(PEER_UNROLL):
                issue_token(idxn_ref, 1 - slot, tt * PEER_UNROLL + j)
            return c
        lax.fori_loop(0, TD // PEER_UNROLL, body, 0)

    @pl.when(i + 1 == n)
    def _():
        batched(TD, lambda t: score_token(t, False))

    batched(TS, lambda t: score_token(t, True))

    z = z_ref[...]
    z = z + pltpu.roll(z, 4, 1)
    z = z + pltpu.roll(z, 2, 1)
    z = z + pltpu.roll(z, 1, 1)
    gt = gate_ref[...]
    g1 = gt.astype(bf16)
    r1 = gt - g1.astype(f32)
    g2 = r1.astype(bf16)
    g3 = (r1 - g2.astype(f32)).astype(bf16)
    spread = spread_ref[...]
    gate8 = (jnp.dot(g1, spread, preferred_element_type=f32) + jnp.dot(g2, spread, preferred_element_type=f32)
             + jnp.dot(g3, spread, preferred_element_type=f32))
    w = gate8 * (0.5 * z * (1.0 + lax.erf(z * (2.0 ** -0.5))))
    w = w + pltpu.roll(w, LW - 1, 1)
    w = w + pltpu.roll(w, LW - 2, 1)
    z_ref[...] = w + pltpu.roll(w, LW - 4, 1)

    batched(TD, lambda t: mix_token(t, False))
    batched(TS, lambda t: mix_token(t, True))
    o_ref[...] = x + y_ref[...].reshape(TT, D_MODEL)


def _experts_chunk(step0, nsteps, idx_d, h2d, g, gate, spread, su, sv, uv):
    TD, TS, NS = PEER_TD, PEER_TS, PEER_SLOTS
    TT, PR = TD + TS, SUBLANES // 2
    last = step0 + nsteps - 1
    staged = pl.BlockSpec((TS * NS * PR, 128), lambda i: (i, 0))
    return pl.pallas_call(
        _expert_kernel,
        grid=(nsteps,),
        in_specs=[pl.BlockSpec((TD, NS), lambda i: (step0 + i, 0), memory_space=pltpu.SMEM),
                  pl.BlockSpec((TD, NS), lambda i: (jnp.minimum(step0 + i + 1, last), 0), memory_space=pltpu.SMEM),
                  pl.BlockSpec((TT, D_MODEL), lambda i: (step0 + i, 0)),
                  pl.BlockSpec((1, D_MODEL), lambda i: (0, 0)),
                  pl.BlockSpec((TT, NS), lambda i: (step0 + i, 0)),
                  pl.BlockSpec((NS, SUBLANES * NS), lambda i: (0, 0)),
                  staged, staged,
                  pl.BlockSpec(memory_space=pl.ANY)],
        out_specs=pl.BlockSpec((TT, D_MODEL), lambda i: (step0 + i, 0)),
        out_shape=jax.ShapeDtypeStruct(h2d.shape, jnp.float32),
        scratch_shapes=[pltpu.VMEM((2, 2, TD * NS * PR, 128), jnp.uint32),
                        pltpu.VMEM((TT, SUBLANES * NS), jnp.float32),
                        pltpu.VMEM((TT, SUBLANES, 128), jnp.float32),
                        pltpu.VMEM((TT, SUBLANES, 128), jnp.float32),
                        pltpu.SemaphoreType.DMA((2,))],
        input_output_aliases={2: 0},
        compiler_params=_cparams(("arbitrary",)),
        name="peer_experts",
    )(idx_d, idx_d, h2d, g, gate, spread, su, sv, uv)


def _experts(h2d, g, eidx, gate, u4, v4):
    T = h2d.shape[0]
    TD, TS, NS = PEER_TD, PEER_TS, PEER_SLOTS
    TT = TD + TS
    steps = T // TT
    chunks = _pick_tile(steps, (12, 8, 6, 4, 3, 2, 1))
    spc = steps // chunks
    e3 = eidx.reshape(steps, TT, NS)
    idx_d = e3[:, :TD].reshape(steps * TD, NS)
    idx_s = e3[:, TD:].reshape(chunks, spc * TS * NS)
    spread = np.zeros((NS, SUBLANES * NS), np.float32)
    spread[np.arange(NS), SUBLANES * np.arange(NS) + SUBLANES - 1] = 1.0
    spread = jnp.asarray(spread, jnp.bfloat16)
    uv = jnp.stack([u4, v4], axis=1)
    for c in range(chunks):
        su, sv = _sc_gather(u4, v4, idx_s[c])
        su, sv = (a.reshape(-1, 128) for a in (su, sv))
        h2d = _experts_chunk(c * spc, spc, idx_d, h2d, g, gate, spread, su, sv, uv)
    return h2d


def _final_kernel(h_ref, g_ref, o_ref):
    o_ref[...] = _rms(h_ref[...], g_ref[...])


def _final(h3, g, b0, nb):
    _, Lp, _ = h3.shape
    S = Lp - BLK
    return pl.pallas_call(
        _final_kernel,
        grid=(nb, S // BLK),
        in_specs=[pl.BlockSpec((None, BLK, D_MODEL), lambda b, r: (b + b0, r + 1, 0)),
                  pl.BlockSpec((1, D_MODEL), lambda b, r: (0, 0))],
        out_specs=pl.BlockSpec((None, BLK, D_MODEL), lambda b, r: (b, r, 0)),
        out_shape=jax.ShapeDtypeStruct((nb, S, D_MODEL), jnp.float32),
        compiler_params=_cparams(("parallel", "parallel")),
        name="final_norm",
    )(h3, g)


def _pick_tile(T, candidates):
    for c in candidates:
        if T % c == 0:
            return c
    raise ValueError(f"no tile for {T}")


def kernel(x_prompt, x_sample, meta_tokens, g_mix, w_in, attn_sinks, lb_logits, hg_norm_g, w_att_branch,
           w_hg_branch, w_out, g_ffn, w_peer_q, peer_sub_keys, peer_u, peer_v, g_final):
    f32, bf16 = jnp.float32, jnp.bfloat16
    depth = w_in.shape[0]
    nb_p, S, _ = x_prompt.shape
    nb_s = x_sample.shape[0]
    assert x_sample.shape[1] == S and S % BLK == 0
    NB = nb_p + nb_s
    Lp = S + BLK
    T = NB * Lp

    x = jnp.concatenate([x_prompt, x_sample], axis=0)
    head = jnp.concatenate([jnp.zeros((PAD_ROWS, D_MODEL), f32), meta_tokens.astype(f32)], axis=0)
    h = jnp.concatenate([jnp.broadcast_to(head[None], (NB, BLK, D_MODEL)), x], axis=1).reshape(T, D_MODEL)

    sizes = (1024, 256, 256, 1024, 1024, 1024, 1024, 1024, 1024, 1024)
    offs = np.concatenate([[0], np.cumsum(sizes)])
    order = (0, 8, 9, 3, 4, 5, 6, 7, 1, 2)
    perm = np.concatenate([np.arange(offs[k], offs[k + 1]) for k in order])

    sm = jax.nn.softmax(lb_logits.astype(f32), axis=0)
    lb_all = jnp.maximum(jnp.cumsum(sm, axis=0) - sm[0:1], 0.0)

    tables = _rope_tables(Lp)
    consts_f = _hg_consts(False)
    consts_b = _hg_consts(True)
    tm_big = _pick_tile(T, (512, 384, 256, 128))
    tm_route = _pick_tile(T, (256, 128))

    for l in range(depth):
        w_l = w_in[l][:, perm].astype(bf16)
        proj = _inproj(h, g_mix[l].reshape(1, -1), w_l, tm_big, IN_W // 4)
        proj3 = proj.reshape(NB, Lp, IN_W)
        a_out = _attention(proj3, attn_sinks[l].astype(f32), tables)
        lb3 = lb_all[l].reshape(2, 1, -1)
        o_fwd, o_bwd = _hgrn(proj3, lb3, consts_f, consts_b)
        h = _merge(a_out.reshape(T, -1), o_fwd.reshape(T, -1), o_bwd.reshape(T, -1), hg_norm_g[l].reshape(1, -1),
                   proj, h, w_att_branch[l].astype(bf16), w_hg_branch[l].astype(bf16), w_out[l].astype(bf16),
                   tm_route)
        g_f = g_ffn[l].reshape(1, -1)
        keys = peer_sub_keys[l].reshape(2 * PEER_HEADS, PEER_NKEYS, PEER_QDIM // 2).astype(bf16)
        eidx_t, gate_t = _route(h, g_f, w_peer_q[l].astype(bf16), keys, tm_route)
        h = _experts(h, g_f, eidx_t.T, gate_t.T, _pack_bf16_pairs(peer_u[l]), _pack_bf16_pairs(peer_v[l]))

    h3 = h.reshape(NB, Lp, D_MODEL)
    g_fin = g_final.reshape(1, -1)
    return _final(h3, g_fin, 0, nb_p), _final(h3, g_fin, nb_p, nb_s)
```

```python
import functools

import numpy as np
import jax
import jax.numpy as jnp
from jax import lax
from jax.experimental import pallas as pl
from jax.experimental.pallas import tpu as pltpu
from jax.experimental.pallas import tpu_sc as plsc

D_MODEL = 1024
N_META = 16
N_Q_HEADS = 16
N_KV_HEADS = 4
HEAD_DIM = 64
WINDOW = 128
ROT_DIM = HEAD_DIM // 4
ROPE_THETA = 500000.0
HG_HEADS = 8
HG_DIM = 128
PEER_HEADS = 8
PEER_NKEYS = 128
PEER_QDIM = 256
PEER_TOPK = 16
EPS = 1e-6
NEG = -1e30

BLK = 128
PAD_ROWS = BLK - N_META
IN_W = 8704
COL_AQ, COL_GA, COL_GH, COL_HQ, COL_ZF, COL_ZB, COL_HI, COL_HGG, COL_AK, COL_AV = (
    0, 1024, 2048, 3072, 4096, 5120, 6144, 7168, 8192, 8448)

VMEM_LIMIT = 56 * 1024 * 1024
PEER_TD = 16
PEER_TS = 16
SC_WINDOW = 64
PEER_SLOTS = PEER_HEADS * PEER_TOPK
HG_HP = 8
PEER_UNROLL = 2
PEER_BATCH = 8
SUBLANES = 8


def _cparams(sem):
    return pltpu.CompilerParams(dimension_semantics=sem, vmem_limit_bytes=VMEM_LIMIT)


def _rms(x, g):
    return x * lax.rsqrt(jnp.mean(x * x, axis=-1, keepdims=True) + EPS) * g


def _nt(a, b):
    return lax.dot_general(a, b, (((1,), (1,)), ((), ())), preferred_element_type=jnp.float32)


def _tn(a, b):
    return lax.dot_general(a, b, (((0,), (0,)), ((), ())), preferred_element_type=jnp.float32)


def _inproj_kernel(x_ref, g_ref, w_ref, o_ref):
    n = _rms(x_ref[...], g_ref[...]).astype(jnp.bfloat16)
    o_ref[...] = jnp.dot(n, w_ref[...], preferred_element_type=jnp.float32)


def _inproj(h2d, g, w, tm, tn):
    T = h2d.shape[0]
    return pl.pallas_call(
        _inproj_kernel,
        grid=(IN_W // tn, T // tm),
        in_specs=[pl.BlockSpec((tm, D_MODEL), lambda j, i: (i, 0)),
                  pl.BlockSpec((1, D_MODEL), lambda j, i: (0, 0)),
                  pl.BlockSpec((D_MODEL, tn), lambda j, i: (0, j))],
        out_specs=pl.BlockSpec((tm, tn), lambda j, i: (i, j)),
        out_shape=jax.ShapeDtypeStruct((T, IN_W), jnp.float32),
        compiler_params=_cparams(("parallel", "parallel")),
        name="inproj",
    )(h2d, g, w)


def _rope_tables(Lp):
    half = ROT_DIM // 2
    pos = jnp.maximum(jnp.arange(Lp) - PAD_ROWS, 0)
    inv = jnp.power(jnp.float32(ROPE_THETA), -jnp.arange(half, dtype=jnp.float32) * 2.0 / ROT_DIM)
    ang = pos.astype(jnp.float32)[:, None] * inv[None, :]
    cos, sin = jnp.cos(ang), jnp.sin(ang)
    ones = jnp.ones((Lp, HEAD_DIM - ROT_DIM), jnp.float32)
    zeros = jnp.zeros((Lp, HEAD_DIM - ROT_DIM), jnp.float32)
    zh = jnp.zeros((Lp, half), jnp.float32)
    c = jnp.concatenate([cos, cos, ones], axis=1)
    s1 = jnp.concatenate([zh, sin, zeros], axis=1)
    s2 = jnp.concatenate([-sin, zh, zeros], axis=1)
    rep = BLK // HEAD_DIM
    return tuple(jnp.tile(t, (1, rep)) for t in (c, s1, s2))


def _attn_kernel(sink_ref, q_ref, kg_ref, k0_ref, k1_ref, k2_ref, vg_ref, v0_ref, v1_ref, v2_ref,
                 c_ref, s1_ref, s2_ref, o_ref):
    r = pl.program_id(1)
    nblk = pl.num_programs(1)
    half = ROT_DIM // 2

    def rope(x, blk):
        row0 = blk * BLK if isinstance(blk, int) else pl.multiple_of(blk * BLK, BLK)
        c = c_ref[pl.ds(row0, BLK), :]
        s1 = s1_ref[pl.ds(row0, BLK), :]
        s2 = s2_ref[pl.ds(row0, BLK), :]
        outs = []
        for j in range(x.shape[1] // BLK):
            xj = x[:, j * BLK:(j + 1) * BLK]
            outs.append(xj * c + pltpu.roll(xj, half, 1) * s1 + pltpu.roll(xj, BLK - half, 1) * s2)
        return jnp.concatenate(outs, axis=1)

    kb = [jnp.clip(r - 1 + j, 0, nblk - 1) for j in range(3)]
    q = rope(q_ref[...], r) * (HEAD_DIM ** -0.5)
    kcat = jnp.concatenate([rope(kg_ref[...], 0), rope(k0_ref[...], kb[0]), rope(k1_ref[...], kb[1]),
                            rope(k2_ref[...], kb[2])], axis=0).astype(jnp.bfloat16)
    vcat = jnp.concatenate([vg_ref[...], v0_ref[...], v1_ref[...], v2_ref[...]], axis=0).astype(jnp.bfloat16)

    qrow = lax.broadcasted_iota(jnp.int32, (BLK, BLK), 0)
    kcol = lax.broadcasted_iota(jnp.int32, (BLK, BLK), 1)
    masks = [kcol >= PAD_ROWS]
    for j in range(3):
        kblk = r - 1 + j
        ok_blk = (kblk >= 1) & (kblk <= nblk - 1)
        masks.append((jnp.abs((j - 1) * BLK + kcol - qrow) <= WINDOW) & ok_blk)
    mask = jnp.concatenate(masks, axis=1)

    G = N_Q_HEADS // N_KV_HEADS
    lane_half = kcol // HEAD_DIM
    pairs_per_slice = (BLK // HEAD_DIM) * G // 2
    outs = [None] * (N_Q_HEADS // 2)
    for m2 in range(N_KV_HEADS * HEAD_DIM // BLK):
        k2 = kcat[:, m2 * BLK:(m2 + 1) * BLK]
        v2 = vcat[:, m2 * BLK:(m2 + 1) * BLK]
        qs, sks, halves = [], [], []
        for j in range(m2 * pairs_per_slice, (m2 + 1) * pairs_per_slice):
            hk = ((2 * j) // G) % 2
            in_half = lane_half == hk
            q2 = q[:, j * BLK:(j + 1) * BLK]
            qs += [jnp.where(in_half, q2, 0.0), jnp.where(in_half, pltpu.roll(q2, HEAD_DIM, 1), 0.0)]
            sks += [jnp.full((BLK, 1), sink_ref[2 * j + hk]), jnp.full((BLK, 1), sink_ref[2 * j + 1 - hk])]
            halves.append(in_half)
        nq = len(qs)
        sk = jnp.concatenate(sks, axis=0)
        s = _nt(jnp.concatenate(qs, axis=0).astype(jnp.bfloat16), k2)
        s = jnp.where(jnp.tile(mask, (nq, 1)), s, NEG)
        m = jnp.maximum(jnp.max(s, axis=-1, keepdims=True), sk)
        p = jnp.exp(s - m)
        p = p / (jnp.sum(p, axis=-1, keepdims=True) + jnp.exp(sk - m))
        o = jnp.dot(p.astype(jnp.bfloat16), v2, preferred_element_type=jnp.float32)
        for jj, in_half in enumerate(halves):
            o_same = o[(2 * jj) * BLK:(2 * jj + 1) * BLK]
            o_other = o[(2 * jj + 1) * BLK:(2 * jj + 2) * BLK]
            outs[m2 * pairs_per_slice + jj] = jnp.where(in_half, o_same, pltpu.roll(o_other, HEAD_DIM, 1))
    o_ref[...] = jnp.concatenate(outs, axis=1).astype(o_ref.dtype)


def _attention(proj3, sinks, tables):
    NB, Lp, _ = proj3.shape
    nblk = Lp // BLK
    kcol, vcol = COL_AK // 256, COL_AV // 256

    def loc(j, col):
        return pl.BlockSpec((None, BLK, 256),
                            lambda b, r: (b, jnp.clip(r - 1 + j, 0, nblk - 1), col))

    glob = lambda col: pl.BlockSpec((None, BLK, 256), lambda b, r: (b, 0, col))
    tab = pl.BlockSpec((Lp, BLK), lambda b, r: (0, 0))
    return pl.pallas_call(
        _attn_kernel,
        grid=(NB, nblk),
        in_specs=[pl.BlockSpec(memory_space=pltpu.SMEM),
                  pl.BlockSpec((None, BLK, 1024), lambda b, r: (b, r, 0)),
                  glob(kcol), loc(0, kcol), loc(1, kcol), loc(2, kcol),
                  glob(vcol), loc(0, vcol), loc(1, vcol), loc(2, vcol),
                  tab, tab, tab],
        out_specs=pl.BlockSpec((None, BLK, 1024), lambda b, r: (b, r, 0)),
        out_shape=jax.ShapeDtypeStruct((NB, Lp, 1024), jnp.bfloat16),
        compiler_params=_cparams(("parallel", "arbitrary")),
        name="window_attn",
    )(sinks, proj3, proj3, proj3, proj3, proj3, proj3, proj3, proj3, proj3, *tables)


HG_ROWS = BLK
_HG_LEVELS = (64, 32, 16, 8, 4, 2, 1)


def _hg_consts(reverse):
    C = HG_ROWS
    t = np.arange(C)[:, None]
    u = np.arange(C)[None, :]
    mats = [u <= t, u > t]
    ups, msks = [], []
    for B in _HG_LEVELS:
        same = (t // (2 * B)) == (u // (2 * B))
        tin, uin = t % (2 * B), u % (2 * B)
        upper = tin >= B
        wq = upper & same & (uin >= B) & (uin <= tin)
        wk = (~upper) & same & (uin > tin) & (uin <= B - 1)
        mats.append(wq | wk)
        ups.append(np.broadcast_to(upper, (C, HG_DIM)))
        msks.append(same & upper & (uin < B))
    msks.append(np.eye(C, dtype=bool))
    if reverse:
        mats = [m[::-1, ::-1] for m in mats]
        ups = [m[::-1] for m in ups]
        msks = [m[::-1, ::-1] for m in msks]
    W = np.concatenate(mats, axis=0).astype(np.float32)
    W2 = np.concatenate([W, W], axis=1)
    return (jnp.asarray(W2, jnp.bfloat16), jnp.asarray(np.stack(ups).astype(np.float32)),
            jnp.asarray(np.stack(msks).astype(np.float32)))


def _hg_chunk(q_raw, z, v, valid, lb, w2, up_ref, msk_ref, st, reverse):
    C = HG_ROWS
    f32, bf16 = jnp.float32, jnp.bfloat16
    log_lb = jnp.log(lb)
    log1m = jnp.log1p(-lb)
    logsig = jnp.minimum(z, 0.0) - jnp.log1p(jnp.exp(-jnp.abs(z)))
    a, b = log_lb, log1m + logsig
    g = jnp.maximum(a, b) + jnp.log1p(jnp.exp(-jnp.abs(a - b)))
    k = (1.0 - lb) * jax.nn.sigmoid(-z)
    q = q_raw * jax.nn.sigmoid(q_raw)
    q = jnp.where(valid, q, 0.0)
    k = jnp.where(valid, k, 0.0)
    v = jnp.where(valid, v, 0.0)
    g = jnp.where(valid, g, 0.0)
    g1 = g.astype(bf16)
    g2 = (g - g1.astype(f32)).astype(bf16)
    E = jnp.dot(w2, jnp.concatenate([g1, g2], axis=0), preferred_element_type=f32)
    b_inc = E[0:C]
    b_sfx = E[C:2 * C]
    b_tot = b_inc[0:1] if reverse else b_inc[C - 1:C]

    A = msk_ref[len(_HG_LEVELS)] * jnp.sum(q * k, axis=-1, keepdims=True)
    for l in range(len(_HG_LEVELS)):
        ex = jnp.exp(E[(2 + l) * C:(3 + l) * C])
        up = up_ref[l] > 0.5
        qs = jnp.where(up, q * ex, 0.0).astype(bf16)
        ks = jnp.where(up, 0.0, k * ex).astype(bf16)
        A = A + _nt(qs, ks) * msk_ref[l]

    vb = v.astype(bf16)
    o = _nt((q * jnp.exp(b_inc)).astype(bf16), st.astype(bf16))
    o = o + jnp.dot(A.astype(bf16), vb, preferred_element_type=f32)
    st = st * jnp.exp(b_tot) + _tn(vb, (k * jnp.exp(b_sfx)).astype(bf16))
    return o, st


def _hg_kernel(qf_ref, zf_ref, vf_ref, qb_ref, zb_ref, vb_ref, lb_ref, w2f_ref, upf_ref, mskf_ref,
               w2b_ref, upb_ref, mskb_ref, of_ref, ob_ref, st_ref):
    i = pl.program_id(2)
    nblk = pl.num_programs(2)

    @pl.when(i == 0)
    def _():
        st_ref[...] = jnp.zeros_like(st_ref)

    rows = lax.broadcasted_iota(jnp.int32, (HG_ROWS, HG_DIM), 0)
    dirs = ((False, qf_ref, zf_ref, vf_ref, w2f_ref, upf_ref, mskf_ref, of_ref, i),
            (True, qb_ref, zb_ref, vb_ref, w2b_ref, upb_ref, mskb_ref, ob_ref, nblk - 1 - i))
    for d, (reverse, q_ref, z_ref, v_ref, w2_ref, up_ref, msk_ref, o_ref, blk) in enumerate(dirs):
        w2 = w2_ref[...]
        for hh in range(HG_HP):
            cols = slice(hh * HG_DIM, (hh + 1) * HG_DIM)
            lb = lb_ref[d, :, cols]
            valid = (rows >= PAD_ROWS) | (blk > 0)
            o, st = _hg_chunk(q_ref[:, cols], z_ref[:, cols], v_ref[:, cols], valid, lb, w2,
                              up_ref, msk_ref, st_ref[d, hh], reverse)
            st_ref[d, hh] = st
            o_ref[:, cols] = o


def _hgrn(proj3, lb3, consts_f, consts_b):
    NB, Lp, _ = proj3.shape
    nblk = Lp // BLK
    W = HG_DIM * HG_HP

    def col(base, reverse):
        return pl.BlockSpec((None, BLK, W),
                            lambda b, h, i: (b, (nblk - 1 - i) if reverse else i, base // W + h))

    def const(c):
        return pl.BlockSpec(c.shape, lambda b, h, i: (0,) * c.ndim)

    in_specs = [col(COL_HQ, False), col(COL_ZF, False), col(COL_HI, False),
                col(COL_HQ, True), col(COL_ZB, True), col(COL_HI, True),
                pl.BlockSpec((2, 1, W), lambda b, h, i: (0, 0, h))]
    in_specs += [const(c) for c in consts_f] + [const(c) for c in consts_b]
    oshape = jax.ShapeDtypeStruct((NB, Lp, HG_HEADS * HG_DIM), jnp.float32)
    return pl.pallas_call(
        _hg_kernel,
        grid=(NB, HG_HEADS // HG_HP, nblk),
        in_specs=in_specs,
        out_specs=(pl.BlockSpec((None, BLK, W), lambda b, h, i: (b, i, h)),
                   pl.BlockSpec((None, BLK, W), lambda b, h, i: (b, nblk - 1 - i, h))),
        out_shape=(oshape, oshape),
        scratch_shapes=[pltpu.VMEM((2, HG_HP, HG_DIM, HG_DIM), jnp.float32)],
        compiler_params=_cparams(("parallel", "parallel", "arbitrary")),
        name="hgrn_bidir",
    )(proj3, proj3, proj3, proj3, proj3, proj3, lb3, *consts_f, *consts_b)


def _merge_kernel(a_ref, of_ref, ob_ref, hgg_ref, ng_ref, ga_ref, gh_ref, h_ref, wa_ref, wh_ref, wo_ref, o_ref):
    f32, bf16 = jnp.float32, jnp.bfloat16
    o = of_ref[...] + ob_ref[...]
    ng = ng_ref[...]
    gt = hgg_ref[...]
    hh = jnp.concatenate([_rms(o[:, c:c + HG_DIM], ng[:, c:c + HG_DIM]) for c in range(0, HG_HEADS * HG_DIM, HG_DIM)],
                         axis=1) * (gt * jax.nn.sigmoid(gt))
    pa = jnp.dot(a_ref[...], wa_ref[...], preferred_element_type=f32)
    ph = jnp.dot(hh.astype(bf16), wh_ref[...], preferred_element_type=f32)
    merged = jax.nn.sigmoid(ga_ref[...]) * pa + jax.nn.sigmoid(gh_ref[...]) * ph
    o_ref[...] = h_ref[...] + jnp.dot(merged.astype(bf16), wo_ref[...], preferred_element_type=f32)


def _merge(a2d, of2d, ob2d, ng, proj2d, h2d, wa, wh, wo, tm):
    T = h2d.shape[0]
    row = lambda c: pl.BlockSpec((tm, D_MODEL), lambda i: (i, c))
    wspec = pl.BlockSpec((D_MODEL, D_MODEL), lambda i: (0, 0))
    return pl.pallas_call(
        _merge_kernel,
        grid=(T // tm,),
        in_specs=[row(0), row(0), row(0), row(COL_HGG // D_MODEL), pl.BlockSpec((1, D_MODEL), lambda i: (0, 0)),
                  row(COL_GA // D_MODEL), row(COL_GH // D_MODEL), row(0), wspec, wspec, wspec],
        out_specs=row(0),
        out_shape=jax.ShapeDtypeStruct((T, D_MODEL), jnp.float32),
        input_output_aliases={7: 0},
        compiler_params=_cparams(("parallel",)),
        name="merge_out",
    )(a2d, of2d, ob2d, proj2d, ng, proj2d, proj2d, h2d, wa, wh, wo)


def _topk_axis0(s, kk):
    n = s.shape[0]
    iota = lax.broadcasted_iota(jnp.int32, s.shape, 0)
    vals, ids = [], []
    for _ in range(kk):
        m = jnp.max(s, axis=0, keepdims=True)
        am = jnp.min(jnp.where(s == m, iota, n), axis=0, keepdims=True)
        s = jnp.where(iota == am, -jnp.inf, s)
        vals.append(m)
        ids.append(am)
    return vals, ids


def _stack_rows(rows, dtype):
    n = len(rows)
    iota = lax.broadcasted_iota(jnp.int32, (n, rows[0].shape[1]), 0)
    out = jnp.zeros((n, rows[0].shape[1]), dtype)
    for j, rj in enumerate(rows):
        out = jnp.where(iota == j, rj, out)
    return out


def _route_kernel(h_ref, g_ref, wq_ref, key_ref, eidx_ref, gate_ref, xn_ref):
    @pl.when(pl.program_id(1) == 0)
    def _():
        xn_ref[...] = _rms(h_ref[...], g_ref[...]).astype(jnp.bfloat16)

    K = PEER_TOPK
    half = PEER_QDIM // 2
    i32 = jnp.int32
    qry = jnp.dot(xn_ref[...], wq_ref[...], preferred_element_type=jnp.float32).astype(jnp.bfloat16)
    s0 = _nt(key_ref[0], qry[:, :half])
    s1 = _nt(key_ref[1], qry[:, half:])
    tm = s0.shape[1]
    v0, i0 = _topk_axis0(s0, K)
    v1, i1 = _topk_axis0(s1, K)
    m1 = _stack_rows(v1, jnp.float32)
    m0_hi = _stack_rows(v0[K // 2:], jnp.float32)
    r8 = lax.broadcasted_iota(i32, (K // 2, tm), 0)
    cand = [v0[0] + m1] + [v0[a] + m1[:K // 2] for a in range(1, K // 2)] + [m0_hi + v1[0]]
    flat = [lax.broadcasted_iota(i32, (K, tm), 0)] + [a * K + r8 for a in range(1, K // 2)] + [(K // 2 + r8) * K]
    cand = jnp.concatenate(cand, axis=0)
    flat = jnp.concatenate(flat, axis=0)
    big = K * K
    ts, tp = [], []
    for _ in range(K):
        m = jnp.max(cand, axis=0, keepdims=True)
        p = jnp.min(jnp.where(cand == m, flat, big), axis=0, keepdims=True)
        cand = jnp.where(flat == p, -jnp.inf, cand)
        ts.append(m)
        tp.append(p)
    pk = _stack_rows(tp, i32)
    pa, pb = pk // K, pk % K
    e0 = jnp.zeros((K, tm), i32)
    e1 = jnp.zeros((K, tm), i32)
    for a in range(K):
        e0 = jnp.where(pa == a, i0[a], e0)
        e1 = jnp.where(pb == a, i1[a], e1)
    ex = [jnp.exp(t - ts[0]) for t in ts]
    den = ex[0]
    for e in ex[1:]:
        den = den + e
    eidx_ref[...] = e0 * PEER_NKEYS + e1
    gate_ref[...] = _stack_rows([e / den for e in ex], jnp.float32)


def _route(h2d, g, wq, keys, tm):
    T = h2d.shape[0]
    out = jax.ShapeDtypeStruct((PEER_SLOTS, T), jnp.int32), jax.ShapeDtypeStruct((PEER_SLOTS, T), jnp.float32)
    ospec = pl.BlockSpec((PEER_TOPK, tm), lambda i, h: (h, i))
    return pl.pallas_call(
        _route_kernel,
        grid=(T // tm, PEER_HEADS),
        in_specs=[pl.BlockSpec((tm, D_MODEL), lambda i, h: (i, 0)),
                  pl.BlockSpec((1, D_MODEL), lambda i, h: (0, 0)),
                  pl.BlockSpec((D_MODEL, PEER_QDIM), lambda i, h: (0, h)),
                  pl.BlockSpec((2, PEER_NKEYS, PEER_QDIM // 2), lambda i, h: (h, 0, 0))],
        out_specs=(ospec, ospec),
        out_shape=out,
        scratch_shapes=[pltpu.VMEM((tm, D_MODEL), jnp.bfloat16)],
        compiler_params=_cparams(("parallel", "arbitrary")),
        name="peer_route",
    )(h2d, g, wq, keys)


def _pack_bf16_pairs(x):
    e, d = x.shape
    bits = lax.bitcast_convert_type(x.astype(jnp.bfloat16), jnp.uint16).astype(jnp.uint32)
    bits = bits.reshape(e, d // 256, 2, 128)
    return bits[:, :, 0, :] | (bits[:, :, 1, :] << 16)


def _sc_gather(u4, v4, idx):
    M = idx.shape[0]
    info = plsc.get_sparse_core_info()
    nw = info.num_cores * info.num_subcores
    per = M // nw
    assert per * nw == M and per % SC_WINDOW == 0
    mesh = plsc.VectorSubcoreMesh(core_axis_name="c", subcore_axis_name="s")
    oshape = jax.ShapeDtypeStruct((M,) + u4.shape[1:], u4.dtype)
    rows = pltpu.VMEM((SC_WINDOW,) + u4.shape[1:], u4.dtype)

    @functools.partial(
        pl.kernel, mesh=mesh, out_type=(oshape, oshape),
        scratch_types=[pltpu.VMEM((SC_WINDOW,), jnp.int32), rows, rows, pltpu.SemaphoreType.DMA, pltpu.SemaphoreType.DMA],
    )
    def gather(u_hbm, v_hbm, idx_hbm, ou_hbm, ov_hbm, idx_v, ru, rv, sem_u, sem_v):
        wid = lax.axis_index("s") * info.num_cores + lax.axis_index("c")
        base = wid * per

        @pl.loop(0, per // SC_WINDOW)
        def _(g):
            off = pl.multiple_of(base + g * SC_WINDOW, SC_WINDOW)
            pltpu.sync_copy(idx_hbm.at[pl.ds(off, SC_WINDOW)], idx_v)
            cu = pltpu.async_copy(u_hbm.at[idx_v], ru, sem_u)
            cv = pltpu.async_copy(v_hbm.at[idx_v], rv, sem_v)
            cu.wait()
            cv.wait()
            pltpu.sync_copy(ru, ou_hbm.at[pl.ds(off, SC_WINDOW)])
            pltpu.sync_copy(rv, ov_hbm.at[pl.ds(off, SC_WINDOW)])

    return gather(u4, v4, idx)


def _expert_kernel(idx_ref, idxn_ref, h_ref, g_ref, gate_ref, spread_ref, su_ref, sv_ref, uv_hbm, o_ref, buf, z_ref,
                   xn_ref, y_ref, sem):
    TD, TS, NS = PEER_TD, PEER_TS, PEER_SLOTS
    TT, LW = TD + TS, SUBLANES * PEER_SLOTS
    PR = SUBLANES // 2
    f32, bf16 = jnp.float32, jnp.bfloat16
    i = pl.program_id(0)
    n = pl.num_programs(0)
    slot = lax.rem(i, 2)

    def issue_token(iref, sl, t):
        for k in range(NS):
            pltpu.make_async_copy(uv_hbm.at[iref[t, k]], buf.at[sl, :, pl.ds((t * NS + k) * PR, PR)],
                                  sem.at[sl]).start(priority=k % 2)

    @pl.when(i == 0)
    def _():
        def body(t, c):
            issue_token(idx_ref, 0, t)
            return c
        lax.fori_loop(0, TD, body, 0)

    pltpu.make_async_copy(buf.at[slot], buf.at[slot], sem.at[slot]).wait()

    x = h_ref[...]
    xn_ref[...] = _rms(x, g_ref[...]).reshape(TT, SUBLANES, 128)
    on_diag = (lax.broadcasted_iota(jnp.int32, (SUBLANES, LW), 1) % SUBLANES
               == lax.broadcasted_iota(jnp.int32, (SUBLANES, LW), 0))

    def pieces_of(t, staged, part):
        r = pl.ds(pl.multiple_of(t * (NS * PR), NS * PR), NS * PR)
        words = (sv_ref if part else su_ref)[r, :] if staged else buf[slot, part, r, :]
        return pltpu.bitcast(words, bf16)

    def score_token(t, staged):
        tg = t + TD if staged else t
        z = _nt(xn_ref[tg].astype(bf16), pieces_of(t, staged, 0))
        z_ref[pl.ds(tg, 1), :] = jnp.sum(jnp.where(on_diag, z, 0.0), axis=0, keepdims=True)

    def mix_token(t, staged):
        tg = t + TD if staged else t
        w = jnp.where(on_diag, jnp.broadcast_to(z_ref[pl.ds(tg, 1), :], (SUBLANES, LW)), 0.0).astype(bf16)
        y_ref[tg] = jnp.dot(w, pieces_of(t, staged, 1), preferred_element_type=f32)

    def batched(count, fn):
        def body(tt, c):
            for j in range(PEER_BATCH):
                fn(tt * PEER_BATCH + j)
            return c
        lax.fori_loop(0, count // PEER_BATCH, body, 0)

    @pl.when(i + 1 < n)
    def _():
        def body(tt, c):
            for j in range(PEER_UNROLL):
                score_token(tt * PEER_UNROLL + j, False)
            for j in range(PEER_UNROLL):
                issue_token(idxn_ref, 1 - slot, tt * PEER_UNROLL + j)
            return c
        lax.fori_loop(0, TD // PEER_UNROLL, body, 0)

    @pl.when(i + 1 == n)
    def _():
        batched(TD, lambda t: score_token(t, False))

    batched(TS, lambda t: score_token(t, True))

    z = z_ref[...]
    z = z + pltpu.roll(z, 4, 1)
    z = z + pltpu.roll(z, 2, 1)
    z = z + pltpu.roll(z, 1, 1)
    gt = gate_ref[...]
    g1 = gt.astype(bf16)
    r1 = gt - g1.astype(f32)
    g2 = r1.astype(bf16)
    g3 = (r1 - g2.astype(f32)).astype(bf16)
    spread = spread_ref[...]
    gate8 = (jnp.dot(g1, spread, preferred_element_type=f32) + jnp.dot(g2, spread, preferred_element_type=f32)
             + jnp.dot(g3, spread, preferred_element_type=f32))
    w = gate8 * (0.5 * z * (1.0 + lax.erf(z * (2.0 ** -0.5))))
    w = w + pltpu.roll(w, LW - 1, 1)
    w = w + pltpu.roll(w, LW - 2, 1)
    z_ref[...] = w + pltpu.roll(w, LW - 4, 1)

    batched(TD, lambda t: mix_token(t, False))
    batched(TS, lambda t: mix_token(t, True))
    o_ref[...] = x + y_ref[...].reshape(TT, D_MODEL)


def _experts_chunk(step0, nsteps, idx_d, h2d, g, gate, spread, su, sv, uv):
    TD, TS, NS = PEER_TD, PEER_TS, PEER_SLOTS
    TT, PR = TD + TS, SUBLANES // 2
    last = step0 + nsteps - 1
    staged = pl.BlockSpec((TS * NS * PR, 128), lambda i: (i, 0))
    return pl.pallas_call(
        _expert_kernel,
        grid=(nsteps,),
        in_specs=[pl.BlockSpec((TD, NS), lambda i: (step0 + i, 0), memory_space=pltpu.SMEM),
                  pl.BlockSpec((TD, NS), lambda i: (jnp.minimum(step0 + i + 1, last), 0), memory_space=pltpu.SMEM),
                  pl.BlockSpec((TT, D_MODEL), lambda i: (step0 + i, 0)),
                  pl.BlockSpec((1, D_MODEL), lambda i: (0, 0)),
                  pl.BlockSpec((TT, NS), lambda i: (step0 + i, 0)),
                  pl.BlockSpec((NS, SUBLANES * NS), lambda i: (0, 0)),
                  staged, staged,
                  pl.BlockSpec(memory_space=pl.ANY)],
        out_specs=pl.BlockSpec((TT, D_MODEL), lambda i: (step0 + i, 0)),
        out_shape=jax.ShapeDtypeStruct(h2d.shape, jnp.float32),
        scratch_shapes=[pltpu.VMEM((2, 2, TD * NS * PR, 128), jnp.uint32),
                        pltpu.VMEM((TT, SUBLANES * NS), jnp.float32),
                        pltpu.VMEM((TT, SUBLANES, 128), jnp.float32),
                        pltpu.VMEM((TT, SUBLANES, 128), jnp.float32),
                        pltpu.SemaphoreType.DMA((2,))],
        input_output_aliases={2: 0},
        compiler_params=_cparams(("arbitrary",)),
        name="peer_experts",
    )(idx_d, idx_d, h2d, g, gate, spread, su, sv, uv)


def _experts(h2d, g, eidx, gate, u4, v4):
    T = h2d.shape[0]
    TD, TS, NS = PEER_TD, PEER_TS, PEER_SLOTS
    TT = TD + TS
    steps = T // TT
    chunks = _pick_tile(steps, (24, 12, 8, 6, 4, 3, 2, 1))
    spc = steps // chunks
    e3 = eidx.reshape(steps, TT, NS)
    idx_d = e3[:, :TD].reshape(steps * TD, NS)
    idx_s = e3[:, TD:].reshape(chunks, spc * TS * NS)
    spread = np.zeros((NS, SUBLANES * NS), np.float32)
    spread[np.arange(NS), SUBLANES * np.arange(NS) + SUBLANES - 1] = 1.0
    spread = jnp.asarray(spread, jnp.bfloat16)
    uv = jnp.stack([u4, v4], axis=1)
    for c in range(chunks):
        su, sv = _sc_gather(u4, v4, idx_s[c])
        su, sv = (a.reshape(-1, 128) for a in (su, sv))
        h2d = _experts_chunk(c * spc, spc, idx_d, h2d, g, gate, spread, su, sv, uv)
    return h2d


def _final_kernel(h_ref, g_ref, o_ref):
    o_ref[...] = _rms(h_ref[...], g_ref[...])


def _final(h3, g, b0, nb):
    _, Lp, _ = h3.shape
    S = Lp - BLK
    return pl.pallas_call(
        _final_kernel,
        grid=(nb, S // BLK),
        in_specs=[pl.BlockSpec((None, BLK, D_MODEL), lambda b, r: (b + b0, r + 1, 0)),
                  pl.BlockSpec((1, D_MODEL), lambda b, r: (0, 0))],
        out_specs=pl.BlockSpec((None, BLK, D_MODEL), lambda b, r: (b, r, 0)),
        out_shape=jax.ShapeDtypeStruct((nb, S, D_MODEL), jnp.float32),
        compiler_params=_cparams(("parallel", "parallel")),
        name="final_norm",
    )(h3, g)


def _pick_tile(T, candidates):
    for c in candidates:
        if T % c == 0:
            return c
    raise ValueError(f"no tile for {T}")


def kernel(x_prompt, x_sample, meta_tokens, g_mix, w_in, attn_sinks, lb_logits, hg_norm_g, w_att_branch,
           w_hg_branch, w_out, g_ffn, w_peer_q, peer_sub_keys, peer_u, peer_v, g_final):
    f32, bf16 = jnp.float32, jnp.bfloat16
    depth = w_in.shape[0]
    nb_p, S, _ = x_prompt.shape
    nb_s = x_sample.shape[0]
    assert x_sample.shape[1] == S and S % BLK == 0
    NB = nb_p + nb_s
    Lp = S + BLK
    T = NB * Lp

    x = jnp.concatenate([x_prompt, x_sample], axis=0)
    head = jnp.concatenate([jnp.zeros((PAD_ROWS, D_MODEL), f32), meta_tokens.astype(f32)], axis=0)
    h = jnp.concatenate([jnp.broadcast_to(head[None], (NB, BLK, D_MODEL)), x], axis=1).reshape(T, D_MODEL)

    sizes = (1024, 256, 256, 1024, 1024, 1024, 1024, 1024, 1024, 1024)
    offs = np.concatenate([[0], np.cumsum(sizes)])
    order = (0, 8, 9, 3, 4, 5, 6, 7, 1, 2)
    perm = np.concatenate([np.arange(offs[k], offs[k + 1]) for k in order])

    sm = jax.nn.softmax(lb_logits.astype(f32), axis=0)
    lb_all = jnp.maximum(jnp.cumsum(sm, axis=0) - sm[0:1], 0.0)

    tables = _rope_tables(Lp)
    consts_f = _hg_consts(False)
    consts_b = _hg_consts(True)
    tm_big = _pick_tile(T, (512, 384, 256, 128))
    tm_route = _pick_tile(T, (256, 128))

    for l in range(depth):
        w_l = w_in[l][:, perm].astype(bf16)
        proj = _inproj(h, g_mix[l].reshape(1, -1), w_l, tm_big, IN_W // 4)
        proj3 = proj.reshape(NB, Lp, IN_W)
        a_out = _attention(proj3, attn_sinks[l].astype(f32), tables)
        lb3 = lb_all[l].reshape(2, 1, -1)
        o_fwd, o_bwd = _hgrn(proj3, lb3, consts_f, consts_b)
        h = _merge(a_out.reshape(T, -1), o_fwd.reshape(T, -1), o_bwd.reshape(T, -1), hg_norm_g[l].reshape(1, -1),
                   proj, h, w_att_branch[l].astype(bf16), w_hg_branch[l].astype(bf16), w_out[l].astype(bf16),
                   tm_route)
        g_f = g_ffn[l].reshape(1, -1)
        keys = peer_sub_keys[l].reshape(2 * PEER_HEADS, PEER_NKEYS, PEER_QDIM // 2).astype(bf16)
        eidx_t, gate_t = _route(h, g_f, w_peer_q[l].astype(bf16), keys, tm_route)
        h = _experts(h, g_f, eidx_t.T, gate_t.T, _pack_bf16_pairs(peer_u[l]), _pack_bf16_pairs(peer_v[l]))

    h3 = h.reshape(NB, Lp, D_MODEL)
    g_fin = g_final.reshape(1, -1)
    return _final(h3, g_fin, 0, nb_p), _final(h3, g_fin, nb_p, nb_s)
```

```python
import functools

import numpy as np
import jax
import jax.numpy as jnp
from jax import lax
from jax.experimental import pallas as pl
from jax.experimental.pallas import tpu as pltpu
from jax.experimental.pallas import tpu_sc as plsc

D_MODEL = 1024
N_META = 16
N_Q_HEADS = 16
N_KV_HEADS = 4
HEAD_DIM = 64
WINDOW = 128
ROT_DIM = HEAD_DIM // 4
ROPE_THETA = 500000.0
HG_HEADS = 8
HG_DIM = 128
PEER_HEADS = 8
PEER_NKEYS = 128
PEER_QDIM = 256
PEER_TOPK = 16
EPS = 1e-6
NEG = -1e30

BLK = 128
PAD_ROWS = BLK - N_META
IN_W = 8704
COL_AQ, COL_GA, COL_GH, COL_HQ, COL_ZF, COL_ZB, COL_HI, COL_HGG, COL_AK, COL_AV = (
    0, 1024, 2048, 3072, 4096, 5120, 6144, 7168, 8192, 8448)

VMEM_LIMIT = 56 * 1024 * 1024
PEER_TD = 16
PEER_TS = 16
SC_WINDOW = 64
PEER_SLOTS = PEER_HEADS * PEER_TOPK
HG_HP = 8
PEER_UNROLL = 2
PEER_BATCH = 8
SUBLANES = 8


def _cparams(sem):
    return pltpu.CompilerParams(dimension_semantics=sem, vmem_limit_bytes=VMEM_LIMIT)


def _rms(x, g):
    return x * lax.rsqrt(jnp.mean(x * x, axis=-1, keepdims=True) + EPS) * g


def _nt(a, b):
    return lax.dot_general(a, b, (((1,), (1,)), ((), ())), preferred_element_type=jnp.float32)


def _tn(a, b):
    return lax.dot_general(a, b, (((0,), (0,)), ((), ())), preferred_element_type=jnp.float32)


def _inproj_kernel(x_ref, g_ref, w_ref, o_ref):
    n = _rms(x_ref[...], g_ref[...]).astype(jnp.bfloat16)
    o_ref[...] = jnp.dot(n, w_ref[...], preferred_element_type=jnp.float32)


def _inproj(h2d, g, w, tm, tn):
    T = h2d.shape[0]
    return pl.pallas_call(
        _inproj_kernel,
        grid=(IN_W // tn, T // tm),
        in_specs=[pl.BlockSpec((tm, D_MODEL), lambda j, i: (i, 0)),
                  pl.BlockSpec((1, D_MODEL), lambda j, i: (0, 0)),
                  pl.BlockSpec((D_MODEL, tn), lambda j, i: (0, j))],
        out_specs=pl.BlockSpec((tm, tn), lambda j, i: (i, j)),
        out_shape=jax.ShapeDtypeStruct((T, IN_W), jnp.float32),
        compiler_params=_cparams(("parallel", "parallel")),
        name="inproj",
    )(h2d, g, w)


def _rope_tables(Lp):
    half = ROT_DIM // 2
    pos = jnp.maximum(jnp.arange(Lp) - PAD_ROWS, 0)
    inv = jnp.power(jnp.float32(ROPE_THETA), -jnp.arange(half, dtype=jnp.float32) * 2.0 / ROT_DIM)
    ang = pos.astype(jnp.float32)[:, None] * inv[None, :]
    cos, sin = jnp.cos(ang), jnp.sin(ang)
    ones = jnp.ones((Lp, HEAD_DIM - ROT_DIM), jnp.float32)
    zeros = jnp.zeros((Lp, HEAD_DIM - ROT_DIM), jnp.float32)
    zh = jnp.zeros((Lp, half), jnp.float32)
    c = jnp.concatenate([cos, cos, ones], axis=1)
    s1 = jnp.concatenate([zh, sin, zeros], axis=1)
    s2 = jnp.concatenate([-sin, zh, zeros], axis=1)
    rep = BLK // HEAD_DIM
    return tuple(jnp.tile(t, (1, rep)) for t in (c, s1, s2))


def _attn_kernel(sink_ref, q_ref, kg_ref, k0_ref, k1_ref, k2_ref, vg_ref, v0_ref, v1_ref, v2_ref,
                 c_ref, s1_ref, s2_ref, o_ref):
    r = pl.program_id(1)
    nblk = pl.num_programs(1)
    half = ROT_DIM // 2

    def rope(x, blk):
        row0 = blk * BLK if isinstance(blk, int) else pl.multiple_of(blk * BLK, BLK)
        c = c_ref[pl.ds(row0, BLK), :]
        s1 = s1_ref[pl.ds(row0, BLK), :]
        s2 = s2_ref[pl.ds(row0, BLK), :]
        outs = []
        for j in range(x.shape[1] // BLK):
            xj = x[:, j * BLK:(j + 1) * BLK]
            outs.append(xj * c + pltpu.roll(xj, half, 1) * s1 + pltpu.roll(xj, BLK - half, 1) * s2)
        return jnp.concatenate(outs, axis=1)

    kb = [jnp.clip(r - 1 + j, 0, nblk - 1) for j in range(3)]
    q = rope(q_ref[...], r) * (HEAD_DIM ** -0.5)
    kcat = jnp.concatenate([rope(kg_ref[...], 0), rope(k0_ref[...], kb[0]), rope(k1_ref[...], kb[1]),
                            rope(k2_ref[...], kb[2])], axis=0).astype(jnp.bfloat16)
    vcat = jnp.concatenate([vg_ref[...], v0_ref[...], v1_ref[...], v2_ref[...]], axis=0).astype(jnp.bfloat16)

    qrow = lax.broadcasted_iota(jnp.int32, (BLK, BLK), 0)
    kcol = lax.broadcasted_iota(jnp.int32, (BLK, BLK), 1)
    masks = [kcol >= PAD_ROWS]
    for j in range(3):
        kblk = r - 1 + j
        ok_blk = (kblk >= 1) & (kblk <= nblk - 1)
        masks.append((jnp.abs((j - 1) * BLK + kcol - qrow) <= WINDOW) & ok_blk)
    mask = jnp.concatenate(masks, axis=1)

    G = N_Q_HEADS // N_KV_HEADS
    lane_half = kcol // HEAD_DIM
    pairs_per_slice = (BLK // HEAD_DIM) * G // 2
    outs = [None] * (N_Q_HEADS // 2)
    for m2 in range(N_KV_HEADS * HEAD_DIM // BLK):
        k2 = kcat[:, m2 * BLK:(m2 + 1) * BLK]
        v2 = vcat[:, m2 * BLK:(m2 + 1) * BLK]
        qs, sks, halves = [], [], []
        for j in range(m2 * pairs_per_slice, (m2 + 1) * pairs_per_slice):
            hk = ((2 * j) // G) % 2
            in_half = lane_half == hk
            q2 = q[:, j * BLK:(j + 1) * BLK]
            qs += [jnp.where(in_half, q2, 0.0), jnp.where(in_half, pltpu.roll(q2, HEAD_DIM, 1), 0.0)]
            sks += [jnp.full((BLK, 1), sink_ref[2 * j + hk]), jnp.full((BLK, 1), sink_ref[2 * j + 1 - hk])]
            halves.append(in_half)
        nq = len(qs)
        sk = jnp.concatenate(sks, axis=0)
        s = _nt(jnp.concatenate(qs, axis=0).astype(jnp.bfloat16), k2)
        s = jnp.where(jnp.tile(mask, (nq, 1)), s, NEG)
        m = jnp.maximum(jnp.max(s, axis=-1, keepdims=True), sk)
        p = jnp.exp(s - m)
        p = p / (jnp.sum(p, axis=-1, keepdims=True) + jnp.exp(sk - m))
        o = jnp.dot(p.astype(jnp.bfloat16), v2, preferred_element_type=jnp.float32)
        for jj, in_half in enumerate(halves):
            o_same = o[(2 * jj) * BLK:(2 * jj + 1) * BLK]
            o_other = o[(2 * jj + 1) * BLK:(2 * jj + 2) * BLK]
            outs[m2 * pairs_per_slice + jj] = jnp.where(in_half, o_same, pltpu.roll(o_other, HEAD_DIM, 1))
    o_ref[...] = jnp.concatenate(outs, axis=1).astype(o_ref.dtype)


def _attention(proj3, sinks, tables):
    NB, Lp, _ = proj3.shape
    nblk = Lp // BLK
    kcol, vcol = COL_AK // 256, COL_AV // 256

    def loc(j, col):
        return pl.BlockSpec((None, BLK, 256),
                            lambda b, r: (b, jnp.clip(r - 1 + j, 0, nblk - 1), col))

    glob = lambda col: pl.BlockSpec((None, BLK, 256), lambda b, r: (b, 0, col))
    tab = pl.BlockSpec((Lp, BLK), lambda b, r: (0, 0))
    return pl.pallas_call(
        _attn_kernel,
        grid=(NB, nblk),
        in_specs=[pl.BlockSpec(memory_space=pltpu.SMEM),
                  pl.BlockSpec((None, BLK, 1024), lambda b, r: (b, r, 0)),
                  glob(kcol), loc(0, kcol), loc(1, kcol), loc(2, kcol),
                  glob(vcol), loc(0, vcol), loc(1, vcol), loc(2, vcol),
                  tab, tab, tab],
        out_specs=pl.BlockSpec((None, BLK, 1024), lambda b, r: (b, r, 0)),
        out_shape=jax.ShapeDtypeStruct((NB, Lp, 1024), jnp.bfloat16),
        compiler_params=_cparams(("parallel", "arbitrary")),
        name="window_attn",
    )(sinks, proj3, proj3, proj3, proj3, proj3, proj3, proj3, proj3, proj3, *tables)


HG_ROWS = BLK
_HG_LEVELS = (64, 32, 16, 8, 4, 2, 1)


def _hg_consts(reverse):
    C = HG_ROWS
    t = np.arange(C)[:, None]
    u = np.arange(C)[None, :]
    mats = [u <= t, u > t]
    ups, msks = [], []
    for B in _HG_LEVELS:
        same = (t // (2 * B)) == (u // (2 * B))
        tin, uin = t % (2 * B), u % (2 * B)
        upper = tin >= B
        wq = upper & same & (uin >= B) & (uin <= tin)
        wk = (~upper) & same & (uin > tin) & (uin <= B - 1)
        mats.append(wq | wk)
        ups.append(np.broadcast_to(upper, (C, HG_DIM)))
        msks.append(same & upper & (uin < B))
    msks.append(np.eye(C, dtype=bool))
    if reverse:
        mats = [m[::-1, ::-1] for m in mats]
        ups = [m[::-1] for m in ups]
        msks = [m[::-1, ::-1] for m in msks]
    W = np.concatenate(mats, axis=0).astype(np.float32)
    W2 = np.concatenate([W, W], axis=1)
    return (jnp.asarray(W2, jnp.bfloat16), jnp.asarray(np.stack(ups).astype(np.float32)),
            jnp.asarray(np.stack(msks).astype(np.float32)))


def _hg_chunk(q_raw, z, v, valid, lb, w2, up_ref, msk_ref, st, reverse):
    C = HG_ROWS
    f32, bf16 = jnp.float32, jnp.bfloat16
    log_lb = jnp.log(lb)
    log1m = jnp.log1p(-lb)
    logsig = jnp.minimum(z, 0.0) - jnp.log1p(jnp.exp(-jnp.abs(z)))
    a, b = log_lb, log1m + logsig
    g = jnp.maximum(a, b) + jnp.log1p(jnp.exp(-jnp.abs(a - b)))
    k = (1.0 - lb) * jax.nn.sigmoid(-z)
    q = q_raw * jax.nn.sigmoid(q_raw)
    q = jnp.where(valid, q, 0.0)
    k = jnp.where(valid, k, 0.0)
    v = jnp.where(valid, v, 0.0)
    g = jnp.where(valid, g, 0.0)
    g1 = g.astype(bf16)
    g2 = (g - g1.astype(f32)).astype(bf16)
    E = jnp.dot(w2, jnp.concatenate([g1, g2], axis=0), preferred_element_type=f32)
    b_inc = E[0:C]
    b_sfx = E[C:2 * C]
    b_tot = b_inc[0:1] if reverse else b_inc[C - 1:C]

    A = msk_ref[len(_HG_LEVELS)] * jnp.sum(q * k, axis=-1, keepdims=True)
    for l in range(len(_HG_LEVELS)):
        ex = jnp.exp(E[(2 + l) * C:(3 + l) * C])
        up = up_ref[l] > 0.5
        qs = jnp.where(up, q * ex, 0.0).astype(bf16)
        ks = jnp.where(up, 0.0, k * ex).astype(bf16)
        A = A + _nt(qs, ks) * msk_ref[l]

    vb = v.astype(bf16)
    o = _nt((q * jnp.exp(b_inc)).astype(bf16), st.astype(bf16))
    o = o + jnp.dot(A.astype(bf16), vb, preferred_element_type=f32)
    st = st * jnp.exp(b_tot) + _tn(vb, (k * jnp.exp(b_sfx)).astype(bf16))
    return o, st


def _hg_kernel(qf_ref, zf_ref, vf_ref, qb_ref, zb_ref, vb_ref, lb_ref, w2f_ref, upf_ref, mskf_ref,
               w2b_ref, upb_ref, mskb_ref, of_ref, ob_ref, st_ref):
    i = pl.program_id(2)
    nblk = pl.num_programs(2)

    @pl.when(i == 0)
    def _():
        st_ref[...] = jnp.zeros_like(st_ref)

    rows = lax.broadcasted_iota(jnp.int32, (HG_ROWS, HG_DIM), 0)
    dirs = ((False, qf_ref, zf_ref, vf_ref, w2f_ref, upf_ref, mskf_ref, of_ref, i),
            (True, qb_ref, zb_ref, vb_ref, w2b_ref, upb_ref, mskb_ref, ob_ref, nblk - 1 - i))
    for d, (reverse, q_ref, z_ref, v_ref, w2_ref, up_ref, msk_ref, o_ref, blk) in enumerate(dirs):
        w2 = w2_ref[...]
        for hh in range(HG_HP):
            cols = slice(hh * HG_DIM, (hh + 1) * HG_DIM)
            lb = lb_ref[d, :, cols]
            valid = (rows >= PAD_ROWS) | (blk > 0)
            o, st = _hg_chunk(q_ref[:, cols], z_ref[:, cols], v_ref[:, cols], valid, lb, w2,
                              up_ref, msk_ref, st_ref[d, hh], reverse)
            st_ref[d, hh] = st
            o_ref[:, cols] = o


def _hgrn(proj3, lb3, consts_f, consts_b):
    NB, Lp, _ = proj3.shape
    nblk = Lp // BLK
    W = HG_DIM * HG_HP

    def col(base, reverse):
        return pl.BlockSpec((None, BLK, W),
                            lambda b, h, i: (b, (nblk - 1 - i) if reverse else i, base // W + h))

    def const(c):
        return pl.BlockSpec(c.shape, lambda b, h, i: (0,) * c.ndim)

    in_specs = [col(COL_HQ, False), col(COL_ZF, False), col(COL_HI, False),
                col(COL_HQ, True), col(COL_ZB, True), col(COL_HI, True),
                pl.BlockSpec((2, 1, W), lambda b, h, i: (0, 0, h))]
    in_specs += [const(c) for c in consts_f] + [const(c) for c in consts_b]
    oshape = jax.ShapeDtypeStruct((NB, Lp, HG_HEADS * HG_DIM), jnp.float32)
    return pl.pallas_call(
        _hg_kernel,
        grid=(NB, HG_HEADS // HG_HP, nblk),
        in_specs=in_specs,
        out_specs=(pl.BlockSpec((None, BLK, W), lambda b, h, i: (b, i, h)),
                   pl.BlockSpec((None, BLK, W), lambda b, h, i: (b, nblk - 1 - i, h))),
        out_shape=(oshape, oshape),
        scratch_shapes=[pltpu.VMEM((2, HG_HP, HG_DIM, HG_DIM), jnp.float32)],
        compiler_params=_cparams(("parallel", "parallel", "arbitrary")),
        name="hgrn_bidir",
    )(proj3, proj3, proj3, proj3, proj3, proj3, lb3, *consts_f, *consts_b)


def _merge_kernel(a_ref, of_ref, ob_ref, hgg_ref, ng_ref, ga_ref, gh_ref, h_ref, wa_ref, wh_ref, wo_ref, o_ref):
    f32, bf16 = jnp.float32, jnp.bfloat16
    o = of_ref[...] + ob_ref[...]
    ng = ng_ref[...]
    gt = hgg_ref[...]
    hh = jnp.concatenate([_rms(o[:, c:c + HG_DIM], ng[:, c:c + HG_DIM]) for c in range(0, HG_HEADS * HG_DIM, HG_DIM)],
                         axis=1) * (gt * jax.nn.sigmoid(gt))
    pa = jnp.dot(a_ref[...], wa_ref[...], preferred_element_type=f32)
    ph = jnp.dot(hh.astype(bf16), wh_ref[...], preferred_element_type=f32)
    merged = jax.nn.sigmoid(ga_ref[...]) * pa + jax.nn.sigmoid(gh_ref[...]) * ph
    o_ref[...] = h_ref[...] + jnp.dot(merged.astype(bf16), wo_ref[...], preferred_element_type=f32)


def _merge(a2d, of2d, ob2d, ng, proj2d, h2d, wa, wh, wo, tm):
    T = h2d.shape[0]
    row = lambda c: pl.BlockSpec((tm, D_MODEL), lambda i: (i, c))
    wspec = pl.BlockSpec((D_MODEL, D_MODEL), lambda i: (0, 0))
    return pl.pallas_call(
        _merge_kernel,
        grid=(T // tm,),
        in_specs=[row(0), row(0), row(0), row(COL_HGG // D_MODEL), pl.BlockSpec((1, D_MODEL), lambda i: (0, 0)),
                  row(COL_GA // D_MODEL), row(COL_GH // D_MODEL), row(0), wspec, wspec, wspec],
        out_specs=row(0),
        out_shape=jax.ShapeDtypeStruct((T, D_MODEL), jnp.float32),
        input_output_aliases={7: 0},
        compiler_params=_cparams(("parallel",)),
        name="merge_out",
    )(a2d, of2d, ob2d, proj2d, ng, proj2d, proj2d, h2d, wa, wh, wo)


def _topk_axis0(s, kk):
    n = s.shape[0]
    iota = lax.broadcasted_iota(jnp.int32, s.shape, 0)
    vals, ids = [], []
    for _ in range(kk):
        m = jnp.max(s, axis=0, keepdims=True)
        am = jnp.min(jnp.where(s == m, iota, n), axis=0, keepdims=True)
        s = jnp.where(iota == am, -jnp.inf, s)
        vals.append(m)
        ids.append(am)
    return vals, ids


def _stack_rows(rows, dtype):
    n = len(rows)
    iota = lax.broadcasted_iota(jnp.int32, (n, rows[0].shape[1]), 0)
    out = jnp.zeros((n, rows[0].shape[1]), dtype)
    for j, rj in enumerate(rows):
        out = jnp.where(iota == j, rj, out)
    return out


def _route_kernel(h_ref, g_ref, wq_ref, key_ref, eidx_ref, gate_ref, xn_ref):
    @pl.when(pl.program_id(1) == 0)
    def _():
        xn_ref[...] = _rms(h_ref[...], g_ref[...]).astype(jnp.bfloat16)

    K = PEER_TOPK
    half = PEER_QDIM // 2
    i32 = jnp.int32
    qry = jnp.dot(xn_ref[...], wq_ref[...], preferred_element_type=jnp.float32).astype(jnp.bfloat16)
    s0 = _nt(key_ref[0], qry[:, :half])
    s1 = _nt(key_ref[1], qry[:, half:])
    tm = s0.shape[1]
    v0, i0 = _topk_axis0(s0, K)
    v1, i1 = _topk_axis0(s1, K)
    m1 = _stack_rows(v1, jnp.float32)
    m0_hi = _stack_rows(v0[K // 2:], jnp.float32)
    r8 = lax.broadcasted_iota(i32, (K // 2, tm), 0)
    cand = [v0[0] + m1] + [v0[a] + m1[:K // 2] for a in range(1, K // 2)] + [m0_hi + v1[0]]
    flat = [lax.broadcasted_iota(i32, (K, tm), 0)] + [a * K + r8 for a in range(1, K // 2)] + [(K // 2 + r8) * K]
    cand = jnp.concatenate(cand, axis=0)
    flat = jnp.concatenate(flat, axis=0)
    big = K * K
    ts, tp = [], []
    for _ in range(K):
        m = jnp.max(cand, axis=0, keepdims=True)
        p = jnp.min(jnp.where(cand == m, flat, big), axis=0, keepdims=True)
        cand = jnp.where(flat == p, -jnp.inf, cand)
        ts.append(m)
        tp.append(p)
    pk = _stack_rows(tp, i32)
    pa, pb = pk // K, pk % K
    e0 = jnp.zeros((K, tm), i32)
    e1 = jnp.zeros((K, tm), i32)
    for a in range(K):
        e0 = jnp.where(pa == a, i0[a], e0)
        e1 = jnp.where(pb == a, i1[a], e1)
    ex = [jnp.exp(t - ts[0]) for t in ts]
    den = ex[0]
    for e in ex[1:]:
        den = den + e
    eidx_ref[...] = e0 * PEER_NKEYS + e1
    gate_ref[...] = _stack_rows([e / den for e in ex], jnp.float32)


def _route(h2d, g, wq, keys, tm):
    T = h2d.shape[0]
    out = jax.ShapeDtypeStruct((PEER_SLOTS, T), jnp.int32), jax.ShapeDtypeStruct((PEER_SLOTS, T), jnp.float32)
    ospec = pl.BlockSpec((PEER_TOPK, tm), lambda i, h: (h, i))
    return pl.pallas_call(
        _route_kernel,
        grid=(T // tm, PEER_HEADS),
        in_specs=[pl.BlockSpec((tm, D_MODEL), lambda i, h: (i, 0)),
                  pl.BlockSpec((1, D_MODEL), lambda i, h: (0, 0)),
                  pl.BlockSpec((D_MODEL, PEER_QDIM), lambda i, h: (0, h)),
                  pl.BlockSpec((2, PEER_NKEYS, PEER_QDIM // 2), lambda i, h: (h, 0, 0))],
        out_specs=(ospec, ospec),
        out_shape=out,
        scratch_shapes=[pltpu.VMEM((tm, D_MODEL), jnp.bfloat16)],
        compiler_params=_cparams(("parallel", "arbitrary")),
        name="peer_route",
    )(h2d, g, wq, keys)


def _pack_bf16_pairs(x):
    e, d = x.shape
    bits = lax.bitcast_convert_type(x.astype(jnp.bfloat16), jnp.uint16).astype(jnp.uint32)
    bits = bits.reshape(e, d // 256, 2, 128)
    return bits[:, :, 0, :] | (bits[:, :, 1, :] << 16)


def _sc_gather(u4, v4, idx):
    M = idx.shape[0]
    info = plsc.get_sparse_core_info()
    nw = info.num_cores * info.num_subcores
    per = M // nw
    assert per * nw == M and per % SC_WINDOW == 0
    mesh = plsc.VectorSubcoreMesh(core_axis_name="c", subcore_axis_name="s")
    oshape = jax.ShapeDtypeStruct((M,) + u4.shape[1:], u4.dtype)
    rows = pltpu.VMEM((SC_WINDOW,) + u4.shape[1:], u4.dtype)

    @functools.partial(
        pl.kernel, mesh=mesh, out_type=(oshape, oshape),
        scratch_types=[pltpu.VMEM((SC_WINDOW,), jnp.int32), rows, rows, pltpu.SemaphoreType.DMA, pltpu.SemaphoreType.DMA],
    )
    def gather(u_hbm, v_hbm, idx_hbm, ou_hbm, ov_hbm, idx_v, ru, rv, sem_u, sem_v):
        wid = lax.axis_index("s") * info.num_cores + lax.axis_index("c")
        base = wid * per

        @pl.loop(0, per // SC_WINDOW)
        def _(g):
            off = pl.multiple_of(base + g * SC_WINDOW, SC_WINDOW)
            pltpu.sync_copy(idx_hbm.at[pl.ds(off, SC_WINDOW)], idx_v)
            cu = pltpu.async_copy(u_hbm.at[idx_v], ru, sem_u)
            cv = pltpu.async_copy(v_hbm.at[idx_v], rv, sem_v)
            cu.wait()
            cv.wait()
            pltpu.sync_copy(ru, ou_hbm.at[pl.ds(off, SC_WINDOW)])
            pltpu.sync_copy(rv, ov_hbm.at[pl.ds(off, SC_WINDOW)])

    return gather(u4, v4, idx)


def _expert_kernel(idx_ref, idxn_ref, h_ref, g_ref, gate_ref, spread_ref, su_ref, sv_ref, uv_hbm, o_ref, buf, z_ref,
                   xn_ref, y_ref, sem, *, step0, seq_steps, pad_steps):
    TD, TS, NS = PEER_TD, PEER_TS, PEER_SLOTS
    TT, LW = TD + TS, SUBLANES * PEER_SLOTS
    PR = SUBLANES // 2
    f32, bf16 = jnp.float32, jnp.bfloat16
    i = pl.program_id(0)
    n = pl.num_programs(0)
    slot = lax.rem(i, 2)
    live = lax.rem(step0 + i, seq_steps) >= pad_steps
    live_next = (i + 1 < n) & (lax.rem(step0 + i + 1, seq_steps) >= pad_steps)
    dead = jnp.logical_not(live)

    def issue_token(iref, sl, t):
        for k in range(NS):
            pltpu.make_async_copy(uv_hbm.at[iref[t, k]], buf.at[sl, :, pl.ds((t * NS + k) * PR, PR)],
                                  sem.at[sl]).start(priority=k % 2)

    def issue_all(iref, sl):
        def body(t, c):
            issue_token(iref, sl, t)
            return c
        lax.fori_loop(0, TD, body, 0)

    @pl.when((i == 0) & live)
    def _():
        issue_all(idx_ref, 0)

    @pl.when(live)
    def _():
        pltpu.make_async_copy(buf.at[slot], buf.at[slot], sem.at[slot]).wait()

    @pl.when(dead & live_next)
    def _():
        issue_all(idxn_ref, 1 - slot)

    @pl.when(dead)
    def _():
        o_ref[...] = h_ref[...]

    @pl.when(live)
    def _():
        _expert_step(idxn_ref, h_ref, g_ref, gate_ref, spread_ref, su_ref, sv_ref, o_ref, buf, z_ref, xn_ref, y_ref,
                     slot, live_next, issue_token)


def _expert_step(idxn_ref, h_ref, g_ref, gate_ref, spread_ref, su_ref, sv_ref, o_ref, buf, z_ref, xn_ref, y_ref,
                 slot, live_next, issue_token):
    TD, TS, NS = PEER_TD, PEER_TS, PEER_SLOTS
    TT, LW = TD + TS, SUBLANES * PEER_SLOTS
    PR = SUBLANES // 2
    f32, bf16 = jnp.float32, jnp.bfloat16

    x = h_ref[...]
    xn_ref[...] = _rms(x, g_ref[...]).reshape(TT, SUBLANES, 128)
    on_diag = (lax.broadcasted_iota(jnp.int32, (SUBLANES, LW), 1) % SUBLANES
               == lax.broadcasted_iota(jnp.int32, (SUBLANES, LW), 0))

    def pieces_of(t, staged, part):
        r = pl.ds(pl.multiple_of(t * (NS * PR), NS * PR), NS * PR)
        words = (sv_ref if part else su_ref)[r, :] if staged else buf[slot, part, r, :]
        return pltpu.bitcast(words, bf16)

    def score_token(t, staged):
        tg = t + TD if staged else t
        z = _nt(xn_ref[tg].astype(bf16), pieces_of(t, staged, 0))
        z_ref[pl.ds(tg, 1), :] = jnp.sum(jnp.where(on_diag, z, 0.0), axis=0, keepdims=True)

    def mix_token(t, staged):
        tg = t + TD if staged else t
        w = jnp.where(on_diag, jnp.broadcast_to(z_ref[pl.ds(tg, 1), :], (SUBLANES, LW)), 0.0).astype(bf16)
        y_ref[tg] = jnp.dot(w, pieces_of(t, staged, 1), preferred_element_type=f32)

    def batched(count, fn):
        def body(tt, c):
            for j in range(PEER_BATCH):
                fn(tt * PEER_BATCH + j)
            return c
        lax.fori_loop(0, count // PEER_BATCH, body, 0)

    @pl.when(live_next)
    def _():
        def body(tt, c):
            for j in range(PEER_UNROLL):
                score_token(tt * PEER_UNROLL + j, False)
            for j in range(PEER_UNROLL):
                issue_token(idxn_ref, 1 - slot, tt * PEER_UNROLL + j)
            return c
        lax.fori_loop(0, TD // PEER_UNROLL, body, 0)

    @pl.when(jnp.logical_not(live_next))
    def _():
        batched(TD, lambda t: score_token(t, False))

    batched(TS, lambda t: score_token(t, True))

    z = z_ref[...]
    z = z + pltpu.roll(z, 4, 1)
    z = z + pltpu.roll(z, 2, 1)
    z = z + pltpu.roll(z, 1, 1)
    gt = gate_ref[...]
    g1 = gt.astype(bf16)
    r1 = gt - g1.astype(f32)
    g2 = r1.astype(bf16)
    g3 = (r1 - g2.astype(f32)).astype(bf16)
    spread = spread_ref[...]
    gate8 = (jnp.dot(g1, spread, preferred_element_type=f32) + jnp.dot(g2, spread, preferred_element_type=f32)
             + jnp.dot(g3, spread, preferred_element_type=f32))
    w = gate8 * (0.5 * z * (1.0 + lax.erf(z * (2.0 ** -0.5))))
    w = w + pltpu.roll(w, LW - 1, 1)
    w = w + pltpu.roll(w, LW - 2, 1)
    z_ref[...] = w + pltpu.roll(w, LW - 4, 1)

    batched(TD, lambda t: mix_token(t, False))
    batched(TS, lambda t: mix_token(t, True))
    o_ref[...] = x + y_ref[...].reshape(TT, D_MODEL)


def _experts_chunk(step0, nsteps, seq_steps, idx_d, h2d, g, gate, spread, su, sv, uv):
    TD, TS, NS = PEER_TD, PEER_TS, PEER_SLOTS
    TT, PR = TD + TS, SUBLANES // 2
    last = step0 + nsteps - 1
    staged = pl.BlockSpec((TS * NS * PR, 128), lambda i: (i, 0))
    return pl.pallas_call(
        functools.partial(_expert_kernel, step0=step0, seq_steps=seq_steps, pad_steps=PAD_ROWS // TT),
        grid=(nsteps,),
        in_specs=[pl.BlockSpec((TD, NS), lambda i: (step0 + i, 0), memory_space=pltpu.SMEM),
                  pl.BlockSpec((TD, NS), lambda i: (jnp.minimum(step0 + i + 1, last), 0), memory_space=pltpu.SMEM),
                  pl.BlockSpec((TT, D_MODEL), lambda i: (step0 + i, 0)),
                  pl.BlockSpec((1, D_MODEL), lambda i: (0, 0)),
                  pl.BlockSpec((TT, NS), lambda i: (step0 + i, 0)),
                  pl.BlockSpec((NS, SUBLANES * NS), lambda i: (0, 0)),
                  staged, staged,
                  pl.BlockSpec(memory_space=pl.ANY)],
        out_specs=pl.BlockSpec((TT, D_MODEL), lambda i: (step0 + i, 0)),
        out_shape=jax.ShapeDtypeStruct(h2d.shape, jnp.float32),
        scratch_shapes=[pltpu.VMEM((2, 2, TD * NS * PR, 128), jnp.uint32),
                        pltpu.VMEM((TT, SUBLANES * NS), jnp.float32),
                        pltpu.VMEM((TT, SUBLANES, 128), jnp.float32),
                        pltpu.VMEM((TT, SUBLANES, 128), jnp.float32),
                        pltpu.SemaphoreType.DMA((2,))],
        input_output_aliases={2: 0},
        compiler_params=_cparams(("arbitrary",)),
        name="peer_experts",
    )(idx_d, idx_d, h2d, g, gate, spread, su, sv, uv)


def _experts(h2d, g, eidx, gate, u4, v4, seq_rows):
    T = h2d.shape[0]
    TD, TS, NS = PEER_TD, PEER_TS, PEER_SLOTS
    TT = TD + TS
    assert seq_rows % TT == 0
    steps = T // TT
    chunks = _pick_tile(steps, (24, 12, 8, 6, 4, 3, 2, 1))
    spc = steps // chunks
    e3 = eidx.reshape(steps, TT, NS)
    idx_d = e3[:, :TD].reshape(steps * TD, NS)
    idx_s = e3[:, TD:].reshape(chunks, spc * TS * NS)
    spread = np.zeros((NS, SUBLANES * NS), np.float32)
    spread[np.arange(NS), SUBLANES * np.arange(NS) + SUBLANES - 1] = 1.0
    spread = jnp.asarray(spread, jnp.bfloat16)
    uv = jnp.stack([u4, v4], axis=1)
    for c in range(chunks):
        su, sv = _sc_gather(u4, v4, idx_s[c])
        su, sv = (a.reshape(-1, 128) for a in (su, sv))
        h2d = _experts_chunk(c * spc, spc, seq_rows // TT, idx_d, h2d, g, gate, spread, su, sv, uv)
    return h2d


def _final_kernel(h_ref, g_ref, o_ref):
    o_ref[...] = _rms(h_ref[...], g_ref[...])


def _final(h3, g, b0, nb):
    _, Lp, _ = h3.shape
    S = Lp - BLK
    return pl.pallas_call(
        _final_kernel,
        grid=(nb, S // BLK),
        in_specs=[pl.BlockSpec((None, BLK, D_MODEL), lambda b, r: (b + b0, r + 1, 0)),
                  pl.BlockSpec((1, D_MODEL), lambda b, r: (0, 0))],
        out_specs=pl.BlockSpec((None, BLK, D_MODEL), lambda b, r: (b, r, 0)),
        out_shape=jax.ShapeDtypeStruct((nb, S, D_MODEL), jnp.float32),
        compiler_params=_cparams(("parallel", "parallel")),
        name="final_norm",
    )(h3, g)


def _pick_tile(T, candidates):
    for c in candidates:
        if T % c == 0:
            return c
    raise ValueError(f"no tile for {T}")


def kernel(x_prompt, x_sample, meta_tokens, g_mix, w_in, attn_sinks, lb_logits, hg_norm_g, w_att_branch,
           w_hg_branch, w_out, g_ffn, w_peer_q, peer_sub_keys, peer_u, peer_v, g_final):
    f32, bf16 = jnp.float32, jnp.bfloat16
    depth = w_in.shape[0]
    nb_p, S, _ = x_prompt.shape
    nb_s = x_sample.shape[0]
    assert x_sample.shape[1] == S and S % BLK == 0
    NB = nb_p + nb_s
    Lp = S + BLK
    T = NB * Lp

    x = jnp.concatenate([x_prompt, x_sample], axis=0)
    head = jnp.concatenate([jnp.zeros((PAD_ROWS, D_MODEL), f32), meta_tokens.astype(f32)], axis=0)
    h = jnp.concatenate([jnp.broadcast_to(head[None], (NB, BLK, D_MODEL)), x], axis=1).reshape(T, D_MODEL)

    sizes = (1024, 256, 256, 1024, 1024, 1024, 1024, 1024, 1024, 1024)
    offs = np.concatenate([[0], np.cumsum(sizes)])
    order = (0, 8, 9, 3, 4, 5, 6, 7, 1, 2)
    perm = np.concatenate([np.arange(offs[k], offs[k + 1]) for k in order])

    sm = jax.nn.softmax(lb_logits.astype(f32), axis=0)
    lb_all = jnp.maximum(jnp.cumsum(sm, axis=0) - sm[0:1], 0.0)

    tables = _rope_tables(Lp)
    consts_f = _hg_consts(False)
    consts_b = _hg_consts(True)
    tm_big = _pick_tile(T, (512, 384, 256, 128))
    tm_route = _pick_tile(T, (256, 128))

    for l in range(depth):
        w_l = w_in[l][:, perm].astype(bf16)
        proj = _inproj(h, g_mix[l].reshape(1, -1), w_l, tm_big, IN_W // 4)
        proj3 = proj.reshape(NB, Lp, IN_W)
        a_out = _attention(proj3, attn_sinks[l].astype(f32), tables)
        lb3 = lb_all[l].reshape(2, 1, -1)
        o_fwd, o_bwd = _hgrn(proj3, lb3, consts_f, consts_b)
        h = _merge(a_out.reshape(T, -1), o_fwd.reshape(T, -1), o_bwd.reshape(T, -1), hg_norm_g[l].reshape(1, -1),
                   proj, h, w_att_branch[l].astype(bf16), w_hg_branch[l].astype(bf16), w_out[l].astype(bf16),
                   tm_route)
        g_f = g_ffn[l].reshape(1, -1)
        keys = peer_sub_keys[l].reshape(2 * PEER_HEADS, PEER_NKEYS, PEER_QDIM // 2).astype(bf16)
        eidx_t, gate_t = _route(h, g_f, w_peer_q[l].astype(bf16), keys, tm_route)
        h = _experts(h, g_f, eidx_t.T, gate_t.T, _pack_bf16_pairs(peer_u[l]), _pack_bf16_pairs(peer_v[l]), Lp)

    h3 = h.reshape(NB, Lp, D_MODEL)
    g_fin = g_final.reshape(1, -1)
    return _final(h3, g_fin, 0, nb_p), _final(h3, g_fin, nb_p, nb_s)
```

```python
import functools

import numpy as np
import jax
import jax.numpy as jnp
from jax import lax
from jax.experimental import pallas as pl
from jax.experimental.pallas import tpu as pltpu
from jax.experimental.pallas import tpu_sc as plsc

D_MODEL = 1024
N_META = 16
N_Q_HEADS = 16
N_KV_HEADS = 4
HEAD_DIM = 64
WINDOW = 128
ROT_DIM = HEAD_DIM // 4
ROPE_THETA = 500000.0
HG_HEADS = 8
HG_DIM = 128
PEER_HEADS = 8
PEER_NKEYS = 128
PEER_QDIM = 256
PEER_TOPK = 16
EPS = 1e-6
NEG = -1e30

BLK = 128
PAD_ROWS = BLK - N_META
IN_W = 8704
COL_AQ, COL_GA, COL_GH, COL_HQ, COL_ZF, COL_ZB, COL_HI, COL_HGG, COL_AK, COL_AV = (
    0, 1024, 2048, 3072, 4096, 5120, 6144, 7168, 8192, 8448)

VMEM_LIMIT = 56 * 1024 * 1024
PEER_TD = 16
PEER_TS = 16
SC_WINDOW = 64
PEER_SLOTS = PEER_HEADS * PEER_TOPK
HG_HP = 8
PEER_UNROLL = 2
PEER_BATCH = 8
SUBLANES = 8


def _cparams(sem):
    return pltpu.CompilerParams(dimension_semantics=sem, vmem_limit_bytes=VMEM_LIMIT)


def _rms(x, g):
    return x * lax.rsqrt(jnp.mean(x * x, axis=-1, keepdims=True) + EPS) * g


def _nt(a, b):
    return lax.dot_general(a, b, (((1,), (1,)), ((), ())), preferred_element_type=jnp.float32)


def _tn(a, b):
    return lax.dot_general(a, b, (((0,), (0,)), ((), ())), preferred_element_type=jnp.float32)


def _inproj_kernel(x_ref, g_ref, w_ref, o_ref):
    n = _rms(x_ref[...], g_ref[...]).astype(jnp.bfloat16)
    o_ref[...] = jnp.dot(n, w_ref[...], preferred_element_type=jnp.float32)


def _inproj(h2d, g, w, tm, tn):
    T = h2d.shape[0]
    return pl.pallas_call(
        _inproj_kernel,
        grid=(IN_W // tn, T // tm),
        in_specs=[pl.BlockSpec((tm, D_MODEL), lambda j, i: (i, 0)),
                  pl.BlockSpec((1, D_MODEL), lambda j, i: (0, 0)),
                  pl.BlockSpec((D_MODEL, tn), lambda j, i: (0, j))],
        out_specs=pl.BlockSpec((tm, tn), lambda j, i: (i, j)),
        out_shape=jax.ShapeDtypeStruct((T, IN_W), jnp.float32),
        compiler_params=_cparams(("parallel", "parallel")),
        name="inproj",
    )(h2d, g, w)


def _rope_tables(Lp):
    half = ROT_DIM // 2
    pos = jnp.maximum(jnp.arange(Lp) - PAD_ROWS, 0)
    inv = jnp.power(jnp.float32(ROPE_THETA), -jnp.arange(half, dtype=jnp.float32) * 2.0 / ROT_DIM)
    ang = pos.astype(jnp.float32)[:, None] * inv[None, :]
    cos, sin = jnp.cos(ang), jnp.sin(ang)
    ones = jnp.ones((Lp, HEAD_DIM - ROT_DIM), jnp.float32)
    zeros = jnp.zeros((Lp, HEAD_DIM - ROT_DIM), jnp.float32)
    zh = jnp.zeros((Lp, half), jnp.float32)
    c = jnp.concatenate([cos, cos, ones], axis=1)
    s1 = jnp.concatenate([zh, sin, zeros], axis=1)
    s2 = jnp.concatenate([-sin, zh, zeros], axis=1)
    rep = BLK // HEAD_DIM
    return tuple(jnp.tile(t, (1, rep)) for t in (c, s1, s2))


def _attn_kernel(sink_ref, q_ref, kg_ref, k0_ref, k1_ref, k2_ref, vg_ref, v0_ref, v1_ref, v2_ref,
                 c_ref, s1_ref, s2_ref, o_ref):
    r = pl.program_id(1)
    nblk = pl.num_programs(1)
    half = ROT_DIM // 2

    def rope(x, blk):
        row0 = blk * BLK if isinstance(blk, int) else pl.multiple_of(blk * BLK, BLK)
        c = c_ref[pl.ds(row0, BLK), :]
        s1 = s1_ref[pl.ds(row0, BLK), :]
        s2 = s2_ref[pl.ds(row0, BLK), :]
        outs = []
        for j in range(x.shape[1] // BLK):
            xj = x[:, j * BLK:(j + 1) * BLK]
            outs.append(xj * c + pltpu.roll(xj, half, 1) * s1 + pltpu.roll(xj, BLK - half, 1) * s2)
        return jnp.concatenate(outs, axis=1)

    kb = [jnp.clip(r - 1 + j, 0, nblk - 1) for j in range(3)]
    q = rope(q_ref[...], r) * (HEAD_DIM ** -0.5)
    kcat = jnp.concatenate([rope(kg_ref[...], 0), rope(k0_ref[...], kb[0]), rope(k1_ref[...], kb[1]),
                            rope(k2_ref[...], kb[2])], axis=0).astype(jnp.bfloat16)
    vcat = jnp.concatenate([vg_ref[...], v0_ref[...], v1_ref[...], v2_ref[...]], axis=0).astype(jnp.bfloat16)

    qrow = lax.broadcasted_iota(jnp.int32, (BLK, BLK), 0)
    kcol = lax.broadcasted_iota(jnp.int32, (BLK, BLK), 1)
    masks = [kcol >= PAD_ROWS]
    for j in range(3):
        kblk = r - 1 + j
        ok_blk = (kblk >= 1) & (kblk <= nblk - 1)
        masks.append((jnp.abs((j - 1) * BLK + kcol - qrow) <= WINDOW) & ok_blk)
    mask = jnp.concatenate(masks, axis=1)

    G = N_Q_HEADS // N_KV_HEADS
    lane_half = kcol // HEAD_DIM
    pairs_per_slice = (BLK // HEAD_DIM) * G // 2
    outs = [None] * (N_Q_HEADS // 2)
    for m2 in range(N_KV_HEADS * HEAD_DIM // BLK):
        k2 = kcat[:, m2 * BLK:(m2 + 1) * BLK]
        v2 = vcat[:, m2 * BLK:(m2 + 1) * BLK]
        qs, sks, halves = [], [], []
        for j in range(m2 * pairs_per_slice, (m2 + 1) * pairs_per_slice):
            hk = ((2 * j) // G) % 2
            in_half = lane_half == hk
            q2 = q[:, j * BLK:(j + 1) * BLK]
            qs += [jnp.where(in_half, q2, 0.0), jnp.where(in_half, pltpu.roll(q2, HEAD_DIM, 1), 0.0)]
            sks += [jnp.full((BLK, 1), sink_ref[2 * j + hk]), jnp.full((BLK, 1), sink_ref[2 * j + 1 - hk])]
            halves.append(in_half)
        nq = len(qs)
        sk = jnp.concatenate(sks, axis=0)
        s = _nt(jnp.concatenate(qs, axis=0).astype(jnp.bfloat16), k2)
        s = jnp.where(jnp.tile(mask, (nq, 1)), s, NEG)
        m = jnp.maximum(jnp.max(s, axis=-1, keepdims=True), sk)
        p = jnp.exp(s - m)
        p = p / (jnp.sum(p, axis=-1, keepdims=True) + jnp.exp(sk - m))
        o = jnp.dot(p.astype(jnp.bfloat16), v2, preferred_element_type=jnp.float32)
        for jj, in_half in enumerate(halves):
            o_same = o[(2 * jj) * BLK:(2 * jj + 1) * BLK]
            o_other = o[(2 * jj + 1) * BLK:(2 * jj + 2) * BLK]
            outs[m2 * pairs_per_slice + jj] = jnp.where(in_half, o_same, pltpu.roll(o_other, HEAD_DIM, 1))
    o_ref[...] = jnp.concatenate(outs, axis=1).astype(o_ref.dtype)


def _attention(proj3, sinks, tables):
    NB, Lp, _ = proj3.shape
    nblk = Lp // BLK
    kcol, vcol = COL_AK // 256, COL_AV // 256

    def loc(j, col):
        return pl.BlockSpec((None, BLK, 256),
                            lambda b, r: (b, jnp.clip(r - 1 + j, 0, nblk - 1), col))

    glob = lambda col: pl.BlockSpec((None, BLK, 256), lambda b, r: (b, 0, col))
    tab = pl.BlockSpec((Lp, BLK), lambda b, r: (0, 0))
    return pl.pallas_call(
        _attn_kernel,
        grid=(NB, nblk),
        in_specs=[pl.BlockSpec(memory_space=pltpu.SMEM),
                  pl.BlockSpec((None, BLK, 1024), lambda b, r: (b, r, 0)),
                  glob(kcol), loc(0, kcol), loc(1, kcol), loc(2, kcol),
                  glob(vcol), loc(0, vcol), loc(1, vcol), loc(2, vcol),
                  tab, tab, tab],
        out_specs=pl.BlockSpec((None, BLK, 1024), lambda b, r: (b, r, 0)),
        out_shape=jax.ShapeDtypeStruct((NB, Lp, 1024), jnp.bfloat16),
        compiler_params=_cparams(("parallel", "arbitrary")),
        name="window_attn",
    )(sinks, proj3, proj3, proj3, proj3, proj3, proj3, proj3, proj3, proj3, *tables)


HG_ROWS = BLK
_HG_LEVELS = (64, 32, 16, 8, 4, 2, 1)


def _hg_consts(reverse):
    C = HG_ROWS
    t = np.arange(C)[:, None]
    u = np.arange(C)[None, :]
    mats = [u <= t, u > t]
    ups, msks = [], []
    for B in _HG_LEVELS:
        same = (t // (2 * B)) == (u // (2 * B))
        tin, uin = t % (2 * B), u % (2 * B)
        upper = tin >= B
        wq = upper & same & (uin >= B) & (uin <= tin)
        wk = (~upper) & same & (uin > tin) & (uin <= B - 1)
        mats.append(wq | wk)
        ups.append(np.broadcast_to(upper, (C, HG_DIM)))
        msks.append(same & upper & (uin < B))
    msks.append(np.eye(C, dtype=bool))
    if reverse:
        mats = [m[::-1, ::-1] for m in mats]
        ups = [m[::-1] for m in ups]
        msks = [m[::-1, ::-1] for m in msks]
    W = np.concatenate(mats, axis=0).astype(np.float32)
    W2 = np.concatenate([W, W], axis=1)
    return (jnp.asarray(W2, jnp.bfloat16), jnp.asarray(np.stack(ups).astype(np.float32)),
            jnp.asarray(np.stack(msks).astype(np.float32)))


def _hg_chunk(q_raw, z, v, valid, lb, w2, up_ref, msk_ref, st, reverse):
    C = HG_ROWS
    f32, bf16 = jnp.float32, jnp.bfloat16
    log_lb = jnp.log(lb)
    log1m = jnp.log1p(-lb)
    logsig = jnp.minimum(z, 0.0) - jnp.log1p(jnp.exp(-jnp.abs(z)))
    a, b = log_lb, log1m + logsig
    g = jnp.maximum(a, b) + jnp.log1p(jnp.exp(-jnp.abs(a - b)))
    k = (1.0 - lb) * jax.nn.sigmoid(-z)
    q = q_raw * jax.nn.sigmoid(q_raw)
    q = jnp.where(valid, q, 0.0)
    k = jnp.where(valid, k, 0.0)
    v = jnp.where(valid, v, 0.0)
    g = jnp.where(valid, g, 0.0)
    g1 = g.astype(bf16)
    g2 = (g - g1.astype(f32)).astype(bf16)
    E = jnp.dot(w2, jnp.concatenate([g1, g2], axis=0), preferred_element_type=f32)
    b_inc = E[0:C]
    b_sfx = E[C:2 * C]
    b_tot = b_inc[0:1] if reverse else b_inc[C - 1:C]

    A = msk_ref[len(_HG_LEVELS)] * jnp.sum(q * k, axis=-1, keepdims=True)
    for l in range(len(_HG_LEVELS)):
        ex = jnp.exp(E[(2 + l) * C:(3 + l) * C])
        up = up_ref[l] > 0.5
        qs = jnp.where(up, q * ex, 0.0).astype(bf16)
        ks = jnp.where(up, 0.0, k * ex).astype(bf16)
        A = A + _nt(qs, ks) * msk_ref[l]

    vb = v.astype(bf16)
    o = _nt((q * jnp.exp(b_inc)).astype(bf16), st.astype(bf16))
    o = o + jnp.dot(A.astype(bf16), vb, preferred_element_type=f32)
    st = st * jnp.exp(b_tot) + _tn(vb, (k * jnp.exp(b_sfx)).astype(bf16))
    return o, st


def _hg_kernel(qf_ref, zf_ref, vf_ref, qb_ref, zb_ref, vb_ref, lb_ref, w2f_ref, upf_ref, mskf_ref,
               w2b_ref, upb_ref, mskb_ref, of_ref, ob_ref, st_ref):
    i = pl.program_id(2)
    nblk = pl.num_programs(2)

    @pl.when(i == 0)
    def _():
        st_ref[...] = jnp.zeros_like(st_ref)

    rows = lax.broadcasted_iota(jnp.int32, (HG_ROWS, HG_DIM), 0)
    dirs = ((False, qf_ref, zf_ref, vf_ref, w2f_ref, upf_ref, mskf_ref, of_ref, i),
            (True, qb_ref, zb_ref, vb_ref, w2b_ref, upb_ref, mskb_ref, ob_ref, nblk - 1 - i))
    for d, (reverse, q_ref, z_ref, v_ref, w2_ref, up_ref, msk_ref, o_ref, blk) in enumerate(dirs):
        w2 = w2_ref[...]
        for hh in range(HG_HP):
            cols = slice(hh * HG_DIM, (hh + 1) * HG_DIM)
            lb = lb_ref[d, :, cols]
            valid = (rows >= PAD_ROWS) | (blk > 0)
            o, st = _hg_chunk(q_ref[:, cols], z_ref[:, cols], v_ref[:, cols], valid, lb, w2,
                              up_ref, msk_ref, st_ref[d, hh], reverse)
            st_ref[d, hh] = st
            o_ref[:, cols] = o


def _hgrn(proj3, lb3, consts_f, consts_b):
    NB, Lp, _ = proj3.shape
    nblk = Lp // BLK
    W = HG_DIM * HG_HP

    def col(base, reverse):
        return pl.BlockSpec((None, BLK, W),
                            lambda b, h, i: (b, (nblk - 1 - i) if reverse else i, base // W + h))

    def const(c):
        return pl.BlockSpec(c.shape, lambda b, h, i: (0,) * c.ndim)

    in_specs = [col(COL_HQ, False), col(COL_ZF, False), col(COL_HI, False),
                col(COL_HQ, True), col(COL_ZB, True), col(COL_HI, True),
                pl.BlockSpec((2, 1, W), lambda b, h, i: (0, 0, h))]
    in_specs += [const(c) for c in consts_f] + [const(c) for c in consts_b]
    oshape = jax.ShapeDtypeStruct((NB, Lp, HG_HEADS * HG_DIM), jnp.float32)
    return pl.pallas_call(
        _hg_kernel,
        grid=(NB, HG_HEADS // HG_HP, nblk),
        in_specs=in_specs,
        out_specs=(pl.BlockSpec((None, BLK, W), lambda b, h, i: (b, i, h)),
                   pl.BlockSpec((None, BLK, W), lambda b, h, i: (b, nblk - 1 - i, h))),
        out_shape=(oshape, oshape),
        scratch_shapes=[pltpu.VMEM((2, HG_HP, HG_DIM, HG_DIM), jnp.float32)],
        compiler_params=_cparams(("parallel", "parallel", "arbitrary")),
        name="hgrn_bidir",
    )(proj3, proj3, proj3, proj3, proj3, proj3, lb3, *consts_f, *consts_b)


def _merge_kernel(a_ref, of_ref, ob_ref, hgg_ref, ng_ref, ga_ref, gh_ref, h_ref, wa_ref, wh_ref, wo_ref, o_ref):
    f32, bf16 = jnp.float32, jnp.bfloat16
    o = of_ref[...] + ob_ref[...]
    ng = ng_ref[...]
    gt = hgg_ref[...]
    hh = jnp.concatenate([_rms(o[:, c:c + HG_DIM], ng[:, c:c + HG_DIM]) for c in range(0, HG_HEADS * HG_DIM, HG_DIM)],
                         axis=1) * (gt * jax.nn.sigmoid(gt))
    pa = jnp.dot(a_ref[...], wa_ref[...], preferred_element_type=f32)
    ph = jnp.dot(hh.astype(bf16), wh_ref[...], preferred_element_type=f32)
    merged = jax.nn.sigmoid(ga_ref[...]) * pa + jax.nn.sigmoid(gh_ref[...]) * ph
    o_ref[...] = h_ref[...] + jnp.dot(merged.astype(bf16), wo_ref[...], preferred_element_type=f32)


def _merge(a2d, of2d, ob2d, ng, proj2d, h2d, wa, wh, wo, tm):
    T = h2d.shape[0]
    row = lambda c: pl.BlockSpec((tm, D_MODEL), lambda i: (i, c))
    wspec = pl.BlockSpec((D_MODEL, D_MODEL), lambda i: (0, 0))
    return pl.pallas_call(
        _merge_kernel,
        grid=(T // tm,),
        in_specs=[row(0), row(0), row(0), row(COL_HGG // D_MODEL), pl.BlockSpec((1, D_MODEL), lambda i: (0, 0)),
                  row(COL_GA // D_MODEL), row(COL_GH // D_MODEL), row(0), wspec, wspec, wspec],
        out_specs=row(0),
        out_shape=jax.ShapeDtypeStruct((T, D_MODEL), jnp.float32),
        input_output_aliases={7: 0},
        compiler_params=_cparams(("parallel",)),
        name="merge_out",
    )(a2d, of2d, ob2d, proj2d, ng, proj2d, proj2d, h2d, wa, wh, wo)


def _topk_axis0(s, kk):
    n = s.shape[0]
    iota = lax.broadcasted_iota(jnp.int32, s.shape, 0)
    vals, ids = [], []
    for _ in range(kk):
        m = jnp.max(s, axis=0, keepdims=True)
        am = jnp.min(jnp.where(s == m, iota, n), axis=0, keepdims=True)
        s = jnp.where(iota == am, -jnp.inf, s)
        vals.append(m)
        ids.append(am)
    return vals, ids


def _stack_rows(rows, dtype):
    n = len(rows)
    iota = lax.broadcasted_iota(jnp.int32, (n, rows[0].shape[1]), 0)
    out = jnp.zeros((n, rows[0].shape[1]), dtype)
    for j, rj in enumerate(rows):
        out = jnp.where(iota == j, rj, out)
    return out


def _route_kernel(h_ref, g_ref, wq_ref, key_ref, eidx_ref, gate_ref, xn_ref):
    @pl.when(pl.program_id(1) == 0)
    def _():
        xn_ref[...] = _rms(h_ref[...], g_ref[...]).astype(jnp.bfloat16)

    K = PEER_TOPK
    half = PEER_QDIM // 2
    i32 = jnp.int32
    qry = jnp.dot(xn_ref[...], wq_ref[...], preferred_element_type=jnp.float32).astype(jnp.bfloat16)
    s0 = _nt(key_ref[0], qry[:, :half])
    s1 = _nt(key_ref[1], qry[:, half:])
    tm = s0.shape[1]
    v0, i0 = _topk_axis0(s0, K)
    v1, i1 = _topk_axis0(s1, K)
    m1 = _stack_rows(v1, jnp.float32)
    m0_hi = _stack_rows(v0[K // 2:], jnp.float32)
    r8 = lax.broadcasted_iota(i32, (K // 2, tm), 0)
    cand = [v0[0] + m1] + [v0[a] + m1[:K // 2] for a in range(1, K // 2)] + [m0_hi + v1[0]]
    flat = [lax.broadcasted_iota(i32, (K, tm), 0)] + [a * K + r8 for a in range(1, K // 2)] + [(K // 2 + r8) * K]
    cand = jnp.concatenate(cand, axis=0)
    flat = jnp.concatenate(flat, axis=0)
    big = K * K
    ts, tp = [], []
    for _ in range(K):
        m = jnp.max(cand, axis=0, keepdims=True)
        p = jnp.min(jnp.where(cand == m, flat, big), axis=0, keepdims=True)
        cand = jnp.where(flat == p, -jnp.inf, cand)
        ts.append(m)
        tp.append(p)
    pk = _stack_rows(tp, i32)
    pa, pb = pk // K, pk % K
    e0 = jnp.zeros((K, tm), i32)
    e1 = jnp.zeros((K, tm), i32)
    for a in range(K):
        e0 = jnp.where(pa == a, i0[a], e0)
        e1 = jnp.where(pb == a, i1[a], e1)
    ex = [jnp.exp(t - ts[0]) for t in ts]
    den = ex[0]
    for e in ex[1:]:
        den = den + e
    eidx_ref[...] = e0 * PEER_NKEYS + e1
    gate_ref[...] = _stack_rows([e / den for e in ex], jnp.float32)


def _route(h2d, g, wq, keys, tm):
    T = h2d.shape[0]
    out = jax.ShapeDtypeStruct((PEER_SLOTS, T), jnp.int32), jax.ShapeDtypeStruct((PEER_SLOTS, T), jnp.float32)
    ospec = pl.BlockSpec((PEER_TOPK, tm), lambda i, h: (h, i))
    return pl.pallas_call(
        _route_kernel,
        grid=(T // tm, PEER_HEADS),
        in_specs=[pl.BlockSpec((tm, D_MODEL), lambda i, h: (i, 0)),
                  pl.BlockSpec((1, D_MODEL), lambda i, h: (0, 0)),
                  pl.BlockSpec((D_MODEL, PEER_QDIM), lambda i, h: (0, h)),
                  pl.BlockSpec((2, PEER_NKEYS, PEER_QDIM // 2), lambda i, h: (h, 0, 0))],
        out_specs=(ospec, ospec),
        out_shape=out,
        scratch_shapes=[pltpu.VMEM((tm, D_MODEL), jnp.bfloat16)],
        compiler_params=_cparams(("parallel", "arbitrary")),
        name="peer_route",
    )(h2d, g, wq, keys)


def _pack_bf16_pairs(x):
    e, d = x.shape
    bits = lax.bitcast_convert_type(x.astype(jnp.bfloat16), jnp.uint16).astype(jnp.uint32)
    bits = bits.reshape(e, d // 256, 2, 128)
    return bits[:, :, 0, :] | (bits[:, :, 1, :] << 16)


def _sc_gather(uv, idx):
    M = idx.shape[0]
    info = plsc.get_sparse_core_info()
    nw = info.num_cores * info.num_subcores
    per = M // nw
    assert per * nw == M and per % SC_WINDOW == 0
    mesh = plsc.VectorSubcoreMesh(core_axis_name="c", subcore_axis_name="s")
    oshape = jax.ShapeDtypeStruct((M,) + uv.shape[2:], uv.dtype)

    @functools.partial(
        pl.kernel, mesh=mesh, out_type=(oshape, oshape),
        scratch_types=[pltpu.VMEM((SC_WINDOW,), jnp.int32), pltpu.VMEM((SC_WINDOW,) + uv.shape[1:], uv.dtype),
                       pltpu.SemaphoreType.DMA],
    )
    def gather(uv_hbm, idx_hbm, ou_hbm, ov_hbm, idx_v, rows_v, sem):
        wid = lax.axis_index("s") * info.num_cores + lax.axis_index("c")
        base = wid * per

        @pl.loop(0, per // SC_WINDOW)
        def _(g):
            off = pl.multiple_of(base + g * SC_WINDOW, SC_WINDOW)
            pltpu.sync_copy(idx_hbm.at[pl.ds(off, SC_WINDOW)], idx_v)
            pltpu.async_copy(uv_hbm.at[idx_v], rows_v, sem).wait()
            pltpu.sync_copy(rows_v.at[:, 0], ou_hbm.at[pl.ds(off, SC_WINDOW)])
            pltpu.sync_copy(rows_v.at[:, 1], ov_hbm.at[pl.ds(off, SC_WINDOW)])

    return gather(uv, idx)


def _expert_kernel(idx_ref, idxn_ref, h_ref, g_ref, gate_ref, spread_ref, su_ref, sv_ref, uv_hbm, o_ref, buf, z_ref,
                   xn_ref, y_ref, sem, *, step0, seq_steps, pad_steps):
    TD, TS, NS = PEER_TD, PEER_TS, PEER_SLOTS
    TT, LW = TD + TS, SUBLANES * PEER_SLOTS
    PR = SUBLANES // 2
    f32, bf16 = jnp.float32, jnp.bfloat16
    i = pl.program_id(0)
    n = pl.num_programs(0)
    slot = lax.rem(i, 2)
    live = lax.rem(step0 + i, seq_steps) >= pad_steps
    live_next = (i + 1 < n) & (lax.rem(step0 + i + 1, seq_steps) >= pad_steps)
    dead = jnp.logical_not(live)

    def issue_token(iref, sl, t):
        for k in range(NS):
            pltpu.make_async_copy(uv_hbm.at[iref[t, k]], buf.at[sl, :, pl.ds((t * NS + k) * PR, PR)],
                                  sem.at[sl]).start(priority=k % 2)

    def issue_all(iref, sl):
        def body(t, c):
            issue_token(iref, sl, t)
            return c
        lax.fori_loop(0, TD, body, 0)

    @pl.when((i == 0) & live)
    def _():
        issue_all(idx_ref, 0)

    @pl.when(live)
    def _():
        pltpu.make_async_copy(buf.at[slot], buf.at[slot], sem.at[slot]).wait()

    @pl.when(dead & live_next)
    def _():
        issue_all(idxn_ref, 1 - slot)

    @pl.when(dead)
    def _():
        o_ref[...] = h_ref[...]

    @pl.when(live)
    def _():
        _expert_step(idxn_ref, h_ref, g_ref, gate_ref, spread_ref, su_ref, sv_ref, o_ref, buf, z_ref, xn_ref, y_ref,
                     slot, live_next, issue_token)


def _expert_step(idxn_ref, h_ref, g_ref, gate_ref, spread_ref, su_ref, sv_ref, o_ref, buf, z_ref, xn_ref, y_ref,
                 slot, live_next, issue_token):
    TD, TS, NS = PEER_TD, PEER_TS, PEER_SLOTS
    TT, LW = TD + TS, SUBLANES * PEER_SLOTS
    PR = SUBLANES // 2
    f32, bf16 = jnp.float32, jnp.bfloat16

    x = h_ref[...]
    xn_ref[...] = _rms(x, g_ref[...]).reshape(TT, SUBLANES, 128)
    on_diag = (lax.broadcasted_iota(jnp.int32, (SUBLANES, LW), 1) % SUBLANES
               == lax.broadcasted_iota(jnp.int32, (SUBLANES, LW), 0))

    def pieces_of(t, staged, part):
        r = pl.ds(pl.multiple_of(t * (NS * PR), NS * PR), NS * PR)
        words = (sv_ref if part else su_ref)[r, :] if staged else buf[slot, part, r, :]
        return pltpu.bitcast(words, bf16)

    def score_token(t, staged):
        tg = t + TD if staged else t
        z = _nt(xn_ref[tg].astype(bf16), pieces_of(t, staged, 0))
        z_ref[pl.ds(tg, 1), :] = jnp.sum(jnp.where(on_diag, z, 0.0), axis=0, keepdims=True)

    def mix_token(t, staged):
        tg = t + TD if staged else t
        w = jnp.where(on_diag, jnp.broadcast_to(z_ref[pl.ds(tg, 1), :], (SUBLANES, LW)), 0.0).astype(bf16)
        y_ref[tg] = jnp.dot(w, pieces_of(t, staged, 1), preferred_element_type=f32)

    def batched(count, fn):
        def body(tt, c):
            for j in range(PEER_BATCH):
                fn(tt * PEER_BATCH + j)
            return c
        lax.fori_loop(0, count // PEER_BATCH, body, 0)

    @pl.when(live_next)
    def _():
        def body(tt, c):
            for j in range(PEER_UNROLL):
                score_token(tt * PEER_UNROLL + j, False)
            for j in range(PEER_UNROLL):
                issue_token(idxn_ref, 1 - slot, tt * PEER_UNROLL + j)
            return c
        lax.fori_loop(0, TD // PEER_UNROLL, body, 0)

    @pl.when(jnp.logical_not(live_next))
    def _():
        batched(TD, lambda t: score_token(t, False))

    batched(TS, lambda t: score_token(t, True))

    z = z_ref[...]
    z = z + pltpu.roll(z, 4, 1)
    z = z + pltpu.roll(z, 2, 1)
    z = z + pltpu.roll(z, 1, 1)
    gt = gate_ref[...]
    g1 = gt.astype(bf16)
    r1 = gt - g1.astype(f32)
    g2 = r1.astype(bf16)
    g3 = (r1 - g2.astype(f32)).astype(bf16)
    spread = spread_ref[...]
    gate8 = (jnp.dot(g1, spread, preferred_element_type=f32) + jnp.dot(g2, spread, preferred_element_type=f32)
             + jnp.dot(g3, spread, preferred_element_type=f32))
    w = gate8 * (0.5 * z * (1.0 + lax.erf(z * (2.0 ** -0.5))))
    w = w + pltpu.roll(w, LW - 1, 1)
    w = w + pltpu.roll(w, LW - 2, 1)
    z_ref[...] = w + pltpu.roll(w, LW - 4, 1)

    batched(TD, lambda t: mix_token(t, False))
    batched(TS, lambda t: mix_token(t, True))
    o_ref[...] = x + y_ref[...].reshape(TT, D_MODEL)


def _experts_chunk(step0, nsteps, seq_steps, idx_d, h2d, g, gate, spread, su, sv, uv):
    TD, TS, NS = PEER_TD, PEER_TS, PEER_SLOTS
    TT, PR = TD + TS, SUBLANES // 2
    last = step0 + nsteps - 1
    staged = pl.BlockSpec((TS * NS * PR, 128), lambda i: (i, 0))
    return pl.pallas_call(
        functools.partial(_expert_kernel, step0=step0, seq_steps=seq_steps, pad_steps=PAD_ROWS // TT),
        grid=(nsteps,),
        in_specs=[pl.BlockSpec((TD, NS), lambda i: (step0 + i, 0), memory_space=pltpu.SMEM),
                  pl.BlockSpec((TD, NS), lambda i: (jnp.minimum(step0 + i + 1, last), 0), memory_space=pltpu.SMEM),
                  pl.BlockSpec((TT, D_MODEL), lambda i: (step0 + i, 0)),
                  pl.BlockSpec((1, D_MODEL), lambda i: (0, 0)),
                  pl.BlockSpec((TT, NS), lambda i: (step0 + i, 0)),
                  pl.BlockSpec((NS, SUBLANES * NS), lambda i: (0, 0)),
                  staged, staged,
                  pl.BlockSpec(memory_space=pl.ANY)],
        out_specs=pl.BlockSpec((TT, D_MODEL), lambda i: (step0 + i, 0)),
        out_shape=jax.ShapeDtypeStruct(h2d.shape, jnp.float32),
        scratch_shapes=[pltpu.VMEM((2, 2, TD * NS * PR, 128), jnp.uint32),
                        pltpu.VMEM((TT, SUBLANES * NS), jnp.float32),
                        pltpu.VMEM((TT, SUBLANES, 128), jnp.float32),
                        pltpu.VMEM((TT, SUBLANES, 128), jnp.float32),
                        pltpu.SemaphoreType.DMA((2,))],
        input_output_aliases={2: 0},
        compiler_params=_cparams(("arbitrary",)),
        name="peer_experts",
    )(idx_d, idx_d, h2d, g, gate, spread, su, sv, uv)


def _experts(h2d, g, eidx, gate, u4, v4, seq_rows):
    T = h2d.shape[0]
    TD, TS, NS = PEER_TD, PEER_TS, PEER_SLOTS
    TT = TD + TS
    assert seq_rows % TT == 0
    steps = T // TT
    chunks = _pick_tile(steps, (24, 12, 8, 6, 4, 3, 2, 1))
    spc = steps // chunks
    e3 = eidx.reshape(steps, TT, NS)
    idx_d = e3[:, :TD].reshape(steps * TD, NS)
    idx_s = e3[:, TD:].reshape(chunks, spc * TS * NS)
    spread = np.zeros((NS, SUBLANES * NS), np.float32)
    spread[np.arange(NS), SUBLANES * np.arange(NS) + SUBLANES - 1] = 1.0
    spread = jnp.asarray(spread, jnp.bfloat16)
    uv = jnp.stack([u4, v4], axis=1)
    for c in range(chunks):
        su, sv = _sc_gather(uv, idx_s[c])
        su, sv = (a.reshape(-1, 128) for a in (su, sv))
        h2d = _experts_chunk(c * spc, spc, seq_rows // TT, idx_d, h2d, g, gate, spread, su, sv, uv)
    return h2d


def _final_kernel(h_ref, g_ref, o_ref):
    o_ref[...] = _rms(h_ref[...], g_ref[...])


def _final(h3, g, b0, nb):
    _, Lp, _ = h3.shape
    S = Lp - BLK
    return pl.pallas_call(
        _final_kernel,
        grid=(nb, S // BLK),
        in_specs=[pl.BlockSpec((None, BLK, D_MODEL), lambda b, r: (b + b0, r + 1, 0)),
                  pl.BlockSpec((1, D_MODEL), lambda b, r: (0, 0))],
        out_specs=pl.BlockSpec((None, BLK, D_MODEL), lambda b, r: (b, r, 0)),
        out_shape=jax.ShapeDtypeStruct((nb, S, D_MODEL), jnp.float32),
        compiler_params=_cparams(("parallel", "parallel")),
        name="final_norm",
    )(h3, g)


def _pick_tile(T, candidates):
    for c in candidates:
        if T % c == 0:
            return c
    raise ValueError(f"no tile for {T}")


def kernel(x_prompt, x_sample, meta_tokens, g_mix, w_in, attn_sinks, lb_logits, hg_norm_g, w_att_branch,
           w_hg_branch, w_out, g_ffn, w_peer_q, peer_sub_keys, peer_u, peer_v, g_final):
    f32, bf16 = jnp.float32, jnp.bfloat16
    depth = w_in.shape[0]
    nb_p, S, _ = x_prompt.shape
    nb_s = x_sample.shape[0]
    assert x_sample.shape[1] == S and S % BLK == 0
    NB = nb_p + nb_s
    Lp = S + BLK
    T = NB * Lp

    x = jnp.concatenate([x_prompt, x_sample], axis=0)
    head = jnp.concatenate([jnp.zeros((PAD_ROWS, D_MODEL), f32), meta_tokens.astype(f32)], axis=0)
    h = jnp.concatenate([jnp.broadcast_to(head[None], (NB, BLK, D_MODEL)), x], axis=1).reshape(T, D_MODEL)

    sizes = (1024, 256, 256, 1024, 1024, 1024, 1024, 1024, 1024, 1024)
    offs = np.concatenate([[0], np.cumsum(sizes)])
    order = (0, 8, 9, 3, 4, 5, 6, 7, 1, 2)
    perm = np.concatenate([np.arange(offs[k], offs[k + 1]) for k in order])

    sm = jax.nn.softmax(lb_logits.astype(f32), axis=0)
    lb_all = jnp.maximum(jnp.cumsum(sm, axis=0) - sm[0:1], 0.0)

    tables = _rope_tables(Lp)
    consts_f = _hg_consts(False)
    consts_b = _hg_consts(True)
    tm_big = _pick_tile(T, (512, 384, 256, 128))
    tm_route = _pick_tile(T, (256, 128))

    for l in range(depth):
        w_l = w_in[l][:, perm].astype(bf16)
        proj = _inproj(h, g_mix[l].reshape(1, -1), w_l, tm_big, IN_W // 4)
        proj3 = proj.reshape(NB, Lp, IN_W)
        a_out = _attention(proj3, attn_sinks[l].astype(f32), tables)
        lb3 = lb_all[l].reshape(2, 1, -1)
        o_fwd, o_bwd = _hgrn(proj3, lb3, consts_f, consts_b)
        h = _merge(a_out.reshape(T, -1), o_fwd.reshape(T, -1), o_bwd.reshape(T, -1), hg_norm_g[l].reshape(1, -1),
                   proj, h, w_att_branch[l].astype(bf16), w_hg_branch[l].astype(bf16), w_out[l].astype(bf16),
                   tm_route)
        g_f = g_ffn[l].reshape(1, -1)
        keys = peer_sub_keys[l].reshape(2 * PEER_HEADS, PEER_NKEYS, PEER_QDIM // 2).astype(bf16)
        eidx_t, gate_t = _route(h, g_f, w_peer_q[l].astype(bf16), keys, tm_route)
        h = _experts(h, g_f, eidx_t.T, gate_t.T, _pack_bf16_pairs(peer_u[l]), _pack_bf16_pairs(peer_v[l]), Lp)

    h3 = h.reshape(NB, Lp, D_MODEL)
    g_fin = g_final.reshape(1, -1)
    return _final(h3, g_fin, 0, nb_p), _final(h3, g_fin, nb_p, nb_s)
```

```python
import functools

import numpy as np
import jax
import jax.numpy as jnp
from jax import lax
from jax.experimental import pallas as pl
from jax.experimental.pallas import tpu as pltpu
from jax.experimental.pallas import tpu_sc as plsc

D_MODEL = 1024
N_META = 16
N_Q_HEADS = 16
N_KV_HEADS = 4
HEAD_DIM = 64
WINDOW = 128
ROT_DIM = HEAD_DIM // 4
ROPE_THETA = 500000.0
HG_HEADS = 8
HG_DIM = 128
PEER_HEADS = 8
PEER_NKEYS = 128
PEER_QDIM = 256
PEER_TOPK = 16
EPS = 1e-6
NEG = -1e30

BLK = 128
PAD_ROWS = BLK - N_META
IN_W = 8704
COL_AQ, COL_GA, COL_GH, COL_HQ, COL_ZF, COL_ZB, COL_HI, COL_HGG, COL_AK, COL_AV = (
    0, 1024, 2048, 3072, 4096, 5120, 6144, 7168, 8192, 8448)

VMEM_LIMIT = 56 * 1024 * 1024
PEER_TD = 20
PEER_TS = 12
SC_WINDOW = 32
PEER_SLOTS = PEER_HEADS * PEER_TOPK
HG_HP = 8
PEER_UNROLL = 2
PEER_BATCH = 4
SUBLANES = 8


def _cparams(sem):
    return pltpu.CompilerParams(dimension_semantics=sem, vmem_limit_bytes=VMEM_LIMIT)


def _rms(x, g):
    return x * lax.rsqrt(jnp.mean(x * x, axis=-1, keepdims=True) + EPS) * g


def _nt(a, b):
    return lax.dot_general(a, b, (((1,), (1,)), ((), ())), preferred_element_type=jnp.float32)


def _tn(a, b):
    return lax.dot_general(a, b, (((0,), (0,)), ((), ())), preferred_element_type=jnp.float32)


def _inproj_kernel(x_ref, g_ref, w_ref, o_ref):
    n = _rms(x_ref[...], g_ref[...]).astype(jnp.bfloat16)
    o_ref[...] = jnp.dot(n, w_ref[...], preferred_element_type=jnp.float32)


def _inproj(h2d, g, w, tm, tn):
    T = h2d.shape[0]
    return pl.pallas_call(
        _inproj_kernel,
        grid=(IN_W // tn, T // tm),
        in_specs=[pl.BlockSpec((tm, D_MODEL), lambda j, i: (i, 0)),
                  pl.BlockSpec((1, D_MODEL), lambda j, i: (0, 0)),
                  pl.BlockSpec((D_MODEL, tn), lambda j, i: (0, j))],
        out_specs=pl.BlockSpec((tm, tn), lambda j, i: (i, j)),
        out_shape=jax.ShapeDtypeStruct((T, IN_W), jnp.float32),
        compiler_params=_cparams(("parallel", "parallel")),
        name="inproj",
    )(h2d, g, w)


def _rope_tables(Lp):
    half = ROT_DIM // 2
    pos = jnp.maximum(jnp.arange(Lp) - PAD_ROWS, 0)
    inv = jnp.power(jnp.float32(ROPE_THETA), -jnp.arange(half, dtype=jnp.float32) * 2.0 / ROT_DIM)
    ang = pos.astype(jnp.float32)[:, None] * inv[None, :]
    cos, sin = jnp.cos(ang), jnp.sin(ang)
    ones = jnp.ones((Lp, HEAD_DIM - ROT_DIM), jnp.float32)
    zeros = jnp.zeros((Lp, HEAD_DIM - ROT_DIM), jnp.float32)
    zh = jnp.zeros((Lp, half), jnp.float32)
    c = jnp.concatenate([cos, cos, ones], axis=1)
    s1 = jnp.concatenate([zh, sin, zeros], axis=1)
    s2 = jnp.concatenate([-sin, zh, zeros], axis=1)
    rep = BLK // HEAD_DIM
    return tuple(jnp.tile(t, (1, rep)) for t in (c, s1, s2))


def _attn_kernel(sink_ref, q_ref, kg_ref, k0_ref, k1_ref, k2_ref, vg_ref, v0_ref, v1_ref, v2_ref,
                 c_ref, s1_ref, s2_ref, o_ref):
    r = pl.program_id(1)
    nblk = pl.num_programs(1)
    half = ROT_DIM // 2

    def rope(x, blk):
        row0 = blk * BLK if isinstance(blk, int) else pl.multiple_of(blk * BLK, BLK)
        c = c_ref[pl.ds(row0, BLK), :]
        s1 = s1_ref[pl.ds(row0, BLK), :]
        s2 = s2_ref[pl.ds(row0, BLK), :]
        outs = []
        for j in range(x.shape[1] // BLK):
            xj = x[:, j * BLK:(j + 1) * BLK]
            outs.append(xj * c + pltpu.roll(xj, half, 1) * s1 + pltpu.roll(xj, BLK - half, 1) * s2)
        return jnp.concatenate(outs, axis=1)

    kb = [jnp.clip(r - 1 + j, 0, nblk - 1) for j in range(3)]
    q = rope(q_ref[...], r) * (HEAD_DIM ** -0.5)
    kcat = jnp.concatenate([rope(kg_ref[...], 0), rope(k0_ref[...], kb[0]), rope(k1_ref[...], kb[1]),
                            rope(k2_ref[...], kb[2])], axis=0).astype(jnp.bfloat16)
    vcat = jnp.concatenate([vg_ref[...], v0_ref[...], v1_ref[...], v2_ref[...]], axis=0).astype(jnp.bfloat16)

    qrow = lax.broadcasted_iota(jnp.int32, (BLK, BLK), 0)
    kcol = lax.broadcasted_iota(jnp.int32, (BLK, BLK), 1)
    masks = [kcol >= PAD_ROWS]
    for j in range(3):
        kblk = r - 1 + j
        ok_blk = (kblk >= 1) & (kblk <= nblk - 1)
        masks.append((jnp.abs((j - 1) * BLK + kcol - qrow) <= WINDOW) & ok_blk)
    mask = jnp.concatenate(masks, axis=1)

    G = N_Q_HEADS // N_KV_HEADS
    lane_half = kcol // HEAD_DIM
    pairs_per_slice = (BLK // HEAD_DIM) * G // 2
    outs = [None] * (N_Q_HEADS // 2)
    for m2 in range(N_KV_HEADS * HEAD_DIM // BLK):
        k2 = kcat[:, m2 * BLK:(m2 + 1) * BLK]
        v2 = vcat[:, m2 * BLK:(m2 + 1) * BLK]
        qs, sks, halves = [], [], []
        for j in range(m2 * pairs_per_slice, (m2 + 1) * pairs_per_slice):
            hk = ((2 * j) // G) % 2
            in_half = lane_half == hk
            q2 = q[:, j * BLK:(j + 1) * BLK]
            qs += [jnp.where(in_half, q2, 0.0), jnp.where(in_half, pltpu.roll(q2, HEAD_DIM, 1), 0.0)]
            sks += [jnp.full((BLK, 1), sink_ref[2 * j + hk]), jnp.full((BLK, 1), sink_ref[2 * j + 1 - hk])]
            halves.append(in_half)
        nq = len(qs)
        sk = jnp.concatenate(sks, axis=0)
        s = _nt(jnp.concatenate(qs, axis=0).astype(jnp.bfloat16), k2)
        s = jnp.where(jnp.tile(mask, (nq, 1)), s, NEG)
        m = jnp.maximum(jnp.max(s, axis=-1, keepdims=True), sk)
        p = jnp.exp(s - m)
        p = p / (jnp.sum(p, axis=-1, keepdims=True) + jnp.exp(sk - m))
        o = jnp.dot(p.astype(jnp.bfloat16), v2, preferred_element_type=jnp.float32)
        for jj, in_half in enumerate(halves):
            o_same = o[(2 * jj) * BLK:(2 * jj + 1) * BLK]
            o_other = o[(2 * jj + 1) * BLK:(2 * jj + 2) * BLK]
            outs[m2 * pairs_per_slice + jj] = jnp.where(in_half, o_same, pltpu.roll(o_other, HEAD_DIM, 1))
    o_ref[...] = jnp.concatenate(outs, axis=1).astype(o_ref.dtype)


def _attention(proj3, sinks, tables):
    NB, Lp, _ = proj3.shape
    nblk = Lp // BLK
    kcol, vcol = COL_AK // 256, COL_AV // 256

    def loc(j, col):
        return pl.BlockSpec((None, BLK, 256),
                            lambda b, r: (b, jnp.clip(r - 1 + j, 0, nblk - 1), col))

    glob = lambda col: pl.BlockSpec((None, BLK, 256), lambda b, r: (b, 0, col))
    tab = pl.BlockSpec((Lp, BLK), lambda b, r: (0, 0))
    return pl.pallas_call(
        _attn_kernel,
        grid=(NB, nblk),
        in_specs=[pl.BlockSpec(memory_space=pltpu.SMEM),
                  pl.BlockSpec((None, BLK, 1024), lambda b, r: (b, r, 0)),
                  glob(kcol), loc(0, kcol), loc(1, kcol), loc(2, kcol),
                  glob(vcol), loc(0, vcol), loc(1, vcol), loc(2, vcol),
                  tab, tab, tab],
        out_specs=pl.BlockSpec((None, BLK, 1024), lambda b, r: (b, r, 0)),
        out_shape=jax.ShapeDtypeStruct((NB, Lp, 1024), jnp.bfloat16),
        compiler_params=_cparams(("parallel", "arbitrary")),
        name="window_attn",
    )(sinks, proj3, proj3, proj3, proj3, proj3, proj3, proj3, proj3, proj3, *tables)


HG_ROWS = BLK
_HG_LEVELS = (64, 32, 16, 8, 4, 2, 1)


def _hg_consts(reverse):
    C = HG_ROWS
    t = np.arange(C)[:, None]
    u = np.arange(C)[None, :]
    mats = [u <= t, u > t]
    ups, msks = [], []
    for B in _HG_LEVELS:
        same = (t // (2 * B)) == (u // (2 * B))
        tin, uin = t % (2 * B), u % (2 * B)
        upper = tin >= B
        wq = upper & same & (uin >= B) & (uin <= tin)
        wk = (~upper) & same & (uin > tin) & (uin <= B - 1)
        mats.append(wq | wk)
        ups.append(np.broadcast_to(upper, (C, HG_DIM)))
        msks.append(same & upper & (uin < B))
    msks.append(np.eye(C, dtype=bool))
    if reverse:
        mats = [m[::-1, ::-1] for m in mats]
        ups = [m[::-1] for m in ups]
        msks = [m[::-1, ::-1] for m in msks]
    W = np.concatenate(mats, axis=0).astype(np.float32)
    W2 = np.concatenate([W, W], axis=1)
    return (jnp.asarray(W2, jnp.bfloat16), jnp.asarray(np.stack(ups).astype(np.float32)),
            jnp.asarray(np.stack(msks).astype(np.float32)))


def _hg_chunk(q_raw, z, v, valid, lb, w2, up_ref, msk_ref, st, reverse):
    C = HG_ROWS
    f32, bf16 = jnp.float32, jnp.bfloat16
    log_lb = jnp.log(lb)
    log1m = jnp.log1p(-lb)
    logsig = jnp.minimum(z, 0.0) - jnp.log1p(jnp.exp(-jnp.abs(z)))
    a, b = log_lb, log1m + logsig
    g = jnp.maximum(a, b) + jnp.log1p(jnp.exp(-jnp.abs(a - b)))
    k = (1.0 - lb) * jax.nn.sigmoid(-z)
    q = q_raw * jax.nn.sigmoid(q_raw)
    q = jnp.where(valid, q, 0.0)
    k = jnp.where(valid, k, 0.0)
    v = jnp.where(valid, v, 0.0)
    g = jnp.where(valid, g, 0.0)
    g1 = g.astype(bf16)
    g2 = (g - g1.astype(f32)).astype(bf16)
    E = jnp.dot(w2, jnp.concatenate([g1, g2], axis=0), preferred_element_type=f32)
    b_inc = E[0:C]
    b_sfx = E[C:2 * C]
    b_tot = b_inc[0:1] if reverse else b_inc[C - 1:C]

    A = msk_ref[len(_HG_LEVELS)] * jnp.sum(q * k, axis=-1, keepdims=True)
    for l in range(len(_HG_LEVELS)):
        ex = jnp.exp(E[(2 + l) * C:(3 + l) * C])
        up = up_ref[l] > 0.5
        qs = jnp.where(up, q * ex, 0.0).astype(bf16)
        ks = jnp.where(up, 0.0, k * ex).astype(bf16)
        A = A + _nt(qs, ks) * msk_ref[l]

    vb = v.astype(bf16)
    o = _nt((q * jnp.exp(b_inc)).astype(bf16), st.astype(bf16))
    o = o + jnp.dot(A.astype(bf16), vb, preferred_element_type=f32)
    st = st * jnp.exp(b_tot) + _tn(vb, (k * jnp.exp(b_sfx)).astype(bf16))
    return o, st


def _hg_kernel(qf_ref, zf_ref, vf_ref, qb_ref, zb_ref, vb_ref, lb_ref, w2f_ref, upf_ref, mskf_ref,
               w2b_ref, upb_ref, mskb_ref, of_ref, ob_ref, st_ref):
    i = pl.program_id(2)
    nblk = pl.num_programs(2)

    @pl.when(i == 0)
    def _():
        st_ref[...] = jnp.zeros_like(st_ref)

    rows = lax.broadcasted_iota(jnp.int32, (HG_ROWS, HG_DIM), 0)
    dirs = ((False, qf_ref, zf_ref, vf_ref, w2f_ref, upf_ref, mskf_ref, of_ref, i),
            (True, qb_ref, zb_ref, vb_ref, w2b_ref, upb_ref, mskb_ref, ob_ref, nblk - 1 - i))
    for d, (reverse, q_ref, z_ref, v_ref, w2_ref, up_ref, msk_ref, o_ref, blk) in enumerate(dirs):
        w2 = w2_ref[...]
        for hh in range(HG_HP):
            cols = slice(hh * HG_DIM, (hh + 1) * HG_DIM)
            lb = lb_ref[d, :, cols]
            valid = (rows >= PAD_ROWS) | (blk > 0)
            o, st = _hg_chunk(q_ref[:, cols], z_ref[:, cols], v_ref[:, cols], valid, lb, w2,
                              up_ref, msk_ref, st_ref[d, hh], reverse)
            st_ref[d, hh] = st
            o_ref[:, cols] = o


def _hgrn(proj3, lb3, consts_f, consts_b):
    NB, Lp, _ = proj3.shape
    nblk = Lp // BLK
    W = HG_DIM * HG_HP

    def col(base, reverse):
        return pl.BlockSpec((None, BLK, W),
                            lambda b, h, i: (b, (nblk - 1 - i) if reverse else i, base // W + h))

    def const(c):
        return pl.BlockSpec(c.shape, lambda b, h, i: (0,) * c.ndim)

    in_specs = [col(COL_HQ, False), col(COL_ZF, False), col(COL_HI, False),
                col(COL_HQ, True), col(COL_ZB, True), col(COL_HI, True),
                pl.BlockSpec((2, 1, W), lambda b, h, i: (0, 0, h))]
    in_specs += [const(c) for c in consts_f] + [const(c) for c in consts_b]
    oshape = jax.ShapeDtypeStruct((NB, Lp, HG_HEADS * HG_DIM), jnp.float32)
    return pl.pallas_call(
        _hg_kernel,
        grid=(NB, HG_HEADS // HG_HP, nblk),
        in_specs=in_specs,
        out_specs=(pl.BlockSpec((None, BLK, W), lambda b, h, i: (b, i, h)),
                   pl.BlockSpec((None, BLK, W), lambda b, h, i: (b, nblk - 1 - i, h))),
        out_shape=(oshape, oshape),
        scratch_shapes=[pltpu.VMEM((2, HG_HP, HG_DIM, HG_DIM), jnp.float32)],
        compiler_params=_cparams(("parallel", "parallel", "arbitrary")),
        name="hgrn_bidir",
    )(proj3, proj3, proj3, proj3, proj3, proj3, lb3, *consts_f, *consts_b)


def _merge_kernel(a_ref, of_ref, ob_ref, hgg_ref, ng_ref, ga_ref, gh_ref, h_ref, wa_ref, wh_ref, wo_ref, o_ref):
    f32, bf16 = jnp.float32, jnp.bfloat16
    o = of_ref[...] + ob_ref[...]
    ng = ng_ref[...]
    gt = hgg_ref[...]
    hh = jnp.concatenate([_rms(o[:, c:c + HG_DIM], ng[:, c:c + HG_DIM]) for c in range(0, HG_HEADS * HG_DIM, HG_DIM)],
                         axis=1) * (gt * jax.nn.sigmoid(gt))
    pa = jnp.dot(a_ref[...], wa_ref[...], preferred_element_type=f32)
    ph = jnp.dot(hh.astype(bf16), wh_ref[...], preferred_element_type=f32)
    merged = jax.nn.sigmoid(ga_ref[...]) * pa + jax.nn.sigmoid(gh_ref[...]) * ph
    o_ref[...] = h_ref[...] + jnp.dot(merged.astype(bf16), wo_ref[...], preferred_element_type=f32)


def _merge(a2d, of2d, ob2d, ng, proj2d, h2d, wa, wh, wo, tm):
    T = h2d.shape[0]
    row = lambda c: pl.BlockSpec((tm, D_MODEL), lambda i: (i, c))
    wspec = pl.BlockSpec((D_MODEL, D_MODEL), lambda i: (0, 0))
    return pl.pallas_call(
        _merge_kernel,
        grid=(T // tm,),
        in_specs=[row(0), row(0), row(0), row(COL_HGG // D_MODEL), pl.BlockSpec((1, D_MODEL), lambda i: (0, 0)),
                  row(COL_GA // D_MODEL), row(COL_GH // D_MODEL), row(0), wspec, wspec, wspec],
        out_specs=row(0),
        out_shape=jax.ShapeDtypeStruct((T, D_MODEL), jnp.float32),
        input_output_aliases={7: 0},
        compiler_params=_cparams(("parallel",)),
        name="merge_out",
    )(a2d, of2d, ob2d, proj2d, ng, proj2d, proj2d, h2d, wa, wh, wo)


def _topk_axis0(s, kk):
    n = s.shape[0]
    iota = lax.broadcasted_iota(jnp.int32, s.shape, 0)
    vals, ids = [], []
    for _ in range(kk):
        m = jnp.max(s, axis=0, keepdims=True)
        am = jnp.min(jnp.where(s == m, iota, n), axis=0, keepdims=True)
        s = jnp.where(iota == am, -jnp.inf, s)
        vals.append(m)
        ids.append(am)
    return vals, ids


def _stack_rows(rows, dtype):
    n = len(rows)
    iota = lax.broadcasted_iota(jnp.int32, (n, rows[0].shape[1]), 0)
    out = jnp.zeros((n, rows[0].shape[1]), dtype)
    for j, rj in enumerate(rows):
        out = jnp.where(iota == j, rj, out)
    return out


def _route_kernel(h_ref, g_ref, wq_ref, key_ref, eidx_ref, gate_ref, xn_ref):
    @pl.when(pl.program_id(1) == 0)
    def _():
        xn_ref[...] = _rms(h_ref[...], g_ref[...]).astype(jnp.bfloat16)

    K = PEER_TOPK
    half = PEER_QDIM // 2
    i32 = jnp.int32
    qry = jnp.dot(xn_ref[...], wq_ref[...], preferred_element_type=jnp.float32).astype(jnp.bfloat16)
    s0 = _nt(key_ref[0], qry[:, :half])
    s1 = _nt(key_ref[1], qry[:, half:])
    tm = s0.shape[1]
    v0, i0 = _topk_axis0(s0, K)
    v1, i1 = _topk_axis0(s1, K)
    m1 = _stack_rows(v1, jnp.float32)
    m0_hi = _stack_rows(v0[K // 2:], jnp.float32)
    r8 = lax.broadcasted_iota(i32, (K // 2, tm), 0)
    cand = [v0[0] + m1] + [v0[a] + m1[:K // 2] for a in range(1, K // 2)] + [m0_hi + v1[0]]
    flat = [lax.broadcasted_iota(i32, (K, tm), 0)] + [a * K + r8 for a in range(1, K // 2)] + [(K // 2 + r8) * K]
    cand = jnp.concatenate(cand, axis=0)
    flat = jnp.concatenate(flat, axis=0)
    big = K * K
    ts, tp = [], []
    for _ in range(K):
        m = jnp.max(cand, axis=0, keepdims=True)
        p = jnp.min(jnp.where(cand == m, flat, big), axis=0, keepdims=True)
        cand = jnp.where(flat == p, -jnp.inf, cand)
        ts.append(m)
        tp.append(p)
    pk = _stack_rows(tp, i32)
    pa, pb = pk // K, pk % K
    e0 = jnp.zeros((K, tm), i32)
    e1 = jnp.zeros((K, tm), i32)
    for a in range(K):
        e0 = jnp.where(pa == a, i0[a], e0)
        e1 = jnp.where(pb == a, i1[a], e1)
    ex = [jnp.exp(t - ts[0]) for t in ts]
    den = ex[0]
    for e in ex[1:]:
        den = den + e
    eidx_ref[...] = e0 * PEER_NKEYS + e1
    gate_ref[...] = _stack_rows([e / den for e in ex], jnp.float32)


def _route(h2d, g, wq, keys, tm):
    T = h2d.shape[0]
    out = jax.ShapeDtypeStruct((PEER_SLOTS, T), jnp.int32), jax.ShapeDtypeStruct((PEER_SLOTS, T), jnp.float32)
    ospec = pl.BlockSpec((PEER_TOPK, tm), lambda i, h: (h, i))
    return pl.pallas_call(
        _route_kernel,
        grid=(T // tm, PEER_HEADS),
        in_specs=[pl.BlockSpec((tm, D_MODEL), lambda i, h: (i, 0)),
                  pl.BlockSpec((1, D_MODEL), lambda i, h: (0, 0)),
                  pl.BlockSpec((D_MODEL, PEER_QDIM), lambda i, h: (0, h)),
                  pl.BlockSpec((2, PEER_NKEYS, PEER_QDIM // 2), lambda i, h: (h, 0, 0))],
        out_specs=(ospec, ospec),
        out_shape=out,
        scratch_shapes=[pltpu.VMEM((tm, D_MODEL), jnp.bfloat16)],
        compiler_params=_cparams(("parallel", "arbitrary")),
        name="peer_route",
    )(h2d, g, wq, keys)


def _pack_bf16_pairs(x):
    e, d = x.shape
    bits = lax.bitcast_convert_type(x.astype(jnp.bfloat16), jnp.uint16).astype(jnp.uint32)
    bits = bits.reshape(e, d // 256, 2, 128)
    return bits[:, :, 0, :] | (bits[:, :, 1, :] << 16)


def _sc_gather(uv, idx):
    M = idx.shape[0]
    info = plsc.get_sparse_core_info()
    nw = info.num_cores * info.num_subcores
    per = M // nw
    assert per * nw == M and per % SC_WINDOW == 0
    mesh = plsc.VectorSubcoreMesh(core_axis_name="c", subcore_axis_name="s")
    oshape = jax.ShapeDtypeStruct((M,) + uv.shape[2:], uv.dtype)

    @functools.partial(
        pl.kernel, mesh=mesh, out_type=(oshape, oshape),
        scratch_types=[pltpu.VMEM((SC_WINDOW,), jnp.int32), pltpu.VMEM((SC_WINDOW,) + uv.shape[1:], uv.dtype),
                       pltpu.SemaphoreType.DMA],
    )
    def gather(uv_hbm, idx_hbm, ou_hbm, ov_hbm, idx_v, rows_v, sem):
        wid = lax.axis_index("s") * info.num_cores + lax.axis_index("c")
        base = wid * per

        @pl.loop(0, per // SC_WINDOW)
        def _(g):
            off = pl.multiple_of(base + g * SC_WINDOW, SC_WINDOW)
            pltpu.sync_copy(idx_hbm.at[pl.ds(off, SC_WINDOW)], idx_v)
            pltpu.async_copy(uv_hbm.at[idx_v], rows_v, sem).wait()
            pltpu.sync_copy(rows_v.at[:, 0], ou_hbm.at[pl.ds(off, SC_WINDOW)])
            pltpu.sync_copy(rows_v.at[:, 1], ov_hbm.at[pl.ds(off, SC_WINDOW)])

    return gather(uv, idx)


def _expert_kernel(idx_ref, idxn_ref, h_ref, g_ref, gate_ref, spread_ref, su_ref, sv_ref, uv_hbm, o_ref, buf, z_ref,
                   xn_ref, y_ref, sem, *, step0, seq_steps, pad_steps):
    TD, TS, NS = PEER_TD, PEER_TS, PEER_SLOTS
    TT, LW = TD + TS, SUBLANES * PEER_SLOTS
    PR = SUBLANES // 2
    f32, bf16 = jnp.float32, jnp.bfloat16
    i = pl.program_id(0)
    n = pl.num_programs(0)
    slot = lax.rem(i, 2)
    live = lax.rem(step0 + i, seq_steps) >= pad_steps
    live_next = (i + 1 < n) & (lax.rem(step0 + i + 1, seq_steps) >= pad_steps)
    dead = jnp.logical_not(live)

    def issue_token(iref, sl, t):
        for k in range(NS):
            pltpu.make_async_copy(uv_hbm.at[iref[t, k]], buf.at[sl, :, pl.ds((t * NS + k) * PR, PR)],
                                  sem.at[sl]).start(priority=k % 2)

    def issue_all(iref, sl):
        def body(t, c):
            issue_token(iref, sl, t)
            return c
        lax.fori_loop(0, TD, body, 0)

    @pl.when((i == 0) & live)
    def _():
        issue_all(idx_ref, 0)

    @pl.when(live)
    def _():
        pltpu.make_async_copy(buf.at[slot], buf.at[slot], sem.at[slot]).wait()

    @pl.when(dead & live_next)
    def _():
        issue_all(idxn_ref, 1 - slot)

    @pl.when(dead)
    def _():
        o_ref[...] = h_ref[...]

    @pl.when(live)
    def _():
        _expert_step(idxn_ref, h_ref, g_ref, gate_ref, spread_ref, su_ref, sv_ref, o_ref, buf, z_ref, xn_ref, y_ref,
                     slot, live_next, issue_token)


def _expert_step(idxn_ref, h_ref, g_ref, gate_ref, spread_ref, su_ref, sv_ref, o_ref, buf, z_ref, xn_ref, y_ref,
                 slot, live_next, issue_token):
    TD, TS, NS = PEER_TD, PEER_TS, PEER_SLOTS
    TT, LW = TD + TS, SUBLANES * PEER_SLOTS
    PR = SUBLANES // 2
    f32, bf16 = jnp.float32, jnp.bfloat16

    x = h_ref[...]
    xn_ref[...] = _rms(x, g_ref[...]).reshape(TT, SUBLANES, 128)
    on_diag = (lax.broadcasted_iota(jnp.int32, (SUBLANES, LW), 1) % SUBLANES
               == lax.broadcasted_iota(jnp.int32, (SUBLANES, LW), 0))

    def pieces_of(t, staged, part):
        r = pl.ds(pl.multiple_of(t * (NS * PR), NS * PR), NS * PR)
        words = (sv_ref if part else su_ref)[r, :] if staged else buf[slot, part, r, :]
        return pltpu.bitcast(words, bf16)

    def score_token(t, staged):
        tg = t + TD if staged else t
        z = _nt(xn_ref[tg].astype(bf16), pieces_of(t, staged, 0))
        z_ref[pl.ds(tg, 1), :] = jnp.sum(jnp.where(on_diag, z, 0.0), axis=0, keepdims=True)

    def mix_token(t, staged):
        tg = t + TD if staged else t
        w = jnp.where(on_diag, jnp.broadcast_to(z_ref[pl.ds(tg, 1), :], (SUBLANES, LW)), 0.0).astype(bf16)
        y_ref[tg] = jnp.dot(w, pieces_of(t, staged, 1), preferred_element_type=f32)

    def batched(count, fn):
        def body(tt, c):
            for j in range(PEER_BATCH):
                fn(tt * PEER_BATCH + j)
            return c
        lax.fori_loop(0, count // PEER_BATCH, body, 0)

    @pl.when(live_next)
    def _():
        def body(tt, c):
            for j in range(PEER_UNROLL):
                score_token(tt * PEER_UNROLL + j, False)
            for j in range(PEER_UNROLL):
                issue_token(idxn_ref, 1 - slot, tt * PEER_UNROLL + j)
            return c
        lax.fori_loop(0, TD // PEER_UNROLL, body, 0)

    @pl.when(jnp.logical_not(live_next))
    def _():
        batched(TD, lambda t: score_token(t, False))

    batched(TS, lambda t: score_token(t, True))

    z = z_ref[...]
    z = z + pltpu.roll(z, 4, 1)
    z = z + pltpu.roll(z, 2, 1)
    z = z + pltpu.roll(z, 1, 1)
    gt = gate_ref[...]
    g1 = gt.astype(bf16)
    r1 = gt - g1.astype(f32)
    g2 = r1.astype(bf16)
    g3 = (r1 - g2.astype(f32)).astype(bf16)
    spread = spread_ref[...]
    gate8 = (jnp.dot(g1, spread, preferred_element_type=f32) + jnp.dot(g2, spread, preferred_element_type=f32)
             + jnp.dot(g3, spread, preferred_element_type=f32))
    w = gate8 * (0.5 * z * (1.0 + lax.erf(z * (2.0 ** -0.5))))
    w = w + pltpu.roll(w, LW - 1, 1)
    w = w + pltpu.roll(w, LW - 2, 1)
    z_ref[...] = w + pltpu.roll(w, LW - 4, 1)

    batched(TD, lambda t: mix_token(t, False))
    batched(TS, lambda t: mix_token(t, True))
    o_ref[...] = x + y_ref[...].reshape(TT, D_MODEL)


def _experts_chunk(step0, nsteps, seq_steps, idx_d, h2d, g, gate, spread, su, sv, uv):
    TD, TS, NS = PEER_TD, PEER_TS, PEER_SLOTS
    TT, PR = TD + TS, SUBLANES // 2
    TDP = idx_d.shape[0] // (h2d.shape[0] // TT)
    last = step0 + nsteps - 1
    staged = pl.BlockSpec((TS * NS * PR, 128), lambda i: (i, 0))
    return pl.pallas_call(
        functools.partial(_expert_kernel, step0=step0, seq_steps=seq_steps, pad_steps=PAD_ROWS // TT),
        grid=(nsteps,),
        in_specs=[pl.BlockSpec((TDP, NS), lambda i: (step0 + i, 0), memory_space=pltpu.SMEM),
                  pl.BlockSpec((TDP, NS), lambda i: (jnp.minimum(step0 + i + 1, last), 0), memory_space=pltpu.SMEM),
                  pl.BlockSpec((TT, D_MODEL), lambda i: (step0 + i, 0)),
                  pl.BlockSpec((1, D_MODEL), lambda i: (0, 0)),
                  pl.BlockSpec((TT, NS), lambda i: (step0 + i, 0)),
                  pl.BlockSpec((NS, SUBLANES * NS), lambda i: (0, 0)),
                  staged, staged,
                  pl.BlockSpec(memory_space=pl.ANY)],
        out_specs=pl.BlockSpec((TT, D_MODEL), lambda i: (step0 + i, 0)),
        out_shape=jax.ShapeDtypeStruct(h2d.shape, jnp.float32),
        scratch_shapes=[pltpu.VMEM((2, 2, TD * NS * PR, 128), jnp.uint32),
                        pltpu.VMEM((TT, SUBLANES * NS), jnp.float32),
                        pltpu.VMEM((TT, SUBLANES, 128), jnp.float32),
                        pltpu.VMEM((TT, SUBLANES, 128), jnp.float32),
                        pltpu.SemaphoreType.DMA((2,))],
        input_output_aliases={2: 0},
        compiler_params=_cparams(("arbitrary",)),
        name="peer_experts",
    )(idx_d, idx_d, h2d, g, gate, spread, su, sv, uv)


def _experts(h2d, g, eidx, gate, u4, v4, seq_rows):
    T = h2d.shape[0]
    TD, TS, NS = PEER_TD, PEER_TS, PEER_SLOTS
    TT = TD + TS
    assert seq_rows % TT == 0
    steps = T // TT
    chunks = _pick_tile(steps, (24, 12, 8, 6, 4, 3, 2, 1))
    spc = steps // chunks
    e3 = eidx.reshape(steps, TT, NS)
    tdp = -(-TD // SUBLANES) * SUBLANES
    idx_d = jnp.pad(e3[:, :TD], ((0, 0), (0, tdp - TD), (0, 0))).reshape(steps * tdp, NS)
    idx_s = e3[:, TD:].reshape(chunks, spc * TS * NS)
    spread = np.zeros((NS, SUBLANES * NS), np.float32)
    spread[np.arange(NS), SUBLANES * np.arange(NS) + SUBLANES - 1] = 1.0
    spread = jnp.asarray(spread, jnp.bfloat16)
    uv = jnp.stack([u4, v4], axis=1)
    for c in range(chunks):
        su, sv = _sc_gather(uv, idx_s[c])
        su, sv = (a.reshape(-1, 128) for a in (su, sv))
        h2d = _experts_chunk(c * spc, spc, seq_rows // TT, idx_d, h2d, g, gate, spread, su, sv, uv)
    return h2d


def _final_kernel(h_ref, g_ref, o_ref):
    o_ref[...] = _rms(h_ref[...], g_ref[...])


def _final(h3, g, b0, nb):
    _, Lp, _ = h3.shape
    S = Lp - BLK
    return pl.pallas_call(
        _final_kernel,
        grid=(nb, S // BLK),
        in_specs=[pl.BlockSpec((None, BLK, D_MODEL), lambda b, r: (b + b0, r + 1, 0)),
                  pl.BlockSpec((1, D_MODEL), lambda b, r: (0, 0))],
        out_specs=pl.BlockSpec((None, BLK, D_MODEL), lambda b, r: (b, r, 0)),
        out_shape=jax.ShapeDtypeStruct((nb, S, D_MODEL), jnp.float32),
        compiler_params=_cparams(("parallel", "parallel")),
        name="final_norm",
    )(h3, g)


def _pick_tile(T, candidates):
    for c in candidates:
        if T % c == 0:
            return c
    raise ValueError(f"no tile for {T}")


def kernel(x_prompt, x_sample, meta_tokens, g_mix, w_in, attn_sinks, lb_logits, hg_norm_g, w_att_branch,
           w_hg_branch, w_out, g_ffn, w_peer_q, peer_sub_keys, peer_u, peer_v, g_final):
    f32, bf16 = jnp.float32, jnp.bfloat16
    depth = w_in.shape[0]
    nb_p, S, _ = x_prompt.shape
    nb_s = x_sample.shape[0]
    assert x_sample.shape[1] == S and S % BLK == 0
    NB = nb_p + nb_s
    Lp = S + BLK
    T = NB * Lp

    x = jnp.concatenate([x_prompt, x_sample], axis=0)
    head = jnp.concatenate([jnp.zeros((PAD_ROWS, D_MODEL), f32), meta_tokens.astype(f32)], axis=0)
    h = jnp.concatenate([jnp.broadcast_to(head[None], (NB, BLK, D_MODEL)), x], axis=1).reshape(T, D_MODEL)

    sizes = (1024, 256, 256, 1024, 1024, 1024, 1024, 1024, 1024, 1024)
    offs = np.concatenate([[0], np.cumsum(sizes)])
    order = (0, 8, 9, 3, 4, 5, 6, 7, 1, 2)
    perm = np.concatenate([np.arange(offs[k], offs[k + 1]) for k in order])

    sm = jax.nn.softmax(lb_logits.astype(f32), axis=0)
    lb_all = jnp.maximum(jnp.cumsum(sm, axis=0) - sm[0:1], 0.0)

    tables = _rope_tables(Lp)
    consts_f = _hg_consts(False)
    consts_b = _hg_consts(True)
    tm_big = _pick_tile(T, (512, 384, 256, 128))
    tm_route = _pick_tile(T, (256, 128))

    for l in range(depth):
        w_l = w_in[l][:, perm].astype(bf16)
        proj = _inproj(h, g_mix[l].reshape(1, -1), w_l, tm_big, IN_W // 4)
        proj3 = proj.reshape(NB, Lp, IN_W)
        a_out = _attention(proj3, attn_sinks[l].astype(f32), tables)
        lb3 = lb_all[l].reshape(2, 1, -1)
        o_fwd, o_bwd = _hgrn(proj3, lb3, consts_f, consts_b)
        h = _merge(a_out.reshape(T, -1), o_fwd.reshape(T, -1), o_bwd.reshape(T, -1), hg_norm_g[l].reshape(1, -1),
                   proj, h, w_att_branch[l].astype(bf16), w_hg_branch[l].astype(bf16), w_out[l].astype(bf16),
                   tm_route)
        g_f = g_ffn[l].reshape(1, -1)
        keys = peer_sub_keys[l].reshape(2 * PEER_HEADS, PEER_NKEYS, PEER_QDIM // 2).astype(bf16)
        eidx_t, gate_t = _route(h, g_f, w_peer_q[l].astype(bf16), keys, tm_route)
        h = _experts(h, g_f, eidx_t.T, gate_t.T, _pack_bf16_pairs(peer_u[l]), _pack_bf16_pairs(peer_v[l]), Lp)

    h3 = h.reshape(NB, Lp, D_MODEL)
    g_fin = g_final.reshape(1, -1)
    return _final(h3, g_fin, 0, nb_p), _final(h3, g_fin, nb_p, nb_s)
```

```python
import functools

import numpy as np
import jax
import jax.numpy as jnp
from jax import lax
from jax.experimental import pallas as pl
from jax.experimental.pallas import tpu as pltpu
from jax.experimental.pallas import tpu_sc as plsc

D_MODEL = 1024
N_META = 16
N_Q_HEADS = 16
N_KV_HEADS = 4
HEAD_DIM = 64
WINDOW = 128
ROT_DIM = HEAD_DIM // 4
ROPE_THETA = 500000.0
HG_HEADS = 8
HG_DIM = 128
PEER_HEADS = 8
PEER_NKEYS = 128
PEER_QDIM = 256
PEER_TOPK = 16
EPS = 1e-6
NEG = -1e30

BLK = 128
PAD_ROWS = BLK - N_META
IN_W = 8704
COL_AQ, COL_GA, COL_GH, COL_HQ, COL_ZF, COL_ZB, COL_HI, COL_HGG, COL_AK, COL_AV = (
    0, 1024, 2048, 3072, 4096, 5120, 6144, 7168, 8192, 8448)

VMEM_LIMIT = 56 * 1024 * 1024
PEER_TD = 16
PEER_TS = 16
SC_WINDOW = 64
PEER_SLOTS = PEER_HEADS * PEER_TOPK
HG_HP = 8
PEER_UNROLL = 2
PEER_BATCH = 16
SUBLANES = 8


def _cparams(sem):
    return pltpu.CompilerParams(dimension_semantics=sem, vmem_limit_bytes=VMEM_LIMIT)


def _rms(x, g):
    return x * lax.rsqrt(jnp.mean(x * x, axis=-1, keepdims=True) + EPS) * g


def _nt(a, b):
    return lax.dot_general(a, b, (((1,), (1,)), ((), ())), preferred_element_type=jnp.float32)


def _tn(a, b):
    return lax.dot_general(a, b, (((0,), (0,)), ((), ())), preferred_element_type=jnp.float32)


def _inproj_kernel(x_ref, g_ref, w_ref, o_ref):
    n = _rms(x_ref[...], g_ref[...]).astype(jnp.bfloat16)
    o_ref[...] = jnp.dot(n, w_ref[...], preferred_element_type=jnp.float32)


def _inproj(h2d, g, w, tm, tn):
    T = h2d.shape[0]
    return pl.pallas_call(
        _inproj_kernel,
        grid=(IN_W // tn, T // tm),
        in_specs=[pl.BlockSpec((tm, D_MODEL), lambda j, i: (i, 0)),
                  pl.BlockSpec((1, D_MODEL), lambda j, i: (0, 0)),
                  pl.BlockSpec((D_MODEL, tn), lambda j, i: (0, j))],
        out_specs=pl.BlockSpec((tm, tn), lambda j, i: (i, j)),
        out_shape=jax.ShapeDtypeStruct((T, IN_W), jnp.float32),
        compiler_params=_cparams(("parallel", "parallel")),
        name="inproj",
    )(h2d, g, w)


def _rope_tables(Lp):
    half = ROT_DIM // 2
    pos = jnp.maximum(jnp.arange(Lp) - PAD_ROWS, 0)
    inv = jnp.power(jnp.float32(ROPE_THETA), -jnp.arange(half, dtype=jnp.float32) * 2.0 / ROT_DIM)
    ang = pos.astype(jnp.float32)[:, None] * inv[None, :]
    cos, sin = jnp.cos(ang), jnp.sin(ang)
    ones = jnp.ones((Lp, HEAD_DIM - ROT_DIM), jnp.float32)
    zeros = jnp.zeros((Lp, HEAD_DIM - ROT_DIM), jnp.float32)
    zh = jnp.zeros((Lp, half), jnp.float32)
    c = jnp.concatenate([cos, cos, ones], axis=1)
    s1 = jnp.concatenate([zh, sin, zeros], axis=1)
    s2 = jnp.concatenate([-sin, zh, zeros], axis=1)
    rep = BLK // HEAD_DIM
    return tuple(jnp.tile(t, (1, rep)) for t in (c, s1, s2))


def _attn_kernel(sink_ref, q_ref, kg_ref, k0_ref, k1_ref, k2_ref, vg_ref, v0_ref, v1_ref, v2_ref,
                 c_ref, s1_ref, s2_ref, o_ref):
    r = pl.program_id(1)
    nblk = pl.num_programs(1)
    half = ROT_DIM // 2

    def rope(x, blk):
        row0 = blk * BLK if isinstance(blk, int) else pl.multiple_of(blk * BLK, BLK)
        c = c_ref[pl.ds(row0, BLK), :]
        s1 = s1_ref[pl.ds(row0, BLK), :]
        s2 = s2_ref[pl.ds(row0, BLK), :]
        outs = []
        for j in range(x.shape[1] // BLK):
            xj = x[:, j * BLK:(j + 1) * BLK]
            outs.append(xj * c + pltpu.roll(xj, half, 1) * s1 + pltpu.roll(xj, BLK - half, 1) * s2)
        return jnp.concatenate(outs, axis=1)

    kb = [jnp.clip(r - 1 + j, 0, nblk - 1) for j in range(3)]
    q = rope(q_ref[...], r) * (HEAD_DIM ** -0.5)
    kcat = jnp.concatenate([rope(kg_ref[...], 0), rope(k0_ref[...], kb[0]), rope(k1_ref[...], kb[1]),
                            rope(k2_ref[...], kb[2])], axis=0).astype(jnp.bfloat16)
    vcat = jnp.concatenate([vg_ref[...], v0_ref[...], v1_ref[...], v2_ref[...]], axis=0).astype(jnp.bfloat16)

    qrow = lax.broadcasted_iota(jnp.int32, (BLK, BLK), 0)
    kcol = lax.broadcasted_iota(jnp.int32, (BLK, BLK), 1)
    masks = [kcol >= PAD_ROWS]
    for j in range(3):
        kblk = r - 1 + j
        ok_blk = (kblk >= 1) & (kblk <= nblk - 1)
        masks.append((jnp.abs((j - 1) * BLK + kcol - qrow) <= WINDOW) & ok_blk)
    mask = jnp.concatenate(masks, axis=1)

    G = N_Q_HEADS // N_KV_HEADS
    lane_half = kcol // HEAD_DIM
    pairs_per_slice = (BLK // HEAD_DIM) * G // 2
    outs = [None] * (N_Q_HEADS // 2)
    for m2 in range(N_KV_HEADS * HEAD_DIM // BLK):
        k2 = kcat[:, m2 * BLK:(m2 + 1) * BLK]
        v2 = vcat[:, m2 * BLK:(m2 + 1) * BLK]
        qs, sks, halves = [], [], []
        for j in range(m2 * pairs_per_slice, (m2 + 1) * pairs_per_slice):
            hk = ((2 * j) // G) % 2
            in_half = lane_half == hk
            q2 = q[:, j * BLK:(j + 1) * BLK]
            qs += [jnp.where(in_half, q2, 0.0), jnp.where(in_half, pltpu.roll(q2, HEAD_DIM, 1), 0.0)]
            sks += [jnp.full((BLK, 1), sink_ref[2 * j + hk]), jnp.full((BLK, 1), sink_ref[2 * j + 1 - hk])]
            halves.append(in_half)
        nq = len(qs)
        sk = jnp.concatenate(sks, axis=0)
        s = _nt(jnp.concatenate(qs, axis=0).astype(jnp.bfloat16), k2)
        s = jnp.where(jnp.tile(mask, (nq, 1)), s, NEG)
        m = jnp.maximum(jnp.max(s, axis=-1, keepdims=True), sk)
        p = jnp.exp(s - m)
        p = p / (jnp.sum(p, axis=-1, keepdims=True) + jnp.exp(sk - m))
        o = jnp.dot(p.astype(jnp.bfloat16), v2, preferred_element_type=jnp.float32)
        for jj, in_half in enumerate(halves):
            o_same = o[(2 * jj) * BLK:(2 * jj + 1) * BLK]
            o_other = o[(2 * jj + 1) * BLK:(2 * jj + 2) * BLK]
            outs[m2 * pairs_per_slice + jj] = jnp.where(in_half, o_same, pltpu.roll(o_other, HEAD_DIM, 1))
    o_ref[...] = jnp.concatenate(outs, axis=1).astype(o_ref.dtype)


def _attention(proj3, sinks, tables):
    NB, Lp, _ = proj3.shape
    nblk = Lp // BLK
    kcol, vcol = COL_AK // 256, COL_AV // 256

    def loc(j, col):
        return pl.BlockSpec((None, BLK, 256),
                            lambda b, r: (b, jnp.clip(r - 1 + j, 0, nblk - 1), col))

    glob = lambda col: pl.BlockSpec((None, BLK, 256), lambda b, r: (b, 0, col))
    tab = pl.BlockSpec((Lp, BLK), lambda b, r: (0, 0))
    return pl.pallas_call(
        _attn_kernel,
        grid=(NB, nblk),
        in_specs=[pl.BlockSpec(memory_space=pltpu.SMEM),
                  pl.BlockSpec((None, BLK, 1024), lambda b, r: (b, r, 0)),
                  glob(kcol), loc(0, kcol), loc(1, kcol), loc(2, kcol),
                  glob(vcol), loc(0, vcol), loc(1, vcol), loc(2, vcol),
                  tab, tab, tab],
        out_specs=pl.BlockSpec((None, BLK, 1024), lambda b, r: (b, r, 0)),
        out_shape=jax.ShapeDtypeStruct((NB, Lp, 1024), jnp.bfloat16),
        compiler_params=_cparams(("parallel", "arbitrary")),
        name="window_attn",
    )(sinks, proj3, proj3, proj3, proj3, proj3, proj3, proj3, proj3, proj3, *tables)


HG_ROWS = BLK
_HG_LEVELS = (64, 32, 16, 8, 4, 2, 1)


def _hg_consts(reverse):
    C = HG_ROWS
    t = np.arange(C)[:, None]
    u = np.arange(C)[None, :]
    mats = [u <= t, u > t]
    ups, msks = [], []
    for B in _HG_LEVELS:
        same = (t // (2 * B)) == (u // (2 * B))
        tin, uin = t % (2 * B), u % (2 * B)
        upper = tin >= B
        wq = upper & same & (uin >= B) & (uin <= tin)
        wk = (~upper) & same & (uin > tin) & (uin <= B - 1)
        mats.append(wq | wk)
        ups.append(np.broadcast_to(upper, (C, HG_DIM)))
        msks.append(same & upper & (uin < B))
    msks.append(np.eye(C, dtype=bool))
    if reverse:
        mats = [m[::-1, ::-1] for m in mats]
        ups = [m[::-1] for m in ups]
        msks = [m[::-1, ::-1] for m in msks]
    W = np.concatenate(mats, axis=0).astype(np.float32)
    W2 = np.concatenate([W, W], axis=1)
    return (jnp.asarray(W2, jnp.bfloat16), jnp.asarray(np.stack(ups).astype(np.float32)),
            jnp.asarray(np.stack(msks).astype(np.float32)))


def _hg_chunk(q_raw, z, v, valid, lb, w2, up_ref, msk_ref, st, reverse):
    C = HG_ROWS
    f32, bf16 = jnp.float32, jnp.bfloat16
    log_lb = jnp.log(lb)
    log1m = jnp.log1p(-lb)
    logsig = jnp.minimum(z, 0.0) - jnp.log1p(jnp.exp(-jnp.abs(z)))
    a, b = log_lb, log1m + logsig
    g = jnp.maximum(a, b) + jnp.log1p(jnp.exp(-jnp.abs(a - b)))
    k = (1.0 - lb) * jax.nn.sigmoid(-z)
    q = q_raw * jax.nn.sigmoid(q_raw)
    q = jnp.where(valid, q, 0.0)
    k = jnp.where(valid, k, 0.0)
    v = jnp.where(valid, v, 0.0)
    g = jnp.where(valid, g, 0.0)
    g1 = g.astype(bf16)
    g2 = (g - g1.astype(f32)).astype(bf16)
    E = jnp.dot(w2, jnp.concatenate([g1, g2], axis=0), preferred_element_type=f32)
    b_inc = E[0:C]
    b_sfx = E[C:2 * C]
    b_tot = b_inc[0:1] if reverse else b_inc[C - 1:C]

    A = msk_ref[len(_HG_LEVELS)] * jnp.sum(q * k, axis=-1, keepdims=True)
    for l in range(len(_HG_LEVELS)):
        ex = jnp.exp(E[(2 + l) * C:(3 + l) * C])
        up = up_ref[l] > 0.5
        qs = jnp.where(up, q * ex, 0.0).astype(bf16)
        ks = jnp.where(up, 0.0, k * ex).astype(bf16)
        A = A + _nt(qs, ks) * msk_ref[l]

    vb = v.astype(bf16)
    o = _nt((q * jnp.exp(b_inc)).astype(bf16), st.astype(bf16))
    o = o + jnp.dot(A.astype(bf16), vb, preferred_element_type=f32)
    st = st * jnp.exp(b_tot) + _tn(vb, (k * jnp.exp(b_sfx)).astype(bf16))
    return o, st


def _hg_kernel(qf_ref, zf_ref, vf_ref, qb_ref, zb_ref, vb_ref, lb_ref, w2f_ref, upf_ref, mskf_ref,
               w2b_ref, upb_ref, mskb_ref, of_ref, ob_ref, st_ref):
    i = pl.program_id(2)
    nblk = pl.num_programs(2)

    @pl.when(i == 0)
    def _():
        st_ref[...] = jnp.zeros_like(st_ref)

    rows = lax.broadcasted_iota(jnp.int32, (HG_ROWS, HG_DIM), 0)
    dirs = ((False, qf_ref, zf_ref, vf_ref, w2f_ref, upf_ref, mskf_ref, of_ref, i),
            (True, qb_ref, zb_ref, vb_ref, w2b_ref, upb_ref, mskb_ref, ob_ref, nblk - 1 - i))
    for d, (reverse, q_ref, z_ref, v_ref, w2_ref, up_ref, msk_ref, o_ref, blk) in enumerate(dirs):
        w2 = w2_ref[...]
        for hh in range(HG_HP):
            cols = slice(hh * HG_DIM, (hh + 1) * HG_DIM)
            lb = lb_ref[d, :, cols]
            valid = (rows >= PAD_ROWS) | (blk > 0)
            o, st = _hg_chunk(q_ref[:, cols], z_ref[:, cols], v_ref[:, cols], valid, lb, w2,
                              up_ref, msk_ref, st_ref[d, hh], reverse)
            st_ref[d, hh] = st
            o_ref[:, cols] = o


def _hgrn(proj3, lb3, consts_f, consts_b):
    NB, Lp, _ = proj3.shape
    nblk = Lp // BLK
    W = HG_DIM * HG_HP

    def col(base, reverse):
        return pl.BlockSpec((None, BLK, W),
                            lambda b, h, i: (b, (nblk - 1 - i) if reverse else i, base // W + h))

    def const(c):
        return pl.BlockSpec(c.shape, lambda b, h, i: (0,) * c.ndim)

    in_specs = [col(COL_HQ, False), col(COL_ZF, False), col(COL_HI, False),
                col(COL_HQ, True), col(COL_ZB, True), col(COL_HI, True),
                pl.BlockSpec((2, 1, W), lambda b, h, i: (0, 0, h))]
    in_specs += [const(c) for c in consts_f] + [const(c) for c in consts_b]
    oshape = jax.ShapeDtypeStruct((NB, Lp, HG_HEADS * HG_DIM), jnp.float32)
    return pl.pallas_call(
        _hg_kernel,
        grid=(NB, HG_HEADS // HG_HP, nblk),
        in_specs=in_specs,
        out_specs=(pl.BlockSpec((None, BLK, W), lambda b, h, i: (b, i, h)),
                   pl.BlockSpec((None, BLK, W), lambda b, h, i: (b, nblk - 1 - i, h))),
        out_shape=(oshape, oshape),
        scratch_shapes=[pltpu.VMEM((2, HG_HP, HG_DIM, HG_DIM), jnp.float32)],
        compiler_params=_cparams(("parallel", "parallel", "arbitrary")),
        name="hgrn_bidir",
    )(proj3, proj3, proj3, proj3, proj3, proj3, lb3, *consts_f, *consts_b)


def _merge_kernel(a_ref, of_ref, ob_ref, hgg_ref, ng_ref, ga_ref, gh_ref, h_ref, wa_ref, wh_ref, wo_ref, o_ref):
    f32, bf16 = jnp.float32, jnp.bfloat16
    o = of_ref[...] + ob_ref[...]
    ng = ng_ref[...]
    gt = hgg_ref[...]
    hh = jnp.concatenate([_rms(o[:, c:c + HG_DIM], ng[:, c:c + HG_DIM]) for c in range(0, HG_HEADS * HG_DIM, HG_DIM)],
                         axis=1) * (gt * jax.nn.sigmoid(gt))
    pa = jnp.dot(a_ref[...], wa_ref[...], preferred_element_type=f32)
    ph = jnp.dot(hh.astype(bf16), wh_ref[...], preferred_element_type=f32)
    merged = jax.nn.sigmoid(ga_ref[...]) * pa + jax.nn.sigmoid(gh_ref[...]) * ph
    o_ref[...] = h_ref[...] + jnp.dot(merged.astype(bf16), wo_ref[...], preferred_element_type=f32)


def _merge(a2d, of2d, ob2d, ng, proj2d, h2d, wa, wh, wo, tm):
    T = h2d.shape[0]
    row = lambda c: pl.BlockSpec((tm, D_MODEL), lambda i: (i, c))
    wspec = pl.BlockSpec((D_MODEL, D_MODEL), lambda i: (0, 0))
    return pl.pallas_call(
        _merge_kernel,
        grid=(T // tm,),
        in_specs=[row(0), row(0), row(0), row(COL_HGG // D_MODEL), pl.BlockSpec((1, D_MODEL), lambda i: (0, 0)),
                  row(COL_GA // D_MODEL), row(COL_GH // D_MODEL), row(0), wspec, wspec, wspec],
        out_specs=row(0),
        out_shape=jax.ShapeDtypeStruct((T, D_MODEL), jnp.float32),
        input_output_aliases={7: 0},
        compiler_params=_cparams(("parallel",)),
        name="merge_out",
    )(a2d, of2d, ob2d, proj2d, ng, proj2d, proj2d, h2d, wa, wh, wo)


def _topk_axis0(s, kk):
    n = s.shape[0]
    iota = lax.broadcasted_iota(jnp.int32, s.shape, 0)
    vals, ids = [], []
    for _ in range(kk):
        m = jnp.max(s, axis=0, keepdims=True)
        am = jnp.min(jnp.where(s == m, iota, n), axis=0, keepdims=True)
        s = jnp.where(iota == am, -jnp.inf, s)
        vals.append(m)
        ids.append(am)
    return vals, ids


def _stack_rows(rows, dtype):
    n = len(rows)
    iota = lax.broadcasted_iota(jnp.int32, (n, rows[0].shape[1]), 0)
    out = jnp.zeros((n, rows[0].shape[1]), dtype)
    for j, rj in enumerate(rows):
        out = jnp.where(iota == j, rj, out)
    return out


def _route_kernel(h_ref, g_ref, wq_ref, key_ref, eidx_ref, gate_ref, xn_ref):
    @pl.when(pl.program_id(1) == 0)
    def _():
        xn_ref[...] = _rms(h_ref[...], g_ref[...]).astype(jnp.bfloat16)

    K = PEER_TOPK
    half = PEER_QDIM // 2
    i32 = jnp.int32
    qry = jnp.dot(xn_ref[...], wq_ref[...], preferred_element_type=jnp.float32).astype(jnp.bfloat16)
    s0 = _nt(key_ref[0], qry[:, :half])
    s1 = _nt(key_ref[1], qry[:, half:])
    tm = s0.shape[1]
    v0, i0 = _topk_axis0(s0, K)
    v1, i1 = _topk_axis0(s1, K)
    m1 = _stack_rows(v1, jnp.float32)
    m0_hi = _stack_rows(v0[K // 2:], jnp.float32)
    r8 = lax.broadcasted_iota(i32, (K // 2, tm), 0)
    cand = [v0[0] + m1] + [v0[a] + m1[:K // 2] for a in range(1, K // 2)] + [m0_hi + v1[0]]
    flat = [lax.broadcasted_iota(i32, (K, tm), 0)] + [a * K + r8 for a in range(1, K // 2)] + [(K // 2 + r8) * K]
    cand = jnp.concatenate(cand, axis=0)
    flat = jnp.concatenate(flat, axis=0)
    big = K * K
    ts, tp = [], []
    for _ in range(K):
        m = jnp.max(cand, axis=0, keepdims=True)
        p = jnp.min(jnp.where(cand == m, flat, big), axis=0, keepdims=True)
        cand = jnp.where(flat == p, -jnp.inf, cand)
        ts.append(m)
        tp.append(p)
    pk = _stack_rows(tp, i32)
    pa, pb = pk // K, pk % K
    e0 = jnp.zeros((K, tm), i32)
    e1 = jnp.zeros((K, tm), i32)
    for a in range(K):
        e0 = jnp.where(pa == a, i0[a], e0)
        e1 = jnp.where(pb == a, i1[a], e1)
    ex = [jnp.exp(t - ts[0]) for t in ts]
    den = ex[0]
    for e in ex[1:]:
        den = den + e
    eidx_ref[...] = e0 * PEER_NKEYS + e1
    gate_ref[...] = _stack_rows([e / den for e in ex], jnp.float32)


def _route(h2d, g, wq, keys, tm):
    T = h2d.shape[0]
    out = jax.ShapeDtypeStruct((PEER_SLOTS, T), jnp.int32), jax.ShapeDtypeStruct((PEER_SLOTS, T), jnp.float32)
    ospec = pl.BlockSpec((PEER_TOPK, tm), lambda i, h: (h, i))
    return pl.pallas_call(
        _route_kernel,
        grid=(T // tm, PEER_HEADS),
        in_specs=[pl.BlockSpec((tm, D_MODEL), lambda i, h: (i, 0)),
                  pl.BlockSpec((1, D_MODEL), lambda i, h: (0, 0)),
                  pl.BlockSpec((D_MODEL, PEER_QDIM), lambda i, h: (0, h)),
                  pl.BlockSpec((2, PEER_NKEYS, PEER_QDIM // 2), lambda i, h: (h, 0, 0))],
        out_specs=(ospec, ospec),
        out_shape=out,
        scratch_shapes=[pltpu.VMEM((tm, D_MODEL), jnp.bfloat16)],
        compiler_params=_cparams(("parallel", "arbitrary")),
        name="peer_route",
    )(h2d, g, wq, keys)


def _pack_bf16_pairs(x):
    e, d = x.shape
    bits = lax.bitcast_convert_type(x.astype(jnp.bfloat16), jnp.uint16).astype(jnp.uint32)
    bits = bits.reshape(e, d // 256, 2, 128)
    return bits[:, :, 0, :] | (bits[:, :, 1, :] << 16)


def _sc_gather(uv, idx):
    M = idx.shape[0]
    info = plsc.get_sparse_core_info()
    nw = info.num_cores * info.num_subcores
    per = M // nw
    assert per * nw == M and per % SC_WINDOW == 0
    mesh = plsc.VectorSubcoreMesh(core_axis_name="c", subcore_axis_name="s")
    oshape = jax.ShapeDtypeStruct((M,) + uv.shape[2:], uv.dtype)

    @functools.partial(
        pl.kernel, mesh=mesh, out_type=(oshape, oshape),
        scratch_types=[pltpu.VMEM((SC_WINDOW,), jnp.int32), pltpu.VMEM((SC_WINDOW,) + uv.shape[1:], uv.dtype),
                       pltpu.SemaphoreType.DMA],
    )
    def gather(uv_hbm, idx_hbm, ou_hbm, ov_hbm, idx_v, rows_v, sem):
        wid = lax.axis_index("s") * info.num_cores + lax.axis_index("c")
        base = wid * per

        @pl.loop(0, per // SC_WINDOW)
        def _(g):
            off = pl.multiple_of(base + g * SC_WINDOW, SC_WINDOW)
            pltpu.sync_copy(idx_hbm.at[pl.ds(off, SC_WINDOW)], idx_v)
            pltpu.async_copy(uv_hbm.at[idx_v], rows_v, sem).wait()
            pltpu.sync_copy(rows_v.at[:, 0], ou_hbm.at[pl.ds(off, SC_WINDOW)])
            pltpu.sync_copy(rows_v.at[:, 1], ov_hbm.at[pl.ds(off, SC_WINDOW)])

    return gather(uv, idx)


def _expert_kernel(idx_ref, idxn_ref, h_ref, g_ref, gate_ref, spread_ref, su_ref, sv_ref, uv_hbm, o_ref, buf, z_ref,
                   xn_ref, y_ref, sem, *, step0, seq_steps, pad_steps):
    TD, TS, NS = PEER_TD, PEER_TS, PEER_SLOTS
    TT, LW = TD + TS, SUBLANES * PEER_SLOTS
    PR = SUBLANES // 2
    f32, bf16 = jnp.float32, jnp.bfloat16
    i = pl.program_id(0)
    n = pl.num_programs(0)
    slot = lax.rem(i, 2)
    live = lax.rem(step0 + i, seq_steps) >= pad_steps
    live_next = (i + 1 < n) & (lax.rem(step0 + i + 1, seq_steps) >= pad_steps)
    dead = jnp.logical_not(live)

    def issue_token(iref, sl, t):
        for k in range(NS):
            pltpu.make_async_copy(uv_hbm.at[iref[t, k]], buf.at[sl, :, pl.ds((t * NS + k) * PR, PR)],
                                  sem.at[sl]).start(priority=k % 2)

    def issue_all(iref, sl):
        def body(t, c):
            issue_token(iref, sl, t)
            return c
        lax.fori_loop(0, TD, body, 0)

    @pl.when((i == 0) & live)
    def _():
        issue_all(idx_ref, 0)

    @pl.when(live)
    def _():
        pltpu.make_async_copy(buf.at[slot], buf.at[slot], sem.at[slot]).wait()

    @pl.when(dead & live_next)
    def _():
        issue_all(idxn_ref, 1 - slot)

    @pl.when(dead)
    def _():
        o_ref[...] = h_ref[...]

    @pl.when(live)
    def _():
        _expert_step(idxn_ref, h_ref, g_ref, gate_ref, spread_ref, su_ref, sv_ref, o_ref, buf, z_ref, xn_ref, y_ref,
                     slot, live_next, issue_token)


def _expert_step(idxn_ref, h_ref, g_ref, gate_ref, spread_ref, su_ref, sv_ref, o_ref, buf, z_ref, xn_ref, y_ref,
                 slot, live_next, issue_token):
    TD, TS, NS = PEER_TD, PEER_TS, PEER_SLOTS
    TT, LW = TD + TS, SUBLANES * PEER_SLOTS
    PR = SUBLANES // 2
    f32, bf16 = jnp.float32, jnp.bfloat16

    x = h_ref[...]
    xn_ref[...] = _rms(x, g_ref[...]).reshape(TT, SUBLANES, 128)
    on_diag = (lax.broadcasted_iota(jnp.int32, (SUBLANES, LW), 1) % SUBLANES
               == lax.broadcasted_iota(jnp.int32, (SUBLANES, LW), 0))

    def pieces_of(t, staged, part):
        r = pl.ds(pl.multiple_of(t * (NS * PR), NS * PR), NS * PR)
        words = (sv_ref if part else su_ref)[r, :] if staged else buf[slot, part, r, :]
        return pltpu.bitcast(words, bf16)

    def score_token(t, staged):
        tg = t + TD if staged else t
        z = _nt(xn_ref[tg].astype(bf16), pieces_of(t, staged, 0))
        z_ref[pl.ds(tg, 1), :] = jnp.sum(jnp.where(on_diag, z, 0.0), axis=0, keepdims=True)

    def mix_token(t, staged):
        tg = t + TD if staged else t
        w = jnp.where(on_diag, jnp.broadcast_to(z_ref[pl.ds(tg, 1), :], (SUBLANES, LW)), 0.0).astype(bf16)
        y_ref[tg] = jnp.dot(w, pieces_of(t, staged, 1), preferred_element_type=f32)

    def batched(count, fn):
        def body(tt, c):
            for j in range(PEER_BATCH):
                fn(tt * PEER_BATCH + j)
            return c
        lax.fori_loop(0, count // PEER_BATCH, body, 0)

    @pl.when(live_next)
    def _():
        def body(tt, c):
            for j in range(PEER_UNROLL):
                score_token(tt * PEER_UNROLL + j, False)
            for j in range(PEER_UNROLL):
                issue_token(idxn_ref, 1 - slot, tt * PEER_UNROLL + j)
            return c
        lax.fori_loop(0, TD // PEER_UNROLL, body, 0)

    @pl.when(jnp.logical_not(live_next))
    def _():
        batched(TD, lambda t: score_token(t, False))

    batched(TS, lambda t: score_token(t, True))

    z = z_ref[...]
    z = z + pltpu.roll(z, 4, 1)
    z = z + pltpu.roll(z, 2, 1)
    z = z + pltpu.roll(z, 1, 1)
    gt = gate_ref[...]
    g1 = gt.astype(bf16)
    r1 = gt - g1.astype(f32)
    g2 = r1.astype(bf16)
    g3 = (r1 - g2.astype(f32)).astype(bf16)
    spread = spread_ref[...]
    gate8 = (jnp.dot(g1, spread, preferred_element_type=f32) + jnp.dot(g2, spread, preferred_element_type=f32)
             + jnp.dot(g3, spread, preferred_element_type=f32))
    w = gate8 * (0.5 * z * (1.0 + lax.erf(z * (2.0 ** -0.5))))
    w = w + pltpu.roll(w, LW - 1, 1)
    w = w + pltpu.roll(w, LW - 2, 1)
    z_ref[...] = w + pltpu.roll(w, LW - 4, 1)

    batched(TD, lambda t: mix_token(t, False))
    batched(TS, lambda t: mix_token(t, True))
    o_ref[...] = x + y_ref[...].reshape(TT, D_MODEL)


def _experts_chunk(step0, nsteps, seq_steps, idx_d, h2d, g, gate, spread, su, sv, uv):
    TD, TS, NS = PEER_TD, PEER_TS, PEER_SLOTS
    TT, PR = TD + TS, SUBLANES // 2
    last = step0 + nsteps - 1
    staged = pl.BlockSpec((TS * NS * PR, 128), lambda i: (i, 0))
    return pl.pallas_call(
        functools.partial(_expert_kernel, step0=step0, seq_steps=seq_steps, pad_steps=PAD_ROWS // TT),
        grid=(nsteps,),
        in_specs=[pl.BlockSpec((TD, NS), lambda i: (step0 + i, 0), memory_space=pltpu.SMEM),
                  pl.BlockSpec((TD, NS), lambda i: (jnp.minimum(step0 + i + 1, last), 0), memory_space=pltpu.SMEM),
                  pl.BlockSpec((TT, D_MODEL), lambda i: (step0 + i, 0)),
                  pl.BlockSpec((1, D_MODEL), lambda i: (0, 0)),
                  pl.BlockSpec((TT, NS), lambda i: (step0 + i, 0)),
                  pl.BlockSpec((NS, SUBLANES * NS), lambda i: (0, 0)),
                  staged, staged,
                  pl.BlockSpec(memory_space=pl.ANY)],
        out_specs=pl.BlockSpec((TT, D_MODEL), lambda i: (step0 + i, 0)),
        out_shape=jax.ShapeDtypeStruct(h2d.shape, jnp.float32),
        scratch_shapes=[pltpu.VMEM((2, 2, TD * NS * PR, 128), jnp.uint32),
                        pltpu.VMEM((TT, SUBLANES * NS), jnp.float32),
                        pltpu.VMEM((TT, SUBLANES, 128), jnp.float32),
                        pltpu.VMEM((TT, SUBLANES, 128), jnp.float32),
                        pltpu.SemaphoreType.DMA((2,))],
        input_output_aliases={2: 0},
        compiler_params=_cparams(("arbitrary",)),
        name="peer_experts",
    )(idx_d, idx_d, h2d, g, gate, spread, su, sv, uv)


def _experts(h2d, g, eidx, gate, u4, v4, seq_rows):
    T = h2d.shape[0]
    TD, TS, NS = PEER_TD, PEER_TS, PEER_SLOTS
    TT = TD + TS
    assert seq_rows % TT == 0
    steps = T // TT
    chunks = _pick_tile(steps, (24, 12, 8, 6, 4, 3, 2, 1))
    spc = steps // chunks
    e3 = eidx.reshape(steps, TT, NS)
    idx_d = e3[:, :TD].reshape(steps * TD, NS)
    idx_s = e3[:, TD:].reshape(chunks, spc * TS * NS)
    spread = np.zeros((NS, SUBLANES * NS), np.float32)
    spread[np.arange(NS), SUBLANES * np.arange(NS) + SUBLANES - 1] = 1.0
    spread = jnp.asarray(spread, jnp.bfloat16)
    uv = jnp.stack([u4, v4], axis=1)
    for c in range(chunks):
        su, sv = _sc_gather(uv, idx_s[c])
        su, sv = (a.reshape(-1, 128) for a in (su, sv))
        h2d = _experts_chunk(c * spc, spc, seq_rows // TT, idx_d, h2d, g, gate, spread, su, sv, uv)
    return h2d


def _final_kernel(h_ref, g_ref, o_ref):
    o_ref[...] = _rms(h_ref[...], g_ref[...])


def _final(h3, g, b0, nb):
    _, Lp, _ = h3.shape
    S = Lp - BLK
    return pl.pallas_call(
        _final_kernel,
        grid=(nb, S // BLK),
        in_specs=[pl.BlockSpec((None, BLK, D_MODEL), lambda b, r: (b + b0, r + 1, 0)),
                  pl.BlockSpec((1, D_MODEL), lambda b, r: (0, 0))],
        out_specs=pl.BlockSpec((None, BLK, D_MODEL), lambda b, r: (b, r, 0)),
        out_shape=jax.ShapeDtypeStruct((nb, S, D_MODEL), jnp.float32),
        compiler_params=_cparams(("parallel", "parallel")),
        name="final_norm",
    )(h3, g)


def _pick_tile(T, candidates):
    for c in candidates:
        if T % c == 0:
            return c
    raise ValueError(f"no tile for {T}")


def kernel(x_prompt, x_sample, meta_tokens, g_mix, w_in, attn_sinks, lb_logits, hg_norm_g, w_att_branch,
           w_hg_branch, w_out, g_ffn, w_peer_q, peer_sub_keys, peer_u, peer_v, g_final):
    f32, bf16 = jnp.float32, jnp.bfloat16
    depth = w_in.shape[0]
    nb_p, S, _ = x_prompt.shape
    nb_s = x_sample.shape[0]
    assert x_sample.shape[1] == S and S % BLK == 0
    NB = nb_p + nb_s
    Lp = S + BLK
    T = NB * Lp

    x = jnp.concatenate([x_prompt, x_sample], axis=0)
    head = jnp.concatenate([jnp.zeros((PAD_ROWS, D_MODEL), f32), meta_tokens.astype(f32)], axis=0)
    h = jnp.concatenate([jnp.broadcast_to(head[None], (NB, BLK, D_MODEL)), x], axis=1).reshape(T, D_MODEL)

    sizes = (1024, 256, 256, 1024, 1024, 1024, 1024, 1024, 1024, 1024)
    offs = np.concatenate([[0], np.cumsum(sizes)])
    order = (0, 8, 9, 3, 4, 5, 6, 7, 1, 2)
    perm = np.concatenate([np.arange(offs[k], offs[k + 1]) for k in order])

    sm = jax.nn.softmax(lb_logits.astype(f32), axis=0)
    lb_all = jnp.maximum(jnp.cumsum(sm, axis=0) - sm[0:1], 0.0)

    tables = _rope_tables(Lp)
    consts_f = _hg_consts(False)
    consts_b = _hg_consts(True)
    tm_big = _pick_tile(T, (512, 384, 256, 128))
    tm_route = _pick_tile(T, (256, 128))

    for l in range(depth):
        w_l = w_in[l][:, perm].astype(bf16)
        proj = _inproj(h, g_mix[l].reshape(1, -1), w_l, tm_big, IN_W // 4)
        proj3 = proj.reshape(NB, Lp, IN_W)
        a_out = _attention(proj3, attn_sinks[l].astype(f32), tables)
        lb3 = lb_all[l].reshape(2, 1, -1)
        o_fwd, o_bwd = _hgrn(proj3, lb3, consts_f, consts_b)
        h = _merge(a_out.reshape(T, -1), o_fwd.reshape(T, -1), o_bwd.reshape(T, -1), hg_norm_g[l].reshape(1, -1),
                   proj, h, w_att_branch[l].astype(bf16), w_hg_branch[l].astype(bf16), w_out[l].astype(bf16),
                   tm_route)
        g_f = g_ffn[l].reshape(1, -1)
        keys = peer_sub_keys[l].reshape(2 * PEER_HEADS, PEER_NKEYS, PEER_QDIM // 2).astype(bf16)
        eidx_t, gate_t = _route(h, g_f, w_peer_q[l].astype(bf16), keys, tm_route)
        h = _experts(h, g_f, eidx_t.T, gate_t.T, _pack_bf16_pairs(peer_u[l]), _pack_bf16_pairs(peer_v[l]), Lp)

    h3 = h.reshape(NB, Lp, D_MODEL)
    g_fin = g_final.reshape(1, -1)
    return _final(h3, g_fin, 0, nb_p), _final(h3, g_fin, nb_p, nb_s)
```

```python
import functools

import numpy as np
import jax
import jax.numpy as jnp
from jax import lax
from jax.experimental import pallas as pl
from jax.experimental.pallas import tpu as pltpu
from jax.experimental.pallas import tpu_sc as plsc

D_MODEL = 1024
N_META = 16
N_Q_HEADS = 16
N_KV_HEADS = 4
HEAD_DIM = 64
WINDOW = 128
ROT_DIM = HEAD_DIM // 4
ROPE_THETA = 500000.0
HG_HEADS = 8
HG_DIM = 128
PEER_HEADS = 8
PEER_NKEYS = 128
PEER_QDIM = 256
PEER_TOPK = 16
EPS = 1e-6
NEG = -1e30

BLK = 128
PAD_ROWS = BLK - N_META
IN_W = 8704
COL_AQ, COL_GA, COL_GH, COL_HQ, COL_ZF, COL_ZB, COL_HI, COL_HGG, COL_AK, COL_AV = (
    0, 1024, 2048, 3072, 4096, 5120, 6144, 7168, 8192, 8448)

VMEM_LIMIT = 56 * 1024 * 1024
PEER_TD = 16
PEER_TS = 16
SC_WINDOW = 64
PEER_SLOTS = PEER_HEADS * PEER_TOPK
HG_HP = 8
PEER_UNROLL = 2
PEER_BATCH = 16
SUBLANES = 8


def _cparams(sem):
    return pltpu.CompilerParams(dimension_semantics=sem, vmem_limit_bytes=VMEM_LIMIT)


def _rms(x, g):
    return x * lax.rsqrt(jnp.mean(x * x, axis=-1, keepdims=True) + EPS) * g


def _nt(a, b):
    return lax.dot_general(a, b, (((1,), (1,)), ((), ())), preferred_element_type=jnp.float32)


def _tn(a, b):
    return lax.dot_general(a, b, (((0,), (0,)), ((), ())), preferred_element_type=jnp.float32)


def _inproj_kernel(x_ref, g_ref, w_ref, o_ref):
    n = _rms(x_ref[...], g_ref[...]).astype(jnp.bfloat16)
    o_ref[...] = jnp.dot(n, w_ref[...], preferred_element_type=jnp.float32)


def _inproj(h2d, g, w, tm, tn):
    T = h2d.shape[0]
    return pl.pallas_call(
        _inproj_kernel,
        grid=(IN_W // tn, T // tm),
        in_specs=[pl.BlockSpec((tm, D_MODEL), lambda j, i: (i, 0)),
                  pl.BlockSpec((1, D_MODEL), lambda j, i: (0, 0)),
                  pl.BlockSpec((D_MODEL, tn), lambda j, i: (0, j))],
        out_specs=pl.BlockSpec((tm, tn), lambda j, i: (i, j)),
        out_shape=jax.ShapeDtypeStruct((T, IN_W), jnp.float32),
        compiler_params=_cparams(("parallel", "parallel")),
        name="inproj",
    )(h2d, g, w)


def _rope_tables(Lp):
    half = ROT_DIM // 2
    pos = jnp.maximum(jnp.arange(Lp) - PAD_ROWS, 0)
    inv = jnp.power(jnp.float32(ROPE_THETA), -jnp.arange(half, dtype=jnp.float32) * 2.0 / ROT_DIM)
    ang = pos.astype(jnp.float32)[:, None] * inv[None, :]
    cos, sin = jnp.cos(ang), jnp.sin(ang)
    ones = jnp.ones((Lp, HEAD_DIM - ROT_DIM), jnp.float32)
    zeros = jnp.zeros((Lp, HEAD_DIM - ROT_DIM), jnp.float32)
    zh = jnp.zeros((Lp, half), jnp.float32)
    c = jnp.concatenate([cos, cos, ones], axis=1)
    s1 = jnp.concatenate([zh, sin, zeros], axis=1)
    s2 = jnp.concatenate([-sin, zh, zeros], axis=1)
    rep = BLK // HEAD_DIM
    return tuple(jnp.tile(t, (1, rep)) for t in (c, s1, s2))


def _attn_kernel(sink_ref, q_ref, kg_ref, k0_ref, k1_ref, k2_ref, vg_ref, v0_ref, v1_ref, v2_ref,
                 c_ref, s1_ref, s2_ref, o_ref):
    r = pl.program_id(1)
    nblk = pl.num_programs(1)
    half = ROT_DIM // 2

    def rope(x, blk):
        row0 = blk * BLK if isinstance(blk, int) else pl.multiple_of(blk * BLK, BLK)
        c = c_ref[pl.ds(row0, BLK), :]
        s1 = s1_ref[pl.ds(row0, BLK), :]
        s2 = s2_ref[pl.ds(row0, BLK), :]
        outs = []
        for j in range(x.shape[1] // BLK):
            xj = x[:, j * BLK:(j + 1) * BLK]
            outs.append(xj * c + pltpu.roll(xj, half, 1) * s1 + pltpu.roll(xj, BLK - half, 1) * s2)
        return jnp.concatenate(outs, axis=1)

    kb = [jnp.clip(r - 1 + j, 0, nblk - 1) for j in range(3)]
    q = rope(q_ref[...], r) * (HEAD_DIM ** -0.5)
    kcat = jnp.concatenate([rope(kg_ref[...], 0), rope(k0_ref[...], kb[0]), rope(k1_ref[...], kb[1]),
                            rope(k2_ref[...], kb[2])], axis=0).astype(jnp.bfloat16)
    vcat = jnp.concatenate([vg_ref[...], v0_ref[...], v1_ref[...], v2_ref[...]], axis=0).astype(jnp.bfloat16)

    qrow = lax.broadcasted_iota(jnp.int32, (BLK, BLK), 0)
    kcol = lax.broadcasted_iota(jnp.int32, (BLK, BLK), 1)
    masks = [kcol >= PAD_ROWS]
    for j in range(3):
        kblk = r - 1 + j
        ok_blk = (kblk >= 1) & (kblk <= nblk - 1)
        masks.append((jnp.abs((j - 1) * BLK + kcol - qrow) <= WINDOW) & ok_blk)
    mask = jnp.concatenate(masks, axis=1)

    G = N_Q_HEADS // N_KV_HEADS
    lane_half = kcol // HEAD_DIM
    pairs_per_slice = (BLK // HEAD_DIM) * G // 2
    outs = [None] * (N_Q_HEADS // 2)
    for m2 in range(N_KV_HEADS * HEAD_DIM // BLK):
        k2 = kcat[:, m2 * BLK:(m2 + 1) * BLK]
        v2 = vcat[:, m2 * BLK:(m2 + 1) * BLK]
        qs, sks, halves = [], [], []
        for j in range(m2 * pairs_per_slice, (m2 + 1) * pairs_per_slice):
            hk = ((2 * j) // G) % 2
            in_half = lane_half == hk
            q2 = q[:, j * BLK:(j + 1) * BLK]
            qs += [jnp.where(in_half, q2, 0.0), jnp.where(in_half, pltpu.roll(q2, HEAD_DIM, 1), 0.0)]
            sks += [jnp.full((BLK, 1), sink_ref[2 * j + hk]), jnp.full((BLK, 1), sink_ref[2 * j + 1 - hk])]
            halves.append(in_half)
        nq = len(qs)
        sk = jnp.concatenate(sks, axis=0)
        s = _nt(jnp.concatenate(qs, axis=0).astype(jnp.bfloat16), k2)
        s = jnp.where(jnp.tile(mask, (nq, 1)), s, NEG)
        m = jnp.maximum(jnp.max(s, axis=-1, keepdims=True), sk)
        p = jnp.exp(s - m)
        p = p / (jnp.sum(p, axis=-1, keepdims=True) + jnp.exp(sk - m))
        o = jnp.dot(p.astype(jnp.bfloat16), v2, preferred_element_type=jnp.float32)
        for jj, in_half in enumerate(halves):
            o_same = o[(2 * jj) * BLK:(2 * jj + 1) * BLK]
            o_other = o[(2 * jj + 1) * BLK:(2 * jj + 2) * BLK]
            outs[m2 * pairs_per_slice + jj] = jnp.where(in_half, o_same, pltpu.roll(o_other, HEAD_DIM, 1))
    o_ref[...] = jnp.concatenate(outs, axis=1).astype(o_ref.dtype)


def _attention(proj3, sinks, tables):
    NB, Lp, _ = proj3.shape
    nblk = Lp // BLK
    kcol, vcol = COL_AK // 256, COL_AV // 256

    def loc(j, col):
        return pl.BlockSpec((None, BLK, 256),
                            lambda b, r: (b, jnp.clip(r - 1 + j, 0, nblk - 1), col))

    glob = lambda col: pl.BlockSpec((None, BLK, 256), lambda b, r: (b, 0, col))
    tab = pl.BlockSpec((Lp, BLK), lambda b, r: (0, 0))
    return pl.pallas_call(
        _attn_kernel,
        grid=(NB, nblk),
        in_specs=[pl.BlockSpec(memory_space=pltpu.SMEM),
                  pl.BlockSpec((None, BLK, 1024), lambda b, r: (b, r, 0)),
                  glob(kcol), loc(0, kcol), loc(1, kcol), loc(2, kcol),
                  glob(vcol), loc(0, vcol), loc(1, vcol), loc(2, vcol),
                  tab, tab, tab],
        out_specs=pl.BlockSpec((None, BLK, 1024), lambda b, r: (b, r, 0)),
        out_shape=jax.ShapeDtypeStruct((NB, Lp, 1024), jnp.bfloat16),
        compiler_params=_cparams(("parallel", "arbitrary")),
        name="window_attn",
    )(sinks, proj3, proj3, proj3, proj3, proj3, proj3, proj3, proj3, proj3, *tables)


HG_ROWS = BLK
_HG_LEVELS = (64, 32, 16, 8, 4, 2, 1)


def _hg_consts(reverse):
    C = HG_ROWS
    t = np.arange(C)[:, None]
    u = np.arange(C)[None, :]
    mats = [u <= t, u > t]
    ups, msks = [], []
    for B in _HG_LEVELS:
        same = (t // (2 * B)) == (u // (2 * B))
        tin, uin = t % (2 * B), u % (2 * B)
        upper = tin >= B
        wq = upper & same & (uin >= B) & (uin <= tin)
        wk = (~upper) & same & (uin > tin) & (uin <= B - 1)
        mats.append(wq | wk)
        ups.append(np.broadcast_to(upper, (C, HG_DIM)))
        msks.append(same & upper & (uin < B))
    msks.append(np.eye(C, dtype=bool))
    if reverse:
        mats = [m[::-1, ::-1] for m in mats]
        ups = [m[::-1] for m in ups]
        msks = [m[::-1, ::-1] for m in msks]
    W = np.concatenate(mats, axis=0).astype(np.float32)
    W2 = np.concatenate([W, W], axis=1)
    return (jnp.asarray(W2, jnp.bfloat16), jnp.asarray(np.stack(ups).astype(np.float32)),
            jnp.asarray(np.stack(msks).astype(np.float32)))


def _hg_chunk(q_raw, z, v, valid, lb, w2, up_ref, msk_ref, st, reverse):
    C = HG_ROWS
    f32, bf16 = jnp.float32, jnp.bfloat16
    log_lb = jnp.log(lb)
    log1m = jnp.log1p(-lb)
    logsig = jnp.minimum(z, 0.0) - jnp.log1p(jnp.exp(-jnp.abs(z)))
    a, b = log_lb, log1m + logsig
    g = jnp.maximum(a, b) + jnp.log1p(jnp.exp(-jnp.abs(a - b)))
    k = (1.0 - lb) * jax.nn.sigmoid(-z)
    q = q_raw * jax.nn.sigmoid(q_raw)
    q = jnp.where(valid, q, 0.0)
    k = jnp.where(valid, k, 0.0)
    v = jnp.where(valid, v, 0.0)
    g = jnp.where(valid, g, 0.0)
    g1 = g.astype(bf16)
    g2 = (g - g1.astype(f32)).astype(bf16)
    E = jnp.dot(w2, jnp.concatenate([g1, g2], axis=0), preferred_element_type=f32)
    b_inc = E[0:C]
    b_sfx = E[C:2 * C]
    b_tot = b_inc[0:1] if reverse else b_inc[C - 1:C]

    A = msk_ref[len(_HG_LEVELS)] * jnp.sum(q * k, axis=-1, keepdims=True)
    for l in range(len(_HG_LEVELS)):
        ex = jnp.exp(E[(2 + l) * C:(3 + l) * C])
        up = up_ref[l] > 0.5
        qs = jnp.where(up, q * ex, 0.0).astype(bf16)
        ks = jnp.where(up, 0.0, k * ex).astype(bf16)
        A = A + _nt(qs, ks) * msk_ref[l]

    vb = v.astype(bf16)
    o = _nt((q * jnp.exp(b_inc)).astype(bf16), st.astype(bf16))
    o = o + jnp.dot(A.astype(bf16), vb, preferred_element_type=f32)
    st = st * jnp.exp(b_tot) + _tn(vb, (k * jnp.exp(b_sfx)).astype(bf16))
    return o, st


def _hg_kernel(qf_ref, zf_ref, vf_ref, qb_ref, zb_ref, vb_ref, lb_ref, w2f_ref, upf_ref, mskf_ref,
               w2b_ref, upb_ref, mskb_ref, of_ref, ob_ref, st_ref):
    i = pl.program_id(2)
    nblk = pl.num_programs(2)

    @pl.when(i == 0)
    def _():
        st_ref[...] = jnp.zeros_like(st_ref)

    rows = lax.broadcasted_iota(jnp.int32, (HG_ROWS, HG_DIM), 0)
    dirs = ((False, qf_ref, zf_ref, vf_ref, w2f_ref, upf_ref, mskf_ref, of_ref, i),
            (True, qb_ref, zb_ref, vb_ref, w2b_ref, upb_ref, mskb_ref, ob_ref, nblk - 1 - i))
    for d, (reverse, q_ref, z_ref, v_ref, w2_ref, up_ref, msk_ref, o_ref, blk) in enumerate(dirs):
        w2 = w2_ref[...]
        for hh in range(HG_HP):
            cols = slice(hh * HG_DIM, (hh + 1) * HG_DIM)
            lb = lb_ref[d, :, cols]
            valid = (rows >= PAD_ROWS) | (blk > 0)
            o, st = _hg_chunk(q_ref[:, cols], z_ref[:, cols], v_ref[:, cols], valid, lb, w2,
                              up_ref, msk_ref, st_ref[d, hh], reverse)
            st_ref[d, hh] = st
            o_ref[:, cols] = o


def _hgrn(proj3, lb3, consts_f, consts_b):
    NB, Lp, _ = proj3.shape
    nblk = Lp // BLK
    W = HG_DIM * HG_HP

    def col(base, reverse):
        return pl.BlockSpec((None, BLK, W),
                            lambda b, h, i: (b, (nblk - 1 - i) if reverse else i, base // W + h))

    def const(c):
        return pl.BlockSpec(c.shape, lambda b, h, i: (0,) * c.ndim)

    in_specs = [col(COL_HQ, False), col(COL_ZF, False), col(COL_HI, False),
                col(COL_HQ, True), col(COL_ZB, True), col(COL_HI, True),
                pl.BlockSpec((2, 1, W), lambda b, h, i: (0, 0, h))]
    in_specs += [const(c) for c in consts_f] + [const(c) for c in consts_b]
    oshape = jax.ShapeDtypeStruct((NB, Lp, HG_HEADS * HG_DIM), jnp.float32)
    return pl.pallas_call(
        _hg_kernel,
        grid=(NB, HG_HEADS // HG_HP, nblk),
        in_specs=in_specs,
        out_specs=(pl.BlockSpec((None, BLK, W), lambda b, h, i: (b, i, h)),
                   pl.BlockSpec((None, BLK, W), lambda b, h, i: (b, nblk - 1 - i, h))),
        out_shape=(oshape, oshape),
        scratch_shapes=[pltpu.VMEM((2, HG_HP, HG_DIM, HG_DIM), jnp.float32)],
        compiler_params=_cparams(("parallel", "parallel", "arbitrary")),
        name="hgrn_bidir",
    )(proj3, proj3, proj3, proj3, proj3, proj3, lb3, *consts_f, *consts_b)


def _merge_kernel(a_ref, of_ref, ob_ref, hgg_ref, ng_ref, ga_ref, gh_ref, h_ref, wa_ref, wh_ref, wo_ref, o_ref):
    f32, bf16 = jnp.float32, jnp.bfloat16
    o = of_ref[...] + ob_ref[...]
    ng = ng_ref[...]
    gt = hgg_ref[...]
    hh = jnp.concatenate([_rms(o[:, c:c + HG_DIM], ng[:, c:c + HG_DIM]) for c in range(0, HG_HEADS * HG_DIM, HG_DIM)],
                         axis=1) * (gt * jax.nn.sigmoid(gt))
    pa = jnp.dot(a_ref[...], wa_ref[...], preferred_element_type=f32)
    ph = jnp.dot(hh.astype(bf16), wh_ref[...], preferred_element_type=f32)
    merged = jax.nn.sigmoid(ga_ref[...]) * pa + jax.nn.sigmoid(gh_ref[...]) * ph
    o_ref[...] = h_ref[...] + jnp.dot(merged.astype(bf16), wo_ref[...], preferred_element_type=f32)


def _merge(a2d, of2d, ob2d, ng, proj2d, h2d, wa, wh, wo, tm):
    T = h2d.shape[0]
    row = lambda c: pl.BlockSpec((tm, D_MODEL), lambda i: (i, c))
    wspec = pl.BlockSpec((D_MODEL, D_MODEL), lambda i: (0, 0))
    return pl.pallas_call(
        _merge_kernel,
        grid=(T // tm,),
        in_specs=[row(0), row(0), row(0), row(COL_HGG // D_MODEL), pl.BlockSpec((1, D_MODEL), lambda i: (0, 0)),
                  row(COL_GA // D_MODEL), row(COL_GH // D_MODEL), row(0), wspec, wspec, wspec],
        out_specs=row(0),
        out_shape=jax.ShapeDtypeStruct((T, D_MODEL), jnp.float32),
        input_output_aliases={7: 0},
        compiler_params=_cparams(("parallel",)),
        name="merge_out",
    )(a2d, of2d, ob2d, proj2d, ng, proj2d, proj2d, h2d, wa, wh, wo)


def _topk_axis0(s, kk):
    n = s.shape[0]
    iota = lax.broadcasted_iota(jnp.int32, s.shape, 0)
    vals, ids = [], []
    for _ in range(kk):
        m = jnp.max(s, axis=0, keepdims=True)
        am = jnp.min(jnp.where(s == m, iota, n), axis=0, keepdims=True)
        s = jnp.where(iota == am, -jnp.inf, s)
        vals.append(m)
        ids.append(am)
    return vals, ids


def _stack_rows(rows, dtype):
    n = len(rows)
    iota = lax.broadcasted_iota(jnp.int32, (n, rows[0].shape[1]), 0)
    out = jnp.zeros((n, rows[0].shape[1]), dtype)
    for j, rj in enumerate(rows):
        out = jnp.where(iota == j, rj, out)
    return out


def _route_kernel(h_ref, g_ref, wq_ref, key_ref, eidx_ref, gate_ref, xn_ref):
    @pl.when(pl.program_id(1) == 0)
    def _():
        xn_ref[...] = _rms(h_ref[...], g_ref[...]).astype(jnp.bfloat16)

    K = PEER_TOPK
    half = PEER_QDIM // 2
    i32 = jnp.int32
    qry = jnp.dot(xn_ref[...], wq_ref[...], preferred_element_type=jnp.float32).astype(jnp.bfloat16)
    s0 = _nt(key_ref[0], qry[:, :half])
    s1 = _nt(key_ref[1], qry[:, half:])
    tm = s0.shape[1]
    v0, i0 = _topk_axis0(s0, K)
    v1, i1 = _topk_axis0(s1, K)
    m1 = _stack_rows(v1, jnp.float32)
    m0_hi = _stack_rows(v0[K // 2:], jnp.float32)
    r8 = lax.broadcasted_iota(i32, (K // 2, tm), 0)
    cand = [v0[0] + m1] + [v0[a] + m1[:K // 2] for a in range(1, K // 2)] + [m0_hi + v1[0]]
    flat = [lax.broadcasted_iota(i32, (K, tm), 0)] + [a * K + r8 for a in range(1, K // 2)] + [(K // 2 + r8) * K]
    cand = jnp.concatenate(cand, axis=0)
    flat = jnp.concatenate(flat, axis=0)
    big = K * K
    ts, tp = [], []
    for _ in range(K):
        m = jnp.max(cand, axis=0, keepdims=True)
        p = jnp.min(jnp.where(cand == m, flat, big), axis=0, keepdims=True)
        cand = jnp.where(flat == p, -jnp.inf, cand)
        ts.append(m)
        tp.append(p)
    pk = _stack_rows(tp, i32)
    pa, pb = pk // K, pk % K
    e0 = jnp.zeros((K, tm), i32)
    e1 = jnp.zeros((K, tm), i32)
    for a in range(K):
        e0 = jnp.where(pa == a, i0[a], e0)
        e1 = jnp.where(pb == a, i1[a], e1)
    ex = [jnp.exp(t - ts[0]) for t in ts]
    den = ex[0]
    for e in ex[1:]:
        den = den + e
    eidx_ref[...] = e0 * PEER_NKEYS + e1
    gate_ref[...] = _stack_rows([e / den for e in ex], jnp.float32)


def _route(h2d, g, wq, keys, tm):
    T = h2d.shape[0]
    out = jax.ShapeDtypeStruct((PEER_SLOTS, T), jnp.int32), jax.ShapeDtypeStruct((PEER_SLOTS, T), jnp.float32)
    ospec = pl.BlockSpec((PEER_TOPK, tm), lambda i, h: (h, i))
    return pl.pallas_call(
        _route_kernel,
        grid=(T // tm, PEER_HEADS),
        in_specs=[pl.BlockSpec((tm, D_MODEL), lambda i, h: (i, 0)),
                  pl.BlockSpec((1, D_MODEL), lambda i, h: (0, 0)),
                  pl.BlockSpec((D_MODEL, PEER_QDIM), lambda i, h: (0, h)),
                  pl.BlockSpec((2, PEER_NKEYS, PEER_QDIM // 2), lambda i, h: (h, 0, 0))],
        out_specs=(ospec, ospec),
        out_shape=out,
        scratch_shapes=[pltpu.VMEM((tm, D_MODEL), jnp.bfloat16)],
        compiler_params=_cparams(("parallel", "arbitrary")),
        name="peer_route",
    )(h2d, g, wq, keys)


def _pack_bf16_pairs(x):
    e, d = x.shape
    bits = lax.bitcast_convert_type(x.astype(jnp.bfloat16), jnp.uint16).astype(jnp.uint32)
    bits = bits.reshape(e, d // 256, 2, 128)
    return bits[:, :, 0, :] | (bits[:, :, 1, :] << 16)


def _sc_gather(uv, idx):
    M = idx.shape[0]
    info = plsc.get_sparse_core_info()
    nw = info.num_cores * info.num_subcores
    per = M // nw
    assert per * nw == M and per % SC_WINDOW == 0
    mesh = plsc.VectorSubcoreMesh(core_axis_name="c", subcore_axis_name="s")
    oshape = jax.ShapeDtypeStruct((M,) + uv.shape[2:], uv.dtype)

    @functools.partial(
        pl.kernel, mesh=mesh, out_type=(oshape, oshape),
        scratch_types=[pltpu.VMEM((SC_WINDOW,), jnp.int32), pltpu.VMEM((SC_WINDOW,) + uv.shape[1:], uv.dtype),
                       pltpu.SemaphoreType.DMA],
    )
    def gather(uv_hbm, idx_hbm, ou_hbm, ov_hbm, idx_v, rows_v, sem):
        wid = lax.axis_index("s") * info.num_cores + lax.axis_index("c")
        base = wid * per

        @pl.loop(0, per // SC_WINDOW)
        def _(g):
            off = pl.multiple_of(base + g * SC_WINDOW, SC_WINDOW)
            pltpu.sync_copy(idx_hbm.at[pl.ds(off, SC_WINDOW)], idx_v)
            pltpu.async_copy(uv_hbm.at[idx_v], rows_v, sem).wait()
            pltpu.sync_copy(rows_v.at[:, 0], ou_hbm.at[pl.ds(off, SC_WINDOW)])
            pltpu.sync_copy(rows_v.at[:, 1], ov_hbm.at[pl.ds(off, SC_WINDOW)])

    return gather(uv, idx)


def _expert_kernel(idx_ref, idxn_ref, h_ref, g_ref, gate_ref, spread_ref, fold_ref, su_ref, sv_ref, uv_hbm, o_ref, buf,
                   z_ref, xn_ref, y_ref, sem, *, step0, seq_steps, pad_steps):
    TD, TS, NS = PEER_TD, PEER_TS, PEER_SLOTS
    TT, LW = TD + TS, SUBLANES * PEER_SLOTS
    PR = SUBLANES // 2
    f32, bf16 = jnp.float32, jnp.bfloat16
    i = pl.program_id(0)
    n = pl.num_programs(0)
    slot = lax.rem(i, 2)
    live = lax.rem(step0 + i, seq_steps) >= pad_steps
    live_next = (i + 1 < n) & (lax.rem(step0 + i + 1, seq_steps) >= pad_steps)
    dead = jnp.logical_not(live)

    def issue_token(iref, sl, t):
        for k in range(NS):
            pltpu.make_async_copy(uv_hbm.at[iref[t, k]], buf.at[sl, :, pl.ds((t * NS + k) * PR, PR)],
                                  sem.at[sl]).start(priority=k % 2)

    def issue_all(iref, sl):
        def body(t, c):
            issue_token(iref, sl, t)
            return c
        lax.fori_loop(0, TD, body, 0)

    @pl.when((i == 0) & live)
    def _():
        issue_all(idx_ref, 0)

    @pl.when(live)
    def _():
        pltpu.make_async_copy(buf.at[slot], buf.at[slot], sem.at[slot]).wait()

    @pl.when(dead & live_next)
    def _():
        issue_all(idxn_ref, 1 - slot)

    @pl.when(dead)
    def _():
        o_ref[...] = h_ref[...]

    @pl.when(live)
    def _():
        _expert_step(idxn_ref, h_ref, g_ref, gate_ref, spread_ref, fold_ref, su_ref, sv_ref, o_ref, buf, z_ref, xn_ref,
                     y_ref, slot, live_next, issue_token)


def _expert_step(idxn_ref, h_ref, g_ref, gate_ref, spread_ref, fold_ref, su_ref, sv_ref, o_ref, buf, z_ref, xn_ref,
                 y_ref, slot, live_next, issue_token):
    TD, TS, NS = PEER_TD, PEER_TS, PEER_SLOTS
    TT, LW = TD + TS, SUBLANES * PEER_SLOTS
    PR = SUBLANES // 2
    f32, bf16 = jnp.float32, jnp.bfloat16

    x = h_ref[...]
    xn_ref[...] = _rms(x, g_ref[...]).reshape(TT, SUBLANES, 128)
    on_diag = (lax.broadcasted_iota(jnp.int32, (SUBLANES, LW), 1) % SUBLANES
               == lax.broadcasted_iota(jnp.int32, (SUBLANES, LW), 0))

    def pieces_of(t, staged, part):
        r = pl.ds(pl.multiple_of(t * (NS * PR), NS * PR), NS * PR)
        words = (sv_ref if part else su_ref)[r, :] if staged else buf[slot, part, r, :]
        return pltpu.bitcast(words, bf16)

    def score_token(t, staged):
        tg = t + TD if staged else t
        z = _nt(xn_ref[tg].astype(bf16), pieces_of(t, staged, 0))
        z_ref[pl.ds(tg, 1), :] = jnp.sum(jnp.where(on_diag, z, 0.0), axis=0, keepdims=True)

    def mix_token(t, staged):
        tg = t + TD if staged else t
        w = jnp.where(on_diag, jnp.broadcast_to(z_ref[pl.ds(tg, 1), :], (SUBLANES, LW)), 0.0).astype(bf16)
        y_ref[tg] = jnp.dot(w, pieces_of(t, staged, 1), preferred_element_type=f32)

    def batched(count, fn):
        def body(tt, c):
            for j in range(PEER_BATCH):
                fn(tt * PEER_BATCH + j)
            return c
        lax.fori_loop(0, count // PEER_BATCH, body, 0)

    @pl.when(live_next)
    def _():
        def body(tt, c):
            for j in range(PEER_UNROLL):
                score_token(tt * PEER_UNROLL + j, False)
            for j in range(PEER_UNROLL):
                issue_token(idxn_ref, 1 - slot, tt * PEER_UNROLL + j)
            return c
        lax.fori_loop(0, TD // PEER_UNROLL, body, 0)

    @pl.when(jnp.logical_not(live_next))
    def _():
        batched(TD, lambda t: score_token(t, False))

    batched(TS, lambda t: score_token(t, True))

    z = z_ref[...]
    z1 = z.astype(bf16)
    r1 = z - z1.astype(f32)
    z2 = r1.astype(bf16)
    z3 = (r1 - z2.astype(f32)).astype(bf16)
    fold = fold_ref[...]
    sc = (jnp.dot(z1, fold, preferred_element_type=f32) + jnp.dot(z2, fold, preferred_element_type=f32)
          + jnp.dot(z3, fold, preferred_element_type=f32))
    w = gate_ref[...] * (0.5 * sc * (1.0 + lax.erf(sc * (2.0 ** -0.5))))
    z_ref[...] = jnp.dot(w.astype(bf16), spread_ref[...], preferred_element_type=f32)

    batched(TD, lambda t: mix_token(t, False))
    batched(TS, lambda t: mix_token(t, True))
    o_ref[...] = x + y_ref[...].reshape(TT, D_MODEL)


def _experts_chunk(step0, nsteps, seq_steps, idx_d, h2d, g, gate, spread, su, sv, uv):
    TD, TS, NS = PEER_TD, PEER_TS, PEER_SLOTS
    TT, PR = TD + TS, SUBLANES // 2
    last = step0 + nsteps - 1
    staged = pl.BlockSpec((TS * NS * PR, 128), lambda i: (i, 0))
    return pl.pallas_call(
        functools.partial(_expert_kernel, step0=step0, seq_steps=seq_steps, pad_steps=PAD_ROWS // TT),
        grid=(nsteps,),
        in_specs=[pl.BlockSpec((TD, NS), lambda i: (step0 + i, 0), memory_space=pltpu.SMEM),
                  pl.BlockSpec((TD, NS), lambda i: (jnp.minimum(step0 + i + 1, last), 0), memory_space=pltpu.SMEM),
                  pl.BlockSpec((TT, D_MODEL), lambda i: (step0 + i, 0)),
                  pl.BlockSpec((1, D_MODEL), lambda i: (0, 0)),
                  pl.BlockSpec((TT, NS), lambda i: (step0 + i, 0)),
                  pl.BlockSpec((NS, SUBLANES * NS), lambda i: (0, 0)),
                  pl.BlockSpec((SUBLANES * NS, NS), lambda i: (0, 0)),
                  staged, staged,
                  pl.BlockSpec(memory_space=pl.ANY)],
        out_specs=pl.BlockSpec((TT, D_MODEL), lambda i: (step0 + i, 0)),
        out_shape=jax.ShapeDtypeStruct(h2d.shape, jnp.float32),
        scratch_shapes=[pltpu.VMEM((2, 2, TD * NS * PR, 128), jnp.uint32),
                        pltpu.VMEM((TT, SUBLANES * NS), jnp.float32),
                        pltpu.VMEM((TT, SUBLANES, 128), jnp.float32),
                        pltpu.VMEM((TT, SUBLANES, 128), jnp.float32),
                        pltpu.SemaphoreType.DMA((2,))],
        input_output_aliases={2: 0},
        compiler_params=_cparams(("arbitrary",)),
        name="peer_experts",
    )(idx_d, idx_d, h2d, g, gate, spread, spread.T, su, sv, uv)


def _experts(h2d, g, eidx, gate, u4, v4, seq_rows):
    T = h2d.shape[0]
    TD, TS, NS = PEER_TD, PEER_TS, PEER_SLOTS
    TT = TD + TS
    assert seq_rows % TT == 0
    steps = T // TT
    chunks = _pick_tile(steps, (24, 12, 8, 6, 4, 3, 2, 1))
    spc = steps // chunks
    e3 = eidx.reshape(steps, TT, NS)
    idx_d = e3[:, :TD].reshape(steps * TD, NS)
    idx_s = e3[:, TD:].reshape(chunks, spc * TS * NS)
    spread = jnp.asarray(np.repeat(np.eye(NS, dtype=np.float32), SUBLANES, axis=1), jnp.bfloat16)
    uv = jnp.stack([u4, v4], axis=1)
    for c in range(chunks):
        su, sv = _sc_gather(uv, idx_s[c])
        su, sv = (a.reshape(-1, 128) for a in (su, sv))
        h2d = _experts_chunk(c * spc, spc, seq_rows // TT, idx_d, h2d, g, gate, spread, su, sv, uv)
    return h2d


def _final_kernel(h_ref, g_ref, o_ref):
    o_ref[...] = _rms(h_ref[...], g_ref[...])


def _final(h3, g, b0, nb):
    _, Lp, _ = h3.shape
    S = Lp - BLK
    return pl.pallas_call(
        _final_kernel,
        grid=(nb, S // BLK),
        in_specs=[pl.BlockSpec((None, BLK, D_MODEL), lambda b, r: (b + b0, r + 1, 0)),
                  pl.BlockSpec((1, D_MODEL), lambda b, r: (0, 0))],
        out_specs=pl.BlockSpec((None, BLK, D_MODEL), lambda b, r: (b, r, 0)),
        out_shape=jax.ShapeDtypeStruct((nb, S, D_MODEL), jnp.float32),
        compiler_params=_cparams(("parallel", "parallel")),
        name="final_norm",
    )(h3, g)


def _pick_tile(T, candidates):
    for c in candidates:
        if T % c == 0:
            return c
    raise ValueError(f"no tile for {T}")


def kernel(x_prompt, x_sample, meta_tokens, g_mix, w_in, attn_sinks, lb_logits, hg_norm_g, w_att_branch,
           w_hg_branch, w_out, g_ffn, w_peer_q, peer_sub_keys, peer_u, peer_v, g_final):
    f32, bf16 = jnp.float32, jnp.bfloat16
    depth = w_in.shape[0]
    nb_p, S, _ = x_prompt.shape
    nb_s = x_sample.shape[0]
    assert x_sample.shape[1] == S and S % BLK == 0
    NB = nb_p + nb_s
    Lp = S + BLK
    T = NB * Lp

    x = jnp.concatenate([x_prompt, x_sample], axis=0)
    head = jnp.concatenate([jnp.zeros((PAD_ROWS, D_MODEL), f32), meta_tokens.astype(f32)], axis=0)
    h = jnp.concatenate([jnp.broadcast_to(head[None], (NB, BLK, D_MODEL)), x], axis=1).reshape(T, D_MODEL)

    sizes = (1024, 256, 256, 1024, 1024, 1024, 1024, 1024, 1024, 1024)
    offs = np.concatenate([[0], np.cumsum(sizes)])
    order = (0, 8, 9, 3, 4, 5, 6, 7, 1, 2)
    perm = np.concatenate([np.arange(offs[k], offs[k + 1]) for k in order])

    sm = jax.nn.softmax(lb_logits.astype(f32), axis=0)
    lb_all = jnp.maximum(jnp.cumsum(sm, axis=0) - sm[0:1], 0.0)

    tables = _rope_tables(Lp)
    consts_f = _hg_consts(False)
    consts_b = _hg_consts(True)
    tm_big = _pick_tile(T, (512, 384, 256, 128))
    tm_route = _pick_tile(T, (256, 128))

    for l in range(depth):
        w_l = w_in[l][:, perm].astype(bf16)
        proj = _inproj(h, g_mix[l].reshape(1, -1), w_l, tm_big, IN_W // 4)
        proj3 = proj.reshape(NB, Lp, IN_W)
        a_out = _attention(proj3, attn_sinks[l].astype(f32), tables)
        lb3 = lb_all[l].reshape(2, 1, -1)
        o_fwd, o_bwd = _hgrn(proj3, lb3, consts_f, consts_b)
        h = _merge(a_out.reshape(T, -1), o_fwd.reshape(T, -1), o_bwd.reshape(T, -1), hg_norm_g[l].reshape(1, -1),
                   proj, h, w_att_branch[l].astype(bf16), w_hg_branch[l].astype(bf16), w_out[l].astype(bf16),
                   tm_route)
        g_f = g_ffn[l].reshape(1, -1)
        keys = peer_sub_keys[l].reshape(2 * PEER_HEADS, PEER_NKEYS, PEER_QDIM // 2).astype(bf16)
        eidx_t, gate_t = _route(h, g_f, w_peer_q[l].astype(bf16), keys, tm_route)
        h = _experts(h, g_f, eidx_t.T, gate_t.T, _pack_bf16_pairs(peer_u[l]), _pack_bf16_pairs(peer_v[l]), Lp)

    h3 = h.reshape(NB, Lp, D_MODEL)
    g_fin = g_final.reshape(1, -1)
    return _final(h3, g_fin, 0, nb_p), _final(h3, g_fin, nb_p, nb_s)
```

```python
import functools

import numpy as np
import jax
import jax.numpy as jnp
from jax import lax
from jax.experimental import pallas as pl
from jax.experimental.pallas import tpu as pltpu
from jax.experimental.pallas import tpu_sc as plsc

D_MODEL = 1024
N_META = 16
N_Q_HEADS = 16
N_KV_HEADS = 4
HEAD_DIM = 64
WINDOW = 128
ROT_DIM = HEAD_DIM // 4
ROPE_THETA = 500000.0
HG_HEADS = 8
HG_DIM = 128
PEER_HEADS = 8
PEER_NKEYS = 128
PEER_QDIM = 256
PEER_TOPK = 16
EPS = 1e-6
NEG = -1e30

BLK = 128
PAD_ROWS = BLK - N_META
IN_W = 8704
COL_AQ, COL_GA, COL_GH, COL_HQ, COL_ZF, COL_ZB, COL_HI, COL_HGG, COL_AK, COL_AV = (
    0, 1024, 2048, 3072, 4096, 5120, 6144, 7168, 8192, 8448)

VMEM_LIMIT = 56 * 1024 * 1024
PEER_TD = 16
PEER_TS = 16
SC_WINDOW = 64
PEER_SLOTS = PEER_HEADS * PEER_TOPK
HG_HP = 8
PEER_UNROLL = 8
PEER_BATCH = 16
SUBLANES = 8


def _cparams(sem):
    return pltpu.CompilerParams(dimension_semantics=sem, vmem_limit_bytes=VMEM_LIMIT)


def _rms(x, g):
    return x * lax.rsqrt(jnp.mean(x * x, axis=-1, keepdims=True) + EPS) * g


def _nt(a, b):
    return lax.dot_general(a, b, (((1,), (1,)), ((), ())), preferred_element_type=jnp.float32)


def _tn(a, b):
    return lax.dot_general(a, b, (((0,), (0,)), ((), ())), preferred_element_type=jnp.float32)


def _inproj_kernel(x_ref, g_ref, w_ref, o_ref):
    n = _rms(x_ref[...], g_ref[...]).astype(jnp.bfloat16)
    o_ref[...] = jnp.dot(n, w_ref[...], preferred_element_type=jnp.float32)


def _inproj(h2d, g, w, tm, tn):
    T = h2d.shape[0]
    return pl.pallas_call(
        _inproj_kernel,
        grid=(IN_W // tn, T // tm),
        in_specs=[pl.BlockSpec((tm, D_MODEL), lambda j, i: (i, 0)),
                  pl.BlockSpec((1, D_MODEL), lambda j, i: (0, 0)),
                  pl.BlockSpec((D_MODEL, tn), lambda j, i: (0, j))],
        out_specs=pl.BlockSpec((tm, tn), lambda j, i: (i, j)),
        out_shape=jax.ShapeDtypeStruct((T, IN_W), jnp.float32),
        compiler_params=_cparams(("parallel", "parallel")),
        name="inproj",
    )(h2d, g, w)


def _rope_tables(Lp):
    half = ROT_DIM // 2
    pos = jnp.maximum(jnp.arange(Lp) - PAD_ROWS, 0)
    inv = jnp.power(jnp.float32(ROPE_THETA), -jnp.arange(half, dtype=jnp.float32) * 2.0 / ROT_DIM)
    ang = pos.astype(jnp.float32)[:, None] * inv[None, :]
    cos, sin = jnp.cos(ang), jnp.sin(ang)
    ones = jnp.ones((Lp, HEAD_DIM - ROT_DIM), jnp.float32)
    zeros = jnp.zeros((Lp, HEAD_DIM - ROT_DIM), jnp.float32)
    zh = jnp.zeros((Lp, half), jnp.float32)
    c = jnp.concatenate([cos, cos, ones], axis=1)
    s1 = jnp.concatenate([zh, sin, zeros], axis=1)
    s2 = jnp.concatenate([-sin, zh, zeros], axis=1)
    rep = BLK // HEAD_DIM
    return tuple(jnp.tile(t, (1, rep)) for t in (c, s1, s2))


def _attn_kernel(sink_ref, q_ref, kg_ref, k0_ref, k1_ref, k2_ref, vg_ref, v0_ref, v1_ref, v2_ref,
                 c_ref, s1_ref, s2_ref, o_ref):
    r = pl.program_id(1)
    nblk = pl.num_programs(1)
    half = ROT_DIM // 2

    def rope(x, blk):
        row0 = blk * BLK if isinstance(blk, int) else pl.multiple_of(blk * BLK, BLK)
        c = c_ref[pl.ds(row0, BLK), :]
        s1 = s1_ref[pl.ds(row0, BLK), :]
        s2 = s2_ref[pl.ds(row0, BLK), :]
        outs = []
        for j in range(x.shape[1] // BLK):
            xj = x[:, j * BLK:(j + 1) * BLK]
            outs.append(xj * c + pltpu.roll(xj, half, 1) * s1 + pltpu.roll(xj, BLK - half, 1) * s2)
        return jnp.concatenate(outs, axis=1)

    kb = [jnp.clip(r - 1 + j, 0, nblk - 1) for j in range(3)]
    q = rope(q_ref[...], r) * (HEAD_DIM ** -0.5)
    kcat = jnp.concatenate([rope(kg_ref[...], 0), rope(k0_ref[...], kb[0]), rope(k1_ref[...], kb[1]),
                            rope(k2_ref[...], kb[2])], axis=0).astype(jnp.bfloat16)
    vcat = jnp.concatenate([vg_ref[...], v0_ref[...], v1_ref[...], v2_ref[...]], axis=0).astype(jnp.bfloat16)

    qrow = lax.broadcasted_iota(jnp.int32, (BLK, BLK), 0)
    kcol = lax.broadcasted_iota(jnp.int32, (BLK, BLK), 1)
    masks = [kcol >= PAD_ROWS]
    for j in range(3):
        kblk = r - 1 + j
        ok_blk = (kblk >= 1) & (kblk <= nblk - 1)
        masks.append((jnp.abs((j - 1) * BLK + kcol - qrow) <= WINDOW) & ok_blk)
    mask = jnp.concatenate(masks, axis=1)

    G = N_Q_HEADS // N_KV_HEADS
    lane_half = kcol // HEAD_DIM
    pairs_per_slice = (BLK // HEAD_DIM) * G // 2
    outs = [None] * (N_Q_HEADS // 2)
    for m2 in range(N_KV_HEADS * HEAD_DIM // BLK):
        k2 = kcat[:, m2 * BLK:(m2 + 1) * BLK]
        v2 = vcat[:, m2 * BLK:(m2 + 1) * BLK]
        qs, sks, halves = [], [], []
        for j in range(m2 * pairs_per_slice, (m2 + 1) * pairs_per_slice):
            hk = ((2 * j) // G) % 2
            in_half = lane_half == hk
            q2 = q[:, j * BLK:(j + 1) * BLK]
            qs += [jnp.where(in_half, q2, 0.0), jnp.where(in_half, pltpu.roll(q2, HEAD_DIM, 1), 0.0)]
            sks += [jnp.full((BLK, 1), sink_ref[2 * j + hk]), jnp.full((BLK, 1), sink_ref[2 * j + 1 - hk])]
            halves.append(in_half)
        nq = len(qs)
        sk = jnp.concatenate(sks, axis=0)
        s = _nt(jnp.concatenate(qs, axis=0).astype(jnp.bfloat16), k2)
        s = jnp.where(jnp.tile(mask, (nq, 1)), s, NEG)
        m = jnp.maximum(jnp.max(s, axis=-1, keepdims=True), sk)
        p = jnp.exp(s - m)
        p = p / (jnp.sum(p, axis=-1, keepdims=True) + jnp.exp(sk - m))
        o = jnp.dot(p.astype(jnp.bfloat16), v2, preferred_element_type=jnp.float32)
        for jj, in_half in enumerate(halves):
            o_same = o[(2 * jj) * BLK:(2 * jj + 1) * BLK]
            o_other = o[(2 * jj + 1) * BLK:(2 * jj + 2) * BLK]
            outs[m2 * pairs_per_slice + jj] = jnp.where(in_half, o_same, pltpu.roll(o_other, HEAD_DIM, 1))
    o_ref[...] = jnp.concatenate(outs, axis=1).astype(o_ref.dtype)


def _attention(proj3, sinks, tables):
    NB, Lp, _ = proj3.shape
    nblk = Lp // BLK
    kcol, vcol = COL_AK // 256, COL_AV // 256

    def loc(j, col):
        return pl.BlockSpec((None, BLK, 256),
                            lambda b, r: (b, jnp.clip(r - 1 + j, 0, nblk - 1), col))

    glob = lambda col: pl.BlockSpec((None, BLK, 256), lambda b, r: (b, 0, col))
    tab = pl.BlockSpec((Lp, BLK), lambda b, r: (0, 0))
    return pl.pallas_call(
        _attn_kernel,
        grid=(NB, nblk),
        in_specs=[pl.BlockSpec(memory_space=pltpu.SMEM),
                  pl.BlockSpec((None, BLK, 1024), lambda b, r: (b, r, 0)),
                  glob(kcol), loc(0, kcol), loc(1, kcol), loc(2, kcol),
                  glob(vcol), loc(0, vcol), loc(1, vcol), loc(2, vcol),
                  tab, tab, tab],
        out_specs=pl.BlockSpec((None, BLK, 1024), lambda b, r: (b, r, 0)),
        out_shape=jax.ShapeDtypeStruct((NB, Lp, 1024), jnp.bfloat16),
        compiler_params=_cparams(("parallel", "arbitrary")),
        name="window_attn",
    )(sinks, proj3, proj3, proj3, proj3, proj3, proj3, proj3, proj3, proj3, *tables)


HG_ROWS = BLK
_HG_LEVELS = (64, 32, 16, 8, 4, 2, 1)


def _hg_consts(reverse):
    C = HG_ROWS
    t = np.arange(C)[:, None]
    u = np.arange(C)[None, :]
    mats = [u <= t, u > t]
    ups, msks = [], []
    for B in _HG_LEVELS:
        same = (t // (2 * B)) == (u // (2 * B))
        tin, uin = t % (2 * B), u % (2 * B)
        upper = tin >= B
        wq = upper & same & (uin >= B) & (uin <= tin)
        wk = (~upper) & same & (uin > tin) & (uin <= B - 1)
        mats.append(wq | wk)
        ups.append(np.broadcast_to(upper, (C, HG_DIM)))
        msks.append(same & upper & (uin < B))
    msks.append(np.eye(C, dtype=bool))
    if reverse:
        mats = [m[::-1, ::-1] for m in mats]
        ups = [m[::-1] for m in ups]
        msks = [m[::-1, ::-1] for m in msks]
    W = np.concatenate(mats, axis=0).astype(np.float32)
    W2 = np.concatenate([W, W], axis=1)
    return (jnp.asarray(W2, jnp.bfloat16), jnp.asarray(np.stack(ups).astype(np.float32)),
            jnp.asarray(np.stack(msks).astype(np.float32)))


def _hg_chunk(q_raw, z, v, valid, lb, w2, up_ref, msk_ref, st, reverse):
    C = HG_ROWS
    f32, bf16 = jnp.float32, jnp.bfloat16
    log_lb = jnp.log(lb)
    log1m = jnp.log1p(-lb)
    logsig = jnp.minimum(z, 0.0) - jnp.log1p(jnp.exp(-jnp.abs(z)))
    a, b = log_lb, log1m + logsig
    g = jnp.maximum(a, b) + jnp.log1p(jnp.exp(-jnp.abs(a - b)))
    k = (1.0 - lb) * jax.nn.sigmoid(-z)
    q = q_raw * jax.nn.sigmoid(q_raw)
    q = jnp.where(valid, q, 0.0)
    k = jnp.where(valid, k, 0.0)
    v = jnp.where(valid, v, 0.0)
    g = jnp.where(valid, g, 0.0)
    g1 = g.astype(bf16)
    g2 = (g - g1.astype(f32)).astype(bf16)
    E = jnp.dot(w2, jnp.concatenate([g1, g2], axis=0), preferred_element_type=f32)
    b_inc = E[0:C]
    b_sfx = E[C:2 * C]
    b_tot = b_inc[0:1] if reverse else b_inc[C - 1:C]

    A = msk_ref[len(_HG_LEVELS)] * jnp.sum(q * k, axis=-1, keepdims=True)
    for l in range(len(_HG_LEVELS)):
        ex = jnp.exp(E[(2 + l) * C:(3 + l) * C])
        up = up_ref[l] > 0.5
        qs = jnp.where(up, q * ex, 0.0).astype(bf16)
        ks = jnp.where(up, 0.0, k * ex).astype(bf16)
        A = A + _nt(qs, ks) * msk_ref[l]

    vb = v.astype(bf16)
    o = _nt((q * jnp.exp(b_inc)).astype(bf16), st.astype(bf16))
    o = o + jnp.dot(A.astype(bf16), vb, preferred_element_type=f32)
    st = st * jnp.exp(b_tot) + _tn(vb, (k * jnp.exp(b_sfx)).astype(bf16))
    return o, st


def _hg_kernel(qf_ref, zf_ref, vf_ref, qb_ref, zb_ref, vb_ref, lb_ref, w2f_ref, upf_ref, mskf_ref,
               w2b_ref, upb_ref, mskb_ref, of_ref, ob_ref, st_ref):
    i = pl.program_id(2)
    nblk = pl.num_programs(2)

    @pl.when(i == 0)
    def _():
        st_ref[...] = jnp.zeros_like(st_ref)

    rows = lax.broadcasted_iota(jnp.int32, (HG_ROWS, HG_DIM), 0)
    dirs = ((False, qf_ref, zf_ref, vf_ref, w2f_ref, upf_ref, mskf_ref, of_ref, i),
            (True, qb_ref, zb_ref, vb_ref, w2b_ref, upb_ref, mskb_ref, ob_ref, nblk - 1 - i))
    for d, (reverse, q_ref, z_ref, v_ref, w2_ref, up_ref, msk_ref, o_ref, blk) in enumerate(dirs):
        w2 = w2_ref[...]
        for hh in range(HG_HP):
            cols = slice(hh * HG_DIM, (hh + 1) * HG_DIM)
            lb = lb_ref[d, :, cols]
            valid = (rows >= PAD_ROWS) | (blk > 0)
            o, st = _hg_chunk(q_ref[:, cols], z_ref[:, cols], v_ref[:, cols], valid, lb, w2,
                              up_ref, msk_ref, st_ref[d, hh], reverse)
            st_ref[d, hh] = st
            o_ref[:, cols] = o


def _hgrn(proj3, lb3, consts_f, consts_b):
    NB, Lp, _ = proj3.shape
    nblk = Lp // BLK
    W = HG_DIM * HG_HP

    def col(base, reverse):
        return pl.BlockSpec((None, BLK, W),
                            lambda b, h, i: (b, (nblk - 1 - i) if reverse else i, base // W + h))

    def const(c):
        return pl.BlockSpec(c.shape, lambda b, h, i: (0,) * c.ndim)

    in_specs = [col(COL_HQ, False), col(COL_ZF, False), col(COL_HI, False),
                col(COL_HQ, True), col(COL_ZB, True), col(COL_HI, True),
                pl.BlockSpec((2, 1, W), lambda b, h, i: (0, 0, h))]
    in_specs += [const(c) for c in consts_f] + [const(c) for c in consts_b]
    oshape = jax.ShapeDtypeStruct((NB, Lp, HG_HEADS * HG_DIM), jnp.float32)
    return pl.pallas_call(
        _hg_kernel,
        grid=(NB, HG_HEADS // HG_HP, nblk),
        in_specs=in_specs,
        out_specs=(pl.BlockSpec((None, BLK, W), lambda b, h, i: (b, i, h)),
                   pl.BlockSpec((None, BLK, W), lambda b, h, i: (b, nblk - 1 - i, h))),
        out_shape=(oshape, oshape),
        scratch_shapes=[pltpu.VMEM((2, HG_HP, HG_DIM, HG_DIM), jnp.float32)],
        compiler_params=_cparams(("parallel", "parallel", "arbitrary")),
        name="hgrn_bidir",
    )(proj3, proj3, proj3, proj3, proj3, proj3, lb3, *consts_f, *consts_b)


def _merge_kernel(a_ref, of_ref, ob_ref, hgg_ref, ng_ref, ga_ref, gh_ref, h_ref, wa_ref, wh_ref, wo_ref, o_ref):
    f32, bf16 = jnp.float32, jnp.bfloat16
    o = of_ref[...] + ob_ref[...]
    ng = ng_ref[...]
    gt = hgg_ref[...]
    hh = jnp.concatenate([_rms(o[:, c:c + HG_DIM], ng[:, c:c + HG_DIM]) for c in range(0, HG_HEADS * HG_DIM, HG_DIM)],
                         axis=1) * (gt * jax.nn.sigmoid(gt))
    pa = jnp.dot(a_ref[...], wa_ref[...], preferred_element_type=f32)
    ph = jnp.dot(hh.astype(bf16), wh_ref[...], preferred_element_type=f32)
    merged = jax.nn.sigmoid(ga_ref[...]) * pa + jax.nn.sigmoid(gh_ref[...]) * ph
    o_ref[...] = h_ref[...] + jnp.dot(merged.astype(bf16), wo_ref[...], preferred_element_type=f32)


def _merge(a2d, of2d, ob2d, ng, proj2d, h2d, wa, wh, wo, tm):
    T = h2d.shape[0]
    row = lambda c: pl.BlockSpec((tm, D_MODEL), lambda i: (i, c))
    wspec = pl.BlockSpec((D_MODEL, D_MODEL), lambda i: (0, 0))
    return pl.pallas_call(
        _merge_kernel,
        grid=(T // tm,),
        in_specs=[row(0), row(0), row(0), row(COL_HGG // D_MODEL), pl.BlockSpec((1, D_MODEL), lambda i: (0, 0)),
                  row(COL_GA // D_MODEL), row(COL_GH // D_MODEL), row(0), wspec, wspec, wspec],
        out_specs=row(0),
        out_shape=jax.ShapeDtypeStruct((T, D_MODEL), jnp.float32),
        input_output_aliases={7: 0},
        compiler_params=_cparams(("parallel",)),
        name="merge_out",
    )(a2d, of2d, ob2d, proj2d, ng, proj2d, proj2d, h2d, wa, wh, wo)


def _topk_axis0(s, kk):
    n = s.shape[0]
    iota = lax.broadcasted_iota(jnp.int32, s.shape, 0)
    vals, ids = [], []
    for _ in range(kk):
        m = jnp.max(s, axis=0, keepdims=True)
        am = jnp.min(jnp.where(s == m, iota, n), axis=0, keepdims=True)
        s = jnp.where(iota == am, -jnp.inf, s)
        vals.append(m)
        ids.append(am)
    return vals, ids


def _stack_rows(rows, dtype):
    n = len(rows)
    iota = lax.broadcasted_iota(jnp.int32, (n, rows[0].shape[1]), 0)
    out = jnp.zeros((n, rows[0].shape[1]), dtype)
    for j, rj in enumerate(rows):
        out = jnp.where(iota == j, rj, out)
    return out


def _route_kernel(h_ref, g_ref, wq_ref, key_ref, eidx_ref, gate_ref, xn_ref):
    @pl.when(pl.program_id(1) == 0)
    def _():
        xn_ref[...] = _rms(h_ref[...], g_ref[...]).astype(jnp.bfloat16)

    K = PEER_TOPK
    half = PEER_QDIM // 2
    i32 = jnp.int32
    qry = jnp.dot(xn_ref[...], wq_ref[...], preferred_element_type=jnp.float32).astype(jnp.bfloat16)
    s0 = _nt(key_ref[0], qry[:, :half])
    s1 = _nt(key_ref[1], qry[:, half:])
    tm = s0.shape[1]
    v0, i0 = _topk_axis0(s0, K)
    v1, i1 = _topk_axis0(s1, K)
    m1 = _stack_rows(v1, jnp.float32)
    m0_hi = _stack_rows(v0[K // 2:], jnp.float32)
    r8 = lax.broadcasted_iota(i32, (K // 2, tm), 0)
    cand = [v0[0] + m1] + [v0[a] + m1[:K // 2] for a in range(1, K // 2)] + [m0_hi + v1[0]]
    flat = [lax.broadcasted_iota(i32, (K, tm), 0)] + [a * K + r8 for a in range(1, K // 2)] + [(K // 2 + r8) * K]
    cand = jnp.concatenate(cand, axis=0)
    flat = jnp.concatenate(flat, axis=0)
    big = K * K
    ts, tp = [], []
    for _ in range(K):
        m = jnp.max(cand, axis=0, keepdims=True)
        p = jnp.min(jnp.where(cand == m, flat, big), axis=0, keepdims=True)
        cand = jnp.where(flat == p, -jnp.inf, cand)
        ts.append(m)
        tp.append(p)
    pk = _stack_rows(tp, i32)
    pa, pb = pk // K, pk % K
    e0 = jnp.zeros((K, tm), i32)
    e1 = jnp.zeros((K, tm), i32)
    for a in range(K):
        e0 = jnp.where(pa == a, i0[a], e0)
        e1 = jnp.where(pb == a, i1[a], e1)
    ex = [jnp.exp(t - ts[0]) for t in ts]
    den = ex[0]
    for e in ex[1:]:
        den = den + e
    eidx_ref[...] = e0 * PEER_NKEYS + e1
    gate_ref[...] = _stack_rows([e / den for e in ex], jnp.float32)


def _route(h2d, g, wq, keys, tm):
    T = h2d.shape[0]
    out = jax.ShapeDtypeStruct((PEER_SLOTS, T), jnp.int32), jax.ShapeDtypeStruct((PEER_SLOTS, T), jnp.float32)
    ospec = pl.BlockSpec((PEER_TOPK, tm), lambda i, h: (h, i))
    return pl.pallas_call(
        _route_kernel,
        grid=(T // tm, PEER_HEADS),
        in_specs=[pl.BlockSpec((tm, D_MODEL), lambda i, h: (i, 0)),
                  pl.BlockSpec((1, D_MODEL), lambda i, h: (0, 0)),
                  pl.BlockSpec((D_MODEL, PEER_QDIM), lambda i, h: (0, h)),
                  pl.BlockSpec((2, PEER_NKEYS, PEER_QDIM // 2), lambda i, h: (h, 0, 0))],
        out_specs=(ospec, ospec),
        out_shape=out,
        scratch_shapes=[pltpu.VMEM((tm, D_MODEL), jnp.bfloat16)],
        compiler_params=_cparams(("parallel", "arbitrary")),
        name="peer_route",
    )(h2d, g, wq, keys)


def _pack_bf16_pairs(x):
    e, d = x.shape
    bits = lax.bitcast_convert_type(x.astype(jnp.bfloat16), jnp.uint16).astype(jnp.uint32)
    bits = bits.reshape(e, d // 256, 2, 128)
    return bits[:, :, 0, :] | (bits[:, :, 1, :] << 16)


def _sc_gather(uv, idx):
    M = idx.shape[0]
    info = plsc.get_sparse_core_info()
    nw = info.num_cores * info.num_subcores
    per = M // nw
    assert per * nw == M and per % SC_WINDOW == 0
    mesh = plsc.VectorSubcoreMesh(core_axis_name="c", subcore_axis_name="s")
    oshape = jax.ShapeDtypeStruct((M,) + uv.shape[2:], uv.dtype)

    @functools.partial(
        pl.kernel, mesh=mesh, out_type=(oshape, oshape),
        scratch_types=[pltpu.VMEM((SC_WINDOW,), jnp.int32), pltpu.VMEM((SC_WINDOW,) + uv.shape[1:], uv.dtype),
                       pltpu.SemaphoreType.DMA],
    )
    def gather(uv_hbm, idx_hbm, ou_hbm, ov_hbm, idx_v, rows_v, sem):
        wid = lax.axis_index("s") * info.num_cores + lax.axis_index("c")
        base = wid * per

        @pl.loop(0, per // SC_WINDOW)
        def _(g):
            off = pl.multiple_of(base + g * SC_WINDOW, SC_WINDOW)
            pltpu.sync_copy(idx_hbm.at[pl.ds(off, SC_WINDOW)], idx_v)
            pltpu.async_copy(uv_hbm.at[idx_v], rows_v, sem).wait()
            pltpu.sync_copy(rows_v.at[:, 0], ou_hbm.at[pl.ds(off, SC_WINDOW)])
            pltpu.sync_copy(rows_v.at[:, 1], ov_hbm.at[pl.ds(off, SC_WINDOW)])

    return gather(uv, idx)


def _expert_kernel(idx_ref, idxn_ref, h_ref, g_ref, gate_ref, spread_ref, fold_ref, su_ref, sv_ref, uv_hbm, o_ref, buf,
                   z_ref, xn_ref, y_ref, sem, *, step0, seq_steps, pad_steps):
    TD, TS, NS = PEER_TD, PEER_TS, PEER_SLOTS
    TT, LW = TD + TS, SUBLANES * PEER_SLOTS
    PR = SUBLANES // 2
    f32, bf16 = jnp.float32, jnp.bfloat16
    i = pl.program_id(0)
    n = pl.num_programs(0)
    slot = lax.rem(i, 2)
    live = lax.rem(step0 + i, seq_steps) >= pad_steps
    live_next = (i + 1 < n) & (lax.rem(step0 + i + 1, seq_steps) >= pad_steps)
    dead = jnp.logical_not(live)

    def issue_token(iref, sl, t):
        for k in range(NS):
            pltpu.make_async_copy(uv_hbm.at[iref[t, k]], buf.at[sl, :, pl.ds((t * NS + k) * PR, PR)],
                                  sem.at[sl]).start(priority=k % 2)

    def issue_all(iref, sl):
        def body(t, c):
            issue_token(iref, sl, t)
            return c
        lax.fori_loop(0, TD, body, 0)

    @pl.when((i == 0) & live)
    def _():
        issue_all(idx_ref, 0)

    @pl.when(live)
    def _():
        pltpu.make_async_copy(buf.at[slot], buf.at[slot], sem.at[slot]).wait()

    @pl.when(dead & live_next)
    def _():
        issue_all(idxn_ref, 1 - slot)

    @pl.when(dead)
    def _():
        o_ref[...] = h_ref[...]

    @pl.when(live)
    def _():
        _expert_step(idxn_ref, h_ref, g_ref, gate_ref, spread_ref, fold_ref, su_ref, sv_ref, o_ref, buf, z_ref, xn_ref,
                     y_ref, slot, live_next, issue_token)


def _expert_step(idxn_ref, h_ref, g_ref, gate_ref, spread_ref, fold_ref, su_ref, sv_ref, o_ref, buf, z_ref, xn_ref,
                 y_ref, slot, live_next, issue_token):
    TD, TS, NS = PEER_TD, PEER_TS, PEER_SLOTS
    TT, LW = TD + TS, SUBLANES * PEER_SLOTS
    PR = SUBLANES // 2
    f32, bf16 = jnp.float32, jnp.bfloat16

    x = h_ref[...]
    xn_ref[...] = _rms(x, g_ref[...]).reshape(TT, SUBLANES, 128)
    on_diag = (lax.broadcasted_iota(jnp.int32, (SUBLANES, LW), 1) % SUBLANES
               == lax.broadcasted_iota(jnp.int32, (SUBLANES, LW), 0))

    def pieces_of(t, staged, part):
        r = pl.ds(pl.multiple_of(t * (NS * PR), NS * PR), NS * PR)
        words = (sv_ref if part else su_ref)[r, :] if staged else buf[slot, part, r, :]
        return pltpu.bitcast(words, bf16)

    def score_token(t, staged):
        tg = t + TD if staged else t
        z = _nt(xn_ref[tg].astype(bf16), pieces_of(t, staged, 0))
        z_ref[pl.ds(tg, 1), :] = jnp.sum(jnp.where(on_diag, z, 0.0), axis=0, keepdims=True)

    def mix_token(t, staged):
        tg = t + TD if staged else t
        w = jnp.where(on_diag, jnp.broadcast_to(z_ref[pl.ds(tg, 1), :], (SUBLANES, LW)), 0.0).astype(bf16)
        y_ref[tg] = jnp.dot(w, pieces_of(t, staged, 1), preferred_element_type=f32)

    def batched(count, fn):
        def body(tt, c):
            for j in range(PEER_BATCH):
                fn(tt * PEER_BATCH + j)
            return c
        lax.fori_loop(0, count // PEER_BATCH, body, 0)

    @pl.when(live_next)
    def _():
        def body(tt, c):
            for j in range(PEER_UNROLL):
                score_token(tt * PEER_UNROLL + j, False)
            for j in range(PEER_UNROLL):
                issue_token(idxn_ref, 1 - slot, tt * PEER_UNROLL + j)
            return c
        lax.fori_loop(0, TD // PEER_UNROLL, body, 0)

    @pl.when(jnp.logical_not(live_next))
    def _():
        batched(TD, lambda t: score_token(t, False))

    batched(TS, lambda t: score_token(t, True))

    z = z_ref[...]
    z1 = z.astype(bf16)
    r1 = z - z1.astype(f32)
    z2 = r1.astype(bf16)
    z3 = (r1 - z2.astype(f32)).astype(bf16)
    fold = fold_ref[...]
    sc = (jnp.dot(z1, fold, preferred_element_type=f32) + jnp.dot(z2, fold, preferred_element_type=f32)
          + jnp.dot(z3, fold, preferred_element_type=f32))
    w = gate_ref[...] * (0.5 * sc * (1.0 + lax.erf(sc * (2.0 ** -0.5))))
    z_ref[...] = jnp.dot(w.astype(bf16), spread_ref[...], preferred_element_type=f32)

    batched(TD, lambda t: mix_token(t, False))
    batched(TS, lambda t: mix_token(t, True))
    o_ref[...] = x + y_ref[...].reshape(TT, D_MODEL)


def _experts_chunk(step0, nsteps, seq_steps, idx_d, h2d, g, gate, spread, su, sv, uv):
    TD, TS, NS = PEER_TD, PEER_TS, PEER_SLOTS
    TT, PR = TD + TS, SUBLANES // 2
    last = step0 + nsteps - 1
    staged = pl.BlockSpec((TS * NS * PR, 128), lambda i: (i, 0))
    return pl.pallas_call(
        functools.partial(_expert_kernel, step0=step0, seq_steps=seq_steps, pad_steps=PAD_ROWS // TT),
        grid=(nsteps,),
        in_specs=[pl.BlockSpec((TD, NS), lambda i: (step0 + i, 0), memory_space=pltpu.SMEM),
                  pl.BlockSpec((TD, NS), lambda i: (jnp.minimum(step0 + i + 1, last), 0), memory_space=pltpu.SMEM),
                  pl.BlockSpec((TT, D_MODEL), lambda i: (step0 + i, 0)),
                  pl.BlockSpec((1, D_MODEL), lambda i: (0, 0)),
                  pl.BlockSpec((TT, NS), lambda i: (step0 + i, 0)),
                  pl.BlockSpec((NS, SUBLANES * NS), lambda i: (0, 0)),
                  pl.BlockSpec((SUBLANES * NS, NS), lambda i: (0, 0)),
                  staged, staged,
                  pl.BlockSpec(memory_space=pl.ANY)],
        out_specs=pl.BlockSpec((TT, D_MODEL), lambda i: (step0 + i, 0)),
        out_shape=jax.ShapeDtypeStruct(h2d.shape, jnp.float32),
        scratch_shapes=[pltpu.VMEM((2, 2, TD * NS * PR, 128), jnp.uint32),
                        pltpu.VMEM((TT, SUBLANES * NS), jnp.float32),
                        pltpu.VMEM((TT, SUBLANES, 128), jnp.float32),
                        pltpu.VMEM((TT, SUBLANES, 128), jnp.float32),
                        pltpu.SemaphoreType.DMA((2,))],
        input_output_aliases={2: 0},
        compiler_params=_cparams(("arbitrary",)),
        name="peer_experts",
    )(idx_d, idx_d, h2d, g, gate, spread, spread.T, su, sv, uv)


def _experts(h2d, g, eidx, gate, u4, v4, seq_rows):
    T = h2d.shape[0]
    TD, TS, NS = PEER_TD, PEER_TS, PEER_SLOTS
    TT = TD + TS
    assert seq_rows % TT == 0
    steps = T // TT
    chunks = _pick_tile(steps, (24, 12, 8, 6, 4, 3, 2, 1))
    spc = steps // chunks
    e3 = eidx.reshape(steps, TT, NS)
    idx_d = e3[:, :TD].reshape(steps * TD, NS)
    idx_s = e3[:, TD:].reshape(chunks, spc * TS * NS)
    spread = jnp.asarray(np.repeat(np.eye(NS, dtype=np.float32), SUBLANES, axis=1), jnp.bfloat16)
    uv = jnp.stack([u4, v4], axis=1)
    for c in range(chunks):
        su, sv = _sc_gather(uv, idx_s[c])
        su, sv = (a.reshape(-1, 128) for a in (su, sv))
        h2d = _experts_chunk(c * spc, spc, seq_rows // TT, idx_d, h2d, g, gate, spread, su, sv, uv)
    return h2d


def _final_kernel(h_ref, g_ref, o_ref):
    o_ref[...] = _rms(h_ref[...], g_ref[...])


def _final(h3, g, b0, nb):
    _, Lp, _ = h3.shape
    S = Lp - BLK
    return pl.pallas_call(
        _final_kernel,
        grid=(nb, S // BLK),
        in_specs=[pl.BlockSpec((None, BLK, D_MODEL), lambda b, r: (b + b0, r + 1, 0)),
                  pl.BlockSpec((1, D_MODEL), lambda b, r: (0, 0))],
        out_specs=pl.BlockSpec((None, BLK, D_MODEL), lambda b, r: (b, r, 0)),
        out_shape=jax.ShapeDtypeStruct((nb, S, D_MODEL), jnp.float32),
        compiler_params=_cparams(("parallel", "parallel")),
        name="final_norm",
    )(h3, g)


def _pick_tile(T, candidates):
    for c in candidates:
        if T % c == 0:
            return c
    raise ValueError(f"no tile for {T}")


def kernel(x_prompt, x_sample, meta_tokens, g_mix, w_in, attn_sinks, lb_logits, hg_norm_g, w_att_branch,
           w_hg_branch, w_out, g_ffn, w_peer_q, peer_sub_keys, peer_u, peer_v, g_final):
    f32, bf16 = jnp.float32, jnp.bfloat16
    depth = w_in.shape[0]
    nb_p, S, _ = x_prompt.shape
    nb_s = x_sample.shape[0]
    assert x_sample.shape[1] == S and S % BLK == 0
    NB = nb_p + nb_s
    Lp = S + BLK
    T = NB * Lp

    x = jnp.concatenate([x_prompt, x_sample], axis=0)
    head = jnp.concatenate([jnp.zeros((PAD_ROWS, D_MODEL), f32), meta_tokens.astype(f32)], axis=0)
    h = jnp.concatenate([jnp.broadcast_to(head[None], (NB, BLK, D_MODEL)), x], axis=1).reshape(T, D_MODEL)

    sizes = (1024, 256, 256, 1024, 1024, 1024, 1024, 1024, 1024, 1024)
    offs = np.concatenate([[0], np.cumsum(sizes)])
    order = (0, 8, 9, 3, 4, 5, 6, 7, 1, 2)
    perm = np.concatenate([np.arange(offs[k], offs[k + 1]) for k in order])

    sm = jax.nn.softmax(lb_logits.astype(f32), axis=0)
    lb_all = jnp.maximum(jnp.cumsum(sm, axis=0) - sm[0:1], 0.0)

    tables = _rope_tables(Lp)
    consts_f = _hg_consts(False)
    consts_b = _hg_consts(True)
    tm_big = _pick_tile(T, (512, 384, 256, 128))
    tm_route = _pick_tile(T, (256, 128))

    for l in range(depth):
        w_l = w_in[l][:, perm].astype(bf16)
        proj = _inproj(h, g_mix[l].reshape(1, -1), w_l, tm_big, IN_W // 4)
        proj3 = proj.reshape(NB, Lp, IN_W)
        a_out = _attention(proj3, attn_sinks[l].astype(f32), tables)
        lb3 = lb_all[l].reshape(2, 1, -1)
        o_fwd, o_bwd = _hgrn(proj3, lb3, consts_f, consts_b)
        h = _merge(a_out.reshape(T, -1), o_fwd.reshape(T, -1), o_bwd.reshape(T, -1), hg_norm_g[l].reshape(1, -1),
                   proj, h, w_att_branch[l].astype(bf16), w_hg_branch[l].astype(bf16), w_out[l].astype(bf16),
                   tm_route)
        g_f = g_ffn[l].reshape(1, -1)
        keys = peer_sub_keys[l].reshape(2 * PEER_HEADS, PEER_NKEYS, PEER_QDIM // 2).astype(bf16)
        eidx_t, gate_t = _route(h, g_f, w_peer_q[l].astype(bf16), keys, tm_route)
        h = _experts(h, g_f, eidx_t.T, gate_t.T, _pack_bf16_pairs(peer_u[l]), _pack_bf16_pairs(peer_v[l]), Lp)

    h3 = h.reshape(NB, Lp, D_MODEL)
    g_fin = g_final.reshape(1, -1)
    return _final(h3, g_fin, 0, nb_p), _final(h3, g_fin, nb_p, nb_s)
```

```python
import functools

import numpy as np
import jax
import jax.numpy as jnp
from jax import lax
from jax.experimental import pallas as pl
from jax.experimental.pallas import tpu as pltpu
from jax.experimental.pallas import tpu_sc as plsc

D_MODEL = 1024
N_META = 16
N_Q_HEADS = 16
N_KV_HEADS = 4
HEAD_DIM = 64
WINDOW = 128
ROT_DIM = HEAD_DIM // 4
ROPE_THETA = 500000.0
HG_HEADS = 8
HG_DIM = 128
PEER_HEADS = 8
PEER_NKEYS = 128
PEER_QDIM = 256
PEER_TOPK = 16
EPS = 1e-6
NEG = -1e30

BLK = 128
PAD_ROWS = BLK - N_META
IN_W = 8704
COL_AQ, COL_GA, COL_GH, COL_HQ, COL_ZF, COL_ZB, COL_HI, COL_HGG, COL_AK, COL_AV = (
    0, 1024, 2048, 3072, 4096, 5120, 6144, 7168, 8192, 8448)

VMEM_LIMIT = 56 * 1024 * 1024
PEER_TD = 16
PEER_TS = 16
SC_WINDOW = 64
PEER_SLOTS = PEER_HEADS * PEER_TOPK
HG_HP = 8
PEER_UNROLL = 8
PEER_BATCH = 16
SUBLANES = 8


def _cparams(sem):
    return pltpu.CompilerParams(dimension_semantics=sem, vmem_limit_bytes=VMEM_LIMIT)


def _rms(x, g):
    return x * lax.rsqrt(jnp.mean(x * x, axis=-1, keepdims=True) + EPS) * g


def _nt(a, b):
    return lax.dot_general(a, b, (((1,), (1,)), ((), ())), preferred_element_type=jnp.float32)


def _tn(a, b):
    return lax.dot_general(a, b, (((0,), (0,)), ((), ())), preferred_element_type=jnp.float32)


def _inproj_kernel(x_ref, g_ref, w_ref, o_ref):
    n = _rms(x_ref[...], g_ref[...]).astype(jnp.bfloat16)
    o_ref[...] = jnp.dot(n, w_ref[...], preferred_element_type=jnp.float32)


def _inproj(h2d, g, w, tm, tn):
    T = h2d.shape[0]
    return pl.pallas_call(
        _inproj_kernel,
        grid=(IN_W // tn, T // tm),
        in_specs=[pl.BlockSpec((tm, D_MODEL), lambda j, i: (i, 0)),
                  pl.BlockSpec((1, D_MODEL), lambda j, i: (0, 0)),
                  pl.BlockSpec((D_MODEL, tn), lambda j, i: (0, j))],
        out_specs=pl.BlockSpec((tm, tn), lambda j, i: (i, j)),
        out_shape=jax.ShapeDtypeStruct((T, IN_W), jnp.float32),
        compiler_params=_cparams(("parallel", "parallel")),
        name="inproj",
    )(h2d, g, w)


def _rope_tables(Lp):
    half = ROT_DIM // 2
    pos = jnp.maximum(jnp.arange(Lp) - PAD_ROWS, 0)
    inv = jnp.power(jnp.float32(ROPE_THETA), -jnp.arange(half, dtype=jnp.float32) * 2.0 / ROT_DIM)
    ang = pos.astype(jnp.float32)[:, None] * inv[None, :]
    cos, sin = jnp.cos(ang), jnp.sin(ang)
    ones = jnp.ones((Lp, HEAD_DIM - ROT_DIM), jnp.float32)
    zeros = jnp.zeros((Lp, HEAD_DIM - ROT_DIM), jnp.float32)
    zh = jnp.zeros((Lp, half), jnp.float32)
    c = jnp.concatenate([cos, cos, ones], axis=1)
    s1 = jnp.concatenate([zh, sin, zeros], axis=1)
    s2 = jnp.concatenate([-sin, zh, zeros], axis=1)
    rep = BLK // HEAD_DIM
    return tuple(jnp.tile(t, (1, rep)) for t in (c, s1, s2))


def _attn_kernel(sink_ref, q_ref, kg_ref, k0_ref, k1_ref, k2_ref, vg_ref, v0_ref, v1_ref, v2_ref,
                 c_ref, s1_ref, s2_ref, o_ref):
    r = pl.program_id(1)
    nblk = pl.num_programs(1)
    half = ROT_DIM // 2

    def rope(x, blk):
        row0 = blk * BLK if isinstance(blk, int) else pl.multiple_of(blk * BLK, BLK)
        c = c_ref[pl.ds(row0, BLK), :]
        s1 = s1_ref[pl.ds(row0, BLK), :]
        s2 = s2_ref[pl.ds(row0, BLK), :]
        outs = []
        for j in range(x.shape[1] // BLK):
            xj = x[:, j * BLK:(j + 1) * BLK]
            outs.append(xj * c + pltpu.roll(xj, half, 1) * s1 + pltpu.roll(xj, BLK - half, 1) * s2)
        return jnp.concatenate(outs, axis=1)

    kb = [jnp.clip(r - 1 + j, 0, nblk - 1) for j in range(3)]
    q = rope(q_ref[...], r) * (HEAD_DIM ** -0.5)
    kcat = jnp.concatenate([rope(kg_ref[...], 0), rope(k0_ref[...], kb[0]), rope(k1_ref[...], kb[1]),
                            rope(k2_ref[...], kb[2])], axis=0).astype(jnp.bfloat16)
    vcat = jnp.concatenate([vg_ref[...], v0_ref[...], v1_ref[...], v2_ref[...]], axis=0).astype(jnp.bfloat16)

    qrow = lax.broadcasted_iota(jnp.int32, (BLK, BLK), 0)
    kcol = lax.broadcasted_iota(jnp.int32, (BLK, BLK), 1)
    masks = [kcol >= PAD_ROWS]
    for j in range(3):
        kblk = r - 1 + j
        ok_blk = (kblk >= 1) & (kblk <= nblk - 1)
        masks.append((jnp.abs((j - 1) * BLK + kcol - qrow) <= WINDOW) & ok_blk)
    mask = jnp.concatenate(masks, axis=1)

    G = N_Q_HEADS // N_KV_HEADS
    lane_half = kcol // HEAD_DIM
    pairs_per_slice = (BLK // HEAD_DIM) * G // 2
    outs = [None] * (N_Q_HEADS // 2)
    for m2 in range(N_KV_HEADS * HEAD_DIM // BLK):
        k2 = kcat[:, m2 * BLK:(m2 + 1) * BLK]
        v2 = vcat[:, m2 * BLK:(m2 + 1) * BLK]
        qs, sks, halves = [], [], []
        for j in range(m2 * pairs_per_slice, (m2 + 1) * pairs_per_slice):
            hk = ((2 * j) // G) % 2
            in_half = lane_half == hk
            q2 = q[:, j * BLK:(j + 1) * BLK]
            qs += [jnp.where(in_half, q2, 0.0), jnp.where(in_half, pltpu.roll(q2, HEAD_DIM, 1), 0.0)]
            sks += [jnp.full((BLK, 1), sink_ref[2 * j + hk]), jnp.full((BLK, 1), sink_ref[2 * j + 1 - hk])]
            halves.append(in_half)
        nq = len(qs)
        sk = jnp.concatenate(sks, axis=0)
        s = _nt(jnp.concatenate(qs, axis=0).astype(jnp.bfloat16), k2)
        s = jnp.where(jnp.tile(mask, (nq, 1)), s, NEG)
        m = jnp.maximum(jnp.max(s, axis=-1, keepdims=True), sk)
        p = jnp.exp(s - m)
        p = p / (jnp.sum(p, axis=-1, keepdims=True) + jnp.exp(sk - m))
        o = jnp.dot(p.astype(jnp.bfloat16), v2, preferred_element_type=jnp.float32)
        for jj, in_half in enumerate(halves):
            o_same = o[(2 * jj) * BLK:(2 * jj + 1) * BLK]
            o_other = o[(2 * jj + 1) * BLK:(2 * jj + 2) * BLK]
            outs[m2 * pairs_per_slice + jj] = jnp.where(in_half, o_same, pltpu.roll(o_other, HEAD_DIM, 1))
    o_ref[...] = jnp.concatenate(outs, axis=1).astype(o_ref.dtype)


def _attention(proj3, sinks, tables):
    NB, Lp, _ = proj3.shape
    nblk = Lp // BLK
    kcol, vcol = COL_AK // 256, COL_AV // 256

    def loc(j, col):
        return pl.BlockSpec((None, BLK, 256),
                            lambda b, r: (b, jnp.clip(r - 1 + j, 0, nblk - 1), col))

    glob = lambda col: pl.BlockSpec((None, BLK, 256), lambda b, r: (b, 0, col))
    tab = pl.BlockSpec((Lp, BLK), lambda b, r: (0, 0))
    return pl.pallas_call(
        _attn_kernel,
        grid=(NB, nblk),
        in_specs=[pl.BlockSpec(memory_space=pltpu.SMEM),
                  pl.BlockSpec((None, BLK, 1024), lambda b, r: (b, r, 0)),
                  glob(kcol), loc(0, kcol), loc(1, kcol), loc(2, kcol),
                  glob(vcol), loc(0, vcol), loc(1, vcol), loc(2, vcol),
                  tab, tab, tab],
        out_specs=pl.BlockSpec((None, BLK, 1024), lambda b, r: (b, r, 0)),
        out_shape=jax.ShapeDtypeStruct((NB, Lp, 1024), jnp.bfloat16),
        compiler_params=_cparams(("parallel", "arbitrary")),
        name="window_attn",
    )(sinks, proj3, proj3, proj3, proj3, proj3, proj3, proj3, proj3, proj3, *tables)


HG_ROWS = BLK
_HG_LEVELS = (64, 32, 16, 8, 4, 2, 1)


def _hg_consts(reverse):
    C = HG_ROWS
    t = np.arange(C)[:, None]
    u = np.arange(C)[None, :]
    mats = [u <= t, u > t]
    ups, msks = [], []
    for B in _HG_LEVELS:
        same = (t // (2 * B)) == (u // (2 * B))
        tin, uin = t % (2 * B), u % (2 * B)
        upper = tin >= B
        wq = upper & same & (uin >= B) & (uin <= tin)
        wk = (~upper) & same & (uin > tin) & (uin <= B - 1)
        mats.append(wq | wk)
        ups.append(np.broadcast_to(upper, (C, HG_DIM)))
        msks.append(same & upper & (uin < B))
    msks.append(np.eye(C, dtype=bool))
    if reverse:
        mats = [m[::-1, ::-1] for m in mats]
        ups = [m[::-1] for m in ups]
        msks = [m[::-1, ::-1] for m in msks]
    W = np.concatenate(mats, axis=0).astype(np.float32)
    W2 = np.concatenate([W, W], axis=1)
    return (jnp.asarray(W2, jnp.bfloat16), jnp.asarray(np.stack(ups).astype(np.float32)),
            jnp.asarray(np.stack(msks).astype(np.float32)))


def _hg_chunk(q_raw, z, v, valid, lb, w2, up_ref, msk_ref, st, reverse):
    C = HG_ROWS
    f32, bf16 = jnp.float32, jnp.bfloat16
    log_lb = jnp.log(lb)
    log1m = jnp.log1p(-lb)
    logsig = jnp.minimum(z, 0.0) - jnp.log1p(jnp.exp(-jnp.abs(z)))
    a, b = log_lb, log1m + logsig
    g = jnp.maximum(a, b) + jnp.log1p(jnp.exp(-jnp.abs(a - b)))
    k = (1.0 - lb) * jax.nn.sigmoid(-z)
    q = q_raw * jax.nn.sigmoid(q_raw)
    q = jnp.where(valid, q, 0.0)
    k = jnp.where(valid, k, 0.0)
    v = jnp.where(valid, v, 0.0)
    g = jnp.where(valid, g, 0.0)
    g1 = g.astype(bf16)
    g2 = (g - g1.astype(f32)).astype(bf16)
    E = jnp.dot(w2, jnp.concatenate([g1, g2], axis=0), preferred_element_type=f32)
    b_inc = E[0:C]
    b_sfx = E[C:2 * C]
    b_tot = b_inc[0:1] if reverse else b_inc[C - 1:C]

    A = msk_ref[len(_HG_LEVELS)] * jnp.sum(q * k, axis=-1, keepdims=True)
    for l in range(len(_HG_LEVELS)):
        ex = jnp.exp(E[(2 + l) * C:(3 + l) * C])
        up = up_ref[l] > 0.5
        qs = jnp.where(up, q * ex, 0.0).astype(bf16)
        ks = jnp.where(up, 0.0, k * ex).astype(bf16)
        A = A + _nt(qs, ks) * msk_ref[l]

    vb = v.astype(bf16)
    o = _nt((q * jnp.exp(b_inc)).astype(bf16), st.astype(bf16))
    o = o + jnp.dot(A.astype(bf16), vb, preferred_element_type=f32)
    st = st * jnp.exp(b_tot) + _tn(vb, (k * jnp.exp(b_sfx)).astype(bf16))
    return o, st


def _hg_kernel(qf_ref, zf_ref, vf_ref, qb_ref, zb_ref, vb_ref, lb_ref, w2f_ref, upf_ref, mskf_ref,
               w2b_ref, upb_ref, mskb_ref, of_ref, ob_ref, st_ref):
    i = pl.program_id(2)
    nblk = pl.num_programs(2)

    @pl.when(i == 0)
    def _():
        st_ref[...] = jnp.zeros_like(st_ref)

    rows = lax.broadcasted_iota(jnp.int32, (HG_ROWS, HG_DIM), 0)
    dirs = ((False, qf_ref, zf_ref, vf_ref, w2f_ref, upf_ref, mskf_ref, of_ref, i),
            (True, qb_ref, zb_ref, vb_ref, w2b_ref, upb_ref, mskb_ref, ob_ref, nblk - 1 - i))
    for d, (reverse, q_ref, z_ref, v_ref, w2_ref, up_ref, msk_ref, o_ref, blk) in enumerate(dirs):
        w2 = w2_ref[...]
        for hh in range(HG_HP):
            cols = slice(hh * HG_DIM, (hh + 1) * HG_DIM)
            lb = lb_ref[d, :, cols]
            valid = (rows >= PAD_ROWS) | (blk > 0)
            o, st = _hg_chunk(q_ref[:, cols], z_ref[:, cols], v_ref[:, cols], valid, lb, w2,
                              up_ref, msk_ref, st_ref[d, hh], reverse)
            st_ref[d, hh] = st
            o_ref[:, cols] = o


def _hgrn(proj3, lb3, consts_f, consts_b):
    NB, Lp, _ = proj3.shape
    nblk = Lp // BLK
    W = HG_DIM * HG_HP

    def col(base, reverse):
        return pl.BlockSpec((None, BLK, W),
                            lambda b, h, i: (b, (nblk - 1 - i) if reverse else i, base // W + h))

    def const(c):
        return pl.BlockSpec(c.shape, lambda b, h, i: (0,) * c.ndim)

    in_specs = [col(COL_HQ, False), col(COL_ZF, False), col(COL_HI, False),
                col(COL_HQ, True), col(COL_ZB, True), col(COL_HI, True),
                pl.BlockSpec((2, 1, W), lambda b, h, i: (0, 0, h))]
    in_specs += [const(c) for c in consts_f] + [const(c) for c in consts_b]
    oshape = jax.ShapeDtypeStruct((NB, Lp, HG_HEADS * HG_DIM), jnp.float32)
    return pl.pallas_call(
        _hg_kernel,
        grid=(NB, HG_HEADS // HG_HP, nblk),
        in_specs=in_specs,
        out_specs=(pl.BlockSpec((None, BLK, W), lambda b, h, i: (b, i, h)),
                   pl.BlockSpec((None, BLK, W), lambda b, h, i: (b, nblk - 1 - i, h))),
        out_shape=(oshape, oshape),
        scratch_shapes=[pltpu.VMEM((2, HG_HP, HG_DIM, HG_DIM), jnp.float32)],
        compiler_params=_cparams(("parallel", "parallel", "arbitrary")),
        name="hgrn_bidir",
    )(proj3, proj3, proj3, proj3, proj3, proj3, lb3, *consts_f, *consts_b)


def _merge_kernel(a_ref, of_ref, ob_ref, hgg_ref, ng_ref, ga_ref, gh_ref, h_ref, wa_ref, wh_ref, wo_ref, o_ref):
    f32, bf16 = jnp.float32, jnp.bfloat16
    o = of_ref[...] + ob_ref[...]
    ng = ng_ref[...]
    gt = hgg_ref[...]
    hh = jnp.concatenate([_rms(o[:, c:c + HG_DIM], ng[:, c:c + HG_DIM]) for c in range(0, HG_HEADS * HG_DIM, HG_DIM)],
                         axis=1) * (gt * jax.nn.sigmoid(gt))
    pa = jnp.dot(a_ref[...], wa_ref[...], preferred_element_type=f32)
    ph = jnp.dot(hh.astype(bf16), wh_ref[...], preferred_element_type=f32)
    merged = jax.nn.sigmoid(ga_ref[...]) * pa + jax.nn.sigmoid(gh_ref[...]) * ph
    o_ref[...] = h_ref[...] + jnp.dot(merged.astype(bf16), wo_ref[...], preferred_element_type=f32)


def _merge(a2d, of2d, ob2d, ng, proj2d, h2d, wa, wh, wo, tm):
    T = h2d.shape[0]
    row = lambda c: pl.BlockSpec((tm, D_MODEL), lambda i: (i, c))
    wspec = pl.BlockSpec((D_MODEL, D_MODEL), lambda i: (0, 0))
    return pl.pallas_call(
        _merge_kernel,
        grid=(T // tm,),
        in_specs=[row(0), row(0), row(0), row(COL_HGG // D_MODEL), pl.BlockSpec((1, D_MODEL), lambda i: (0, 0)),
                  row(COL_GA // D_MODEL), row(COL_GH // D_MODEL), row(0), wspec, wspec, wspec],
        out_specs=row(0),
        out_shape=jax.ShapeDtypeStruct((T, D_MODEL), jnp.float32),
        input_output_aliases={7: 0},
        compiler_params=_cparams(("parallel",)),
        name="merge_out",
    )(a2d, of2d, ob2d, proj2d, ng, proj2d, proj2d, h2d, wa, wh, wo)


def _topk_axis0(s, kk):
    n = s.shape[0]
    iota = lax.broadcasted_iota(jnp.int32, s.shape, 0)
    vals, ids = [], []
    for _ in range(kk):
        m = jnp.max(s, axis=0, keepdims=True)
        am = jnp.min(jnp.where(s == m, iota, n), axis=0, keepdims=True)
        s = jnp.where(iota == am, -jnp.inf, s)
        vals.append(m)
        ids.append(am)
    return vals, ids


def _stack_rows(rows, dtype):
    n = len(rows)
    iota = lax.broadcasted_iota(jnp.int32, (n, rows[0].shape[1]), 0)
    out = jnp.zeros((n, rows[0].shape[1]), dtype)
    for j, rj in enumerate(rows):
        out = jnp.where(iota == j, rj, out)
    return out


def _route_kernel(h_ref, g_ref, wq_ref, key_ref, eidx_ref, gate_ref, xn_ref):
    @pl.when(pl.program_id(1) == 0)
    def _():
        xn_ref[...] = _rms(h_ref[...], g_ref[...]).astype(jnp.bfloat16)

    K = PEER_TOPK
    half = PEER_QDIM // 2
    i32 = jnp.int32
    qry = jnp.dot(xn_ref[...], wq_ref[...], preferred_element_type=jnp.float32).astype(jnp.bfloat16)
    s0 = _nt(key_ref[0], qry[:, :half])
    s1 = _nt(key_ref[1], qry[:, half:])
    tm = s0.shape[1]
    v0, i0 = _topk_axis0(s0, K)
    v1, i1 = _topk_axis0(s1, K)
    m1 = _stack_rows(v1, jnp.float32)
    m0_hi = _stack_rows(v0[K // 2:], jnp.float32)
    r8 = lax.broadcasted_iota(i32, (K // 2, tm), 0)
    cand = [v0[0] + m1] + [v0[a] + m1[:K // 2] for a in range(1, K // 2)] + [m0_hi + v1[0]]
    flat = [lax.broadcasted_iota(i32, (K, tm), 0)] + [a * K + r8 for a in range(1, K // 2)] + [(K // 2 + r8) * K]
    cand = jnp.concatenate(cand, axis=0)
    flat = jnp.concatenate(flat, axis=0)
    big = K * K
    ts, tp = [], []
    for _ in range(K):
        m = jnp.max(cand, axis=0, keepdims=True)
        p = jnp.min(jnp.where(cand == m, flat, big), axis=0, keepdims=True)
        cand = jnp.where(flat == p, -jnp.inf, cand)
        ts.append(m)
        tp.append(p)
    pk = _stack_rows(tp, i32)
    pa, pb = pk // K, pk % K
    e0 = jnp.zeros((K, tm), i32)
    e1 = jnp.zeros((K, tm), i32)
    for a in range(K):
        e0 = jnp.where(pa == a, i0[a], e0)
        e1 = jnp.where(pb == a, i1[a], e1)
    ex = [jnp.exp(t - ts[0]) for t in ts]
    den = ex[0]
    for e in ex[1:]:
        den = den + e
    eidx_ref[...] = e0 * PEER_NKEYS + e1
    gate_ref[...] = _stack_rows([e / den for e in ex], jnp.float32)


def _route(h2d, g, wq, keys, tm):
    T = h2d.shape[0]
    out = jax.ShapeDtypeStruct((PEER_SLOTS, T), jnp.int32), jax.ShapeDtypeStruct((PEER_SLOTS, T), jnp.float32)
    ospec = pl.BlockSpec((PEER_TOPK, tm), lambda i, h: (h, i))
    return pl.pallas_call(
        _route_kernel,
        grid=(T // tm, PEER_HEADS),
        in_specs=[pl.BlockSpec((tm, D_MODEL), lambda i, h: (i, 0)),
                  pl.BlockSpec((1, D_MODEL), lambda i, h: (0, 0)),
                  pl.BlockSpec((D_MODEL, PEER_QDIM), lambda i, h: (0, h)),
                  pl.BlockSpec((2, PEER_NKEYS, PEER_QDIM // 2), lambda i, h: (h, 0, 0))],
        out_specs=(ospec, ospec),
        out_shape=out,
        scratch_shapes=[pltpu.VMEM((tm, D_MODEL), jnp.bfloat16)],
        compiler_params=_cparams(("parallel", "arbitrary")),
        name="peer_route",
    )(h2d, g, wq, keys)


def _pack_bf16_pairs(x):
    e, d = x.shape
    bits = lax.bitcast_convert_type(x.astype(jnp.bfloat16), jnp.uint16).astype(jnp.uint32)
    bits = bits.reshape(e, d // 256, 2, 128)
    return bits[:, :, 0, :] | (bits[:, :, 1, :] << 16)


def _sc_gather(uv, idx):
    M = idx.shape[0]
    info = plsc.get_sparse_core_info()
    nw = info.num_cores * info.num_subcores
    per = M // nw
    assert per * nw == M and per % SC_WINDOW == 0
    mesh = plsc.VectorSubcoreMesh(core_axis_name="c", subcore_axis_name="s")
    oshape = jax.ShapeDtypeStruct((M,) + uv.shape[2:], uv.dtype)

    @functools.partial(
        pl.kernel, mesh=mesh, out_type=(oshape, oshape),
        scratch_types=[pltpu.VMEM((SC_WINDOW,), jnp.int32), pltpu.VMEM((SC_WINDOW,) + uv.shape[1:], uv.dtype),
                       pltpu.SemaphoreType.DMA],
    )
    def gather(uv_hbm, idx_hbm, ou_hbm, ov_hbm, idx_v, rows_v, sem):
        wid = lax.axis_index("s") * info.num_cores + lax.axis_index("c")
        base = wid * per

        @pl.loop(0, per // SC_WINDOW)
        def _(g):
            off = pl.multiple_of(base + g * SC_WINDOW, SC_WINDOW)
            pltpu.sync_copy(idx_hbm.at[pl.ds(off, SC_WINDOW)], idx_v)
            pltpu.async_copy(uv_hbm.at[idx_v], rows_v, sem).wait()
            pltpu.sync_copy(rows_v.at[:, 0], ou_hbm.at[pl.ds(off, SC_WINDOW)])
            pltpu.sync_copy(rows_v.at[:, 1], ov_hbm.at[pl.ds(off, SC_WINDOW)])

    return gather(uv, idx)


def _expert_kernel(idx_ref, idxn_ref, h_ref, g_ref, gate_ref, spread_ref, fold_ref, su_ref, sv_ref, uv_hbm, o_ref, buf,
                   z_ref, xn_ref, y_ref, sem, *, step0, seq_steps, pad_steps):
    TD, TS, NS = PEER_TD, PEER_TS, PEER_SLOTS
    TT, LW = TD + TS, SUBLANES * PEER_SLOTS
    PR = SUBLANES // 2
    f32, bf16 = jnp.float32, jnp.bfloat16
    i = pl.program_id(0)
    n = pl.num_programs(0)
    slot = lax.rem(i, 2)
    live = lax.rem(step0 + i, seq_steps) >= pad_steps
    live_next = (i + 1 < n) & (lax.rem(step0 + i + 1, seq_steps) >= pad_steps)
    dead = jnp.logical_not(live)

    def issue_token(iref, sl, t):
        for k in range(NS):
            pltpu.make_async_copy(uv_hbm.at[iref[t, k]], buf.at[sl, :, pl.ds((t * NS + k) * PR, PR)],
                                  sem.at[sl]).start(priority=k % 2)

    def issue_all(iref, sl):
        def body(t, c):
            issue_token(iref, sl, t)
            return c
        lax.fori_loop(0, TD, body, 0)

    @pl.when((i == 0) & live)
    def _():
        issue_all(idx_ref, 0)

    @pl.when(live)
    def _():
        pltpu.make_async_copy(buf.at[slot], buf.at[slot], sem.at[slot]).wait()

    @pl.when(dead & live_next)
    def _():
        issue_all(idxn_ref, 1 - slot)

    @pl.when(dead)
    def _():
        o_ref[...] = h_ref[...]

    @pl.when(live)
    def _():
        _expert_step(idxn_ref, h_ref, g_ref, gate_ref, spread_ref, fold_ref, su_ref, sv_ref, o_ref, buf, z_ref, xn_ref,
                     y_ref, slot, live_next, issue_token)


def _expert_step(idxn_ref, h_ref, g_ref, gate_ref, spread_ref, fold_ref, su_ref, sv_ref, o_ref, buf, z_ref, xn_ref,
                 y_ref, slot, live_next, issue_token):
    TD, TS, NS = PEER_TD, PEER_TS, PEER_SLOTS
    TT, LW = TD + TS, SUBLANES * PEER_SLOTS
    PR = SUBLANES // 2
    f32, bf16 = jnp.float32, jnp.bfloat16

    x = h_ref[...]
    xn_ref[...] = _rms(x, g_ref[...]).reshape(TT, SUBLANES, 128)
    on_diag = (lax.broadcasted_iota(jnp.int32, (SUBLANES, LW), 1) % SUBLANES
               == lax.broadcasted_iota(jnp.int32, (SUBLANES, LW), 0))

    def pieces_of(t, staged, part):
        r = pl.ds(pl.multiple_of(t * (NS * PR), NS * PR), NS * PR)
        words = (sv_ref if part else su_ref)[r, :] if staged else buf[slot, part, r, :]
        return pltpu.bitcast(words, bf16)

    def score_token(t, staged):
        tg = t + TD if staged else t
        z = _nt(xn_ref[tg].astype(bf16), pieces_of(t, staged, 0))
        z_ref[pl.ds(tg, 1), :] = jnp.sum(jnp.where(on_diag, z, 0.0), axis=0, keepdims=True)

    def mix_token(t, staged):
        tg = t + TD if staged else t
        w = jnp.where(on_diag, jnp.broadcast_to(z_ref[pl.ds(tg, 1), :], (SUBLANES, LW)), 0.0).astype(bf16)
        y_ref[tg] = jnp.dot(w, pieces_of(t, staged, 1), preferred_element_type=f32)

    def batched(count, fn):
        def body(tt, c):
            for j in range(PEER_BATCH):
                fn(tt * PEER_BATCH + j)
            return c
        lax.fori_loop(0, count // PEER_BATCH, body, 0)

    assert TS == TD

    @pl.when(live_next)
    def _():
        def body(tt, c):
            for j in range(PEER_UNROLL):
                score_token(tt * PEER_UNROLL + j, False)
                score_token(tt * PEER_UNROLL + j, True)
            for j in range(PEER_UNROLL):
                issue_token(idxn_ref, 1 - slot, tt * PEER_UNROLL + j)
            return c
        lax.fori_loop(0, TD // PEER_UNROLL, body, 0)

    @pl.when(jnp.logical_not(live_next))
    def _():
        batched(TD, lambda t: score_token(t, False))
        batched(TS, lambda t: score_token(t, True))

    z = z_ref[...]
    z1 = z.astype(bf16)
    r1 = z - z1.astype(f32)
    z2 = r1.astype(bf16)
    z3 = (r1 - z2.astype(f32)).astype(bf16)
    fold = fold_ref[...]
    sc = (jnp.dot(z1, fold, preferred_element_type=f32) + jnp.dot(z2, fold, preferred_element_type=f32)
          + jnp.dot(z3, fold, preferred_element_type=f32))
    w = gate_ref[...] * (0.5 * sc * (1.0 + lax.erf(sc * (2.0 ** -0.5))))
    z_ref[...] = jnp.dot(w.astype(bf16), spread_ref[...], preferred_element_type=f32)

    batched(TD, lambda t: mix_token(t, False))
    batched(TS, lambda t: mix_token(t, True))
    o_ref[...] = x + y_ref[...].reshape(TT, D_MODEL)


def _experts_chunk(step0, nsteps, seq_steps, idx_d, h2d, g, gate, spread, su, sv, uv):
    TD, TS, NS = PEER_TD, PEER_TS, PEER_SLOTS
    TT, PR = TD + TS, SUBLANES // 2
    last = step0 + nsteps - 1
    staged = pl.BlockSpec((TS * NS * PR, 128), lambda i: (i, 0))
    return pl.pallas_call(
        functools.partial(_expert_kernel, step0=step0, seq_steps=seq_steps, pad_steps=PAD_ROWS // TT),
        grid=(nsteps,),
        in_specs=[pl.BlockSpec((TD, NS), lambda i: (step0 + i, 0), memory_space=pltpu.SMEM),
                  pl.BlockSpec((TD, NS), lambda i: (jnp.minimum(step0 + i + 1, last), 0), memory_space=pltpu.SMEM),
                  pl.BlockSpec((TT, D_MODEL), lambda i: (step0 + i, 0)),
                  pl.BlockSpec((1, D_MODEL), lambda i: (0, 0)),
                  pl.BlockSpec((TT, NS), lambda i: (step0 + i, 0)),
                  pl.BlockSpec((NS, SUBLANES * NS), lambda i: (0, 0)),
                  pl.BlockSpec((SUBLANES * NS, NS), lambda i: (0, 0)),
                  staged, staged,
                  pl.BlockSpec(memory_space=pl.ANY)],
        out_specs=pl.BlockSpec((TT, D_MODEL), lambda i: (step0 + i, 0)),
        out_shape=jax.ShapeDtypeStruct(h2d.shape, jnp.float32),
        scratch_shapes=[pltpu.VMEM((2, 2, TD * NS * PR, 128), jnp.uint32),
                        pltpu.VMEM((TT, SUBLANES * NS), jnp.float32),
                        pltpu.VMEM((TT, SUBLANES, 128), jnp.float32),
                        pltpu.VMEM((TT, SUBLANES, 128), jnp.float32),
                        pltpu.SemaphoreType.DMA((2,))],
        input_output_aliases={2: 0},
        compiler_params=_cparams(("arbitrary",)),
        name="peer_experts",
    )(idx_d, idx_d, h2d, g, gate, spread, spread.T, su, sv, uv)


def _experts(h2d, g, eidx, gate, u4, v4, seq_rows):
    T = h2d.shape[0]
    TD, TS, NS = PEER_TD, PEER_TS, PEER_SLOTS
    TT = TD + TS
    assert seq_rows % TT == 0
    steps = T // TT
    chunks = _pick_tile(steps, (24, 12, 8, 6, 4, 3, 2, 1))
    spc = steps // chunks
    e3 = eidx.reshape(steps, TT, NS)
    idx_d = e3[:, :TD].reshape(steps * TD, NS)
    idx_s = e3[:, TD:].reshape(chunks, spc * TS * NS)
    spread = jnp.asarray(np.repeat(np.eye(NS, dtype=np.float32), SUBLANES, axis=1), jnp.bfloat16)
    uv = jnp.stack([u4, v4], axis=1)
    for c in range(chunks):
        su, sv = _sc_gather(uv, idx_s[c])
        su, sv = (a.reshape(-1, 128) for a in (su, sv))
        h2d = _experts_chunk(c * spc, spc, seq_rows // TT, idx_d, h2d, g, gate, spread, su, sv, uv)
    return h2d


def _final_kernel(h_ref, g_ref, o_ref):
    o_ref[...] = _rms(h_ref[...], g_ref[...])


def _final(h3, g, b0, nb):
    _, Lp, _ = h3.shape
    S = Lp - BLK
    return pl.pallas_call(
        _final_kernel,
        grid=(nb, S // BLK),
        in_specs=[pl.BlockSpec((None, BLK, D_MODEL), lambda b, r: (b + b0, r + 1, 0)),
                  pl.BlockSpec((1, D_MODEL), lambda b, r: (0, 0))],
        out_specs=pl.BlockSpec((None, BLK, D_MODEL), lambda b, r: (b, r, 0)),
        out_shape=jax.ShapeDtypeStruct((nb, S, D_MODEL), jnp.float32),
        compiler_params=_cparams(("parallel", "parallel")),
        name="final_norm",
    )(h3, g)


def _pick_tile(T, candidates):
    for c in candidates:
        if T % c == 0:
            return c
    raise ValueError(f"no tile for {T}")


def kernel(x_prompt, x_sample, meta_tokens, g_mix, w_in, attn_sinks, lb_logits, hg_norm_g, w_att_branch,
           w_hg_branch, w_out, g_ffn, w_peer_q, peer_sub_keys, peer_u, peer_v, g_final):
    f32, bf16 = jnp.float32, jnp.bfloat16
    depth = w_in.shape[0]
    nb_p, S, _ = x_prompt.shape
    nb_s = x_sample.shape[0]
    assert x_sample.shape[1] == S and S % BLK == 0
    NB = nb_p + nb_s
    Lp = S + BLK
    T = NB * Lp

    x = jnp.concatenate([x_prompt, x_sample], axis=0)
    head = jnp.concatenate([jnp.zeros((PAD_ROWS, D_MODEL), f32), meta_tokens.astype(f32)], axis=0)
    h = jnp.concatenate([jnp.broadcast_to(head[None], (NB, BLK, D_MODEL)), x], axis=1).reshape(T, D_MODEL)

    sizes = (1024, 256, 256, 1024, 1024, 1024, 1024, 1024, 1024, 1024)
    offs = np.concatenate([[0], np.cumsum(sizes)])
    order = (0, 8, 9, 3, 4, 5, 6, 7, 1, 2)
    perm = np.concatenate([np.arange(offs[k], offs[k + 1]) for k in order])

    sm = jax.nn.softmax(lb_logits.astype(f32), axis=0)
    lb_all = jnp.maximum(jnp.cumsum(sm, axis=0) - sm[0:1], 0.0)

    tables = _rope_tables(Lp)
    consts_f = _hg_consts(False)
    consts_b = _hg_consts(True)
    tm_big = _pick_tile(T, (512, 384, 256, 128))
    tm_route = _pick_tile(T, (256, 128))

    for l in range(depth):
        w_l = w_in[l][:, perm].astype(bf16)
        proj = _inproj(h, g_mix[l].reshape(1, -1), w_l, tm_big, IN_W // 4)
        proj3 = proj.reshape(NB, Lp, IN_W)
        a_out = _attention(proj3, attn_sinks[l].astype(f32), tables)
        lb3 = lb_all[l].reshape(2, 1, -1)
        o_fwd, o_bwd = _hgrn(proj3, lb3, consts_f, consts_b)
        h = _merge(a_out.reshape(T, -1), o_fwd.reshape(T, -1), o_bwd.reshape(T, -1), hg_norm_g[l].reshape(1, -1),
                   proj, h, w_att_branch[l].astype(bf16), w_hg_branch[l].astype(bf16), w_out[l].astype(bf16),
                   tm_route)
        g_f = g_ffn[l].reshape(1, -1)
        keys = peer_sub_keys[l].reshape(2 * PEER_HEADS, PEER_NKEYS, PEER_QDIM // 2).astype(bf16)
        eidx_t, gate_t = _route(h, g_f, w_peer_q[l].astype(bf16), keys, tm_route)
        h = _experts(h, g_f, eidx_t.T, gate_t.T, _pack_bf16_pairs(peer_u[l]), _pack_bf16_pairs(peer_v[l]), Lp)

    h3 = h.reshape(NB, Lp, D_MODEL)
    g_fin = g_final.reshape(1, -1)
    return _final(h3, g_fin, 0, nb_p), _final(h3, g_fin, nb_p, nb_s)
```
